```python
import math
import jax
import jax.numpy as jnp
from jax import lax
import numpy as np

D_MODEL = 1024
BATCH = 4
SEQ = 8192
DEPTH = 2

CHUNK = 64
HG_HEADS = 4
HG_WIDTH = D_MODEL // 2
HG_DK = HG_WIDTH // HG_HEADS
RET_HEADS = 4
RET_WIDTH = D_MODEL // 4
RET_DK = RET_WIDTH // RET_HEADS
GDN_HEADS = 4
GDN_WIDTH = D_MODEL // 4
GDN_DK = GDN_WIDTH // GDN_HEADS
MIX_WIDTH = HG_WIDTH + RET_WIDTH + GDN_WIDTH
N_IN = 4 * HG_WIDTH + 4 * RET_WIDTH + 4 * GDN_WIDTH + 2 * GDN_HEADS
CONV_K = 4
ROPE_BASE = 10000.0
N_EXPERTS = 32
TOP_K = 4
D_FF_EXPERT = D_MODEL
SWIGLU_ALPHA = 1.702
SWIGLU_LIMIT = 7.0
EXPERT_BLOCK = 512
NORM_EPS = 1e-6
L2_EPS = 1e-6

kernel_name = 'hybrid_hgrn2_retnet_gdn_moe_adaln'


def rmsnorm(x, g):
    x32 = x.astype(jnp.float32)
    y = x32 * lax.rsqrt(jnp.mean(x32 * x32, axis=-1, keepdims=True) + NORM_EPS)
    return (y * g.astype(jnp.float32)).astype(x.dtype)


def l2norm(t):
    return t * lax.rsqrt(jnp.sum(t * t, axis=-1, keepdims=True) + L2_EPS)


def to_chunks(t):
    B, S, H = t.shape[:3]
    t = t.reshape((B, S // CHUNK, CHUNK, H) + t.shape[3:])
    return jnp.moveaxis(t, (1, 3), (0, 2))


def from_chunks(t):
    nC, B, H, C, d = t.shape
    return jnp.moveaxis(t, (0, 2), (1, 3)).reshape(B, nC * C, H, d)


def hgrn2_mix(q, f_logit, v, lb):
    log_f = jnp.logaddexp(jnp.log(lb), jnp.log1p(-lb) + jax.nn.log_sigmoid(f_logit))
    k = (1.0 - lb) * jax.nn.sigmoid(-f_logit)
    q = jax.nn.silu(q)
    qc, kc, vc, lfc = to_chunks(q), to_chunks(k), to_chunks(v), to_chunks(log_f)
    bc = jnp.cumsum(lfc, axis=-2)
    causal = jnp.tril(jnp.ones((CHUNK, CHUNK), bool))[:, :, None]

    def step(state, inp):
        q_, k_, v_, b_ = inp
        rel = jnp.exp(jnp.where(causal, b_[..., :, None, :] - b_[..., None, :, :], -jnp.inf))
        scores = jnp.einsum('bhtd,bhsd,bhtsd->bhts', q_, k_, rel)
        b_last = b_[..., -1:, :]
        o = scores @ v_ + jnp.einsum('bhtd,bhde->bhte', q_ * jnp.exp(b_), state)
        state = jnp.exp(b_last[..., 0, :])[..., None] * state + jnp.einsum('bhsd,bhse->bhde', k_ * jnp.exp(b_last - b_), v_)
        return state, o

    B, _, H, dk = q.shape
    s0 = jnp.zeros((B, H, dk, v.shape[-1]), jnp.float32)
    _, o = lax.scan(step, s0, (qc, kc, vc, bc))
    return from_chunks(o)


def rotary(t):
    S, d = t.shape[1], t.shape[-1]
    half = d // 2
    inv = ROPE_BASE ** (-jnp.linspace(0.0, 1.0, half, dtype=jnp.float32))
    ang = jnp.arange(S, dtype=jnp.float32)[:, None] * inv[None, :]
    cos, sin = jnp.cos(ang)[:, None, :], jnp.sin(ang)[:, None, :]
    t1, t2 = t[..., :half], t[..., half:]
    return jnp.concatenate([t1 * cos - t2 * sin, t1 * sin + t2 * cos], axis=-1)


def retention_mix(q, k, v):
    B, _, H, dk = q.shape
    q = rotary(q)
    k = rotary(k) * (dk ** -0.5)
    log_g = jnp.log1p(-jnp.exp2(-5.0 - jnp.arange(H, dtype=jnp.float32)))
    qc, kc, vc = to_chunks(q), to_chunks(k), to_chunks(v)
    j = jnp.arange(CHUNK, dtype=jnp.float32)
    causal = jnp.tril(jnp.ones((CHUNK, CHUNK), bool))
    decay = jnp.exp(jnp.where(causal, (j[:, None] - j[None, :])[None] * log_g[:, None, None], -jnp.inf))
    intra = jnp.einsum('nbhts,nbhse->nbhte', jnp.einsum('nbhtd,nbhsd->nbhts', qc, kc) * decay, vc)
    k_w = jnp.exp(log_g[:, None] * (CHUNK - 1.0 - j)[None, :])[:, :, None]
    kv = jnp.einsum('nbhsd,nbhse->nbhde', kc * k_w, vc)
    chunk_decay = jnp.exp(CHUNK * log_g)[:, None, None]

    def step(state, kv_n):
        return chunk_decay * state + kv_n, state

    _, s_prev = lax.scan(step, jnp.zeros((B, H, dk, v.shape[-1]), jnp.float32), kv)
    q_w = jnp.exp(log_g[:, None] * (j + 1.0)[None, :])[:, :, None]
    inter = jnp.einsum('nbhtd,nbhde->nbhte', qc * q_w, s_prev)
    return from_chunks(intra + inter)


def causal_depthwise_conv(u, w):
    return lax.conv_general_dilated(u, w[:, None, :], window_strides=(1,), padding=[(CONV_K - 1, 0)],
                                    dimension_numbers=('NWC', 'WIO', 'NWC'), feature_group_count=u.shape[-1])


def gated_delta_mix(qkv, a_logit, b_logit, A_log, dt_bias):
    B, S, _ = qkv.shape
    q, k, v = jnp.split(qkv, 3, axis=-1)
    q = l2norm(q.reshape(B, S, GDN_HEADS, GDN_DK)) * (GDN_DK ** -0.5)
    k = l2norm(k.reshape(B, S, GDN_HEADS, GDN_DK))
    v = v.reshape(B, S, GDN_HEADS, GDN_DK)
    beta = jax.nn.sigmoid(b_logit)
    g = -jnp.exp(A_log) * jax.nn.softplus(a_logit + dt_bias)
    qc, kc, vc = to_chunks(q), to_chunks(k), to_chunks(v)
    bc = to_chunks(beta)
    gc = jnp.cumsum(to_chunks(g), axis=-1)
    incl = jnp.tril(jnp.ones((CHUNK, CHUNK), bool))
    strict = jnp.tril(jnp.ones((CHUNK, CHUNK), bool), k=-1)
    rel = jnp.exp(jnp.where(incl, gc[..., :, None] - gc[..., None, :], -jnp.inf))
    kk = jnp.einsum('nbhtd,nbhsd->nbhts', kc, kc)
    m = jnp.where(strict, bc[..., :, None] * kk * rel, 0.0)
    rhs = jnp.concatenate([vc * bc[..., None], kc * (bc * jnp.exp(gc))[..., None]], axis=-1)
    sol = lax.linalg.triangular_solve(m, rhs, left_side=True, lower=True, unit_diagonal=True)
    w_val, k_cum = sol[..., :GDN_DK], sol[..., GDN_DK:]
    qk = jnp.einsum('nbhtd,nbhsd->nbhts', qc, kc) * rel
    q_dec = qc * jnp.exp(gc)[..., None]
    k_dec = kc * jnp.exp(gc[..., -1:] - gc)[..., None]
    last = jnp.exp(gc[..., -1])[..., None, None]

    def step(state, inp):
        w_, kcum_, qk_, qdec_, kdec_, last_ = inp
        v_new = w_ - kcum_ @ state
        o = qdec_ @ state + qk_ @ v_new
        state = last_ * state + jnp.swapaxes(kdec_, -1, -2) @ v_new
        return state, o

    s0 = jnp.zeros((B, GDN_HEADS, GDN_DK, GDN_DK), jnp.float32)
    _, o = lax.scan(step, s0, (w_val, k_cum, qk, q_dec, k_dec, last))
    return from_chunks(o)


def hybrid_mixer(h, w_in, w_out, hg_lb, hg_norm_g, ret_norm_g, gdn_conv_w, gdn_A_log, gdn_dt_bias, gdn_norm_g):
    B, S, _ = h.shape
    f32 = jnp.float32
    sizes = [HG_WIDTH] * 4 + [RET_WIDTH] * 4 + [3 * GDN_WIDTH, GDN_WIDTH, GDN_HEADS, GDN_HEADS]
    z = (h @ w_in).astype(f32)
    (hq, hf, hv, hgt, rq, rk, rv, rgt, gqkv, ggt, ga, gb) = jnp.split(z, np.cumsum(sizes)[:-1].tolist(), axis=-1)

    def heads(t, n):
        return t.reshape(B, S, n, -1)

    o_hg = hgrn2_mix(heads(hq, HG_HEADS), heads(hf, HG_HEADS), heads(hv, HG_HEADS), hg_lb)
    o_hg = rmsnorm(o_hg, hg_norm_g) * jax.nn.silu(heads(hgt, HG_HEADS))
    o_ret = retention_mix(heads(rq, RET_HEADS), heads(rk, RET_HEADS), heads(rv, RET_HEADS))
    o_ret = rmsnorm(o_ret, ret_norm_g) * jax.nn.silu(heads(rgt, RET_HEADS))
    qkv = jax.nn.silu(causal_depthwise_conv(gqkv, gdn_conv_w.astype(f32)))
    o_gdn = gated_delta_mix(qkv, ga, gb, gdn_A_log.astype(f32), gdn_dt_bias.astype(f32))
    o_gdn = rmsnorm(o_gdn, gdn_norm_g) * jax.nn.silu(heads(ggt, GDN_HEADS))
    o = jnp.concatenate([o_hg.reshape(B, S, HG_WIDTH), o_ret.reshape(B, S, RET_WIDTH),
                         o_gdn.reshape(B, S, GDN_WIDTH)], axis=-1)
    return o.astype(h.dtype) @ w_out


def moe_ffn(h, router_w, router_b, w1, b1, w2, b2):
    B, S, D = h.shape
    N = B * S
    NK = N * TOP_K
    xf = h.reshape(N, D)
    logits = (xf @ router_w + router_b).astype(jnp.float32)
    top_v, top_e = lax.top_k(logits, TOP_K)
    gates = jax.nn.softmax(top_v, axis=-1).astype(h.dtype)
    flat_e = top_e.reshape(NK)
    flat_tok = jnp.repeat(jnp.arange(N, dtype=jnp.int32), TOP_K)
    order = jnp.argsort(flat_e)
    se, stok, sg = flat_e[order], flat_tok[order], gates.reshape(NK)[order]
    counts = jax.ops.segment_sum(jnp.ones((NK,), jnp.int32), flat_e, num_segments=N_EXPERTS)
    padded = (counts + EXPERT_BLOCK - 1) // EXPERT_BLOCK * EXPERT_BLOCK
    pend = jnp.cumsum(padded)
    pstart = pend - padded
    cstart = jnp.cumsum(counts) - counts
    dest = pstart[se] + jnp.arange(NK, dtype=jnp.int32) - cstart[se]
    n_blocks = -(-NK // EXPERT_BLOCK) + N_EXPERTS
    P = n_blocks * EXPERT_BLOCK
    buf_tok = jnp.full((P,), N, jnp.int32).at[dest].set(stok)
    buf_gate = jnp.zeros((P,), h.dtype).at[dest].set(sg)
    block_e = jnp.minimum(jnp.searchsorted(pend, jnp.arange(n_blocks, dtype=jnp.int32) * EXPERT_BLOCK, side='right'),
                          N_EXPERTS - 1)
    xpad = jnp.concatenate([xf, jnp.zeros((1, D), xf.dtype)], axis=0)

    def expert_block(args):
        tok, e = args
        hid = xpad[tok] @ w1[e] + b1[e]
        x_glu = jnp.minimum(hid[:, :D_FF_EXPERT], SWIGLU_LIMIT)
        x_lin = jnp.clip(hid[:, D_FF_EXPERT:], -SWIGLU_LIMIT, SWIGLU_LIMIT)
        act = x_glu * jax.nn.sigmoid(SWIGLU_ALPHA * x_glu) * (x_lin + 1.0)
        return act @ w2[e] + b2[e]

    y = lax.map(expert_block, (buf_tok.reshape(n_blocks, EXPERT_BLOCK), block_e))
    y = y.reshape(P, D) * buf_gate[:, None]
    out = jax.ops.segment_sum(y, buf_tok, num_segments=N + 1)[:N]
    return out.reshape(B, S, D)


def setup_inputs(seed: int = 0) -> dict:
    key = jax.random.key(seed)
    ks = jax.random.split(key, 24)
    f32 = jnp.float32
    D = D_MODEL

    def nrm(k, shape, s):
        return jax.random.normal(k, shape, f32) * s

    dt = jnp.exp(jax.random.uniform(ks[13], (DEPTH, GDN_HEADS), f32, math.log(1e-3), math.log(1e-1)))
    return {
        'x': nrm(ks[0], (BATCH, SEQ, D), 1.0),
        'c': nrm(ks[1], (BATCH, D), 1.0),
        'ada_w': nrm(ks[2], (DEPTH, D, 6 * D), 0.5 * D ** -0.5),
        'ada_b': nrm(ks[3], (DEPTH, 6 * D), 0.02),
        'norm1_g': 1.0 + nrm(ks[4], (DEPTH, D), 0.02),
        'norm2_g': 1.0 + nrm(ks[5], (DEPTH, D), 0.02),
        'w_in': nrm(ks[6], (DEPTH, D, N_IN), D ** -0.5),
        'w_out': nrm(ks[7], (DEPTH, MIX_WIDTH, D), MIX_WIDTH ** -0.5),
        'hg_lb_logits': nrm(ks[8], (DEPTH, HG_WIDTH), 0.5),
        'hg_norm_g': 1.0 + nrm(ks[9], (DEPTH, HG_HEADS, HG_DK), 0.02),
        'ret_norm_g': 1.0 + nrm(ks[10], (DEPTH, RET_HEADS, RET_DK), 0.02),
        'gdn_conv_w': nrm(ks[11], (DEPTH, CONV_K, 3 * GDN_WIDTH), CONV_K ** -0.5),
        'gdn_A_log': jnp.log(jax.random.uniform(ks[12], (DEPTH, GDN_HEADS), f32, 1.0, 16.0)),
        'gdn_dt_bias': dt + jnp.log(-jnp.expm1(-dt)),
        'gdn_norm_g': 1.0 + nrm(ks[14], (DEPTH, GDN_HEADS, GDN_DK), 0.02),
        'router_w': nrm(ks[15], (DEPTH, D, N_EXPERTS), D ** -0.5),
        'router_b': nrm(ks[16], (DEPTH, N_EXPERTS), 0.01),
        'exp_w1': nrm(ks[17], (DEPTH, N_EXPERTS, D, 2 * D_FF_EXPERT), D ** -0.5),
        'exp_b1': nrm(ks[18], (DEPTH, N_EXPERTS, 2 * D_FF_EXPERT), 0.01),
        'exp_w2': nrm(ks[19], (DEPTH, N_EXPERTS, D_FF_EXPERT, D), D_FF_EXPERT ** -0.5),
        'exp_b2': nrm(ks[20], (DEPTH, N_EXPERTS, D), 0.01),
        'final_norm_g': 1.0 + nrm(ks[21], (D,), 0.02),
    }


def reference(x, c, ada_w, ada_b, norm1_g, norm2_g, w_in, w_out, hg_lb_logits, hg_norm_g, ret_norm_g,
              gdn_conv_w, gdn_A_log, gdn_dt_bias, gdn_norm_g, router_w, router_b, exp_w1, exp_b1, exp_w2,
              exp_b2, final_norm_g):
    lb = jnp.cumsum(jax.nn.softmax(hg_lb_logits.astype(jnp.float32), axis=0), axis=0)
    lb = jnp.maximum(lb - lb[0], 0.0)
    cond = jax.nn.silu(c)
    for l in range(DEPTH):
        mod = cond @ ada_w[l] + ada_b[l]
        sh1, sc1, gt1, sh2, sc2, gt2 = jnp.split(mod[:, None, :], 6, axis=-1)
        hn = rmsnorm(x, norm1_g[l]) * (1.0 + sc1) + sh1
        x = x + gt1 * hybrid_mixer(hn, w_in[l], w_out[l], lb[l].reshape(HG_HEADS, HG_DK), hg_norm_g[l],
                                   ret_norm_g[l], gdn_conv_w[l], gdn_A_log[l], gdn_dt_bias[l], gdn_norm_g[l])
        hn = rmsnorm(x, norm2_g[l]) * (1.0 + sc2) + sh2
        x = x + gt2 * moe_ffn(hn, router_w[l], router_b[l], exp_w1[l], exp_b1[l], exp_w2[l], exp_b2[l])
    return rmsnorm(x, final_norm_g)
```

```python
import functools
import math

import numpy as np
import jax
import jax.numpy as jnp
from jax import lax
from jax.experimental import pallas as pl
from jax.experimental.pallas import tpu as pltpu

F32 = jnp.float32
BF16 = jnp.bfloat16

D_MODEL = 1024
HG_HEADS, HG_DK = 4, 128
HG_WIDTH = HG_HEADS * HG_DK
RET_HEADS, RET_DK = 4, 64
RET_WIDTH = RET_HEADS * RET_DK
GDN_HEADS, GDN_DK = 4, 64
GDN_WIDTH = GDN_HEADS * GDN_DK
CONV_K = 4
ROPE_BASE = 10000.0
N_EXPERTS = 32
TOP_K = 4
D_FF = D_MODEL
SWIGLU_ALPHA = 1.702
SWIGLU_LIMIT = 7.0
NORM_EPS = 1e-6
L2_EPS = 1e-6
GDN_CHUNK = 64

LANES = 128
SUBLANES = 8
VMEM_LIMIT = 56 * 1024 * 1024

T_BLK = 256
TM = 256
R_BLK = 512
N_LEVELS = 8

ZH_W = 4 * HG_WIDTH
ZR_W = 6 * RET_WIDTH
ZG_W = 4 * GDN_WIDTH
ZAB_W = LANES
Z_W = ZH_W + ZR_W + ZG_W + ZAB_W


def _dot(a, b):
    return jnp.dot(a, b, preferred_element_type=F32)


def _dot_nt(a, b):
    return lax.dot_general(a, b, (((1,), (1,)), ((), ())), preferred_element_type=F32)


def _dot_tn(a, b):
    return lax.dot_general(a, b, (((0,), (0,)), ((), ())), preferred_element_type=F32)


def _split3(x):
    hi = x.astype(BF16)
    r = x - hi.astype(F32)
    mid = r.astype(BF16)
    lo = (r - mid.astype(F32)).astype(BF16)
    return hi, mid, lo


def _dot_exact_lhs01(c, x):
    hi, mid, lo = _split3(x)
    return _dot(c, hi) + _dot(c, mid) + _dot(c, lo)


def _dot_exact_rhs01(x, c):
    hi, mid, lo = _split3(x)
    return _dot(hi, c) + _dot(mid, c) + _dot(lo, c)


def _sigmoid(x):
    return 1.0 / (1.0 + jnp.exp(-x))


def _silu(x):
    return x * _sigmoid(x)


def _softplus(x):
    return jnp.maximum(x, 0.0) + jnp.log1p(jnp.exp(-jnp.abs(x)))


def _bf(x):
    return x.astype(BF16)


def _level_matrix(t):
    i = np.arange(t)[:, None]
    j = np.arange(t)[None, :]
    x = i ^ j
    lv = np.floor(np.log2(np.maximum(x, 1))).astype(np.int32)
    lv = np.where(i == j, int(math.log2(t)), lv)
    lv = np.where(i < j, -1, lv)
    return lv.astype(np.int32)


def _hgrn_exponent_matrix(t):
    n_lev = int(math.log2(t))
    f = np.zeros((2 + n_lev, t, t), np.float32)
    u = np.arange(t)[None, :]
    r = np.arange(t)[:, None]
    f[0] = (u <= r)
    f[1] = (u > r)
    for l in range(n_lev):
        h = 1 << l
        base = (r // (2 * h)) * (2 * h)
        mid = base + h
        upper = (r - base) >= h
        f[2 + l] = np.where(upper, (u >= mid) & (u <= r), (u > r) & (u < mid))
    return f.reshape((2 + n_lev) * t, t)


def _gdn_masks(t):
    lv = _level_matrix(t)
    top = int(math.log2(GDN_CHUNK)) - 1
    strict = (lv >= 0) & (lv <= top)
    incl = strict | (lv == int(math.log2(t)))
    d8 = (lv >= 0) & (lv <= 2)
    merges = [(lv == l) for l in range(3, top + 1)]
    return np.stack([strict, incl, d8] + merges).astype(np.float32)


def _block_diag_mask(t, blk):
    i = np.arange(t)
    return (i[:, None] // blk == i[None, :] // blk).astype(np.float32)


def _head_masks(width, heads):
    lane = np.arange(width)[None, :]
    m = np.zeros((SUBLANES, width), np.float32)
    for h in range(heads):
        m[h] = (lane // (width // heads) == h)[0]
    return m


def _adaln_kernel(c_ref, w_ref, b_ref, o_ref):
    cond = _silu(c_ref[...])
    o_ref[0] = jnp.dot(cond, w_ref[0], preferred_element_type=F32,
                       precision=lax.Precision.HIGHEST) + b_ref[0]


def _adaln(c_pad, ada_w, ada_b):
    depth, d, n6 = ada_w.shape
    tn = n6 // 4
    return pl.pallas_call(
        _adaln_kernel,
        grid=(depth, n6 // tn),
        in_specs=[pl.BlockSpec((SUBLANES, d), lambda l, j: (0, 0)),
                  pl.BlockSpec((1, d, tn), lambda l, j: (l, 0, j)),
                  pl.BlockSpec((1, 1, tn), lambda l, j: (l, 0, j))],
        out_specs=pl.BlockSpec((1, SUBLANES, tn), lambda l, j: (l, 0, j)),
        out_shape=jax.ShapeDtypeStruct((depth, SUBLANES, n6), F32),
        compiler_params=pltpu.CompilerParams(vmem_limit_bytes=VMEM_LIMIT),
        name="adaln",
    )(c_pad, ada_w, ada_b.reshape(depth, 1, n6))


def _inproj_kernel(x_ref, mod_ref, g_ref, w_ref, zh_ref, zr_ref, zg_ref, zab_ref):
    x = x_ref[...]
    y = x * lax.rsqrt(jnp.mean(x * x, axis=-1, keepdims=True) + NORM_EPS)
    hn = (y * g_ref[0:1, :]) * (1.0 + mod_ref[0, 0:1, :]) + mod_ref[0, 1:2, :]
    hb = _bf(hn)
    zh_ref[...] = _dot(hb, w_ref[:, 0:ZH_W])
    zr_ref[...] = _dot(hb, w_ref[:, ZH_W:ZH_W + ZR_W])
    zg_ref[...] = _dot(hb, w_ref[:, ZH_W + ZR_W:ZH_W + ZR_W + ZG_W])
    zab_ref[...] = _dot(hb, w_ref[:, ZH_W + ZR_W + ZG_W:Z_W])


def _inproj(x2d, modv, gpar, w_bf, tiles_per_batch):
    n = x2d.shape[0]
    row = lambda i: (i, 0)
    return pl.pallas_call(
        _inproj_kernel,
        grid=(n // TM,),
        in_specs=[pl.BlockSpec((TM, D_MODEL), row),
                  pl.BlockSpec((1, SUBLANES, D_MODEL), lambda i: (i // tiles_per_batch, 0, 0)),
                  pl.BlockSpec((SUBLANES, D_MODEL), lambda i: (0, 0)),
                  pl.BlockSpec((D_MODEL, Z_W), lambda i: (0, 0))],
        out_specs=[pl.BlockSpec((TM, ZH_W), row), pl.BlockSpec((TM, ZR_W), row),
                   pl.BlockSpec((TM, ZG_W), row), pl.BlockSpec((TM, ZAB_W), row)],
        out_shape=[jax.ShapeDtypeStruct((n, ZH_W), F32), jax.ShapeDtypeStruct((n, ZR_W), F32),
                   jax.ShapeDtypeStruct((n, ZG_W), F32), jax.ShapeDtypeStruct((n, ZAB_W), F32)],
        compiler_params=pltpu.CompilerParams(dimension_semantics=("arbitrary",),
                                             vmem_limit_bytes=VMEM_LIMIT),
        name="inproj",
    )(x2d, modv, gpar, w_bf)


def _hgrn_kernel(zh_ref, f_ref, lv_ref, par_ref, o_ref, state_ref, ex_ref):
    @pl.when(pl.program_id(1) == 0)
    def _():
        state_ref[...] = jnp.zeros_like(state_ref)

    W = HG_WIDTH
    hq = zh_ref[:, 0:W]
    hf = zh_ref[:, W:2 * W]
    loglb = par_ref[0:1, :]
    log1mlb = par_ref[1:2, :]
    onemlb = par_ref[2:3, :]

    q = _silu(hq)
    e = jnp.exp(-jnp.abs(hf))
    inv = 1.0 / (1.0 + e)
    k = onemlb * (jnp.where(hf >= 0, e, 1.0) * inv)
    logsig = jnp.minimum(hf, 0.0) - jnp.log1p(e)
    c = log1mlb + logsig
    lf = jnp.maximum(loglb, c) + jnp.log1p(jnp.exp(-jnp.abs(loglb - c)))

    ex_ref[...] = jnp.exp(_dot_exact_lhs01(f_ref[...], lf))

    lv = lv_ref[...]
    t = T_BLK
    for h in range(HG_HEADS):
        cs = slice(h * HG_DK, (h + 1) * HG_DK)
        qh = q[:, cs]
        kh = k[:, cs]
        vh = _bf(zh_ref[:, 2 * W + h * HG_DK:2 * W + (h + 1) * HG_DK])
        s = jnp.where(lv == N_LEVELS, _dot_nt(_bf(qh), _bf(kh)), 0.0)
        for l in range(N_LEVELS):
            el = ex_ref[(2 + l) * t:(3 + l) * t, cs]
            s = jnp.where(lv == l, _dot_nt(_bf(qh * el), _bf(kh * el)), s)
        eb = ex_ref[0:t, cs]
        ebl = ex_ref[t:2 * t, cs]
        st = state_ref[h]
        o = _dot(_bf(s), vh) + _dot_nt(_bf(qh * eb), _bf(st))
        upd = _dot_tn(vh, _bf(kh * ebl))
        state_ref[h] = st * ex_ref[t - 1:t, cs] + upd
        ms = jnp.mean(o * o, axis=-1, keepdims=True)
        gate = _silu(zh_ref[:, 3 * W + h * HG_DK:3 * W + (h + 1) * HG_DK])
        o_ref[:, cs] = _bf(o * lax.rsqrt(ms + NORM_EPS) * par_ref[3:4, cs] * gate)


def _hgrn(zh, f_mat, lv, par, batch, steps):
    n = zh.shape[0]
    const = lambda b, j: (0, 0)
    return pl.pallas_call(
        _hgrn_kernel,
        grid=(batch, steps),
        in_specs=[pl.BlockSpec((T_BLK, ZH_W), lambda b, j: (b * steps + j, 0)),
                  pl.BlockSpec(f_mat.shape, const),
                  pl.BlockSpec(lv.shape, const),
                  pl.BlockSpec(par.shape, const)],
        out_specs=pl.BlockSpec((T_BLK, HG_WIDTH), lambda b, j: (b * steps + j, 0)),
        out_shape=jax.ShapeDtypeStruct((n, HG_WIDTH), BF16),
        scratch_shapes=[pltpu.VMEM((HG_HEADS, HG_DK, HG_DK), F32),
                        pltpu.VMEM(((2 + N_LEVELS) * T_BLK, HG_WIDTH), F32)],
        compiler_params=pltpu.CompilerParams(dimension_semantics=("arbitrary", "arbitrary"),
                                             vmem_limit_bytes=VMEM_LIMIT),
        name="hgrn2",
    )(zh, f_mat, lv, par)


def _ret_kernel(zr_ref, cos_ref, sin_ref, dmat_ref, qw_ref, kw_ref, par_ref, hm_ref, bd_ref,
                o_ref, state_ref):
    @pl.when(pl.program_id(1) == 0)
    def _():
        state_ref[...] = jnp.zeros_like(state_ref)

    W = RET_WIDTH
    cos = cos_ref[...]
    sin = sin_ref[...]
    q = zr_ref[:, 0:W] * cos + zr_ref[:, W:2 * W] * sin
    k = (zr_ref[:, 2 * W:3 * W] * cos + zr_ref[:, 3 * W:4 * W] * sin) * (RET_DK ** -0.5)
    v = zr_ref[:, 4 * W:5 * W]
    kb = _bf(k)
    bd = bd_ref[...]

    s_parts = []
    v_parts = []
    for h in range(RET_HEADS):
        hm = hm_ref[h:h + 1, :]
        s_parts.append(_bf(_dot_nt(_bf(q * hm), kb) * dmat_ref[h]))
        v_parts.append(_bf(v * hm))
    st = state_ref[...]
    o = (_dot(jnp.concatenate(s_parts, axis=1), jnp.concatenate(v_parts, axis=0))
         + _dot(_bf(q * qw_ref[...]), _bf(st)))
    kv = _dot_tn(_bf(k * kw_ref[...]), _bf(v))
    state_ref[...] = st * par_ref[0:1, :] + bd * kv

    ms = _dot_exact_rhs01(o * o, _bf(bd)) * (1.0 / RET_DK)
    gate = _silu(zr_ref[:, 5 * W:6 * W])
    o_ref[...] = _bf(o * lax.rsqrt(ms + NORM_EPS) * par_ref[1:2, :] * gate)


def _ret(zr, cos_t, sin_t, dmat, qw, kw, par, hm, bd, batch, steps):
    n = zr.shape[0]
    const2 = lambda b, j: (0, 0)
    return pl.pallas_call(
        _ret_kernel,
        grid=(batch, steps),
        in_specs=[pl.BlockSpec((T_BLK, ZR_W), lambda b, j: (b * steps + j, 0)),
                  pl.BlockSpec((T_BLK, RET_WIDTH), lambda b, j: (j, 0)),
                  pl.BlockSpec((T_BLK, RET_WIDTH), lambda b, j: (j, 0)),
                  pl.BlockSpec(dmat.shape, lambda b, j: (0, 0, 0)),
                  pl.BlockSpec(qw.shape, const2),
                  pl.BlockSpec(kw.shape, const2),
                  pl.BlockSpec(par.shape, const2),
                  pl.BlockSpec(hm.shape, const2),
                  pl.BlockSpec(bd.shape, const2)],
        out_specs=pl.BlockSpec((T_BLK, RET_WIDTH), lambda b, j: (b * steps + j, 0)),
        out_shape=jax.ShapeDtypeStruct((n, RET_WIDTH), BF16),
        scratch_shapes=[pltpu.VMEM((RET_WIDTH, RET_WIDTH), F32)],
        compiler_params=pltpu.CompilerParams(dimension_semantics=("arbitrary", "arbitrary"),
                                             vmem_limit_bytes=VMEM_LIMIT),
        name="retention",
    )(zr, cos_t, sin_t, dmat, qw, kw, par, hm, bd)


def _gdn_kernel(zg_ref, zab_ref, convw_ref, par_ref, eab_ref, gm_ref, bd_ref, tril_ref, hm_ref,
                o_ref, ext_ref, state_ref, obuf_ref):
    @pl.when(pl.program_id(1) == 0)
    def _():
        ext_ref[0:SUBLANES, :] = jnp.zeros((SUBLANES, 3 * GDN_WIDTH), F32)
        state_ref[...] = jnp.zeros_like(state_ref)

    W = GDN_WIDTH
    t = T_BLK
    u = zg_ref[:, 0:3 * W]
    ext_ref[SUBLANES:SUBLANES + t, :] = u
    conv = (convw_ref[3:4, :] * u
            + convw_ref[2:3, :] * ext_ref[SUBLANES - 1:SUBLANES - 1 + t, :]
            + convw_ref[1:2, :] * ext_ref[SUBLANES - 2:SUBLANES - 2 + t, :]
            + convw_ref[0:1, :] * ext_ref[SUBLANES - 3:SUBLANES - 3 + t, :])
    ext_ref[0:SUBLANES, :] = u[t - SUBLANES:t, :]
    qkv = _silu(conv)
    q = qkv[:, 0:W]
    k = qkv[:, W:2 * W]
    v = qkv[:, 2 * W:3 * W]

    bd = bd_ref[...]
    bdb = _bf(bd)
    qn = q * lax.rsqrt(_dot_exact_rhs01(q * q, bdb) + L2_EPS) * (GDN_DK ** -0.5)
    kn = k * lax.rsqrt(_dot_exact_rhs01(k * k, bdb) + L2_EPS)

    ab = zab_ref[...]
    a_exp = _dot_exact_rhs01(ab, eab_ref[:, 0:W])
    b_exp = _dot_exact_rhs01(ab, eab_ref[:, W:2 * W])
    beta = _sigmoid(b_exp)
    g = par_ref[0:1, :] * _softplus(a_exp + par_ref[1:2, :])
    gc = _dot_exact_lhs01(tril_ref[...], g)
    gl = _dot_exact_lhs01(bdb, g)
    eg = jnp.exp(gc)
    vb = v * beta
    kbeta = kn * beta * eg
    qdec = qn * eg
    kdec = kn * jnp.exp(gl - gc)

    strict = gm_ref[0]
    incl = gm_ref[1]
    knb = _bf(kn)
    x_parts, qk_parts, hms = [], [], []
    for h in range(GDN_HEADS):
        hm = hm_ref[h:h + 1, :]
        hms.append(hm)
        cb = jnp.broadcast_to(gc[:, h * GDN_DK:h * GDN_DK + 1], (t, t))
        rel = jnp.exp(jnp.where(incl > 0, cb - cb.T, -jnp.inf))
        kk = _dot_nt(_bf(kn * hm), knb)
        bcol = jnp.broadcast_to(beta[:, h * GDN_DK:h * GDN_DK + 1], (t, t))
        m = strict * (bcol * kk * rel)
        d = _bf(m * gm_ref[2])
        d2 = _dot(d, d)
        d2b = _bf(d2)
        d4 = _dot(d2b, d2b)
        x = d2 - d.astype(F32) - _dot(d, d2b)
        x = x + d4 + _dot(_bf(x), _bf(d4))
        for lvl in range(3, gm_ref.shape[0]):
            lo = _bf(m * gm_ref[lvl])
            xb = _bf(x)
            y = lo.astype(F32) + _dot(xb, lo)
            x = x - (y + _dot(_bf(y), xb))
        x_parts.append(_bf(x))
        qk_parts.append(_bf(_dot_nt(_bf(qn * hm), knb) * rel))
    xcat = jnp.concatenate(x_parts, axis=1)
    qkcat = jnp.concatenate(qk_parts, axis=1)

    def stack(a):
        return jnp.concatenate([_bf(a * hm) for hm in hms], axis=0)

    w = vb + _dot(xcat, stack(vb))
    kcum = kbeta + _dot(xcat, stack(kbeta))
    a1 = _dot(qkcat, stack(w))
    qeff = qdec - _dot(qkcat, stack(kcum))

    for c in range(t // GDN_CHUNK):
        rows = slice(c * GDN_CHUNK, (c + 1) * GDN_CHUNK)
        st = state_ref[...]
        stb = _bf(st)
        vnew = w[rows] - _dot(_bf(kcum[rows]), stb)
        obuf_ref[rows, :] = _dot(_bf(qeff[rows]), stb) + a1[rows]
        upd = _dot_tn(_bf(kdec[rows]), _bf(vnew))
        last = eg[(c + 1) * GDN_CHUNK - 1:(c + 1) * GDN_CHUNK, :]
        state_ref[...] = st * last + bd * upd

    o = obuf_ref[...]
    ms = _dot_exact_rhs01(o * o, bdb) * (1.0 / GDN_DK)
    gate = _silu(zg_ref[:, 3 * W:4 * W])
    o_ref[...] = _bf(o * lax.rsqrt(ms + NORM_EPS) * par_ref[2:3, :] * gate)


def _gdn(zg, zab, convw, par, eab, gm, bd, tril, hm, batch, steps):
    n = zg.shape[0]
    const2 = lambda b, j: (0, 0)
    return pl.pallas_call(
        _gdn_kernel,
        grid=(batch, steps),
        in_specs=[pl.BlockSpec((T_BLK, ZG_W), lambda b, j: (b * steps + j, 0)),
                  pl.BlockSpec((T_BLK, ZAB_W), lambda b, j: (b * steps + j, 0)),
                  pl.BlockSpec(convw.shape, const2),
                  pl.BlockSpec(par.shape, const2),
                  pl.BlockSpec(eab.shape, const2),
                  pl.BlockSpec(gm.shape, lambda b, j: (0, 0, 0)),
                  pl.BlockSpec(bd.shape, const2),
                  pl.BlockSpec(tril.shape, const2),
                  pl.BlockSpec(hm.shape, const2)],
        out_specs=pl.BlockSpec((T_BLK, GDN_WIDTH), lambda b, j: (b * steps + j, 0)),
        out_shape=jax.ShapeDtypeStruct((n, GDN_WIDTH), BF16),
        scratch_shapes=[pltpu.VMEM((SUBLANES + T_BLK, 3 * GDN_WIDTH), F32),
                        pltpu.VMEM((GDN_WIDTH, GDN_WIDTH), F32),
                        pltpu.VMEM((T_BLK, GDN_WIDTH), F32)],
        compiler_params=pltpu.CompilerParams(dimension_semantics=("arbitrary", "arbitrary"),
                                             vmem_limit_bytes=VMEM_LIMIT),
        name="gdn",
    )(zg, zab, convw, par, eab, gm, bd, tril, hm)


def _outproj_kernel(ohg_ref, oret_ref, ogdn_ref, x_ref, mod_ref, g_ref, w_ref, rw_ref, rb_ref,
                    sl_ref, x1_ref, hn_ref, ri_ref, gates_ref, cnt_ref, run_ref):
    @pl.when(pl.program_id(0) == 0)
    def _():
        run_ref[...] = jnp.zeros_like(run_ref)

    y = (_dot(ohg_ref[...], w_ref[0:HG_WIDTH, :])
         + _dot(oret_ref[...], w_ref[HG_WIDTH:HG_WIDTH + RET_WIDTH, :])
         + _dot(ogdn_ref[...], w_ref[HG_WIDTH + RET_WIDTH:, :]))
    x1 = x_ref[...] + mod_ref[0, 0:1, :] * y
    x1_ref[...] = x1
    n = x1 * lax.rsqrt(jnp.mean(x1 * x1, axis=-1, keepdims=True) + NORM_EPS)
    hn = (n * g_ref[0:1, :]) * (1.0 + mod_ref[0, 1:2, :]) + mod_ref[0, 2:3, :]
    hn_ref[...] = _bf(hn)

    lane = lax.broadcasted_iota(jnp.int32, (TM, LANES), 1)
    logits = jnp.dot(hn, rw_ref[...], preferred_element_type=F32,
                     precision=lax.Precision.HIGHEST) + rb_ref[0:1, :]
    work = jnp.where(lane < N_EXPERTS, logits, -jnp.inf)
    vals, idxs = [], []
    multihot = jnp.zeros((TM, LANES), F32)
    for _ in range(TOP_K):
        mx = jnp.max(work, axis=-1, keepdims=True)
        ix = jnp.min(jnp.where(work == mx, lane, LANES), axis=-1, keepdims=True)
        sel = lane == ix
        multihot = jnp.where(sel, 1.0, multihot)
        work = jnp.where(sel, -jnp.inf, work)
        vals.append(mx)
        idxs.append(ix)
    ex = [jnp.exp(vv - vals[0]) for vv in vals]
    den = ex[0] + ex[1] + ex[2] + ex[3]

    before = _dot(sl_ref[...], _bf(multihot)) + run_ref[0:1, :]
    run_ref[0:1, :] = run_ref[0:1, :] + jnp.sum(multihot, axis=0, keepdims=True)
    cnt_ref[...] = jnp.broadcast_to(run_ref[0:1, :], cnt_ref.shape)

    lane8 = lax.broadcasted_iota(jnp.int32, (TM, 2 * TOP_K), 1)
    ri = jnp.zeros((TM, 2 * TOP_K), jnp.int32)
    gt = jnp.zeros((TM, 2 * TOP_K), F32)
    for kk in range(TOP_K):
        rank = jnp.sum(jnp.where(lane == idxs[kk], before, 0.0), axis=-1, keepdims=True)
        ri = jnp.where(lane8 == kk, idxs[kk], ri)
        ri = jnp.where(lane8 == TOP_K + kk, rank.astype(jnp.int32), ri)
        gt = jnp.where(lane8 == kk, ex[kk] / den, gt)
    ri_ref[...] = ri
    gates_ref[...] = gt


def _outproj(ohg, oret, ogdn, x2d, modv, gpar, w_bf, rw, rb, sl, tiles_per_batch):
    n = x2d.shape[0]
    row = lambda i: (i, 0)
    const = lambda i: (0, 0)
    return pl.pallas_call(
        _outproj_kernel,
        grid=(n // TM,),
        in_specs=[pl.BlockSpec((TM, HG_WIDTH), row), pl.BlockSpec((TM, RET_WIDTH), row),
                  pl.BlockSpec((TM, GDN_WIDTH), row), pl.BlockSpec((TM, D_MODEL), row),
                  pl.BlockSpec((1, SUBLANES, D_MODEL), lambda i: (i // tiles_per_batch, 0, 0)),
                  pl.BlockSpec((SUBLANES, D_MODEL), const),
                  pl.BlockSpec((D_MODEL, D_MODEL), const),
                  pl.BlockSpec((D_MODEL, LANES), const),
                  pl.BlockSpec((SUBLANES, LANES), const),
                  pl.BlockSpec((TM, TM), const)],
        out_specs=[pl.BlockSpec((TM, D_MODEL), row), pl.BlockSpec((TM, D_MODEL), row),
                   pl.BlockSpec((TM, 2 * TOP_K), row), pl.BlockSpec((TM, 2 * TOP_K), row),
                   pl.BlockSpec((SUBLANES, LANES), const)],
        out_shape=[jax.ShapeDtypeStruct((n, D_MODEL), F32), jax.ShapeDtypeStruct((n, D_MODEL), BF16),
                   jax.ShapeDtypeStruct((n, 2 * TOP_K), jnp.int32),
                   jax.ShapeDtypeStruct((n, 2 * TOP_K), F32),
                   jax.ShapeDtypeStruct((SUBLANES, LANES), F32)],
        scratch_shapes=[pltpu.VMEM((SUBLANES, LANES), F32)],
        compiler_params=pltpu.CompilerParams(dimension_semantics=("arbitrary",),
                                             vmem_limit_bytes=VMEM_LIMIT),
        name="outproj_router",
    )(ohg, oret, ogdn, x2d, modv, gpar, w_bf, rw, rb, sl)


def _expert_kernel(be_ref, nu_ref, x_ref, w1_ref, b1_ref, w2_ref, b2_ref, y_ref, w1b_ref, w2b_ref):
    i = pl.program_id(0)
    prev = be_ref[jnp.maximum(i - 1, 0)]
    fresh = jnp.logical_or(i == 0, be_ref[i] != prev)

    @pl.when(fresh)
    def _():
        w1b_ref[...] = _bf(w1_ref[0])
        w2b_ref[...] = _bf(w2_ref[0])

    @pl.when(i < nu_ref[0])
    def _():
        hid = _dot(x_ref[...], w1b_ref[...]) + b1_ref[0]
        x_glu = jnp.minimum(hid[:, 0:D_FF], SWIGLU_LIMIT)
        x_lin = jnp.clip(hid[:, D_FF:], -SWIGLU_LIMIT, SWIGLU_LIMIT)
        act = x_glu * _sigmoid(SWIGLU_ALPHA * x_glu) * (x_lin + 1.0)
        y_ref[...] = _dot(_bf(act), w2b_ref[...]) + b2_ref[0]


def _experts(block_e, n_used, xs, w1, b1, w2, b2):
    p = xs.shape[0]
    n_blocks = p // R_BLK
    rowmap = lambda i, be, nu: (jnp.minimum(i, nu[0] - 1), 0)
    emap = lambda i, be, nu: (be[i], 0, 0)
    grid_spec = pltpu.PrefetchScalarGridSpec(
        num_scalar_prefetch=2,
        grid=(n_blocks,),
        in_specs=[pl.BlockSpec((R_BLK, D_MODEL), rowmap),
                  pl.BlockSpec((1, D_MODEL, 2 * D_FF), emap),
                  pl.BlockSpec((1, 1, 2 * D_FF), emap),
                  pl.BlockSpec((1, D_FF, D_MODEL), emap),
                  pl.BlockSpec((1, 1, D_MODEL), emap)],
        out_specs=pl.BlockSpec((R_BLK, D_MODEL), rowmap),
        scratch_shapes=[pltpu.VMEM((D_MODEL, 2 * D_FF), BF16), pltpu.VMEM((D_FF, D_MODEL), BF16)],
    )
    return pl.pallas_call(
        _expert_kernel,
        grid_spec=grid_spec,
        out_shape=jax.ShapeDtypeStruct((p, D_MODEL), F32),
        compiler_params=pltpu.CompilerParams(dimension_semantics=("arbitrary",),
                                             vmem_limit_bytes=VMEM_LIMIT),
        name="experts",
    )(block_e, n_used, xs, w1, b1.reshape(N_EXPERTS, 1, 2 * D_FF), w2, b2.reshape(N_EXPERTS, 1, D_MODEL))


def _combine_kernel(x_ref, y0_ref, y1_ref, y2_ref, y3_ref, gates_ref, mod_ref, g_ref, o_ref, *, final):
    gt = gates_ref[...]
    acc = (gt[:, 0:1] * y0_ref[0] + gt[:, 1:2] * y1_ref[0]
           + gt[:, 2:3] * y2_ref[0] + gt[:, 3:4] * y3_ref[0])
    x2 = x_ref[...] + mod_ref[0, 3:4, :] * acc
    if final:
        x2 = x2 * lax.rsqrt(jnp.mean(x2 * x2, axis=-1, keepdims=True) + NORM_EPS) * g_ref[1:2, :]
    o_ref[...] = x2


def _combine(x1, yg, gates, modv, gpar, tiles_per_batch, final):
    n = x1.shape[0]
    row = lambda i: (i, 0)
    ysp = [pl.BlockSpec((1, TM, D_MODEL), (lambda i, kk=kk: (kk, i, 0))) for kk in range(TOP_K)]
    return pl.pallas_call(
        functools.partial(_combine_kernel, final=final),
        grid=(n // TM,),
        in_specs=[pl.BlockSpec((TM, D_MODEL), row)] + ysp + [
            pl.BlockSpec((TM, 2 * TOP_K), row),
            pl.BlockSpec((1, SUBLANES, D_MODEL), lambda i: (i // tiles_per_batch, 0, 0)),
            pl.BlockSpec((SUBLANES, D_MODEL), lambda i: (0, 0))],
        out_specs=pl.BlockSpec((TM, D_MODEL), row),
        out_shape=jax.ShapeDtypeStruct((n, D_MODEL), F32),
        compiler_params=pltpu.CompilerParams(dimension_semantics=("arbitrary",),
                                             vmem_limit_bytes=VMEM_LIMIT),
        name="combine",
    )(x1, yg, yg, yg, yg, gates, modv, gpar)


def _rot_cols(w):
    d = w.shape[0]
    w4 = w.reshape(d, RET_HEADS, 2, RET_DK // 2)
    return jnp.concatenate([-w4[:, :, 1:2], w4[:, :, 0:1]], axis=2).reshape(d, RET_WIDTH)


def _widen_w_in(w):
    o = 0
    hg = w[:, o:o + 4 * HG_WIDTH]; o += 4 * HG_WIDTH
    rq = w[:, o:o + RET_WIDTH]; o += RET_WIDTH
    rk = w[:, o:o + RET_WIDTH]; o += RET_WIDTH
    rv = w[:, o:o + RET_WIDTH]; o += RET_WIDTH
    rg = w[:, o:o + RET_WIDTH]; o += RET_WIDTH
    gd = w[:, o:o + 4 * GDN_WIDTH]; o += 4 * GDN_WIDTH
    gab = w[:, o:o + 2 * GDN_HEADS]
    pad = jnp.zeros((w.shape[0], ZAB_W - 2 * GDN_HEADS), w.dtype)
    return _bf(jnp.concatenate([hg, rq, _rot_cols(rq), rk, _rot_cols(rk), rv, rg, gd, gab, pad], axis=1))


def _pad_rows(a, rows=SUBLANES):
    return jnp.concatenate([a, jnp.zeros((rows - a.shape[0],) + a.shape[1:], a.dtype)], axis=0)


def kernel(x, c, ada_w, ada_b, norm1_g, norm2_g, w_in, w_out, hg_lb_logits, hg_norm_g, ret_norm_g,
           gdn_conv_w, gdn_A_log, gdn_dt_bias, gdn_norm_g, router_w, router_b, exp_w1, exp_b1,
           exp_w2, exp_b2, final_norm_g):
    batch, seq, d = x.shape
    depth = ada_w.shape[0]
    n = batch * seq
    steps = seq // T_BLK
    tiles_per_batch = seq // TM
    nk = n * TOP_K
    n_blocks = nk // R_BLK + N_EXPERTS
    p_rows = n_blocks * R_BLK

    lv_np = _level_matrix(T_BLK)
    lv = jnp.asarray(lv_np)
    f_mat = jnp.asarray(_hgrn_exponent_matrix(T_BLK), BF16)
    gm = jnp.asarray(_gdn_masks(T_BLK))
    bd = jnp.asarray(_block_diag_mask(T_BLK, GDN_CHUNK))
    tril = jnp.asarray(_block_diag_mask(T_BLK, GDN_CHUNK) * (lv_np >= 0), BF16)
    hm = jnp.asarray(_head_masks(RET_WIDTH, RET_HEADS))
    sl = jnp.asarray(np.tril(np.ones((TM, TM), np.float32), -1), BF16)
    eab_np = np.zeros((LANES, 2 * GDN_WIDTH), np.float32)
    for h in range(GDN_HEADS):
        eab_np[h, h * GDN_DK:(h + 1) * GDN_DK] = 1.0
        eab_np[GDN_HEADS + h, GDN_WIDTH + h * GDN_DK:GDN_WIDTH + (h + 1) * GDN_DK] = 1.0
    eab = jnp.asarray(eab_np, BF16)

    half = RET_DK // 2
    inv = ROPE_BASE ** (-jnp.linspace(0.0, 1.0, half, dtype=F32))
    ang = jnp.arange(seq, dtype=F32)[:, None] * inv[None, :]
    cos_t = jnp.tile(jnp.cos(ang), (1, 2 * RET_HEADS))
    sin_t = jnp.tile(jnp.sin(ang), (1, 2 * RET_HEADS))
    log_g = jnp.log1p(-jnp.exp2(-5.0 - jnp.arange(RET_HEADS, dtype=F32)))
    jj = jnp.arange(T_BLK, dtype=F32)
    diff = jj[:, None] - jj[None, :]
    dmat = jnp.where(diff[None] >= 0, jnp.exp(diff[None] * log_g[:, None, None]), 0.0)
    lg_lane = jnp.repeat(log_g, RET_DK)[None, :]
    qw = jnp.exp(lg_lane * (jj[:, None] + 1.0))
    kw = jnp.exp(lg_lane * (T_BLK - 1.0 - jj[:, None]))
    cdec = jnp.exp(T_BLK * lg_lane)

    lb = jnp.cumsum(jax.nn.softmax(hg_lb_logits.astype(F32), axis=0), axis=0)
    lb = jnp.maximum(lb - lb[0], 0.0)
    c_pad = _pad_rows(c.astype(F32))
    mod = _adaln(c_pad, ada_w, ada_b)[:, :batch, :]

    x2d = x.reshape(n, d)
    out = None
    for l in range(depth):
        sh1, sc1, gt1, sh2, sc2, gt2 = [mod[l][:, i * d:(i + 1) * d] for i in range(6)]
        zeros = jnp.zeros_like(sh1)
        modv_a = jnp.stack([sc1, sh1, zeros, zeros, zeros, zeros, zeros, zeros], axis=1)
        modv_c = jnp.stack([gt1, sc2, sh2, gt2, zeros, zeros, zeros, zeros], axis=1)
        gpar_a = _pad_rows(norm1_g[l][None, :])
        gpar_c = _pad_rows(jnp.stack([norm2_g[l], final_norm_g], axis=0))

        zh, zr, zg, zab = _inproj(x2d, modv_a, gpar_a, _widen_w_in(w_in[l]), tiles_per_batch)

        hg_par = _pad_rows(jnp.stack([jnp.log(lb[l]), jnp.log1p(-lb[l]), 1.0 - lb[l],
                                      hg_norm_g[l].reshape(-1)], axis=0))
        o_hg = _hgrn(zh, f_mat, lv, hg_par, batch, steps)

        ret_par = _pad_rows(jnp.concatenate([cdec, ret_norm_g[l].reshape(1, -1)], axis=0))
        o_ret = _ret(zr, cos_t, sin_t, dmat, qw, kw, ret_par, hm, bd, batch, steps)

        gdn_par = _pad_rows(jnp.stack([jnp.repeat(-jnp.exp(gdn_A_log[l].astype(F32)), GDN_DK),
                                       jnp.repeat(gdn_dt_bias[l].astype(F32), GDN_DK),
                                       gdn_norm_g[l].reshape(-1)], axis=0))
        o_gdn = _gdn(zg, zab, _pad_rows(gdn_conv_w[l].astype(F32)), gdn_par, eab, gm, bd, tril, hm,
                     batch, steps)

        rw = jnp.concatenate([router_w[l], jnp.zeros((d, LANES - N_EXPERTS), F32)], axis=1)
        rb = _pad_rows(jnp.concatenate([router_b[l], jnp.zeros((LANES - N_EXPERTS,), F32)])[None, :])
        x1, hn2, ri, gates, cnt = _outproj(o_hg, o_ret, o_gdn, x2d, modv_c, gpar_c, _bf(w_out[l]),
                                           rw, rb, sl, tiles_per_batch)

        counts = cnt[0, :N_EXPERTS].astype(jnp.int32)
        padded = (counts + R_BLK - 1) // R_BLK * R_BLK
        pend = jnp.cumsum(padded)
        pstart = pend - padded
        n_used = (pend[-1] // R_BLK).astype(jnp.int32)
        blk_start = jnp.arange(n_blocks, dtype=jnp.int32) * R_BLK
        block_e = jnp.minimum(jnp.searchsorted(pend, blk_start, side='right'), N_EXPERTS - 1)
        last_e = block_e[jnp.maximum(n_used - 1, 0)]
        block_e = jnp.where(jnp.arange(n_blocks) < n_used, block_e, last_e).astype(jnp.int32)
        pos = pstart[ri[:, :TOP_K]] + ri[:, TOP_K:]

        tok = jnp.broadcast_to(jnp.arange(n, dtype=jnp.int32)[:, None], (n, TOP_K))
        buf_tok = jnp.zeros((p_rows,), jnp.int32).at[pos.reshape(-1)].set(tok.reshape(-1))
        xs = jnp.take(hn2, buf_tok, axis=0)
        y = _experts(block_e, n_used.reshape(1), xs, exp_w1[l], exp_b1[l], exp_w2[l], exp_b2[l])
        yg = jnp.take(y, pos.T, axis=0)
        x2d = _combine(x1, yg, gates, modv_c, gpar_c, tiles_per_batch, final=(l == depth - 1))
    return x2d.reshape(batch, seq, d)
```

```python
import functools
import math

import numpy as np
import jax
import jax.numpy as jnp
from jax import lax
from jax.experimental import pallas as pl
from jax.experimental.pallas import tpu as pltpu

F32 = jnp.float32
BF16 = jnp.bfloat16

D_MODEL = 1024
HG_HEADS, HG_DK = 4, 128
HG_WIDTH = HG_HEADS * HG_DK
RET_HEADS, RET_DK = 4, 64
RET_WIDTH = RET_HEADS * RET_DK
GDN_HEADS, GDN_DK = 4, 64
GDN_WIDTH = GDN_HEADS * GDN_DK
CONV_K = 4
ROPE_BASE = 10000.0
N_EXPERTS = 32
TOP_K = 4
D_FF = D_MODEL
SWIGLU_ALPHA = 1.702
SWIGLU_LIMIT = 7.0
NORM_EPS = 1e-6
L2_EPS = 1e-6
GDN_CHUNK = 64

LANES = 128
SUBLANES = 8
VMEM_LIMIT = 56 * 1024 * 1024

T_BLK = 256
TM = 256
R_BLK = 512
N_LEVELS = 8

ZH_W = 4 * HG_WIDTH
ZR_W = 6 * RET_WIDTH
ZG_W = 4 * GDN_WIDTH
ZAB_W = LANES
Z_W = ZH_W + ZR_W + ZG_W + ZAB_W


def _dot(a, b):
    return jnp.dot(a, b, preferred_element_type=F32)


def _dot_nt(a, b):
    return lax.dot_general(a, b, (((1,), (1,)), ((), ())), preferred_element_type=F32)


def _dot_tn(a, b):
    return lax.dot_general(a, b, (((0,), (0,)), ((), ())), preferred_element_type=F32)


def _split2(x):
    hi = x.astype(BF16)
    return hi, (x - hi.astype(F32)).astype(BF16)


def _dot2_lhs01(c, x):
    hi, lo = _split2(x)
    return _dot(c, hi) + _dot(c, lo)


def _dot2_rhs01(x, c):
    hi, lo = _split2(x)
    return _dot(hi, c) + _dot(lo, c)


def _sigmoid(x):
    return 1.0 / (1.0 + jnp.exp(-x))


def _silu(x):
    return x * _sigmoid(x)


def _softplus(x):
    return jnp.maximum(x, 0.0) + jnp.log1p(jnp.exp(-jnp.abs(x)))


def _bf(x):
    return x.astype(BF16)


def _level_matrix(t):
    i = np.arange(t)[:, None]
    j = np.arange(t)[None, :]
    x = i ^ j
    lv = np.floor(np.log2(np.maximum(x, 1))).astype(np.int32)
    lv = np.where(i == j, int(math.log2(t)), lv)
    lv = np.where(i < j, -1, lv)
    return lv.astype(np.int32)


def _hgrn_exponent_matrix(t):
    n_lev = int(math.log2(t))
    f = np.zeros((2 + n_lev, t, t), np.float32)
    u = np.arange(t)[None, :]
    r = np.arange(t)[:, None]
    f[0] = (u <= r)
    f[1] = (u > r)
    for l in range(n_lev):
        h = 1 << l
        base = (r // (2 * h)) * (2 * h)
        mid = base + h
        upper = (r - base) >= h
        f[2 + l] = np.where(upper, (u >= mid) & (u <= r), (u > r) & (u < mid))
    return f.reshape((2 + n_lev) * t, t)


def _gdn_masks(t):
    lv = _level_matrix(t)
    top = int(math.log2(GDN_CHUNK)) - 1
    strict = (lv >= 0) & (lv <= top)
    incl = strict | (lv == int(math.log2(t)))
    d8 = (lv >= 0) & (lv <= 2)
    merges = [(lv == l) for l in range(3, top + 1)]
    return np.stack([strict, incl, d8] + merges).astype(np.float32)


def _block_diag_mask(t, blk):
    i = np.arange(t)
    return (i[:, None] // blk == i[None, :] // blk).astype(np.float32)


def _head_masks(width, heads):
    lane = np.arange(width)[None, :]
    m = np.zeros((SUBLANES, width), np.float32)
    for h in range(heads):
        m[h] = (lane // (width // heads) == h)[0]
    return m


def _adaln_kernel(c_ref, w_ref, b_ref, o_ref):
    cond = _silu(c_ref[...])
    o_ref[0] = jnp.dot(cond, w_ref[0], preferred_element_type=F32,
                       precision=lax.Precision.HIGHEST) + b_ref[0]


def _adaln(c_pad, ada_w, ada_b):
    depth, d, n6 = ada_w.shape
    tn = n6 // 4
    return pl.pallas_call(
        _adaln_kernel,
        grid=(depth, n6 // tn),
        in_specs=[pl.BlockSpec((SUBLANES, d), lambda l, j: (0, 0)),
                  pl.BlockSpec((1, d, tn), lambda l, j: (l, 0, j)),
                  pl.BlockSpec((1, 1, tn), lambda l, j: (l, 0, j))],
        out_specs=pl.BlockSpec((1, SUBLANES, tn), lambda l, j: (l, 0, j)),
        out_shape=jax.ShapeDtypeStruct((depth, SUBLANES, n6), F32),
        compiler_params=pltpu.CompilerParams(vmem_limit_bytes=VMEM_LIMIT),
        name="adaln",
    )(c_pad, ada_w, ada_b.reshape(depth, 1, n6))


def _inproj_kernel(x_ref, mod_ref, g_ref, w_ref, zh_ref, zr_ref, zg_ref, zab_ref):
    x = x_ref[...]
    y = x * lax.rsqrt(jnp.mean(x * x, axis=-1, keepdims=True) + NORM_EPS)
    hn = (y * g_ref[0:1, :]) * (1.0 + mod_ref[0, 0:1, :]) + mod_ref[0, 1:2, :]
    hb = _bf(hn)
    zh_ref[...] = _dot(hb, w_ref[:, 0:ZH_W])
    zr_ref[...] = _dot(hb, w_ref[:, ZH_W:ZH_W + ZR_W])
    zg_ref[...] = _dot(hb, w_ref[:, ZH_W + ZR_W:ZH_W + ZR_W + ZG_W])
    zab_ref[...] = _dot(hb, w_ref[:, ZH_W + ZR_W + ZG_W:Z_W])


def _inproj(x2d, modv, gpar, w_bf, tiles_per_batch):
    n = x2d.shape[0]
    row = lambda i: (i, 0)
    return pl.pallas_call(
        _inproj_kernel,
        grid=(n // TM,),
        in_specs=[pl.BlockSpec((TM, D_MODEL), row),
                  pl.BlockSpec((1, SUBLANES, D_MODEL), lambda i: (i // tiles_per_batch, 0, 0)),
                  pl.BlockSpec((SUBLANES, D_MODEL), lambda i: (0, 0)),
                  pl.BlockSpec((D_MODEL, Z_W), lambda i: (0, 0))],
        out_specs=[pl.BlockSpec((TM, ZH_W), row), pl.BlockSpec((TM, ZR_W), row),
                   pl.BlockSpec((TM, ZG_W), row), pl.BlockSpec((TM, ZAB_W), row)],
        out_shape=[jax.ShapeDtypeStruct((n, ZH_W), F32), jax.ShapeDtypeStruct((n, ZR_W), F32),
                   jax.ShapeDtypeStruct((n, ZG_W), F32), jax.ShapeDtypeStruct((n, ZAB_W), F32)],
        compiler_params=pltpu.CompilerParams(dimension_semantics=("arbitrary",),
                                             vmem_limit_bytes=VMEM_LIMIT),
        name="inproj",
    )(x2d, modv, gpar, w_bf)


def _hgrn_kernel(zh_ref, f_ref, lv_ref, par_ref, o_ref, state_ref, ex_ref):
    @pl.when(pl.program_id(1) == 0)
    def _():
        state_ref[...] = jnp.zeros_like(state_ref)

    W = HG_WIDTH
    hq = zh_ref[:, 0:W]
    hf = zh_ref[:, W:2 * W]
    loglb = par_ref[0:1, :]
    log1mlb = par_ref[1:2, :]
    onemlb = par_ref[2:3, :]

    q = _silu(hq)
    e = jnp.exp(-jnp.abs(hf))
    inv = 1.0 / (1.0 + e)
    k = onemlb * (jnp.where(hf >= 0, e, 1.0) * inv)
    logsig = jnp.minimum(hf, 0.0) - jnp.log1p(e)
    c = log1mlb + logsig
    lf = jnp.maximum(loglb, c) + jnp.log1p(jnp.exp(-jnp.abs(loglb - c)))

    ex_ref[...] = jnp.exp(_dot2_lhs01(f_ref[...], lf))

    lv = lv_ref[...]
    t = T_BLK
    for h in range(HG_HEADS):
        cs = slice(h * HG_DK, (h + 1) * HG_DK)
        qh = q[:, cs]
        kh = k[:, cs]
        vh = _bf(zh_ref[:, 2 * W + h * HG_DK:2 * W + (h + 1) * HG_DK])
        s = jnp.where(lv == N_LEVELS, _dot_nt(_bf(qh), _bf(kh)), 0.0)
        for l in range(N_LEVELS):
            el = ex_ref[(2 + l) * t:(3 + l) * t, cs]
            s = jnp.where(lv == l, _dot_nt(_bf(qh * el), _bf(kh * el)), s)
        eb = ex_ref[0:t, cs]
        ebl = ex_ref[t:2 * t, cs]
        st = state_ref[h]
        o = _dot(_bf(s), vh) + _dot_nt(_bf(qh * eb), _bf(st))
        upd = _dot_tn(vh, _bf(kh * ebl))
        state_ref[h] = st * ex_ref[t - 1:t, cs] + upd
        ms = jnp.mean(o * o, axis=-1, keepdims=True)
        gate = _silu(zh_ref[:, 3 * W + h * HG_DK:3 * W + (h + 1) * HG_DK])
        o_ref[:, cs] = _bf(o * lax.rsqrt(ms + NORM_EPS) * par_ref[3:4, cs] * gate)


def _hgrn(zh, f_mat, lv, par, batch, steps):
    n = zh.shape[0]
    const = lambda b, j: (0, 0)
    return pl.pallas_call(
        _hgrn_kernel,
        grid=(batch, steps),
        in_specs=[pl.BlockSpec((T_BLK, ZH_W), lambda b, j: (b * steps + j, 0)),
                  pl.BlockSpec(f_mat.shape, const),
                  pl.BlockSpec(lv.shape, const),
                  pl.BlockSpec(par.shape, const)],
        out_specs=pl.BlockSpec((T_BLK, HG_WIDTH), lambda b, j: (b * steps + j, 0)),
        out_shape=jax.ShapeDtypeStruct((n, HG_WIDTH), BF16),
        scratch_shapes=[pltpu.VMEM((HG_HEADS, HG_DK, HG_DK), F32),
                        pltpu.VMEM(((2 + N_LEVELS) * T_BLK, HG_WIDTH), F32)],
        compiler_params=pltpu.CompilerParams(dimension_semantics=("arbitrary", "arbitrary"),
                                             vmem_limit_bytes=VMEM_LIMIT),
        name="hgrn2",
    )(zh, f_mat, lv, par)


def _ret_kernel(zr_ref, cos_ref, sin_ref, dmat_ref, qw_ref, kw_ref, par_ref, hm_ref, bd_ref,
                o_ref, state_ref):
    @pl.when(pl.program_id(1) == 0)
    def _():
        state_ref[...] = jnp.zeros_like(state_ref)

    W = RET_WIDTH
    cos = cos_ref[...]
    sin = sin_ref[...]
    q = zr_ref[:, 0:W] * cos + zr_ref[:, W:2 * W] * sin
    k = (zr_ref[:, 2 * W:3 * W] * cos + zr_ref[:, 3 * W:4 * W] * sin) * (RET_DK ** -0.5)
    v = zr_ref[:, 4 * W:5 * W]
    kb = _bf(k)
    bd = bd_ref[...]

    s_parts = []
    v_parts = []
    for h in range(RET_HEADS):
        hm = hm_ref[h:h + 1, :]
        s_parts.append(_bf(_dot_nt(_bf(q * hm), kb) * dmat_ref[h]))
        v_parts.append(_bf(v * hm))
    st = state_ref[...]
    o = (_dot(jnp.concatenate(s_parts, axis=1), jnp.concatenate(v_parts, axis=0))
         + _dot(_bf(q * qw_ref[...]), _bf(st)))
    kv = _dot_tn(_bf(k * kw_ref[...]), _bf(v))
    state_ref[...] = st * par_ref[0:1, :] + bd * kv

    ms = _dot2_rhs01(o * o, _bf(bd)) * (1.0 / RET_DK)
    gate = _silu(zr_ref[:, 5 * W:6 * W])
    o_ref[...] = _bf(o * lax.rsqrt(ms + NORM_EPS) * par_ref[1:2, :] * gate)


def _ret(zr, cos_t, sin_t, dmat, qw, kw, par, hm, bd, batch, steps):
    n = zr.shape[0]
    const2 = lambda b, j: (0, 0)
    return pl.pallas_call(
        _ret_kernel,
        grid=(batch, steps),
        in_specs=[pl.BlockSpec((T_BLK, ZR_W), lambda b, j: (b * steps + j, 0)),
                  pl.BlockSpec((T_BLK, RET_WIDTH), lambda b, j: (j, 0)),
                  pl.BlockSpec((T_BLK, RET_WIDTH), lambda b, j: (j, 0)),
                  pl.BlockSpec(dmat.shape, lambda b, j: (0, 0, 0)),
                  pl.BlockSpec(qw.shape, const2),
                  pl.BlockSpec(kw.shape, const2),
                  pl.BlockSpec(par.shape, const2),
                  pl.BlockSpec(hm.shape, const2),
                  pl.BlockSpec(bd.shape, const2)],
        out_specs=pl.BlockSpec((T_BLK, RET_WIDTH), lambda b, j: (b * steps + j, 0)),
        out_shape=jax.ShapeDtypeStruct((n, RET_WIDTH), BF16),
        scratch_shapes=[pltpu.VMEM((RET_WIDTH, RET_WIDTH), F32)],
        compiler_params=pltpu.CompilerParams(dimension_semantics=("arbitrary", "arbitrary"),
                                             vmem_limit_bytes=VMEM_LIMIT),
        name="retention",
    )(zr, cos_t, sin_t, dmat, qw, kw, par, hm, bd)


def _gdn_kernel(zg_ref, zab_ref, convw_ref, par_ref, eab_ref, gm_ref, bd_ref, tril_ref, hm_ref,
                o_ref, ext_ref, state_ref, obuf_ref):
    @pl.when(pl.program_id(1) == 0)
    def _():
        ext_ref[0:SUBLANES, :] = jnp.zeros((SUBLANES, 3 * GDN_WIDTH), F32)
        state_ref[...] = jnp.zeros_like(state_ref)

    W = GDN_WIDTH
    t = T_BLK
    u = zg_ref[:, 0:3 * W]
    ext_ref[SUBLANES:SUBLANES + t, :] = u
    conv = (convw_ref[3:4, :] * u
            + convw_ref[2:3, :] * ext_ref[SUBLANES - 1:SUBLANES - 1 + t, :]
            + convw_ref[1:2, :] * ext_ref[SUBLANES - 2:SUBLANES - 2 + t, :]
            + convw_ref[0:1, :] * ext_ref[SUBLANES - 3:SUBLANES - 3 + t, :])
    ext_ref[0:SUBLANES, :] = u[t - SUBLANES:t, :]
    qkv = _silu(conv)
    q = qkv[:, 0:W]
    k = qkv[:, W:2 * W]
    v = qkv[:, 2 * W:3 * W]

    bd = bd_ref[...]
    bdb = _bf(bd)
    qn = q * lax.rsqrt(_dot2_rhs01(q * q, bdb) + L2_EPS) * (GDN_DK ** -0.5)
    kn = k * lax.rsqrt(_dot2_rhs01(k * k, bdb) + L2_EPS)

    ab = zab_ref[...]
    a_exp = _dot2_rhs01(ab, eab_ref[:, 0:W])
    b_exp = _dot2_rhs01(ab, eab_ref[:, W:2 * W])
    beta = _sigmoid(b_exp)
    g = par_ref[0:1, :] * _softplus(a_exp + par_ref[1:2, :])
    gc = _dot2_lhs01(tril_ref[...], g)
    gl = _dot2_lhs01(bdb, g)
    eg = jnp.exp(gc)
    vb = v * beta
    kbeta = kn * beta * eg
    qdec = qn * eg
    kdec = kn * jnp.exp(gl - gc)

    incl = gm_ref[1]
    knb = _bf(kn)
    knbeta = kn * beta
    x_parts, qk_parts, hms = [], [], []
    for h in range(GDN_HEADS):
        hm = hm_ref[h:h + 1, :]
        hms.append(hm)
        cb = jnp.broadcast_to(gc[:, h * GDN_DK:h * GDN_DK + 1], (t, t))
        rel = jnp.exp(jnp.where(incl > 0, cb - cb.T, -jnp.inf))
        m = _dot_nt(_bf(knbeta * hm), knb) * rel
        d = _bf(m * gm_ref[2])
        d2 = _dot(d, d)
        d2b = _bf(d2)
        d4 = _dot(d2b, d2b)
        x = d2 - d.astype(F32) - _dot(d, d2b)
        x = x + d4 + _dot(_bf(x), _bf(d4))
        for lvl in range(3, gm_ref.shape[0]):
            lo = _bf(m * gm_ref[lvl])
            xb = _bf(x)
            y = lo.astype(F32) + _dot(xb, lo)
            x = x - (y + _dot(_bf(y), xb))
        x_parts.append(_bf(x))
        qk_parts.append(_bf(_dot_nt(_bf(qn * hm), knb) * rel))
    xcat = jnp.concatenate(x_parts, axis=1)
    qkcat = jnp.concatenate(qk_parts, axis=1)

    def stack(a):
        return jnp.concatenate([_bf(a * hm) for hm in hms], axis=0)

    w = vb + _dot(xcat, stack(vb))
    kcum = kbeta + _dot(xcat, stack(kbeta))
    a1 = _dot(qkcat, stack(w))
    qeff = qdec - _dot(qkcat, stack(kcum))

    for c in range(t // GDN_CHUNK):
        rows = slice(c * GDN_CHUNK, (c + 1) * GDN_CHUNK)
        st = state_ref[...]
        stb = _bf(st)
        vnew = w[rows] - _dot(_bf(kcum[rows]), stb)
        obuf_ref[rows, :] = _dot(_bf(qeff[rows]), stb) + a1[rows]
        upd = _dot_tn(_bf(kdec[rows]), _bf(vnew))
        last = eg[(c + 1) * GDN_CHUNK - 1:(c + 1) * GDN_CHUNK, :]
        state_ref[...] = st * last + bd * upd

    o = obuf_ref[...]
    ms = _dot2_rhs01(o * o, bdb) * (1.0 / GDN_DK)
    gate = _silu(zg_ref[:, 3 * W:4 * W])
    o_ref[...] = _bf(o * lax.rsqrt(ms + NORM_EPS) * par_ref[2:3, :] * gate)


def _gdn(zg, zab, convw, par, eab, gm, bd, tril, hm, batch, steps):
    n = zg.shape[0]
    const2 = lambda b, j: (0, 0)
    return pl.pallas_call(
        _gdn_kernel,
        grid=(batch, steps),
        in_specs=[pl.BlockSpec((T_BLK, ZG_W), lambda b, j: (b * steps + j, 0)),
                  pl.BlockSpec((T_BLK, ZAB_W), lambda b, j: (b * steps + j, 0)),
                  pl.BlockSpec(convw.shape, const2),
                  pl.BlockSpec(par.shape, const2),
                  pl.BlockSpec(eab.shape, const2),
                  pl.BlockSpec(gm.shape, lambda b, j: (0, 0, 0)),
                  pl.BlockSpec(bd.shape, const2),
                  pl.BlockSpec(tril.shape, const2),
                  pl.BlockSpec(hm.shape, const2)],
        out_specs=pl.BlockSpec((T_BLK, GDN_WIDTH), lambda b, j: (b * steps + j, 0)),
        out_shape=jax.ShapeDtypeStruct((n, GDN_WIDTH), BF16),
        scratch_shapes=[pltpu.VMEM((SUBLANES + T_BLK, 3 * GDN_WIDTH), F32),
                        pltpu.VMEM((GDN_WIDTH, GDN_WIDTH), F32),
                        pltpu.VMEM((T_BLK, GDN_WIDTH), F32)],
        compiler_params=pltpu.CompilerParams(dimension_semantics=("arbitrary", "arbitrary"),
                                             vmem_limit_bytes=VMEM_LIMIT),
        name="gdn",
    )(zg, zab, convw, par, eab, gm, bd, tril, hm)


def _outproj_kernel(ohg_ref, oret_ref, ogdn_ref, x_ref, mod_ref, g_ref, w_ref, rw_ref, rb_ref,
                    sl_ref, x1_ref, hn_ref, ri_ref, gates_ref, cnt_ref, run_ref):
    @pl.when(pl.program_id(0) == 0)
    def _():
        run_ref[...] = jnp.zeros_like(run_ref)

    y = (_dot(ohg_ref[...], w_ref[0:HG_WIDTH, :])
         + _dot(oret_ref[...], w_ref[HG_WIDTH:HG_WIDTH + RET_WIDTH, :])
         + _dot(ogdn_ref[...], w_ref[HG_WIDTH + RET_WIDTH:, :]))
    x1 = x_ref[...] + mod_ref[0, 0:1, :] * y
    x1_ref[...] = x1
    n = x1 * lax.rsqrt(jnp.mean(x1 * x1, axis=-1, keepdims=True) + NORM_EPS)
    hn = (n * g_ref[0:1, :]) * (1.0 + mod_ref[0, 1:2, :]) + mod_ref[0, 2:3, :]
    hn_ref[...] = _bf(hn)

    lane = lax.broadcasted_iota(jnp.int32, (TM, LANES), 1)
    hn_hi, hn_lo = _split2(hn)
    rw_hi = rw_ref[:, 0:LANES]
    logits = (_dot(hn_hi, rw_hi) + _dot(hn_hi, rw_ref[:, LANES:2 * LANES]) + _dot(hn_lo, rw_hi)
              + rb_ref[0:1, :])
    work = jnp.where(lane < N_EXPERTS, logits, -jnp.inf)
    vals, idxs = [], []
    multihot = jnp.zeros((TM, LANES), F32)
    for _ in range(TOP_K):
        mx = jnp.max(work, axis=-1, keepdims=True)
        ix = jnp.min(jnp.where(work == mx, lane, LANES), axis=-1, keepdims=True)
        sel = lane == ix
        multihot = jnp.where(sel, 1.0, multihot)
        work = jnp.where(sel, -jnp.inf, work)
        vals.append(mx)
        idxs.append(ix)
    ex = [jnp.exp(vv - vals[0]) for vv in vals]
    den = ex[0] + ex[1] + ex[2] + ex[3]

    before = _dot(sl_ref[...], _bf(multihot)) + run_ref[0:1, :]
    run_ref[0:1, :] = run_ref[0:1, :] + jnp.sum(multihot, axis=0, keepdims=True)
    cnt_ref[...] = jnp.broadcast_to(run_ref[0:1, :], cnt_ref.shape)

    lane8 = lax.broadcasted_iota(jnp.int32, (TM, 2 * TOP_K), 1)
    ri = jnp.zeros((TM, 2 * TOP_K), jnp.int32)
    gt = jnp.zeros((TM, 2 * TOP_K), F32)
    for kk in range(TOP_K):
        rank = jnp.sum(jnp.where(lane == idxs[kk], before, 0.0), axis=-1, keepdims=True)
        ri = jnp.where(lane8 == kk, idxs[kk], ri)
        ri = jnp.where(lane8 == TOP_K + kk, rank.astype(jnp.int32), ri)
        gt = jnp.where(lane8 == kk, ex[kk] / den, gt)
    ri_ref[...] = ri
    gates_ref[...] = gt


def _outproj(ohg, oret, ogdn, x2d, modv, gpar, w_bf, rw, rb, sl, tiles_per_batch):
    n = x2d.shape[0]
    row = lambda i: (i, 0)
    const = lambda i: (0, 0)
    return pl.pallas_call(
        _outproj_kernel,
        grid=(n // TM,),
        in_specs=[pl.BlockSpec((TM, HG_WIDTH), row), pl.BlockSpec((TM, RET_WIDTH), row),
                  pl.BlockSpec((TM, GDN_WIDTH), row), pl.BlockSpec((TM, D_MODEL), row),
                  pl.BlockSpec((1, SUBLANES, D_MODEL), lambda i: (i // tiles_per_batch, 0, 0)),
                  pl.BlockSpec((SUBLANES, D_MODEL), const),
                  pl.BlockSpec((D_MODEL, D_MODEL), const),
                  pl.BlockSpec((D_MODEL, 2 * LANES), const),
                  pl.BlockSpec((SUBLANES, LANES), const),
                  pl.BlockSpec((TM, TM), const)],
        out_specs=[pl.BlockSpec((TM, D_MODEL), row), pl.BlockSpec((TM, D_MODEL), row),
                   pl.BlockSpec((TM, 2 * TOP_K), row), pl.BlockSpec((TM, 2 * TOP_K), row),
                   pl.BlockSpec((SUBLANES, LANES), const)],
        out_shape=[jax.ShapeDtypeStruct((n, D_MODEL), F32), jax.ShapeDtypeStruct((n, D_MODEL), BF16),
                   jax.ShapeDtypeStruct((n, 2 * TOP_K), jnp.int32),
                   jax.ShapeDtypeStruct((n, 2 * TOP_K), F32),
                   jax.ShapeDtypeStruct((SUBLANES, LANES), F32)],
        scratch_shapes=[pltpu.VMEM((SUBLANES, LANES), F32)],
        compiler_params=pltpu.CompilerParams(dimension_semantics=("arbitrary",),
                                             vmem_limit_bytes=VMEM_LIMIT),
        name="outproj_router",
    )(ohg, oret, ogdn, x2d, modv, gpar, w_bf, rw, rb, sl)


def _expert_kernel(be_ref, nu_ref, x_ref, w1_ref, b1_ref, w2_ref, b2_ref, y_ref, w1b_ref, w2b_ref):
    i = pl.program_id(0)
    prev = be_ref[jnp.maximum(i - 1, 0)]
    fresh = jnp.logical_or(i == 0, be_ref[i] != prev)

    @pl.when(fresh)
    def _():
        w1b_ref[...] = _bf(w1_ref[0])
        w2b_ref[...] = _bf(w2_ref[0])

    @pl.when(i < nu_ref[0])
    def _():
        hid = _dot(x_ref[...], w1b_ref[...]) + b1_ref[0]
        x_glu = jnp.minimum(hid[:, 0:D_FF], SWIGLU_LIMIT)
        x_lin = jnp.clip(hid[:, D_FF:], -SWIGLU_LIMIT, SWIGLU_LIMIT)
        act = x_glu * _sigmoid(SWIGLU_ALPHA * x_glu) * (x_lin + 1.0)
        y_ref[...] = _dot(_bf(act), w2b_ref[...]) + b2_ref[0]


def _experts(block_e, n_used, xs, w1, b1, w2, b2):
    p = xs.shape[0]
    n_blocks = p // R_BLK
    ne = w1.shape[0] * w1.shape[1]
    w1 = w1.reshape(ne, D_MODEL, 2 * D_FF)
    w2 = w2.reshape(ne, D_FF, D_MODEL)
    rowmap = lambda i, be, nu: (jnp.minimum(i, nu[0] - 1), 0)
    emap = lambda i, be, nu: (be[i], 0, 0)
    grid_spec = pltpu.PrefetchScalarGridSpec(
        num_scalar_prefetch=2,
        grid=(n_blocks,),
        in_specs=[pl.BlockSpec((R_BLK, D_MODEL), rowmap),
                  pl.BlockSpec((1, D_MODEL, 2 * D_FF), emap),
                  pl.BlockSpec((1, 1, 2 * D_FF), emap),
                  pl.BlockSpec((1, D_FF, D_MODEL), emap),
                  pl.BlockSpec((1, 1, D_MODEL), emap)],
        out_specs=pl.BlockSpec((R_BLK, D_MODEL), rowmap),
        scratch_shapes=[pltpu.VMEM((D_MODEL, 2 * D_FF), BF16), pltpu.VMEM((D_FF, D_MODEL), BF16)],
    )
    return pl.pallas_call(
        _expert_kernel,
        grid_spec=grid_spec,
        out_shape=jax.ShapeDtypeStruct((p, D_MODEL), F32),
        compiler_params=pltpu.CompilerParams(dimension_semantics=("arbitrary",),
                                             vmem_limit_bytes=VMEM_LIMIT),
        name="experts",
    )(block_e, n_used, xs, w1, b1.reshape(ne, 1, 2 * D_FF), w2, b2.reshape(ne, 1, D_MODEL))


def _combine_kernel(x_ref, y0_ref, y1_ref, y2_ref, y3_ref, gates_ref, mod_ref, g_ref, o_ref, *, final):
    gt = gates_ref[...]
    acc = (gt[:, 0:1] * y0_ref[0] + gt[:, 1:2] * y1_ref[0]
           + gt[:, 2:3] * y2_ref[0] + gt[:, 3:4] * y3_ref[0])
    x2 = x_ref[...] + mod_ref[0, 3:4, :] * acc
    if final:
        x2 = x2 * lax.rsqrt(jnp.mean(x2 * x2, axis=-1, keepdims=True) + NORM_EPS) * g_ref[1:2, :]
    o_ref[...] = x2


def _combine(x1, yg, gates, modv, gpar, tiles_per_batch, final):
    n = x1.shape[0]
    row = lambda i: (i, 0)
    ysp = [pl.BlockSpec((1, TM, D_MODEL), (lambda i, kk=kk: (kk, i, 0))) for kk in range(TOP_K)]
    return pl.pallas_call(
        functools.partial(_combine_kernel, final=final),
        grid=(n // TM,),
        in_specs=[pl.BlockSpec((TM, D_MODEL), row)] + ysp + [
            pl.BlockSpec((TM, 2 * TOP_K), row),
            pl.BlockSpec((1, SUBLANES, D_MODEL), lambda i: (i // tiles_per_batch, 0, 0)),
            pl.BlockSpec((SUBLANES, D_MODEL), lambda i: (0, 0))],
        out_specs=pl.BlockSpec((TM, D_MODEL), row),
        out_shape=jax.ShapeDtypeStruct((n, D_MODEL), F32),
        compiler_params=pltpu.CompilerParams(dimension_semantics=("arbitrary",),
                                             vmem_limit_bytes=VMEM_LIMIT),
        name="combine",
    )(x1, yg, yg, yg, yg, gates, modv, gpar)


def _rot_cols(w):
    d = w.shape[0]
    w4 = w.reshape(d, RET_HEADS, 2, RET_DK // 2)
    return jnp.concatenate([-w4[:, :, 1:2], w4[:, :, 0:1]], axis=2).reshape(d, RET_WIDTH)


def _widen_w_in(w):
    o = 0
    hg = w[:, o:o + 4 * HG_WIDTH]; o += 4 * HG_WIDTH
    rq = w[:, o:o + RET_WIDTH]; o += RET_WIDTH
    rk = w[:, o:o + RET_WIDTH]; o += RET_WIDTH
    rv = w[:, o:o + RET_WIDTH]; o += RET_WIDTH
    rg = w[:, o:o + RET_WIDTH]; o += RET_WIDTH
    gd = w[:, o:o + 4 * GDN_WIDTH]; o += 4 * GDN_WIDTH
    gab = w[:, o:o + 2 * GDN_HEADS]
    pad = jnp.zeros((w.shape[0], ZAB_W - 2 * GDN_HEADS), w.dtype)
    return _bf(jnp.concatenate([hg, rq, _rot_cols(rq), rk, _rot_cols(rk), rv, rg, gd, gab, pad], axis=1))


def _pad_rows(a, rows=SUBLANES):
    return jnp.concatenate([a, jnp.zeros((rows - a.shape[0],) + a.shape[1:], a.dtype)], axis=0)


def kernel(x, c, ada_w, ada_b, norm1_g, norm2_g, w_in, w_out, hg_lb_logits, hg_norm_g, ret_norm_g,
           gdn_conv_w, gdn_A_log, gdn_dt_bias, gdn_norm_g, router_w, router_b, exp_w1, exp_b1,
           exp_w2, exp_b2, final_norm_g):
    batch, seq, d = x.shape
    depth = ada_w.shape[0]
    n = batch * seq
    steps = seq // T_BLK
    tiles_per_batch = seq // TM
    nk = n * TOP_K
    n_blocks = nk // R_BLK + N_EXPERTS
    p_rows = n_blocks * R_BLK

    lv_np = _level_matrix(T_BLK)
    lv = jnp.asarray(lv_np)
    f_mat = jnp.asarray(_hgrn_exponent_matrix(T_BLK), BF16)
    gm = jnp.asarray(_gdn_masks(T_BLK))
    bd = jnp.asarray(_block_diag_mask(T_BLK, GDN_CHUNK))
    tril = jnp.asarray(_block_diag_mask(T_BLK, GDN_CHUNK) * (lv_np >= 0), BF16)
    hm = jnp.asarray(_head_masks(RET_WIDTH, RET_HEADS))
    sl = jnp.asarray(np.tril(np.ones((TM, TM), np.float32), -1), BF16)
    eab_np = np.zeros((LANES, 2 * GDN_WIDTH), np.float32)
    for h in range(GDN_HEADS):
        eab_np[h, h * GDN_DK:(h + 1) * GDN_DK] = 1.0
        eab_np[GDN_HEADS + h, GDN_WIDTH + h * GDN_DK:GDN_WIDTH + (h + 1) * GDN_DK] = 1.0
    eab = jnp.asarray(eab_np, BF16)

    half = RET_DK // 2
    inv = ROPE_BASE ** (-jnp.linspace(0.0, 1.0, half, dtype=F32))
    ang = jnp.arange(seq, dtype=F32)[:, None] * inv[None, :]
    cos_t = jnp.tile(jnp.cos(ang), (1, 2 * RET_HEADS))
    sin_t = jnp.tile(jnp.sin(ang), (1, 2 * RET_HEADS))
    log_g = jnp.log1p(-jnp.exp2(-5.0 - jnp.arange(RET_HEADS, dtype=F32)))
    jj = jnp.arange(T_BLK, dtype=F32)
    diff = jj[:, None] - jj[None, :]
    dmat = jnp.where(diff[None] >= 0, jnp.exp(diff[None] * log_g[:, None, None]), 0.0)
    lg_lane = jnp.repeat(log_g, RET_DK)[None, :]
    qw = jnp.exp(lg_lane * (jj[:, None] + 1.0))
    kw = jnp.exp(lg_lane * (T_BLK - 1.0 - jj[:, None]))
    cdec = jnp.exp(T_BLK * lg_lane)

    lb = jnp.cumsum(jax.nn.softmax(hg_lb_logits.astype(F32), axis=0), axis=0)
    lb = jnp.maximum(lb - lb[0], 0.0)
    c_pad = _pad_rows(c.astype(F32))
    mod = _adaln(c_pad, ada_w, ada_b)[:, :batch, :]

    x2d = x.reshape(n, d)
    out = None
    for l in range(depth):
        sh1, sc1, gt1, sh2, sc2, gt2 = [mod[l][:, i * d:(i + 1) * d] for i in range(6)]
        zeros = jnp.zeros_like(sh1)
        modv_a = jnp.stack([sc1, sh1, zeros, zeros, zeros, zeros, zeros, zeros], axis=1)
        modv_c = jnp.stack([gt1, sc2, sh2, gt2, zeros, zeros, zeros, zeros], axis=1)
        gpar_a = _pad_rows(norm1_g[l][None, :])
        gpar_c = _pad_rows(jnp.stack([norm2_g[l], final_norm_g], axis=0))

        zh, zr, zg, zab = _inproj(x2d, modv_a, gpar_a, _widen_w_in(w_in[l]), tiles_per_batch)

        hg_par = _pad_rows(jnp.stack([jnp.log(lb[l]), jnp.log1p(-lb[l]), 1.0 - lb[l],
                                      hg_norm_g[l].reshape(-1)], axis=0))
        o_hg = _hgrn(zh, f_mat, lv, hg_par, batch, steps)

        ret_par = _pad_rows(jnp.concatenate([cdec, ret_norm_g[l].reshape(1, -1)], axis=0))
        o_ret = _ret(zr, cos_t, sin_t, dmat, qw, kw, ret_par, hm, bd, batch, steps)

        gdn_par = _pad_rows(jnp.stack([jnp.repeat(-jnp.exp(gdn_A_log[l].astype(F32)), GDN_DK),
                                       jnp.repeat(gdn_dt_bias[l].astype(F32), GDN_DK),
                                       gdn_norm_g[l].reshape(-1)], axis=0))
        o_gdn = _gdn(zg, zab, _pad_rows(gdn_conv_w[l].astype(F32)), gdn_par, eab, gm, bd, tril, hm,
                     batch, steps)

        rw_f = jnp.concatenate([router_w[l], jnp.zeros((d, LANES - N_EXPERTS), F32)], axis=1)
        rw = jnp.concatenate(_split2(rw_f), axis=1)
        rb =_pad_rows(jnp.concatenate([router_b[l], jnp.zeros((LANES - N_EXPERTS,), F32)])[None, :])
        x1, hn2, ri, gates, cnt = _outproj(o_hg, o_ret, o_gdn, x2d, modv_c, gpar_c, _bf(w_out[l]),
                                           rw, rb, sl, tiles_per_batch)

        counts = cnt[0, :N_EXPERTS].astype(jnp.int32)
        padded = (counts + R_BLK - 1) // R_BLK * R_BLK
        pend = jnp.cumsum(padded)
        pstart = pend - padded
        n_used = (pend[-1] // R_BLK).astype(jnp.int32)
        blk_start = jnp.arange(n_blocks, dtype=jnp.int32) * R_BLK
        blk_start = jnp.minimum(blk_start, pend[-1] - R_BLK)
        block_e = jnp.sum(blk_start[:, None] >= pend[None, :], axis=1).astype(jnp.int32)
        eid = jnp.arange(N_EXPERTS, dtype=jnp.int32)
        pos = ri[:, TOP_K:] + jnp.sum(jnp.where(ri[:, :TOP_K, None] == eid, pstart, 0), axis=-1)

        tok = jnp.broadcast_to(jnp.arange(n, dtype=jnp.int32)[:, None], (n, TOP_K))
        buf_tok = jnp.zeros((p_rows,), jnp.int32).at[pos.reshape(-1)].set(
            tok.reshape(-1), mode="promise_in_bounds", unique_indices=True)
        xs = hn2.at[buf_tok].get(mode="promise_in_bounds")
        y = _experts(block_e + l * N_EXPERTS, n_used.reshape(1), xs, exp_w1, exp_b1, exp_w2, exp_b2)
        yg = y.at[pos.T].get(mode="promise_in_bounds")
        x2d = _combine(x1, yg, gates, modv_c, gpar_c, tiles_per_batch, final=(l == depth - 1))
    return x2d.reshape(batch, seq, d)
```

```python
import functools
import math

import numpy as np
import jax
import jax.numpy as jnp
from jax import lax
from jax.experimental import pallas as pl
from jax.experimental.pallas import tpu as pltpu
from jax.experimental.pallas import tpu_sc as plsc

F32 = jnp.float32
BF16 = jnp.bfloat16

D_MODEL = 1024
HG_HEADS, HG_DK = 4, 128
HG_WIDTH = HG_HEADS * HG_DK
RET_HEADS, RET_DK = 4, 64
RET_WIDTH = RET_HEADS * RET_DK
GDN_HEADS, GDN_DK = 4, 64
GDN_WIDTH = GDN_HEADS * GDN_DK
CONV_K = 4
ROPE_BASE = 10000.0
N_EXPERTS = 32
TOP_K = 4
D_FF = D_MODEL
SWIGLU_ALPHA = 1.702
SWIGLU_LIMIT = 7.0
NORM_EPS = 1e-6
L2_EPS = 1e-6
GDN_CHUNK = 64

LANES = 128
SUBLANES = 8
VMEM_LIMIT = 56 * 1024 * 1024

T_BLK = 256
TM = 256
R_BLK = 512
SC_ROWS = 128
N_LEVELS = 8

ZH_W = 4 * HG_WIDTH
ZR_W = 6 * RET_WIDTH
ZG_W = 4 * GDN_WIDTH
ZAB_W = LANES
Z_W = ZH_W + ZR_W + ZG_W + ZAB_W


def _dot(a, b):
    return jnp.dot(a, b, preferred_element_type=F32)


def _dot_nt(a, b):
    return lax.dot_general(a, b, (((1,), (1,)), ((), ())), preferred_element_type=F32)


def _dot_tn(a, b):
    return lax.dot_general(a, b, (((0,), (0,)), ((), ())), preferred_element_type=F32)


def _split2(x):
    hi = x.astype(BF16)
    return hi, (x - hi.astype(F32)).astype(BF16)


def _dot2_lhs01(c, x):
    hi, lo = _split2(x)
    return _dot(c, hi) + _dot(c, lo)


def _dot2_rhs01(x, c):
    hi, lo = _split2(x)
    return _dot(hi, c) + _dot(lo, c)


def _sigmoid(x):
    return 1.0 / (1.0 + jnp.exp(-x))


def _silu(x):
    return x * _sigmoid(x)


def _softplus(x):
    return jnp.maximum(x, 0.0) + jnp.log1p(jnp.exp(-jnp.abs(x)))


def _bf(x):
    return x.astype(BF16)


def _pack_halves(x):
    w = x.shape[1] // 2
    bits = lax.bitcast_convert_type(_bf(x).astype(F32), jnp.uint32)
    return (bits[:, :w] >> 16) | bits[:, w:]


def _unpack_halves(p):
    lo = lax.bitcast_convert_type(p << 16, F32)
    hi = lax.bitcast_convert_type(p & jnp.uint32(0xFFFF0000), F32)
    return lo, hi


def _level_matrix(t):
    i = np.arange(t)[:, None]
    j = np.arange(t)[None, :]
    x = i ^ j
    lv = np.floor(np.log2(np.maximum(x, 1))).astype(np.int32)
    lv = np.where(i == j, int(math.log2(t)), lv)
    lv = np.where(i < j, -1, lv)
    return lv.astype(np.int32)


def _hgrn_exponent_matrix(t):
    n_lev = int(math.log2(t))
    f = np.zeros((2 + n_lev, t, t), np.float32)
    u = np.arange(t)[None, :]
    r = np.arange(t)[:, None]
    f[0] = (u <= r)
    f[1] = (u > r)
    for l in range(n_lev):
        h = 1 << l
        base = (r // (2 * h)) * (2 * h)
        mid = base + h
        upper = (r - base) >= h
        f[2 + l] = np.where(upper, (u >= mid) & (u <= r), (u > r) & (u < mid))
    return f.reshape((2 + n_lev) * t, t)


def _gdn_masks(t):
    lv = _level_matrix(t)
    top = int(math.log2(GDN_CHUNK)) - 1
    strict = (lv >= 0) & (lv <= top)
    incl = strict | (lv == int(math.log2(t)))
    d8 = (lv >= 0) & (lv <= 2)
    merges = [(lv == l) for l in range(3, top + 1)]
    return np.stack([strict, incl, d8] + merges).astype(np.float32)


def _block_diag_mask(t, blk):
    i = np.arange(t)
    return (i[:, None] // blk == i[None, :] // blk).astype(np.float32)


def _head_masks(width, heads):
    lane = np.arange(width)[None, :]
    m = np.zeros((SUBLANES, width), np.float32)
    for h in range(heads):
        m[h] = (lane // (width // heads) == h)[0]
    return m


def _adaln_kernel(c_ref, w_ref, b_ref, o_ref):
    cond = _silu(c_ref[...])
    o_ref[0] = jnp.dot(cond, w_ref[0], preferred_element_type=F32,
                       precision=lax.Precision.HIGHEST) + b_ref[0]


def _adaln(c_pad, ada_w, ada_b):
    depth, d, n6 = ada_w.shape
    tn = n6 // 4
    return pl.pallas_call(
        _adaln_kernel,
        grid=(depth, n6 // tn),
        in_specs=[pl.BlockSpec((SUBLANES, d), lambda l, j: (0, 0)),
                  pl.BlockSpec((1, d, tn), lambda l, j: (l, 0, j)),
                  pl.BlockSpec((1, 1, tn), lambda l, j: (l, 0, j))],
        out_specs=pl.BlockSpec((1, SUBLANES, tn), lambda l, j: (l, 0, j)),
        out_shape=jax.ShapeDtypeStruct((depth, SUBLANES, n6), F32),
        compiler_params=pltpu.CompilerParams(vmem_limit_bytes=VMEM_LIMIT),
        name="adaln",
    )(c_pad, ada_w, ada_b.reshape(depth, 1, n6))


def _inproj_kernel(x_ref, mod_ref, g_ref, w_ref, zh_ref, zr_ref, zg_ref, zab_ref):
    x = x_ref[...]
    y = x * lax.rsqrt(jnp.mean(x * x, axis=-1, keepdims=True) + NORM_EPS)
    hn = (y * g_ref[0:1, :]) * (1.0 + mod_ref[0, 0:1, :]) + mod_ref[0, 1:2, :]
    hb = _bf(hn)
    zh_ref[...] = _dot(hb, w_ref[:, 0:ZH_W])
    zr_ref[...] = _dot(hb, w_ref[:, ZH_W:ZH_W + ZR_W])
    zg_ref[...] = _dot(hb, w_ref[:, ZH_W + ZR_W:ZH_W + ZR_W + ZG_W])
    zab_ref[...] = _dot(hb, w_ref[:, ZH_W + ZR_W + ZG_W:Z_W])


def _inproj(x2d, modv, gpar, w_bf, tiles_per_batch):
    n = x2d.shape[0]
    row = lambda i: (i, 0)
    return pl.pallas_call(
        _inproj_kernel,
        grid=(n // TM,),
        in_specs=[pl.BlockSpec((TM, D_MODEL), row),
                  pl.BlockSpec((1, SUBLANES, D_MODEL), lambda i: (i // tiles_per_batch, 0, 0)),
                  pl.BlockSpec((SUBLANES, D_MODEL), lambda i: (0, 0)),
                  pl.BlockSpec((D_MODEL, Z_W), lambda i: (0, 0))],
        out_specs=[pl.BlockSpec((TM, ZH_W), row), pl.BlockSpec((TM, ZR_W), row),
                   pl.BlockSpec((TM, ZG_W), row), pl.BlockSpec((TM, ZAB_W), row)],
        out_shape=[jax.ShapeDtypeStruct((n, ZH_W), F32), jax.ShapeDtypeStruct((n, ZR_W), F32),
                   jax.ShapeDtypeStruct((n, ZG_W), F32), jax.ShapeDtypeStruct((n, ZAB_W), F32)],
        compiler_params=pltpu.CompilerParams(dimension_semantics=("arbitrary",),
                                             vmem_limit_bytes=VMEM_LIMIT),
        name="inproj",
    )(x2d, modv, gpar, w_bf)


def _hgrn_kernel(zh_ref, f_ref, lv_ref, par_ref, o_ref, state_ref, ex_ref):
    @pl.when(pl.program_id(1) == 0)
    def _():
        state_ref[...] = jnp.zeros_like(state_ref)

    W = HG_WIDTH
    hq = zh_ref[:, 0:W]
    hf = zh_ref[:, W:2 * W]
    loglb = par_ref[0:1, :]
    log1mlb = par_ref[1:2, :]
    onemlb = par_ref[2:3, :]

    q = _silu(hq)
    e = jnp.exp(-jnp.abs(hf))
    inv = 1.0 / (1.0 + e)
    k = onemlb * (jnp.where(hf >= 0, e, 1.0) * inv)
    logsig = jnp.minimum(hf, 0.0) - jnp.log1p(e)
    c = log1mlb + logsig
    lf = jnp.maximum(loglb, c) + jnp.log1p(jnp.exp(-jnp.abs(loglb - c)))

    ex_ref[...] = jnp.exp(_dot2_lhs01(f_ref[...], lf))

    lv = lv_ref[...]
    t = T_BLK
    for h in range(HG_HEADS):
        cs = slice(h * HG_DK, (h + 1) * HG_DK)
        qh = q[:, cs]
        kh = k[:, cs]
        vh = _bf(zh_ref[:, 2 * W + h * HG_DK:2 * W + (h + 1) * HG_DK])
        s = jnp.where(lv == N_LEVELS, _dot_nt(_bf(qh), _bf(kh)), 0.0)
        for l in range(N_LEVELS):
            el = ex_ref[(2 + l) * t:(3 + l) * t, cs]
            s = jnp.where(lv == l, _dot_nt(_bf(qh * el), _bf(kh * el)), s)
        eb = ex_ref[0:t, cs]
        ebl = ex_ref[t:2 * t, cs]
        st = state_ref[h]
        o = _dot(_bf(s), vh) + _dot_nt(_bf(qh * eb), _bf(st))
        upd = _dot_tn(vh, _bf(kh * ebl))
        state_ref[h] = st * ex_ref[t - 1:t, cs] + upd
        ms = jnp.mean(o * o, axis=-1, keepdims=True)
        gate = _silu(zh_ref[:, 3 * W + h * HG_DK:3 * W + (h + 1) * HG_DK])
        o_ref[:, cs] = _bf(o * lax.rsqrt(ms + NORM_EPS) * par_ref[3:4, cs] * gate)


def _hgrn(zh, f_mat, lv, par, batch, steps):
    n = zh.shape[0]
    const = lambda b, j: (0, 0)
    return pl.pallas_call(
        _hgrn_kernel,
        grid=(batch, steps),
        in_specs=[pl.BlockSpec((T_BLK, ZH_W), lambda b, j: (b * steps + j, 0)),
                  pl.BlockSpec(f_mat.shape, const),
                  pl.BlockSpec(lv.shape, const),
                  pl.BlockSpec(par.shape, const)],
        out_specs=pl.BlockSpec((T_BLK, HG_WIDTH), lambda b, j: (b * steps + j, 0)),
        out_shape=jax.ShapeDtypeStruct((n, HG_WIDTH), BF16),
        scratch_shapes=[pltpu.VMEM((HG_HEADS, HG_DK, HG_DK), F32),
                        pltpu.VMEM(((2 + N_LEVELS) * T_BLK, HG_WIDTH), F32)],
        compiler_params=pltpu.CompilerParams(dimension_semantics=("arbitrary", "arbitrary"),
                                             vmem_limit_bytes=VMEM_LIMIT),
        name="hgrn2",
    )(zh, f_mat, lv, par)


def _ret_kernel(zr_ref, cos_ref, sin_ref, dmat_ref, qw_ref, kw_ref, par_ref, hm_ref, bd_ref,
                o_ref, state_ref):
    @pl.when(pl.program_id(1) == 0)
    def _():
        state_ref[...] = jnp.zeros_like(state_ref)

    W = RET_WIDTH
    cos = cos_ref[...]
    sin = sin_ref[...]
    q = zr_ref[:, 0:W] * cos + zr_ref[:, W:2 * W] * sin
    k = (zr_ref[:, 2 * W:3 * W] * cos + zr_ref[:, 3 * W:4 * W] * sin) * (RET_DK ** -0.5)
    v = zr_ref[:, 4 * W:5 * W]
    kb = _bf(k)
    bd = bd_ref[...]

    s_parts = []
    v_parts = []
    for h in range(RET_HEADS):
        hm = hm_ref[h:h + 1, :]
        s_parts.append(_bf(_dot_nt(_bf(q * hm), kb) * dmat_ref[h]))
        v_parts.append(_bf(v * hm))
    st = state_ref[...]
    o = (_dot(jnp.concatenate(s_parts, axis=1), jnp.concatenate(v_parts, axis=0))
         + _dot(_bf(q * qw_ref[...]), _bf(st)))
    kv = _dot_tn(_bf(k * kw_ref[...]), _bf(v))
    state_ref[...] = st * par_ref[0:1, :] + bd * kv

    ms = _dot2_rhs01(o * o, _bf(bd)) * (1.0 / RET_DK)
    gate = _silu(zr_ref[:, 5 * W:6 * W])
    o_ref[...] = _bf(o * lax.rsqrt(ms + NORM_EPS) * par_ref[1:2, :] * gate)


def _ret(zr, cos_t, sin_t, dmat, qw, kw, par, hm, bd, batch, steps):
    n = zr.shape[0]
    const2 = lambda b, j: (0, 0)
    return pl.pallas_call(
        _ret_kernel,
        grid=(batch, steps),
        in_specs=[pl.BlockSpec((T_BLK, ZR_W), lambda b, j: (b * steps + j, 0)),
                  pl.BlockSpec((T_BLK, RET_WIDTH), lambda b, j: (j, 0)),
                  pl.BlockSpec((T_BLK, RET_WIDTH), lambda b, j: (j, 0)),
                  pl.BlockSpec(dmat.shape, lambda b, j: (0, 0, 0)),
                  pl.BlockSpec(qw.shape, const2),
                  pl.BlockSpec(kw.shape, const2),
                  pl.BlockSpec(par.shape, const2),
                  pl.BlockSpec(hm.shape, const2),
                  pl.BlockSpec(bd.shape, const2)],
        out_specs=pl.BlockSpec((T_BLK, RET_WIDTH), lambda b, j: (b * steps + j, 0)),
        out_shape=jax.ShapeDtypeStruct((n, RET_WIDTH), BF16),
        scratch_shapes=[pltpu.VMEM((RET_WIDTH, RET_WIDTH), F32)],
        compiler_params=pltpu.CompilerParams(dimension_semantics=("arbitrary", "arbitrary"),
                                             vmem_limit_bytes=VMEM_LIMIT),
        name="retention",
    )(zr, cos_t, sin_t, dmat, qw, kw, par, hm, bd)


def _gdn_kernel(zg_ref, zab_ref, convw_ref, par_ref, eab_ref, gm_ref, bd_ref, tril_ref, hm_ref,
                o_ref, ext_ref, state_ref, obuf_ref):
    @pl.when(pl.program_id(1) == 0)
    def _():
        ext_ref[0:SUBLANES, :] = jnp.zeros((SUBLANES, 3 * GDN_WIDTH), F32)
        state_ref[...] = jnp.zeros_like(state_ref)

    W = GDN_WIDTH
    t = T_BLK
    u = zg_ref[:, 0:3 * W]
    ext_ref[SUBLANES:SUBLANES + t, :] = u
    conv = (convw_ref[3:4, :] * u
            + convw_ref[2:3, :] * ext_ref[SUBLANES - 1:SUBLANES - 1 + t, :]
            + convw_ref[1:2, :] * ext_ref[SUBLANES - 2:SUBLANES - 2 + t, :]
            + convw_ref[0:1, :] * ext_ref[SUBLANES - 3:SUBLANES - 3 + t, :])
    ext_ref[0:SUBLANES, :] = u[t - SUBLANES:t, :]
    qkv = _silu(conv)
    q = qkv[:, 0:W]
    k = qkv[:, W:2 * W]
    v = qkv[:, 2 * W:3 * W]

    bd = bd_ref[...]
    bdb = _bf(bd)
    qn = q * lax.rsqrt(_dot2_rhs01(q * q, bdb) + L2_EPS) * (GDN_DK ** -0.5)
    kn = k * lax.rsqrt(_dot2_rhs01(k * k, bdb) + L2_EPS)

    ab = zab_ref[...]
    a_exp = _dot2_rhs01(ab, eab_ref[:, 0:W])
    b_exp = _dot2_rhs01(ab, eab_ref[:, W:2 * W])
    beta = _sigmoid(b_exp)
    g = par_ref[0:1, :] * _softplus(a_exp + par_ref[1:2, :])
    gc = _dot2_lhs01(tril_ref[...], g)
    gl = _dot2_lhs01(bdb, g)
    eg = jnp.exp(gc)
    vb = v * beta
    kbeta = kn * beta * eg
    qdec = qn * eg
    kdec = kn * jnp.exp(gl - gc)

    incl = gm_ref[1]
    knb = _bf(kn)
    knbeta = kn * beta
    x_parts, qk_parts, hms = [], [], []
    for h in range(GDN_HEADS):
        hm = hm_ref[h:h + 1, :]
        hms.append(hm)
        cb = jnp.broadcast_to(gc[:, h * GDN_DK:h * GDN_DK + 1], (t, t))
        rel = jnp.exp(jnp.where(incl > 0, cb - cb.T, -jnp.inf))
        m = _dot_nt(_bf(knbeta * hm), knb) * rel
        d = _bf(m * gm_ref[2])
        d2 = _dot(d, d)
        d2b = _bf(d2)
        d4 = _dot(d2b, d2b)
        x = d2 - d.astype(F32) - _dot(d, d2b)
        x = x + d4 + _dot(_bf(x), _bf(d4))
        for lvl in range(3, gm_ref.shape[0]):
            lo = _bf(m * gm_ref[lvl])
            xb = _bf(x)
            y = lo.astype(F32) + _dot(xb, lo)
            x = x - (y + _dot(_bf(y), xb))
        x_parts.append(_bf(x))
        qk_parts.append(_bf(_dot_nt(_bf(qn * hm), knb) * rel))
    xcat = jnp.concatenate(x_parts, axis=1)
    qkcat = jnp.concatenate(qk_parts, axis=1)

    def stack(a):
        return jnp.concatenate([_bf(a * hm) for hm in hms], axis=0)

    w = vb + _dot(xcat, stack(vb))
    kcum = kbeta + _dot(xcat, stack(kbeta))
    a1 = _dot(qkcat, stack(w))
    qeff = qdec - _dot(qkcat, stack(kcum))

    for c in range(t // GDN_CHUNK):
        rows = slice(c * GDN_CHUNK, (c + 1) * GDN_CHUNK)
        st = state_ref[...]
        stb = _bf(st)
        vnew = w[rows] - _dot(_bf(kcum[rows]), stb)
        obuf_ref[rows, :] = _dot(_bf(qeff[rows]), stb) + a1[rows]
        upd = _dot_tn(_bf(kdec[rows]), _bf(vnew))
        last = eg[(c + 1) * GDN_CHUNK - 1:(c + 1) * GDN_CHUNK, :]
        state_ref[...] = st * last + bd * upd

    o = obuf_ref[...]
    ms = _dot2_rhs01(o * o, bdb) * (1.0 / GDN_DK)
    gate = _silu(zg_ref[:, 3 * W:4 * W])
    o_ref[...] = _bf(o * lax.rsqrt(ms + NORM_EPS) * par_ref[2:3, :] * gate)


def _gdn(zg, zab, convw, par, eab, gm, bd, tril, hm, batch, steps):
    n = zg.shape[0]
    const2 = lambda b, j: (0, 0)
    return pl.pallas_call(
        _gdn_kernel,
        grid=(batch, steps),
        in_specs=[pl.BlockSpec((T_BLK, ZG_W), lambda b, j: (b * steps + j, 0)),
                  pl.BlockSpec((T_BLK, ZAB_W), lambda b, j: (b * steps + j, 0)),
                  pl.BlockSpec(convw.shape, const2),
                  pl.BlockSpec(par.shape, const2),
                  pl.BlockSpec(eab.shape, const2),
                  pl.BlockSpec(gm.shape, lambda b, j: (0, 0, 0)),
                  pl.BlockSpec(bd.shape, const2),
                  pl.BlockSpec(tril.shape, const2),
                  pl.BlockSpec(hm.shape, const2)],
        out_specs=pl.BlockSpec((T_BLK, GDN_WIDTH), lambda b, j: (b * steps + j, 0)),
        out_shape=jax.ShapeDtypeStruct((n, GDN_WIDTH), BF16),
        scratch_shapes=[pltpu.VMEM((SUBLANES + T_BLK, 3 * GDN_WIDTH), F32),
                        pltpu.VMEM((GDN_WIDTH, GDN_WIDTH), F32),
                        pltpu.VMEM((T_BLK, GDN_WIDTH), F32)],
        compiler_params=pltpu.CompilerParams(dimension_semantics=("arbitrary", "arbitrary"),
                                             vmem_limit_bytes=VMEM_LIMIT),
        name="gdn",
    )(zg, zab, convw, par, eab, gm, bd, tril, hm)


def _outproj_kernel(ohg_ref, oret_ref, ogdn_ref, x_ref, mod_ref, g_ref, w_ref, rw_ref, rb_ref,
                    sl_ref, x1_ref, hn_ref, ri_ref, gates_ref, cnt_ref, run_ref):
    @pl.when(pl.program_id(0) == 0)
    def _():
        run_ref[...] = jnp.zeros_like(run_ref)

    y = (_dot(ohg_ref[...], w_ref[0:HG_WIDTH, :])
         + _dot(oret_ref[...], w_ref[HG_WIDTH:HG_WIDTH + RET_WIDTH, :])
         + _dot(ogdn_ref[...], w_ref[HG_WIDTH + RET_WIDTH:, :]))
    x1 = x_ref[...] + mod_ref[0, 0:1, :] * y
    x1_ref[...] = x1
    n = x1 * lax.rsqrt(jnp.mean(x1 * x1, axis=-1, keepdims=True) + NORM_EPS)
    hn = (n * g_ref[0:1, :]) * (1.0 + mod_ref[0, 1:2, :]) + mod_ref[0, 2:3, :]
    hn_ref[...] = _pack_halves(hn)

    lane = lax.broadcasted_iota(jnp.int32, (TM, LANES), 1)
    hn_hi, hn_lo = _split2(hn)
    rw_hi = rw_ref[:, 0:LANES]
    logits = (_dot(hn_hi, rw_hi) + _dot(hn_hi, rw_ref[:, LANES:2 * LANES]) + _dot(hn_lo, rw_hi)
              + rb_ref[0:1, :])
    work = jnp.where(lane < N_EXPERTS, logits, -jnp.inf)
    vals, idxs = [], []
    multihot = jnp.zeros((TM, LANES), F32)
    for _ in range(TOP_K):
        mx = jnp.max(work, axis=-1, keepdims=True)
        ix = jnp.min(jnp.where(work == mx, lane, LANES), axis=-1, keepdims=True)
        sel = lane == ix
        multihot = jnp.where(sel, 1.0, multihot)
        work = jnp.where(sel, -jnp.inf, work)
        vals.append(mx)
        idxs.append(ix)
    ex = [jnp.exp(vv - vals[0]) for vv in vals]
    den = ex[0] + ex[1] + ex[2] + ex[3]

    before = _dot(sl_ref[...], _bf(multihot)) + run_ref[0:1, :]
    run_ref[0:1, :] = run_ref[0:1, :] + jnp.sum(multihot, axis=0, keepdims=True)
    cnt_ref[...] = jnp.broadcast_to(run_ref[0:1, :], cnt_ref.shape)

    lane8 = lax.broadcasted_iota(jnp.int32, (TM, 2 * TOP_K), 1)
    ri = jnp.zeros((TM, 2 * TOP_K), jnp.int32)
    gt = jnp.zeros((TM, 2 * TOP_K), F32)
    for kk in range(TOP_K):
        rank = jnp.sum(jnp.where(lane == idxs[kk], before, 0.0), axis=-1, keepdims=True)
        ri = jnp.where(lane8 == kk, idxs[kk], ri)
        ri = jnp.where(lane8 == TOP_K + kk, rank.astype(jnp.int32), ri)
        gt = jnp.where(lane8 == kk, ex[kk] / den, gt)
    ri_ref[...] = ri
    gates_ref[...] = gt


def _outproj(ohg, oret, ogdn, x2d, modv, gpar, w_bf, rw, rb, sl, tiles_per_batch):
    n = x2d.shape[0]
    row = lambda i: (i, 0)
    const = lambda i: (0, 0)
    return pl.pallas_call(
        _outproj_kernel,
        grid=(n // TM,),
        in_specs=[pl.BlockSpec((TM, HG_WIDTH), row), pl.BlockSpec((TM, RET_WIDTH), row),
                  pl.BlockSpec((TM, GDN_WIDTH), row), pl.BlockSpec((TM, D_MODEL), row),
                  pl.BlockSpec((1, SUBLANES, D_MODEL), lambda i: (i // tiles_per_batch, 0, 0)),
                  pl.BlockSpec((SUBLANES, D_MODEL), const),
                  pl.BlockSpec((D_MODEL, D_MODEL), const),
                  pl.BlockSpec((D_MODEL, 2 * LANES), const),
                  pl.BlockSpec((SUBLANES, LANES), const),
                  pl.BlockSpec((TM, TM), const)],
        out_specs=[pl.BlockSpec((TM, D_MODEL), row), pl.BlockSpec((TM, D_MODEL // 2), row),
                   pl.BlockSpec((TM, 2 * TOP_K), row), pl.BlockSpec((TM, 2 * TOP_K), row),
                   pl.BlockSpec((SUBLANES, LANES), const)],
        out_shape=[jax.ShapeDtypeStruct((n, D_MODEL), F32),
                   jax.ShapeDtypeStruct((n, D_MODEL // 2), jnp.uint32),
                   jax.ShapeDtypeStruct((n, 2 * TOP_K), jnp.int32),
                   jax.ShapeDtypeStruct((n, 2 * TOP_K), F32),
                   jax.ShapeDtypeStruct((SUBLANES, LANES), F32)],
        scratch_shapes=[pltpu.VMEM((SUBLANES, LANES), F32)],
        compiler_params=pltpu.CompilerParams(dimension_semantics=("arbitrary",),
                                             vmem_limit_bytes=VMEM_LIMIT),
        name="outproj_router",
    )(ohg, oret, ogdn, x2d, modv, gpar, w_bf, rw, rb, sl)


def _expert_kernel(be_ref, nu_ref, x_ref, w1_ref, b1_ref, w2_ref, b2_ref, y_ref, w1b_ref, w2b_ref):
    i = pl.program_id(0)
    prev = be_ref[jnp.maximum(i - 1, 0)]
    fresh = jnp.logical_or(i == 0, be_ref[i] != prev)

    @pl.when(fresh)
    def _():
        w1b_ref[...] = _bf(w1_ref[0])
        w2b_ref[...] = _bf(w2_ref[0])

    @pl.when(i < nu_ref[0])
    def _():
        half = D_MODEL // 2
        x_lo, x_hi = _unpack_halves(x_ref[...])
        hid = (_dot(_bf(x_lo), w1b_ref[0:half, :]) + _dot(_bf(x_hi), w1b_ref[half:, :])
               + b1_ref[0])
        x_glu = jnp.minimum(hid[:, 0:D_FF], SWIGLU_LIMIT)
        x_lin = jnp.clip(hid[:, D_FF:], -SWIGLU_LIMIT, SWIGLU_LIMIT)
        act = x_glu * _sigmoid(SWIGLU_ALPHA * x_glu) * (x_lin + 1.0)
        y_ref[...] = _pack_halves(_dot(_bf(act), w2b_ref[...]) + b2_ref[0])


def _experts(block_e, n_used, xs, w1, b1, w2, b2):
    p = xs.shape[0]
    n_blocks = p // R_BLK
    ne = w1.shape[0] * w1.shape[1]
    w1 = w1.reshape(ne, D_MODEL, 2 * D_FF)
    w2 = w2.reshape(ne, D_FF, D_MODEL)
    rowmap = lambda i, be, nu: (jnp.minimum(i, nu[0] - 1), 0)
    emap = lambda i, be, nu: (be[i], 0, 0)
    grid_spec = pltpu.PrefetchScalarGridSpec(
        num_scalar_prefetch=2,
        grid=(n_blocks,),
        in_specs=[pl.BlockSpec((R_BLK, D_MODEL // 2), rowmap),
                  pl.BlockSpec((1, D_MODEL, 2 * D_FF), emap),
                  pl.BlockSpec((1, 1, 2 * D_FF), emap),
                  pl.BlockSpec((1, D_FF, D_MODEL), emap),
                  pl.BlockSpec((1, 1, D_MODEL), emap)],
        out_specs=pl.BlockSpec((R_BLK, D_MODEL // 2), rowmap),
        scratch_shapes=[pltpu.VMEM((D_MODEL, 2 * D_FF), BF16), pltpu.VMEM((D_FF, D_MODEL), BF16)],
    )
    return pl.pallas_call(
        _expert_kernel,
        grid_spec=grid_spec,
        out_shape=jax.ShapeDtypeStruct((p, D_MODEL // 2), jnp.uint32),
        compiler_params=pltpu.CompilerParams(dimension_semantics=("arbitrary",),
                                             vmem_limit_bytes=VMEM_LIMIT),
        name="experts",
    )(block_e, n_used, xs, w1, b1.reshape(ne, 1, 2 * D_FF), w2, b2.reshape(ne, 1, D_MODEL))


def _sc_mesh():
    return plsc.VectorSubcoreMesh(core_axis_name="c", subcore_axis_name="s")


def _sc_scatter_rows(x, pos, p_rows):
    mesh = _sc_mesh()
    n, w = x.shape
    per_worker = n // (mesh.num_cores * mesh.num_subcores)
    assert per_worker % SC_ROWS == 0 and pos.shape == (TOP_K, n)

    @functools.partial(
        pl.kernel, out_type=jax.ShapeDtypeStruct((p_rows, w), x.dtype), mesh=mesh,
        scratch_types=[pltpu.VMEM((SC_ROWS,), jnp.int32)] * TOP_K
        + [pltpu.VMEM((SC_ROWS, w), x.dtype), pltpu.SemaphoreType.DMA],
        name="dispatch_rows")
    def scatter(x_hbm, p_hbm, o_hbm, i0, i1, i2, i3, rows_v, sem):
        idx = (i0, i1, i2, i3)
        worker = lax.axis_index("s") * mesh.num_cores + lax.axis_index("c")

        @pl.loop(0, per_worker // SC_ROWS)
        def _(g):
            base = pl.multiple_of(worker * per_worker + g * SC_ROWS, SC_ROWS)
            pltpu.sync_copy(x_hbm.at[pl.ds(base, SC_ROWS)], rows_v)
            for kk in range(TOP_K):
                pltpu.sync_copy(p_hbm.at[kk, pl.ds(base, SC_ROWS)], idx[kk])
            copies = [pltpu.async_copy(rows_v, o_hbm.at[idx[kk]], sem) for kk in range(TOP_K)]
            for cp in copies:
                cp.wait()

    return scatter(x, pos)


def _sc_gather_rows(table, idx):
    mesh = _sc_mesh()
    n_idx = idx.shape[0]
    w = table.shape[1]
    per_worker = n_idx // (mesh.num_cores * mesh.num_subcores)
    assert per_worker % SC_ROWS == 0

    @functools.partial(
        pl.kernel, out_type=jax.ShapeDtypeStruct((n_idx, w), table.dtype), mesh=mesh,
        scratch_types=[pltpu.VMEM((SC_ROWS,), jnp.int32), pltpu.VMEM((SC_ROWS, w), table.dtype),
                       pltpu.SemaphoreType.DMA],
        name="combine_rows")
    def gather(t_hbm, i_hbm, o_hbm, idx_v, rows_v, sem):
        worker = lax.axis_index("s") * mesh.num_cores + lax.axis_index("c")

        @pl.loop(0, per_worker // SC_ROWS)
        def _(g):
            base = pl.multiple_of(worker * per_worker + g * SC_ROWS, SC_ROWS)
            pltpu.sync_copy(i_hbm.at[pl.ds(base, SC_ROWS)], idx_v)
            pltpu.async_copy(t_hbm.at[idx_v], rows_v, sem).wait()
            pltpu.sync_copy(rows_v, o_hbm.at[pl.ds(base, SC_ROWS)])

    return gather(table, idx)


def _combine_kernel(x_ref, y0_ref, y1_ref, y2_ref, y3_ref, gates_ref, mod_ref, g_ref, o_ref, *, final):
    gt = gates_ref[...]
    half = D_MODEL // 2
    acc_lo, acc_hi = None, None
    for kk, y_ref in enumerate((y0_ref, y1_ref, y2_ref, y3_ref)):
        lo, hi = _unpack_halves(y_ref[0])
        g = gt[:, kk:kk + 1]
        acc_lo = g * lo if acc_lo is None else acc_lo + g * lo
        acc_hi = g * hi if acc_hi is None else acc_hi + g * hi
    x_lo = x_ref[:, 0:half] + mod_ref[0, 3:4, 0:half] * acc_lo
    x_hi = x_ref[:, half:] + mod_ref[0, 3:4, half:] * acc_hi
    if final:
        ssq = (jnp.sum(x_lo * x_lo, axis=-1, keepdims=True)
               + jnp.sum(x_hi * x_hi, axis=-1, keepdims=True))
        r = lax.rsqrt(ssq * (1.0 / D_MODEL) + NORM_EPS)
        x_lo = x_lo * r * g_ref[1:2, 0:half]
        x_hi = x_hi * r * g_ref[1:2, half:]
    o_ref[:, 0:half] = x_lo
    o_ref[:, half:] = x_hi


def _combine(x1, yg, gates, modv, gpar, tiles_per_batch, final):
    n = x1.shape[0]
    row = lambda i: (i, 0)
    ysp = [pl.BlockSpec((1, TM, D_MODEL // 2), (lambda i, kk=kk: (kk, i, 0))) for kk in range(TOP_K)]
    return pl.pallas_call(
        functools.partial(_combine_kernel, final=final),
        grid=(n // TM,),
        in_specs=[pl.BlockSpec((TM, D_MODEL), row)] + ysp + [
            pl.BlockSpec((TM, 2 * TOP_K), row),
            pl.BlockSpec((1, SUBLANES, D_MODEL), lambda i: (i // tiles_per_batch, 0, 0)),
            pl.BlockSpec((SUBLANES, D_MODEL), lambda i: (0, 0))],
        out_specs=pl.BlockSpec((TM, D_MODEL), row),
        out_shape=jax.ShapeDtypeStruct((n, D_MODEL), F32),
        compiler_params=pltpu.CompilerParams(dimension_semantics=("arbitrary",),
                                             vmem_limit_bytes=VMEM_LIMIT),
        name="combine",
    )(x1, yg, yg, yg, yg, gates, modv, gpar)


def _rot_cols(w):
    d = w.shape[0]
    w4 = w.reshape(d, RET_HEADS, 2, RET_DK // 2)
    return jnp.concatenate([-w4[:, :, 1:2], w4[:, :, 0:1]], axis=2).reshape(d, RET_WIDTH)


def _widen_w_in(w):
    o = 0
    hg = w[:, o:o + 4 * HG_WIDTH]; o += 4 * HG_WIDTH
    rq = w[:, o:o + RET_WIDTH]; o += RET_WIDTH
    rk = w[:, o:o + RET_WIDTH]; o += RET_WIDTH
    rv = w[:, o:o + RET_WIDTH]; o += RET_WIDTH
    rg = w[:, o:o + RET_WIDTH]; o += RET_WIDTH
    gd = w[:, o:o + 4 * GDN_WIDTH]; o += 4 * GDN_WIDTH
    gab = w[:, o:o + 2 * GDN_HEADS]
    pad = jnp.zeros((w.shape[0], ZAB_W - 2 * GDN_HEADS), w.dtype)
    return _bf(jnp.concatenate([hg, rq, _rot_cols(rq), rk, _rot_cols(rk), rv, rg, gd, gab, pad], axis=1))


def _pad_rows(a, rows=SUBLANES):
    return jnp.concatenate([a, jnp.zeros((rows - a.shape[0],) + a.shape[1:], a.dtype)], axis=0)


def kernel(x, c, ada_w, ada_b, norm1_g, norm2_g, w_in, w_out, hg_lb_logits, hg_norm_g, ret_norm_g,
           gdn_conv_w, gdn_A_log, gdn_dt_bias, gdn_norm_g, router_w, router_b, exp_w1, exp_b1,
           exp_w2, exp_b2, final_norm_g):
    batch, seq, d = x.shape
    depth = ada_w.shape[0]
    n = batch * seq
    steps = seq // T_BLK
    tiles_per_batch = seq // TM
    nk = n * TOP_K
    n_blocks = nk // R_BLK + N_EXPERTS
    p_rows = n_blocks * R_BLK

    lv_np = _level_matrix(T_BLK)
    lv = jnp.asarray(lv_np)
    f_mat = jnp.asarray(_hgrn_exponent_matrix(T_BLK), BF16)
    gm = jnp.asarray(_gdn_masks(T_BLK))
    bd = jnp.asarray(_block_diag_mask(T_BLK, GDN_CHUNK))
    tril = jnp.asarray(_block_diag_mask(T_BLK, GDN_CHUNK) * (lv_np >= 0), BF16)
    hm = jnp.asarray(_head_masks(RET_WIDTH, RET_HEADS))
    sl = jnp.asarray(np.tril(np.ones((TM, TM), np.float32), -1), BF16)
    eab_np = np.zeros((LANES, 2 * GDN_WIDTH), np.float32)
    for h in range(GDN_HEADS):
        eab_np[h, h * GDN_DK:(h + 1) * GDN_DK] = 1.0
        eab_np[GDN_HEADS + h, GDN_WIDTH + h * GDN_DK:GDN_WIDTH + (h + 1) * GDN_DK] = 1.0
    eab = jnp.asarray(eab_np, BF16)

    half = RET_DK // 2
    inv = ROPE_BASE ** (-jnp.linspace(0.0, 1.0, half, dtype=F32))
    ang = jnp.arange(seq, dtype=F32)[:, None] * inv[None, :]
    cos_t = jnp.tile(jnp.cos(ang), (1, 2 * RET_HEADS))
    sin_t = jnp.tile(jnp.sin(ang), (1, 2 * RET_HEADS))
    log_g = jnp.log1p(-jnp.exp2(-5.0 - jnp.arange(RET_HEADS, dtype=F32)))
    jj = jnp.arange(T_BLK, dtype=F32)
    diff = jj[:, None] - jj[None, :]
    dmat = jnp.where(diff[None] >= 0, jnp.exp(diff[None] * log_g[:, None, None]), 0.0)
    lg_lane = jnp.repeat(log_g, RET_DK)[None, :]
    qw = jnp.exp(lg_lane * (jj[:, None] + 1.0))
    kw = jnp.exp(lg_lane * (T_BLK - 1.0 - jj[:, None]))
    cdec = jnp.exp(T_BLK * lg_lane)

    lb = jnp.cumsum(jax.nn.softmax(hg_lb_logits.astype(F32), axis=0), axis=0)
    lb = jnp.maximum(lb - lb[0], 0.0)
    c_pad = _pad_rows(c.astype(F32))
    mod = _adaln(c_pad, ada_w, ada_b)[:, :batch, :]

    x2d = x.reshape(n, d)
    out = None
    for l in range(depth):
        sh1, sc1, gt1, sh2, sc2, gt2 = [mod[l][:, i * d:(i + 1) * d] for i in range(6)]
        zeros = jnp.zeros_like(sh1)
        modv_a = jnp.stack([sc1, sh1, zeros, zeros, zeros, zeros, zeros, zeros], axis=1)
        modv_c = jnp.stack([gt1, sc2, sh2, gt2, zeros, zeros, zeros, zeros], axis=1)
        gpar_a = _pad_rows(norm1_g[l][None, :])
        gpar_c = _pad_rows(jnp.stack([norm2_g[l], final_norm_g], axis=0))

        zh, zr, zg, zab = _inproj(x2d, modv_a, gpar_a, _widen_w_in(w_in[l]), tiles_per_batch)

        hg_par = _pad_rows(jnp.stack([jnp.log(lb[l]), jnp.log1p(-lb[l]), 1.0 - lb[l],
                                      hg_norm_g[l].reshape(-1)], axis=0))
        o_hg = _hgrn(zh, f_mat, lv, hg_par, batch, steps)

        ret_par = _pad_rows(jnp.concatenate([cdec, ret_norm_g[l].reshape(1, -1)], axis=0))
        o_ret = _ret(zr, cos_t, sin_t, dmat, qw, kw, ret_par, hm, bd, batch, steps)

        gdn_par = _pad_rows(jnp.stack([jnp.repeat(-jnp.exp(gdn_A_log[l].astype(F32)), GDN_DK),
                                       jnp.repeat(gdn_dt_bias[l].astype(F32), GDN_DK),
                                       gdn_norm_g[l].reshape(-1)], axis=0))
        o_gdn = _gdn(zg, zab, _pad_rows(gdn_conv_w[l].astype(F32)), gdn_par, eab, gm, bd, tril, hm,
                     batch, steps)

        rw_f = jnp.concatenate([router_w[l], jnp.zeros((d, LANES - N_EXPERTS), F32)], axis=1)
        rw = jnp.concatenate(_split2(rw_f), axis=1)
        rb =_pad_rows(jnp.concatenate([router_b[l], jnp.zeros((LANES - N_EXPERTS,), F32)])[None, :])
        x1, hn2, ri, gates, cnt = _outproj(o_hg, o_ret, o_gdn, x2d, modv_c, gpar_c, _bf(w_out[l]),
                                           rw, rb, sl, tiles_per_batch)

        counts = cnt[0, :N_EXPERTS].astype(jnp.int32)
        padded = (counts + R_BLK - 1) // R_BLK * R_BLK
        pend = jnp.cumsum(padded)
        pstart = pend - padded
        n_used = (pend[-1] // R_BLK).astype(jnp.int32)
        blk_start = jnp.arange(n_blocks, dtype=jnp.int32) * R_BLK
        blk_start = jnp.minimum(blk_start, pend[-1] - R_BLK)
        block_e = jnp.sum(blk_start[:, None] >= pend[None, :], axis=1).astype(jnp.int32)
        eid = jnp.arange(N_EXPERTS, dtype=jnp.int32)
        pos = ri[:, TOP_K:] + jnp.sum(jnp.where(ri[:, :TOP_K, None] == eid, pstart, 0), axis=-1)

        pos_t = pos.T
        xs = _sc_scatter_rows(hn2, pos_t, p_rows)
        y = _experts(block_e + l * N_EXPERTS, n_used.reshape(1), xs, exp_w1, exp_b1, exp_w2, exp_b2)
        yg = _sc_gather_rows(y, pos_t.reshape(-1)).reshape(TOP_K, n, d // 2)
        x2d = _combine(x1, yg, gates, modv_c, gpar_c, tiles_per_batch, final=(l == depth - 1))
    return x2d.reshape(batch, seq, d)
```

```python
import functools
import math

import numpy as np
import jax
import jax.numpy as jnp
from jax import lax
from jax.experimental import pallas as pl
from jax.experimental.pallas import tpu as pltpu
from jax.experimental.pallas import tpu_sc as plsc

F32 = jnp.float32
BF16 = jnp.bfloat16

D_MODEL = 1024
HG_HEADS, HG_DK = 4, 128
HG_WIDTH = HG_HEADS * HG_DK
RET_HEADS, RET_DK = 4, 64
RET_WIDTH = RET_HEADS * RET_DK
GDN_HEADS, GDN_DK = 4, 64
GDN_WIDTH = GDN_HEADS * GDN_DK
CONV_K = 4
ROPE_BASE = 10000.0
N_EXPERTS = 32
TOP_K = 4
D_FF = D_MODEL
SWIGLU_ALPHA = 1.702
SWIGLU_LIMIT = 7.0
NORM_EPS = 1e-6
L2_EPS = 1e-6
GDN_CHUNK = 64

LANES = 128
SUBLANES = 8
VMEM_LIMIT = 56 * 1024 * 1024

T_BLK = 256
TM = 256
R_BLK = 512
SC_ROWS = 128
HG_SUB = 128
HG_LEVELS = 7

ZH_W = 4 * HG_WIDTH
ZR_W = 6 * RET_WIDTH
ZG_W = 4 * GDN_WIDTH
ZAB_W = LANES
Z_W = ZH_W + ZR_W + ZG_W + ZAB_W


def _dot(a, b):
    return jnp.dot(a, b, preferred_element_type=F32)


def _dot_nt(a, b):
    return lax.dot_general(a, b, (((1,), (1,)), ((), ())), preferred_element_type=F32)


def _dot_tn(a, b):
    return lax.dot_general(a, b, (((0,), (0,)), ((), ())), preferred_element_type=F32)


def _split2(x):
    hi = x.astype(BF16)
    return hi, (x - hi.astype(F32)).astype(BF16)


def _dot2_lhs01(c, x):
    hi, lo = _split2(x)
    return _dot(c, hi) + _dot(c, lo)


def _dot2_rhs01(x, c):
    hi, lo = _split2(x)
    return _dot(hi, c) + _dot(lo, c)


def _sigmoid(x):
    return 1.0 / (1.0 + jnp.exp(-x))


def _silu(x):
    return x * _sigmoid(x)


def _softplus(x):
    return jnp.maximum(x, 0.0) + jnp.log1p(jnp.exp(-jnp.abs(x)))


def _bf(x):
    return x.astype(BF16)


def _pack_halves(x):
    w = x.shape[1] // 2
    bits = lax.bitcast_convert_type(_bf(x).astype(F32), jnp.uint32)
    return (bits[:, :w] >> 16) | bits[:, w:]


def _unpack_halves(p):
    lo = lax.bitcast_convert_type(p << 16, F32)
    hi = lax.bitcast_convert_type(p & jnp.uint32(0xFFFF0000), F32)
    return lo, hi


def _level_matrix(t):
    i = np.arange(t)[:, None]
    j = np.arange(t)[None, :]
    x = i ^ j
    lv = np.floor(np.log2(np.maximum(x, 1))).astype(np.int32)
    lv = np.where(i == j, int(math.log2(t)), lv)
    lv = np.where(i < j, -1, lv)
    return lv.astype(np.int32)


def _hgrn_exponent_matrix(t):
    n_lev = int(math.log2(t))
    f = np.zeros((2 + n_lev, t, t), np.float32)
    u = np.arange(t)[None, :]
    r = np.arange(t)[:, None]
    f[0] = (u <= r)
    f[1] = (u > r)
    for l in range(n_lev):
        h = 1 << l
        base = (r // (2 * h)) * (2 * h)
        mid = base + h
        upper = (r - base) >= h
        f[2 + l] = np.where(upper, (u >= mid) & (u <= r), (u > r) & (u < mid))
    return f.reshape((2 + n_lev) * t, t)


def _gdn_masks(t):
    lv = _level_matrix(GDN_CHUNK)
    top = int(math.log2(GDN_CHUNK))
    incl = lv >= 0
    d8 = (lv >= 0) & (lv <= 2)
    merges = [(lv == l) for l in range(3, top)]
    eye = lv == top
    pats = np.stack([incl, d8] + merges + [eye]).astype(np.float32)
    return np.tile(pats, (1, t // GDN_CHUNK, GDN_HEADS))


def _block_diag_mask(t, blk):
    i = np.arange(t)
    return (i[:, None] // blk == i[None, :] // blk).astype(np.float32)


def _head_masks(width, heads):
    lane = np.arange(width)[None, :]
    m = np.zeros((SUBLANES, width), np.float32)
    for h in range(heads):
        m[h] = (lane // (width // heads) == h)[0]
    return m


def _adaln_kernel(c_ref, w_ref, b_ref, o_ref):
    cond = _silu(c_ref[...])
    o_ref[0] = jnp.dot(cond, w_ref[0], preferred_element_type=F32,
                       precision=lax.Precision.HIGHEST) + b_ref[0]


def _adaln(c_pad, ada_w, ada_b):
    depth, d, n6 = ada_w.shape
    tn = n6 // 4
    return pl.pallas_call(
        _adaln_kernel,
        grid=(depth, n6 // tn),
        in_specs=[pl.BlockSpec((SUBLANES, d), lambda l, j: (0, 0)),
                  pl.BlockSpec((1, d, tn), lambda l, j: (l, 0, j)),
                  pl.BlockSpec((1, 1, tn), lambda l, j: (l, 0, j))],
        out_specs=pl.BlockSpec((1, SUBLANES, tn), lambda l, j: (l, 0, j)),
        out_shape=jax.ShapeDtypeStruct((depth, SUBLANES, n6), F32),
        compiler_params=pltpu.CompilerParams(vmem_limit_bytes=VMEM_LIMIT),
        name="adaln",
    )(c_pad, ada_w, ada_b.reshape(depth, 1, n6))


def _inproj_kernel(x_ref, mod_ref, g_ref, w_ref, zh_ref, zr_ref, zg_ref, zab_ref):
    x = x_ref[...]
    y = x * lax.rsqrt(jnp.mean(x * x, axis=-1, keepdims=True) + NORM_EPS)
    hn = (y * g_ref[0:1, :]) * (1.0 + mod_ref[0, 0:1, :]) + mod_ref[0, 1:2, :]
    hb = _bf(hn)
    zh_ref[...] = _dot(hb, w_ref[:, 0:ZH_W])
    zr_ref[...] = _dot(hb, w_ref[:, ZH_W:ZH_W + ZR_W])
    zg_ref[...] = _dot(hb, w_ref[:, ZH_W + ZR_W:ZH_W + ZR_W + ZG_W])
    zab_ref[...] = _dot(hb, w_ref[:, ZH_W + ZR_W + ZG_W:Z_W])


def _inproj(x2d, modv, gpar, w_bf, tiles_per_batch):
    n = x2d.shape[0]
    row = lambda i: (i, 0)
    return pl.pallas_call(
        _inproj_kernel,
        grid=(n // TM,),
        in_specs=[pl.BlockSpec((TM, D_MODEL), row),
                  pl.BlockSpec((1, SUBLANES, D_MODEL), lambda i: (i // tiles_per_batch, 0, 0)),
                  pl.BlockSpec((SUBLANES, D_MODEL), lambda i: (0, 0)),
                  pl.BlockSpec((D_MODEL, Z_W), lambda i: (0, 0))],
        out_specs=[pl.BlockSpec((TM, ZH_W), row), pl.BlockSpec((TM, ZR_W), row),
                   pl.BlockSpec((TM, ZG_W), row), pl.BlockSpec((TM, ZAB_W), row)],
        out_shape=[jax.ShapeDtypeStruct((n, ZH_W), F32), jax.ShapeDtypeStruct((n, ZR_W), F32),
                   jax.ShapeDtypeStruct((n, ZG_W), F32), jax.ShapeDtypeStruct((n, ZAB_W), F32)],
        compiler_params=pltpu.CompilerParams(dimension_semantics=("arbitrary",),
                                             vmem_limit_bytes=VMEM_LIMIT),
        name="inproj",
    )(x2d, modv, gpar, w_bf)


def _hgrn_kernel(zh_ref, f_ref, lv_ref, par_ref, o_ref, state_ref, ex_ref):
    @pl.when(pl.program_id(1) == 0)
    def _():
        state_ref[...] = jnp.zeros_like(state_ref)

    W = HG_WIDTH
    hq = zh_ref[:, 0:W]
    hf = zh_ref[:, W:2 * W]
    loglb = par_ref[0:1, :]
    log1mlb = par_ref[1:2, :]
    onemlb = par_ref[2:3, :]

    q = _silu(hq)
    e = jnp.exp(-jnp.abs(hf))
    inv = 1.0 / (1.0 + e)
    k = onemlb * (jnp.where(hf >= 0, e, 1.0) * inv)
    logsig = jnp.minimum(hf, 0.0) - jnp.log1p(e)
    c = log1mlb + logsig
    lf = jnp.maximum(loglb, c) + jnp.log1p(jnp.exp(-jnp.abs(loglb - c)))

    lv = lv_ref[...]
    t = HG_SUB
    for sb in range(T_BLK // HG_SUB):
        rows = slice(sb * t, (sb + 1) * t)
        ex_ref[sb] = jnp.exp(_dot2_lhs01(f_ref[...], lf[rows]))
        for h in range(HG_HEADS):
            cs = slice(h * HG_DK, (h + 1) * HG_DK)
            qh = q[rows, cs]
            kh = k[rows, cs]
            vh = _bf(zh_ref[rows, 2 * W + h * HG_DK:2 * W + (h + 1) * HG_DK])
            s = jnp.where(lv == HG_LEVELS, _dot_nt(_bf(qh), _bf(kh)), 0.0)
            for l in range(HG_LEVELS):
                el = ex_ref[sb, (2 + l) * t:(3 + l) * t, cs]
                s = jnp.where(lv == l, _dot_nt(_bf(qh * el), _bf(kh * el)), s)
            eb = ex_ref[sb, 0:t, cs]
            ebl = ex_ref[sb, t:2 * t, cs]
            st = state_ref[h]
            o = _dot(_bf(s), vh) + _dot_nt(_bf(qh * eb), _bf(st))
            upd = _dot_tn(vh, _bf(kh * ebl))
            state_ref[h] = st * ex_ref[sb, t - 1:t, cs] + upd
            ms = jnp.mean(o * o, axis=-1, keepdims=True)
            gate = _silu(zh_ref[rows, 3 * W + h * HG_DK:3 * W + (h + 1) * HG_DK])
            o_ref[rows, cs] = _bf(o * lax.rsqrt(ms + NORM_EPS) * par_ref[3:4, cs] * gate)


def _hgrn(zh, f_mat, lv, par, batch, steps):
    n = zh.shape[0]
    const = lambda b, j: (0, 0)
    return pl.pallas_call(
        _hgrn_kernel,
        grid=(batch, steps),
        in_specs=[pl.BlockSpec((T_BLK, ZH_W), lambda b, j: (b * steps + j, 0)),
                  pl.BlockSpec(f_mat.shape, const),
                  pl.BlockSpec(lv.shape, const),
                  pl.BlockSpec(par.shape, const)],
        out_specs=pl.BlockSpec((T_BLK, HG_WIDTH), lambda b, j: (b * steps + j, 0)),
        out_shape=jax.ShapeDtypeStruct((n, HG_WIDTH), BF16),
        scratch_shapes=[pltpu.VMEM((HG_HEADS, HG_DK, HG_DK), F32),
                        pltpu.VMEM((T_BLK // HG_SUB, (2 + HG_LEVELS) * HG_SUB, HG_WIDTH), F32)],
        compiler_params=pltpu.CompilerParams(dimension_semantics=("arbitrary", "arbitrary"),
                                             vmem_limit_bytes=VMEM_LIMIT),
        name="hgrn2",
    )(zh, f_mat, lv, par)


def _ret_kernel(zr_ref, cos_ref, sin_ref, dmat_ref, qw_ref, kw_ref, par_ref, hm_ref, bd_ref,
                o_ref, state_ref):
    @pl.when(pl.program_id(1) == 0)
    def _():
        state_ref[...] = jnp.zeros_like(state_ref)

    W = RET_WIDTH
    cos = cos_ref[...]
    sin = sin_ref[...]
    q = zr_ref[:, 0:W] * cos + zr_ref[:, W:2 * W] * sin
    k = (zr_ref[:, 2 * W:3 * W] * cos + zr_ref[:, 3 * W:4 * W] * sin) * (RET_DK ** -0.5)
    v = zr_ref[:, 4 * W:5 * W]
    kb = _bf(k)
    bd = bd_ref[...]

    s_parts = []
    v_parts = []
    for h in range(RET_HEADS):
        hm = hm_ref[h:h + 1, :]
        s_parts.append(_bf(_dot_nt(_bf(q * hm), kb) * dmat_ref[h]))
        v_parts.append(_bf(v * hm))
    st = state_ref[...]
    o = (_dot(jnp.concatenate(s_parts, axis=1), jnp.concatenate(v_parts, axis=0))
         + _dot(_bf(q * qw_ref[...]), _bf(st)))
    kv = _dot_tn(_bf(k * kw_ref[...]), _bf(v))
    state_ref[...] = st * par_ref[0:1, :] + bd * kv

    ms = _dot2_rhs01(o * o, _bf(bd)) * (1.0 / RET_DK)
    gate = _silu(zr_ref[:, 5 * W:6 * W])
    o_ref[...] = _bf(o * lax.rsqrt(ms + NORM_EPS) * par_ref[1:2, :] * gate)


def _ret(zr, cos_t, sin_t, dmat, qw, kw, par, hm, bd, batch, steps):
    n = zr.shape[0]
    const2 = lambda b, j: (0, 0)
    return pl.pallas_call(
        _ret_kernel,
        grid=(batch, steps),
        in_specs=[pl.BlockSpec((T_BLK, ZR_W), lambda b, j: (b * steps + j, 0)),
                  pl.BlockSpec((T_BLK, RET_WIDTH), lambda b, j: (j, 0)),
                  pl.BlockSpec((T_BLK, RET_WIDTH), lambda b, j: (j, 0)),
                  pl.BlockSpec(dmat.shape, lambda b, j: (0, 0, 0)),
                  pl.BlockSpec(qw.shape, const2),
                  pl.BlockSpec(kw.shape, const2),
                  pl.BlockSpec(par.shape, const2),
                  pl.BlockSpec(hm.shape, const2),
                  pl.BlockSpec(bd.shape, const2)],
        out_specs=pl.BlockSpec((T_BLK, RET_WIDTH), lambda b, j: (b * steps + j, 0)),
        out_shape=jax.ShapeDtypeStruct((n, RET_WIDTH), BF16),
        scratch_shapes=[pltpu.VMEM((RET_WIDTH, RET_WIDTH), F32)],
        compiler_params=pltpu.CompilerParams(dimension_semantics=("arbitrary", "arbitrary"),
                                             vmem_limit_bytes=VMEM_LIMIT),
        name="retention",
    )(zr, cos_t, sin_t, dmat, qw, kw, par, hm, bd)


def _gdn_kernel(zg_ref, zab_ref, convw_ref, par_ref, eab_ref, gm_ref, bd_ref, tril_ref, hm_ref,
                o_ref, ext_ref, state_ref, obuf_ref):
    @pl.when(pl.program_id(1) == 0)
    def _():
        ext_ref[0:SUBLANES, :] = jnp.zeros((SUBLANES, 3 * GDN_WIDTH), F32)
        state_ref[...] = jnp.zeros_like(state_ref)

    W = GDN_WIDTH
    t = T_BLK
    u = zg_ref[:, 0:3 * W]
    ext_ref[SUBLANES:SUBLANES + t, :] = u
    conv = (convw_ref[3:4, :] * u
            + convw_ref[2:3, :] * ext_ref[SUBLANES - 1:SUBLANES - 1 + t, :]
            + convw_ref[1:2, :] * ext_ref[SUBLANES - 2:SUBLANES - 2 + t, :]
            + convw_ref[0:1, :] * ext_ref[SUBLANES - 3:SUBLANES - 3 + t, :])
    ext_ref[0:SUBLANES, :] = u[t - SUBLANES:t, :]
    qkv = _silu(conv)
    q = qkv[:, 0:W]
    k = qkv[:, W:2 * W]
    v = qkv[:, 2 * W:3 * W]

    bd = bd_ref[...]
    bdb = _bf(bd)
    qn = q * lax.rsqrt(_dot2_rhs01(q * q, bdb) + L2_EPS) * (GDN_DK ** -0.5)
    kn = k * lax.rsqrt(_dot2_rhs01(k * k, bdb) + L2_EPS)

    ab = zab_ref[...]
    a_exp = _dot2_rhs01(ab, eab_ref[:, 0:W])
    b_exp = _dot2_rhs01(ab, eab_ref[:, W:2 * W])
    beta = _sigmoid(b_exp)
    g = par_ref[0:1, :] * _softplus(a_exp + par_ref[1:2, :])
    gc = _dot2_lhs01(tril_ref[...], g)
    gl = _dot2_lhs01(bdb, g)
    eg = jnp.exp(gc)
    vb = v * beta
    kbeta = kn * beta * eg
    qdec = qn * eg
    kdec = kn * jnp.exp(gl - gc)

    n_chunks = t // GDN_CHUNK
    hms = [hm_ref[h:h + 1, :] for h in range(GDN_HEADS)]
    hms2 = [jnp.concatenate([m, m], axis=1) for m in hms]

    def chunk(a, c):
        return a[c * GDN_CHUNK:(c + 1) * GDN_CHUNK]

    def expand(y, masks):
        return jnp.concatenate([_bf(y * m) for m in masks], axis=0)

    def blockprod(x, y, masks):
        xb = _bf(x)
        return jnp.concatenate([_dot(chunk(xb, c), expand(chunk(y, c), masks))
                                for c in range(n_chunks)], axis=0)

    gc_row = _dot2_lhs01(bdb, gc * gm_ref[5])
    rel = jnp.exp(jnp.where(gm_ref[0] > 0, gc - gc_row, -jnp.inf))

    knbeta = kn * beta
    kq = [_dot_nt(_bf(jnp.concatenate([chunk(knbeta, c), chunk(qn, c)], axis=0)),
                  expand(chunk(kn, c), hms)) for c in range(n_chunks)]
    m = jnp.concatenate([r[0:GDN_CHUNK] for r in kq], axis=0) * rel
    qk = jnp.concatenate([r[GDN_CHUNK:] for r in kq], axis=0) * rel

    d = m * gm_ref[1]
    d2 = blockprod(d, d, hms)
    d4 = blockprod(d2, d2, hms)
    x = d2 - d - blockprod(d, d2, hms)
    x = x + d4 + blockprod(x, d4, hms)
    for lvl in range(2, 5):
        lo = m * gm_ref[lvl]
        y = lo + blockprod(x, lo, hms)
        x = x - (y + blockprod(y, x, hms))

    vk = jnp.concatenate([vb, kbeta], axis=1)
    wk = vk + blockprod(x, vk, hms2)
    w = wk[:, 0:W]
    kcum = wk[:, W:2 * W]
    ag = blockprod(qk, wk, hms2)
    a1 = ag[:, 0:W]
    qeff = qdec - ag[:, W:2 * W]

    for c in range(t // GDN_CHUNK):
        rows = slice(c * GDN_CHUNK, (c + 1) * GDN_CHUNK)
        st = state_ref[...]
        stb = _bf(st)
        vnew = w[rows] - _dot(_bf(kcum[rows]), stb)
        obuf_ref[rows, :] = _dot(_bf(qeff[rows]), stb) + a1[rows]
        upd = _dot_tn(_bf(kdec[rows]), _bf(vnew))
        last = eg[(c + 1) * GDN_CHUNK - 1:(c + 1) * GDN_CHUNK, :]
        state_ref[...] = st * last + bd * upd

    o = obuf_ref[...]
    ms = _dot2_rhs01(o * o, bdb) * (1.0 / GDN_DK)
    gate = _silu(zg_ref[:, 3 * W:4 * W])
    o_ref[...] = _bf(o * lax.rsqrt(ms + NORM_EPS) * par_ref[2:3, :] * gate)


def _gdn(zg, zab, convw, par, eab, gm, bd, tril, hm, batch, steps):
    n = zg.shape[0]
    const2 = lambda b, j: (0, 0)
    return pl.pallas_call(
        _gdn_kernel,
        grid=(batch, steps),
        in_specs=[pl.BlockSpec((T_BLK, ZG_W), lambda b, j: (b * steps + j, 0)),
                  pl.BlockSpec((T_BLK, ZAB_W), lambda b, j: (b * steps + j, 0)),
                  pl.BlockSpec(convw.shape, const2),
                  pl.BlockSpec(par.shape, const2),
                  pl.BlockSpec(eab.shape, const2),
                  pl.BlockSpec(gm.shape, lambda b, j: (0, 0, 0)),
                  pl.BlockSpec(bd.shape, const2),
                  pl.BlockSpec(tril.shape, const2),
                  pl.BlockSpec(hm.shape, const2)],
        out_specs=pl.BlockSpec((T_BLK, GDN_WIDTH), lambda b, j: (b * steps + j, 0)),
        out_shape=jax.ShapeDtypeStruct((n, GDN_WIDTH), BF16),
        scratch_shapes=[pltpu.VMEM((SUBLANES + T_BLK, 3 * GDN_WIDTH), F32),
                        pltpu.VMEM((GDN_WIDTH, GDN_WIDTH), F32),
                        pltpu.VMEM((T_BLK, GDN_WIDTH), F32)],
        compiler_params=pltpu.CompilerParams(dimension_semantics=("arbitrary", "arbitrary"),
                                             vmem_limit_bytes=VMEM_LIMIT),
        name="gdn",
    )(zg, zab, convw, par, eab, gm, bd, tril, hm)


def _outproj_kernel(ohg_ref, oret_ref, ogdn_ref, x_ref, mod_ref, g_ref, w_ref, rw_ref, rb_ref,
                    sl_ref, x1_ref, hn_ref, ri_ref, gates_ref, cnt_ref, run_ref):
    @pl.when(pl.program_id(0) == 0)
    def _():
        run_ref[...] = jnp.zeros_like(run_ref)

    y = (_dot(ohg_ref[...], w_ref[0:HG_WIDTH, :])
         + _dot(oret_ref[...], w_ref[HG_WIDTH:HG_WIDTH + RET_WIDTH, :])
         + _dot(ogdn_ref[...], w_ref[HG_WIDTH + RET_WIDTH:, :]))
    x1 = x_ref[...] + mod_ref[0, 0:1, :] * y
    x1_ref[...] = x1
    n = x1 * lax.rsqrt(jnp.mean(x1 * x1, axis=-1, keepdims=True) + NORM_EPS)
    hn = (n * g_ref[0:1, :]) * (1.0 + mod_ref[0, 1:2, :]) + mod_ref[0, 2:3, :]
    hn_ref[...] = _pack_halves(hn)

    lane = lax.broadcasted_iota(jnp.int32, (TM, LANES), 1)
    hn_hi, hn_lo = _split2(hn)
    rw_hi = rw_ref[:, 0:LANES]
    logits = (_dot(hn_hi, rw_hi) + _dot(hn_hi, rw_ref[:, LANES:2 * LANES]) + _dot(hn_lo, rw_hi)
              + rb_ref[0:1, :])
    work = jnp.where(lane < N_EXPERTS, logits, -jnp.inf)
    vals, idxs = [], []
    multihot = jnp.zeros((TM, LANES), F32)
    for _ in range(TOP_K):
        mx = jnp.max(work, axis=-1, keepdims=True)
        ix = jnp.min(jnp.where(work == mx, lane, LANES), axis=-1, keepdims=True)
        sel = lane == ix
        multihot = jnp.where(sel, 1.0, multihot)
        work = jnp.where(sel, -jnp.inf, work)
        vals.append(mx)
        idxs.append(ix)
    ex = [jnp.exp(vv - vals[0]) for vv in vals]
    den = ex[0] + ex[1] + ex[2] + ex[3]

    before = _dot(sl_ref[...], _bf(multihot)) + run_ref[0:1, :]
    run_ref[0:1, :] = run_ref[0:1, :] + jnp.sum(multihot, axis=0, keepdims=True)
    cnt_ref[...] = jnp.broadcast_to(run_ref[0:1, :], cnt_ref.shape)

    lane8 = lax.broadcasted_iota(jnp.int32, (TM, 2 * TOP_K), 1)
    ri = jnp.zeros((TM, 2 * TOP_K), jnp.int32)
    gt = jnp.zeros((TM, 2 * TOP_K), F32)
    for kk in range(TOP_K):
        rank = jnp.sum(jnp.where(lane == idxs[kk], before, 0.0), axis=-1, keepdims=True)
        ri = jnp.where(lane8 == kk, idxs[kk], ri)
        ri = jnp.where(lane8 == TOP_K + kk, rank.astype(jnp.int32), ri)
        gt = jnp.where(lane8 == kk, ex[kk] / den, gt)
    ri_ref[...] = ri
    gates_ref[...] = gt


def _outproj(ohg, oret, ogdn, x2d, modv, gpar, w_bf, rw, rb, sl, tiles_per_batch):
    n = x2d.shape[0]
    row = lambda i: (i, 0)
    const = lambda i: (0, 0)
    return pl.pallas_call(
        _outproj_kernel,
        grid=(n // TM,),
        in_specs=[pl.BlockSpec((TM, HG_WIDTH), row), pl.BlockSpec((TM, RET_WIDTH), row),
                  pl.BlockSpec((TM, GDN_WIDTH), row), pl.BlockSpec((TM, D_MODEL), row),
                  pl.BlockSpec((1, SUBLANES, D_MODEL), lambda i: (i // tiles_per_batch, 0, 0)),
                  pl.BlockSpec((SUBLANES, D_MODEL), const),
                  pl.BlockSpec((D_MODEL, D_MODEL), const),
                  pl.BlockSpec((D_MODEL, 2 * LANES), const),
                  pl.BlockSpec((SUBLANES, LANES), const),
                  pl.BlockSpec((TM, TM), const)],
        out_specs=[pl.BlockSpec((TM, D_MODEL), row), pl.BlockSpec((TM, D_MODEL // 2), row),
                   pl.BlockSpec((TM, 2 * TOP_K), row), pl.BlockSpec((TM, 2 * TOP_K), row),
                   pl.BlockSpec((SUBLANES, LANES), const)],
        out_shape=[jax.ShapeDtypeStruct((n, D_MODEL), F32),
                   jax.ShapeDtypeStruct((n, D_MODEL // 2), jnp.uint32),
                   jax.ShapeDtypeStruct((n, 2 * TOP_K), jnp.int32),
                   jax.ShapeDtypeStruct((n, 2 * TOP_K), F32),
                   jax.ShapeDtypeStruct((SUBLANES, LANES), F32)],
        scratch_shapes=[pltpu.VMEM((SUBLANES, LANES), F32)],
        compiler_params=pltpu.CompilerParams(dimension_semantics=("arbitrary",),
                                             vmem_limit_bytes=VMEM_LIMIT),
        name="outproj_router",
    )(ohg, oret, ogdn, x2d, modv, gpar, w_bf, rw, rb, sl)


def _expert_kernel(be_ref, nu_ref, x_ref, w1_ref, b1_ref, w2_ref, b2_ref, y_ref, w1b_ref, w2b_ref):
    i = pl.program_id(0)
    prev = be_ref[jnp.maximum(i - 1, 0)]
    fresh = jnp.logical_or(i == 0, be_ref[i] != prev)

    @pl.when(fresh)
    def _():
        w1b_ref[...] = _bf(w1_ref[0])
        w2b_ref[...] = _bf(w2_ref[0])

    @pl.when(i < nu_ref[0])
    def _():
        half = D_MODEL // 2
        x_lo, x_hi = _unpack_halves(x_ref[...])
        hid = (_dot(_bf(x_lo), w1b_ref[0:half, :]) + _dot(_bf(x_hi), w1b_ref[half:, :])
               + b1_ref[0])
        x_glu = jnp.minimum(hid[:, 0:D_FF], SWIGLU_LIMIT)
        x_lin = jnp.clip(hid[:, D_FF:], -SWIGLU_LIMIT, SWIGLU_LIMIT)
        act = x_glu * _sigmoid(SWIGLU_ALPHA * x_glu) * (x_lin + 1.0)
        y_ref[...] = _pack_halves(_dot(_bf(act), w2b_ref[...]) + b2_ref[0])


def _experts(block_e, n_used, xs, w1, b1, w2, b2):
    p = xs.shape[0]
    n_blocks = p // R_BLK
    ne = w1.shape[0] * w1.shape[1]
    w1 = w1.reshape(ne, D_MODEL, 2 * D_FF)
    w2 = w2.reshape(ne, D_FF, D_MODEL)
    rowmap = lambda i, be, nu: (jnp.minimum(i, nu[0] - 1), 0)
    emap = lambda i, be, nu: (be[i], 0, 0)
    grid_spec = pltpu.PrefetchScalarGridSpec(
        num_scalar_prefetch=2,
        grid=(n_blocks,),
        in_specs=[pl.BlockSpec((R_BLK, D_MODEL // 2), rowmap),
                  pl.BlockSpec((1, D_MODEL, 2 * D_FF), emap),
                  pl.BlockSpec((1, 1, 2 * D_FF), emap),
                  pl.BlockSpec((1, D_FF, D_MODEL), emap),
                  pl.BlockSpec((1, 1, D_MODEL), emap)],
        out_specs=pl.BlockSpec((R_BLK, D_MODEL // 2), rowmap),
        scratch_shapes=[pltpu.VMEM((D_MODEL, 2 * D_FF), BF16), pltpu.VMEM((D_FF, D_MODEL), BF16)],
    )
    return pl.pallas_call(
        _expert_kernel,
        grid_spec=grid_spec,
        out_shape=jax.ShapeDtypeStruct((p, D_MODEL // 2), jnp.uint32),
        compiler_params=pltpu.CompilerParams(dimension_semantics=("arbitrary",),
                                             vmem_limit_bytes=VMEM_LIMIT),
        name="experts",
    )(block_e, n_used, xs, w1, b1.reshape(ne, 1, 2 * D_FF), w2, b2.reshape(ne, 1, D_MODEL))


def _sc_mesh():
    return plsc.VectorSubcoreMesh(core_axis_name="c", subcore_axis_name="s")


def _sc_scatter_rows(x, pos, p_rows):
    mesh = _sc_mesh()
    n, w = x.shape
    per_worker = n // (mesh.num_cores * mesh.num_subcores)
    assert per_worker % SC_ROWS == 0 and pos.shape == (TOP_K, n)

    @functools.partial(
        pl.kernel, out_type=jax.ShapeDtypeStruct((p_rows, w), x.dtype), mesh=mesh,
        scratch_types=[pltpu.VMEM((SC_ROWS,), jnp.int32)] * TOP_K
        + [pltpu.VMEM((SC_ROWS, w), x.dtype), pltpu.SemaphoreType.DMA],
        name="dispatch_rows")
    def scatter(x_hbm, p_hbm, o_hbm, i0, i1, i2, i3, rows_v, sem):
        idx = (i0, i1, i2, i3)
        worker = lax.axis_index("s") * mesh.num_cores + lax.axis_index("c")

        @pl.loop(0, per_worker // SC_ROWS)
        def _(g):
            base = pl.multiple_of(worker * per_worker + g * SC_ROWS, SC_ROWS)
            pltpu.sync_copy(x_hbm.at[pl.ds(base, SC_ROWS)], rows_v)
            for kk in range(TOP_K):
                pltpu.sync_copy(p_hbm.at[kk, pl.ds(base, SC_ROWS)], idx[kk])
            copies = [pltpu.async_copy(rows_v, o_hbm.at[idx[kk]], sem) for kk in range(TOP_K)]
            for cp in copies:
                cp.wait()

    return scatter(x, pos)


def _sc_gather_rows(table, idx):
    mesh = _sc_mesh()
    n_idx = idx.shape[0]
    w = table.shape[1]
    per_worker = n_idx // (mesh.num_cores * mesh.num_subcores)
    assert per_worker % SC_ROWS == 0

    @functools.partial(
        pl.kernel, out_type=jax.ShapeDtypeStruct((n_idx, w), table.dtype), mesh=mesh,
        scratch_types=[pltpu.VMEM((SC_ROWS,), jnp.int32), pltpu.VMEM((SC_ROWS, w), table.dtype),
                       pltpu.SemaphoreType.DMA],
        name="combine_rows")
    def gather(t_hbm, i_hbm, o_hbm, idx_v, rows_v, sem):
        worker = lax.axis_index("s") * mesh.num_cores + lax.axis_index("c")

        @pl.loop(0, per_worker // SC_ROWS)
        def _(g):
            base = pl.multiple_of(worker * per_worker + g * SC_ROWS, SC_ROWS)
            pltpu.sync_copy(i_hbm.at[pl.ds(base, SC_ROWS)], idx_v)
            pltpu.async_copy(t_hbm.at[idx_v], rows_v, sem).wait()
            pltpu.sync_copy(rows_v, o_hbm.at[pl.ds(base, SC_ROWS)])

    return gather(table, idx)


def _combine_kernel(x_ref, y0_ref, y1_ref, y2_ref, y3_ref, gates_ref, mod_ref, g_ref, o_ref, *, final):
    gt = gates_ref[...]
    half = D_MODEL // 2
    acc_lo, acc_hi = None, None
    for kk, y_ref in enumerate((y0_ref, y1_ref, y2_ref, y3_ref)):
        lo, hi = _unpack_halves(y_ref[0])
        g = gt[:, kk:kk + 1]
        acc_lo = g * lo if acc_lo is None else acc_lo + g * lo
        acc_hi = g * hi if acc_hi is None else acc_hi + g * hi
    x_lo = x_ref[:, 0:half] + mod_ref[0, 3:4, 0:half] * acc_lo
    x_hi = x_ref[:, half:] + mod_ref[0, 3:4, half:] * acc_hi
    if final:
        ssq = (jnp.sum(x_lo * x_lo, axis=-1, keepdims=True)
               + jnp.sum(x_hi * x_hi, axis=-1, keepdims=True))
        r = lax.rsqrt(ssq * (1.0 / D_MODEL) + NORM_EPS)
        x_lo = x_lo * r * g_ref[1:2, 0:half]
        x_hi = x_hi * r * g_ref[1:2, half:]
    o_ref[:, 0:half] = x_lo
    o_ref[:, half:] = x_hi


def _combine(x1, yg, gates, modv, gpar, tiles_per_batch, final):
    n = x1.shape[0]
    row = lambda i: (i, 0)
    ysp = [pl.BlockSpec((1, TM, D_MODEL // 2), (lambda i, kk=kk: (kk, i, 0))) for kk in range(TOP_K)]
    return pl.pallas_call(
        functools.partial(_combine_kernel, final=final),
        grid=(n // TM,),
        in_specs=[pl.BlockSpec((TM, D_MODEL), row)] + ysp + [
            pl.BlockSpec((TM, 2 * TOP_K), row),
            pl.BlockSpec((1, SUBLANES, D_MODEL), lambda i: (i // tiles_per_batch, 0, 0)),
            pl.BlockSpec((SUBLANES, D_MODEL), lambda i: (0, 0))],
        out_specs=pl.BlockSpec((TM, D_MODEL), row),
        out_shape=jax.ShapeDtypeStruct((n, D_MODEL), F32),
        compiler_params=pltpu.CompilerParams(dimension_semantics=("arbitrary",),
                                             vmem_limit_bytes=VMEM_LIMIT),
        name="combine",
    )(x1, yg, yg, yg, yg, gates, modv, gpar)


def _rot_cols(w):
    d = w.shape[0]
    w4 = w.reshape(d, RET_HEADS, 2, RET_DK // 2)
    return jnp.concatenate([-w4[:, :, 1:2], w4[:, :, 0:1]], axis=2).reshape(d, RET_WIDTH)


def _widen_w_in(w):
    o = 0
    hg = w[:, o:o + 4 * HG_WIDTH]; o += 4 * HG_WIDTH
    rq = w[:, o:o + RET_WIDTH]; o += RET_WIDTH
    rk = w[:, o:o + RET_WIDTH]; o += RET_WIDTH
    rv = w[:, o:o + RET_WIDTH]; o += RET_WIDTH
    rg = w[:, o:o + RET_WIDTH]; o += RET_WIDTH
    gd = w[:, o:o + 4 * GDN_WIDTH]; o += 4 * GDN_WIDTH
    gab = w[:, o:o + 2 * GDN_HEADS]
    pad = jnp.zeros((w.shape[0], ZAB_W - 2 * GDN_HEADS), w.dtype)
    return _bf(jnp.concatenate([hg, rq, _rot_cols(rq), rk, _rot_cols(rk), rv, rg, gd, gab, pad], axis=1))


def _pad_rows(a, rows=SUBLANES):
    return jnp.concatenate([a, jnp.zeros((rows - a.shape[0],) + a.shape[1:], a.dtype)], axis=0)


def kernel(x, c, ada_w, ada_b, norm1_g, norm2_g, w_in, w_out, hg_lb_logits, hg_norm_g, ret_norm_g,
           gdn_conv_w, gdn_A_log, gdn_dt_bias, gdn_norm_g, router_w, router_b, exp_w1, exp_b1,
           exp_w2, exp_b2, final_norm_g):
    batch, seq, d = x.shape
    depth = ada_w.shape[0]
    n = batch * seq
    steps = seq // T_BLK
    tiles_per_batch = seq // TM
    nk = n * TOP_K
    n_blocks = nk // R_BLK + N_EXPERTS
    p_rows = n_blocks * R_BLK

    lv_np = _level_matrix(T_BLK)
    lv = jnp.asarray(_level_matrix(HG_SUB))
    f_mat = jnp.asarray(_hgrn_exponent_matrix(HG_SUB), BF16)
    gm = jnp.asarray(_gdn_masks(T_BLK))
    bd = jnp.asarray(_block_diag_mask(T_BLK, GDN_CHUNK))
    tril = jnp.asarray(_block_diag_mask(T_BLK, GDN_CHUNK) * (lv_np >= 0), BF16)
    hm = jnp.asarray(_head_masks(RET_WIDTH, RET_HEADS))
    sl = jnp.asarray(np.tril(np.ones((TM, TM), np.float32), -1), BF16)
    eab_np = np.zeros((LANES, 2 * GDN_WIDTH), np.float32)
    for h in range(GDN_HEADS):
        eab_np[h, h * GDN_DK:(h + 1) * GDN_DK] = 1.0
        eab_np[GDN_HEADS + h, GDN_WIDTH + h * GDN_DK:GDN_WIDTH + (h + 1) * GDN_DK] = 1.0
    eab = jnp.asarray(eab_np, BF16)

    half = RET_DK // 2
    inv = ROPE_BASE ** (-jnp.linspace(0.0, 1.0, half, dtype=F32))
    ang = jnp.arange(seq, dtype=F32)[:, None] * inv[None, :]
    cos_t = jnp.tile(jnp.cos(ang), (1, 2 * RET_HEADS))
    sin_t = jnp.tile(jnp.sin(ang), (1, 2 * RET_HEADS))
    log_g = jnp.log1p(-jnp.exp2(-5.0 - jnp.arange(RET_HEADS, dtype=F32)))
    jj = jnp.arange(T_BLK, dtype=F32)
    diff = jj[:, None] - jj[None, :]
    dmat = jnp.where(diff[None] >= 0, jnp.exp(diff[None] * log_g[:, None, None]), 0.0)
    lg_lane = jnp.repeat(log_g, RET_DK)[None, :]
    qw = jnp.exp(lg_lane * (jj[:, None] + 1.0))
    kw = jnp.exp(lg_lane * (T_BLK - 1.0 - jj[:, None]))
    cdec = jnp.exp(T_BLK * lg_lane)

    lb = jnp.cumsum(jax.nn.softmax(hg_lb_logits.astype(F32), axis=0), axis=0)
    lb = jnp.maximum(lb - lb[0], 0.0)
    c_pad = _pad_rows(c.astype(F32))
    mod = _adaln(c_pad, ada_w, ada_b)[:, :batch, :]

    x2d = x.reshape(n, d)
    out = None
    for l in range(depth):
        sh1, sc1, gt1, sh2, sc2, gt2 = [mod[l][:, i * d:(i + 1) * d] for i in range(6)]
        zeros = jnp.zeros_like(sh1)
        modv_a = jnp.stack([sc1, sh1, zeros, zeros, zeros, zeros, zeros, zeros], axis=1)
        modv_c = jnp.stack([gt1, sc2, sh2, gt2, zeros, zeros, zeros, zeros], axis=1)
        gpar_a = _pad_rows(norm1_g[l][None, :])
        gpar_c = _pad_rows(jnp.stack([norm2_g[l], final_norm_g], axis=0))

        zh, zr, zg, zab = _inproj(x2d, modv_a, gpar_a, _widen_w_in(w_in[l]), tiles_per_batch)

        hg_par = _pad_rows(jnp.stack([jnp.log(lb[l]), jnp.log1p(-lb[l]), 1.0 - lb[l],
                                      hg_norm_g[l].reshape(-1)], axis=0))
        o_hg = _hgrn(zh, f_mat, lv, hg_par, batch, steps)

        ret_par = _pad_rows(jnp.concatenate([cdec, ret_norm_g[l].reshape(1, -1)], axis=0))
        o_ret = _ret(zr, cos_t, sin_t, dmat, qw, kw, ret_par, hm, bd, batch, steps)

        gdn_par = _pad_rows(jnp.stack([jnp.repeat(-jnp.exp(gdn_A_log[l].astype(F32)), GDN_DK),
                                       jnp.repeat(gdn_dt_bias[l].astype(F32), GDN_DK),
                                       gdn_norm_g[l].reshape(-1)], axis=0))
        o_gdn = _gdn(zg, zab, _pad_rows(gdn_conv_w[l].astype(F32)), gdn_par, eab, gm, bd, tril, hm,
                     batch, steps)

        rw_f = jnp.concatenate([router_w[l], jnp.zeros((d, LANES - N_EXPERTS), F32)], axis=1)
        rw = jnp.concatenate(_split2(rw_f), axis=1)
        rb =_pad_rows(jnp.concatenate([router_b[l], jnp.zeros((LANES - N_EXPERTS,), F32)])[None, :])
        x1, hn2, ri, gates, cnt = _outproj(o_hg, o_ret, o_gdn, x2d, modv_c, gpar_c, _bf(w_out[l]),
                                           rw, rb, sl, tiles_per_batch)

        counts = cnt[0, :N_EXPERTS].astype(jnp.int32)
        padded = (counts + R_BLK - 1) // R_BLK * R_BLK
        pend = jnp.cumsum(padded)
        pstart = pend - padded
        n_used = (pend[-1] // R_BLK).astype(jnp.int32)
        blk_start = jnp.arange(n_blocks, dtype=jnp.int32) * R_BLK
        blk_start = jnp.minimum(blk_start, pend[-1] - R_BLK)
        block_e = jnp.sum(blk_start[:, None] >= pend[None, :], axis=1).astype(jnp.int32)
        eid = jnp.arange(N_EXPERTS, dtype=jnp.int32)
        pos = ri[:, TOP_K:] + jnp.sum(jnp.where(ri[:, :TOP_K, None] == eid, pstart, 0), axis=-1)

        pos_t = pos.T
        xs = _sc_scatter_rows(hn2, pos_t, p_rows)
        y = _experts(block_e + l * N_EXPERTS, n_used.reshape(1), xs, exp_w1, exp_b1, exp_w2, exp_b2)
        yg = _sc_gather_rows(y, pos_t.reshape(-1)).reshape(TOP_K, n, d // 2)
        x2d = _combine(x1, yg, gates, modv_c, gpar_c, tiles_per_batch, final=(l == depth - 1))
    return x2d.reshape(batch, seq, d)
```

```python
import functools
import math

import numpy as np
import jax
import jax.numpy as jnp
from jax import lax
from jax.experimental import pallas as pl
from jax.experimental.pallas import tpu as pltpu
from jax.experimental.pallas import tpu_sc as plsc

F32 = jnp.float32
BF16 = jnp.bfloat16

D_MODEL = 1024
HG_HEADS, HG_DK = 4, 128
HG_WIDTH = HG_HEADS * HG_DK
RET_HEADS, RET_DK = 4, 64
RET_WIDTH = RET_HEADS * RET_DK
GDN_HEADS, GDN_DK = 4, 64
GDN_WIDTH = GDN_HEADS * GDN_DK
CONV_K = 4
ROPE_BASE = 10000.0
N_EXPERTS = 32
TOP_K = 4
D_FF = D_MODEL
SWIGLU_ALPHA = 1.702
SWIGLU_LIMIT = 7.0
NORM_EPS = 1e-6
L2_EPS = 1e-6
GDN_CHUNK = 64

LANES = 128
SUBLANES = 8
VMEM_LIMIT = 56 * 1024 * 1024

T_BLK = 256
TM = 256
R_BLK = 512
SC_ROWS = 128
HG_SUB = 128
HG_LEVELS = 7

ZH_W = 4 * HG_WIDTH
ZR_W = 6 * RET_WIDTH
ZG_W = 4 * GDN_WIDTH
ZAB_W = LANES
Z_W = ZH_W + ZR_W + ZG_W + ZAB_W


def _dot(a, b):
    return jnp.dot(a, b, preferred_element_type=F32)


def _dot_nt(a, b):
    return lax.dot_general(a, b, (((1,), (1,)), ((), ())), preferred_element_type=F32)


def _dot_tn(a, b):
    return lax.dot_general(a, b, (((0,), (0,)), ((), ())), preferred_element_type=F32)


def _split2(x):
    hi = x.astype(BF16)
    return hi, (x - hi.astype(F32)).astype(BF16)


def _dot2_lhs01(c, x):
    hi, lo = _split2(x)
    return _dot(c, hi) + _dot(c, lo)


def _dot2_rhs01(x, c):
    hi, lo = _split2(x)
    return _dot(hi, c) + _dot(lo, c)


def _sigmoid(x):
    return 1.0 / (1.0 + jnp.exp(-x))


def _silu(x):
    return x * _sigmoid(x)


def _softplus(x):
    return jnp.maximum(x, 0.0) + jnp.log1p(jnp.exp(-jnp.abs(x)))


def _bf(x):
    return x.astype(BF16)


def _pack_halves(x):
    w = x.shape[1] // 2
    bits = lax.bitcast_convert_type(_bf(x).astype(F32), jnp.uint32)
    return (bits[:, :w] >> 16) | bits[:, w:]


def _unpack_halves(p):
    lo = lax.bitcast_convert_type(p << 16, F32)
    hi = lax.bitcast_convert_type(p & jnp.uint32(0xFFFF0000), F32)
    return lo, hi


def _level_matrix(t):
    i = np.arange(t)[:, None]
    j = np.arange(t)[None, :]
    x = i ^ j
    lv = np.floor(np.log2(np.maximum(x, 1))).astype(np.int32)
    lv = np.where(i == j, int(math.log2(t)), lv)
    lv = np.where(i < j, -1, lv)
    return lv.astype(np.int32)


def _hgrn_exponent_matrix(t):
    n_lev = int(math.log2(t))
    f = np.zeros((2 + n_lev, t, t), np.float32)
    u = np.arange(t)[None, :]
    r = np.arange(t)[:, None]
    f[0] = (u <= r)
    f[1] = (u > r)
    for l in range(n_lev):
        h = 1 << l
        base = (r // (2 * h)) * (2 * h)
        mid = base + h
        upper = (r - base) >= h
        f[2 + l] = np.where(upper, (u >= mid) & (u <= r), (u > r) & (u < mid))
    return f.reshape((2 + n_lev) * t, t)


def _gdn_masks(t):
    lv = _level_matrix(GDN_CHUNK)
    top = int(math.log2(GDN_CHUNK))
    incl = lv >= 0
    d8 = (lv >= 0) & (lv <= 2)
    merges = [(lv == l) for l in range(3, top)]
    eye = lv == top
    pats = np.stack([incl, d8] + merges + [eye]).astype(np.float32)
    return np.tile(pats, (1, t // GDN_CHUNK, GDN_HEADS))


def _block_diag_mask(t, blk):
    i = np.arange(t)
    return (i[:, None] // blk == i[None, :] // blk).astype(np.float32)


def _head_masks(width, heads):
    lane = np.arange(width)[None, :]
    m = np.zeros((SUBLANES, width), np.float32)
    for h in range(heads):
        m[h] = (lane // (width // heads) == h)[0]
    return m


def _adaln_kernel(c_ref, w_ref, b_ref, o_ref):
    cond = _silu(c_ref[...])
    o_ref[0] = jnp.dot(cond, w_ref[0], preferred_element_type=F32,
                       precision=lax.Precision.HIGHEST) + b_ref[0]


def _adaln(c_pad, ada_w, ada_b):
    depth, d, n6 = ada_w.shape
    tn = n6 // 4
    return pl.pallas_call(
        _adaln_kernel,
        grid=(depth, n6 // tn),
        in_specs=[pl.BlockSpec((SUBLANES, d), lambda l, j: (0, 0)),
                  pl.BlockSpec((1, d, tn), lambda l, j: (l, 0, j)),
                  pl.BlockSpec((1, 1, tn), lambda l, j: (l, 0, j))],
        out_specs=pl.BlockSpec((1, SUBLANES, tn), lambda l, j: (l, 0, j)),
        out_shape=jax.ShapeDtypeStruct((depth, SUBLANES, n6), F32),
        compiler_params=pltpu.CompilerParams(vmem_limit_bytes=VMEM_LIMIT),
        name="adaln",
    )(c_pad, ada_w, ada_b.reshape(depth, 1, n6))


def _inproj_kernel(x_ref, mod_ref, g_ref, w_ref, zh_ref, zr_ref, zg_ref, zab_ref):
    x = x_ref[...]
    y = x * lax.rsqrt(jnp.mean(x * x, axis=-1, keepdims=True) + NORM_EPS)
    hn = (y * g_ref[0:1, :]) * (1.0 + mod_ref[0, 0:1, :]) + mod_ref[0, 1:2, :]
    hb = _bf(hn)
    zh_ref[...] = _dot(hb, w_ref[:, 0:ZH_W])
    zr_ref[...] = _dot(hb, w_ref[:, ZH_W:ZH_W + ZR_W])
    zg_ref[...] = _dot(hb, w_ref[:, ZH_W + ZR_W:ZH_W + ZR_W + ZG_W])
    zab_ref[...] = _dot(hb, w_ref[:, ZH_W + ZR_W + ZG_W:Z_W])


def _inproj(x2d, modv, gpar, w_bf, tiles_per_batch):
    n = x2d.shape[0]
    row = lambda i: (i, 0)
    return pl.pallas_call(
        _inproj_kernel,
        grid=(n // TM,),
        in_specs=[pl.BlockSpec((TM, D_MODEL), row),
                  pl.BlockSpec((1, SUBLANES, D_MODEL), lambda i: (i // tiles_per_batch, 0, 0)),
                  pl.BlockSpec((SUBLANES, D_MODEL), lambda i: (0, 0)),
                  pl.BlockSpec((D_MODEL, Z_W), lambda i: (0, 0))],
        out_specs=[pl.BlockSpec((TM, ZH_W), row), pl.BlockSpec((TM, ZR_W), row),
                   pl.BlockSpec((TM, ZG_W), row), pl.BlockSpec((TM, ZAB_W), row)],
        out_shape=[jax.ShapeDtypeStruct((n, ZH_W), F32), jax.ShapeDtypeStruct((n, ZR_W), F32),
                   jax.ShapeDtypeStruct((n, ZG_W), F32), jax.ShapeDtypeStruct((n, ZAB_W), F32)],
        compiler_params=pltpu.CompilerParams(dimension_semantics=("arbitrary",),
                                             vmem_limit_bytes=VMEM_LIMIT),
        name="inproj",
    )(x2d, modv, gpar, w_bf)


def _hgrn_kernel(zh_ref, f_ref, lv_ref, par_ref, o_ref, state_ref, ex_ref):

    W = HG_WIDTH
    hq = zh_ref[:, 0:W]
    hf = zh_ref[:, W:2 * W]
    loglb = par_ref[0:1, :]
    log1mlb = par_ref[1:2, :]
    onemlb = par_ref[2:3, :]

    q = _silu(hq)
    e = jnp.exp(-jnp.abs(hf))
    inv = 1.0 / (1.0 + e)
    k = onemlb * (jnp.where(hf >= 0, e, 1.0) * inv)
    logsig = jnp.minimum(hf, 0.0) - jnp.log1p(e)
    c = log1mlb + logsig
    lf = jnp.maximum(loglb, c) + jnp.log1p(jnp.exp(-jnp.abs(loglb - c)))

    lv = lv_ref[...]
    t = HG_SUB
    for sb in range(T_BLK // HG_SUB):
        rows = slice(sb * t, (sb + 1) * t)
        ex_ref[sb] = jnp.exp(_dot2_lhs01(f_ref[...], lf[rows]))
        for h in range(HG_HEADS):
            cs = slice(h * HG_DK, (h + 1) * HG_DK)
            qh = q[rows, cs]
            kh = k[rows, cs]
            vh = _bf(zh_ref[rows, 2 * W + h * HG_DK:2 * W + (h + 1) * HG_DK])
            s = jnp.where(lv == HG_LEVELS, _dot_nt(_bf(qh), _bf(kh)), 0.0)
            for l in range(HG_LEVELS):
                el = ex_ref[sb, (2 + l) * t:(3 + l) * t, cs]
                s = jnp.where(lv == l, _dot_nt(_bf(qh * el), _bf(kh * el)), s)
            eb = ex_ref[sb, 0:t, cs]
            ebl = ex_ref[sb, t:2 * t, cs]
            st = state_ref[h]
            o = _dot(_bf(s), vh) + _dot_nt(_bf(qh * eb), _bf(st))
            upd = _dot_tn(vh, _bf(kh * ebl))
            state_ref[h] = st * ex_ref[sb, t - 1:t, cs] + upd
            ms = jnp.mean(o * o, axis=-1, keepdims=True)
            gate = _silu(zh_ref[rows, 3 * W + h * HG_DK:3 * W + (h + 1) * HG_DK])
            o_ref[rows, cs] = _bf(o * lax.rsqrt(ms + NORM_EPS) * par_ref[3:4, cs] * gate)


def _ret_kernel(zr_ref, cos_ref, sin_ref, dmat_ref, qw_ref, kw_ref, par_ref, hm_ref, bd_ref,
                o_ref, state_ref):
    W = RET_WIDTH
    cos = cos_ref[...]
    sin = sin_ref[...]
    q = zr_ref[:, 0:W] * cos + zr_ref[:, W:2 * W] * sin
    k = (zr_ref[:, 2 * W:3 * W] * cos + zr_ref[:, 3 * W:4 * W] * sin) * (RET_DK ** -0.5)
    v = zr_ref[:, 4 * W:5 * W]
    kb = _bf(k)
    bd = bd_ref[...]

    s_parts = []
    v_parts = []
    for h in range(RET_HEADS):
        hm = hm_ref[h:h + 1, :]
        s_parts.append(_bf(_dot_nt(_bf(q * hm), kb) * dmat_ref[h]))
        v_parts.append(_bf(v * hm))
    st = state_ref[...]
    o = (_dot(jnp.concatenate(s_parts, axis=1), jnp.concatenate(v_parts, axis=0))
         + _dot(_bf(q * qw_ref[...]), _bf(st)))
    kv = _dot_tn(_bf(k * kw_ref[...]), _bf(v))
    state_ref[...] = st * par_ref[0:1, :] + bd * kv

    ms = _dot2_rhs01(o * o, _bf(bd)) * (1.0 / RET_DK)
    gate = _silu(zr_ref[:, 5 * W:6 * W])
    o_ref[...] = _bf(o * lax.rsqrt(ms + NORM_EPS) * par_ref[1:2, :] * gate)


def _gdn_kernel(zg_ref, zab_ref, convw_ref, par_ref, eab_ref, gm_ref, bd_ref, tril_ref, hm_ref,
                o_ref, ext_ref, state_ref, obuf_ref):
    W = GDN_WIDTH
    t = T_BLK
    u = zg_ref[:, 0:3 * W]
    ext_ref[SUBLANES:SUBLANES + t, :] = u
    conv = (convw_ref[3:4, :] * u
            + convw_ref[2:3, :] * ext_ref[SUBLANES - 1:SUBLANES - 1 + t, :]
            + convw_ref[1:2, :] * ext_ref[SUBLANES - 2:SUBLANES - 2 + t, :]
            + convw_ref[0:1, :] * ext_ref[SUBLANES - 3:SUBLANES - 3 + t, :])
    ext_ref[0:SUBLANES, :] = u[t - SUBLANES:t, :]
    qkv = _silu(conv)
    q = qkv[:, 0:W]
    k = qkv[:, W:2 * W]
    v = qkv[:, 2 * W:3 * W]

    bd = bd_ref[...]
    bdb = _bf(bd)
    qn = q * lax.rsqrt(_dot2_rhs01(q * q, bdb) + L2_EPS) * (GDN_DK ** -0.5)
    kn = k * lax.rsqrt(_dot2_rhs01(k * k, bdb) + L2_EPS)

    ab = zab_ref[...]
    a_exp = _dot2_rhs01(ab, eab_ref[:, 0:W])
    b_exp = _dot2_rhs01(ab, eab_ref[:, W:2 * W])
    beta = _sigmoid(b_exp)
    g = par_ref[0:1, :] * _softplus(a_exp + par_ref[1:2, :])
    gc = _dot2_lhs01(tril_ref[...], g)
    gl = _dot2_lhs01(bdb, g)
    eg = jnp.exp(gc)
    vb = v * beta
    kbeta = kn * beta * eg
    qdec = qn * eg
    kdec = kn * jnp.exp(gl - gc)

    n_chunks = t // GDN_CHUNK
    hms = [hm_ref[h:h + 1, :] for h in range(GDN_HEADS)]
    hms2 = [jnp.concatenate([m, m], axis=1) for m in hms]

    def chunk(a, c):
        return a[c * GDN_CHUNK:(c + 1) * GDN_CHUNK]

    def expand(y, masks):
        return jnp.concatenate([_bf(y * m) for m in masks], axis=0)

    def blockprod(x, y, masks):
        xb = _bf(x)
        return jnp.concatenate([_dot(chunk(xb, c), expand(chunk(y, c), masks))
                                for c in range(n_chunks)], axis=0)

    gc_row = _dot2_lhs01(bdb, gc * gm_ref[5])
    rel = jnp.exp(jnp.where(gm_ref[0] > 0, gc - gc_row, -jnp.inf))

    knbeta = kn * beta
    kq = [_dot_nt(_bf(jnp.concatenate([chunk(knbeta, c), chunk(qn, c)], axis=0)),
                  expand(chunk(kn, c), hms)) for c in range(n_chunks)]
    m = jnp.concatenate([r[0:GDN_CHUNK] for r in kq], axis=0) * rel
    qk = jnp.concatenate([r[GDN_CHUNK:] for r in kq], axis=0) * rel

    d = m * gm_ref[1]
    d2 = blockprod(d, d, hms)
    d4 = blockprod(d2, d2, hms)
    x = d2 - d - blockprod(d, d2, hms)
    x = x + d4 + blockprod(x, d4, hms)
    for lvl in range(2, 5):
        lo = m * gm_ref[lvl]
        y = lo + blockprod(x, lo, hms)
        x = x - (y + blockprod(y, x, hms))

    vk = jnp.concatenate([vb, kbeta], axis=1)
    wk = vk + blockprod(x, vk, hms2)
    w = wk[:, 0:W]
    kcum = wk[:, W:2 * W]
    ag = blockprod(qk, wk, hms2)
    a1 = ag[:, 0:W]
    qeff = qdec - ag[:, W:2 * W]

    for c in range(t // GDN_CHUNK):
        rows = slice(c * GDN_CHUNK, (c + 1) * GDN_CHUNK)
        st = state_ref[...]
        stb = _bf(st)
        vnew = w[rows] - _dot(_bf(kcum[rows]), stb)
        obuf_ref[rows, :] = _dot(_bf(qeff[rows]), stb) + a1[rows]
        upd = _dot_tn(_bf(kdec[rows]), _bf(vnew))
        last = eg[(c + 1) * GDN_CHUNK - 1:(c + 1) * GDN_CHUNK, :]
        state_ref[...] = st * last + bd * upd

    o = obuf_ref[...]
    ms = _dot2_rhs01(o * o, bdb) * (1.0 / GDN_DK)
    gate = _silu(zg_ref[:, 3 * W:4 * W])
    o_ref[...] = _bf(o * lax.rsqrt(ms + NORM_EPS) * par_ref[2:3, :] * gate)


def _mixers_kernel(zh_ref, f_ref, lv_ref, hpar_ref,
                   zr_ref, cos_ref, sin_ref, dmat_ref, qw_ref, kw_ref, rpar_ref,
                   zg_ref, zab_ref, convw_ref, gpar_ref, eab_ref, gm_ref, tril_ref, hm_ref, bd_ref,
                   ohg_ref, oret_ref, ogdn_ref,
                   hstate_ref, hex_ref, rstate_ref, gext_ref, gstate_ref, gobuf_ref):
    @pl.when(pl.program_id(1) == 0)
    def _():
        hstate_ref[...] = jnp.zeros_like(hstate_ref)
        rstate_ref[...] = jnp.zeros_like(rstate_ref)
        gstate_ref[...] = jnp.zeros_like(gstate_ref)
        gext_ref[0:SUBLANES, :] = jnp.zeros((SUBLANES, 3 * GDN_WIDTH), F32)

    _hgrn_kernel(zh_ref, f_ref, lv_ref, hpar_ref, ohg_ref, hstate_ref, hex_ref)
    _ret_kernel(zr_ref, cos_ref, sin_ref, dmat_ref, qw_ref, kw_ref, rpar_ref, hm_ref, bd_ref,
                oret_ref, rstate_ref)
    _gdn_kernel(zg_ref, zab_ref, convw_ref, gpar_ref, eab_ref, gm_ref, bd_ref, tril_ref, hm_ref,
                ogdn_ref, gext_ref, gstate_ref, gobuf_ref)


def _mixers(zh, zr, zg, zab, hg_tabs, ret_tabs, gdn_tabs, hm, bd, batch, steps):
    n = zh.shape[0]
    seq_row = lambda b, j: (b * steps + j, 0)

    def const(a):
        return pl.BlockSpec(a.shape, lambda b, j, nd=a.ndim: (0,) * nd)

    def rows(width):
        return pl.BlockSpec((T_BLK, width), seq_row)

    pos_rows = pl.BlockSpec((T_BLK, RET_WIDTH), lambda b, j: (j, 0))
    cos_t, sin_t = ret_tabs[0], ret_tabs[1]
    in_specs = ([rows(ZH_W)] + [const(a) for a in hg_tabs]
                + [rows(ZR_W), pos_rows, pos_rows] + [const(a) for a in ret_tabs[2:]]
                + [rows(ZG_W), rows(ZAB_W)] + [const(a) for a in gdn_tabs] + [const(hm), const(bd)])
    return pl.pallas_call(
        _mixers_kernel,
        grid=(batch, steps),
        in_specs=in_specs,
        out_specs=[rows(HG_WIDTH), rows(RET_WIDTH), rows(GDN_WIDTH)],
        out_shape=[jax.ShapeDtypeStruct((n, HG_WIDTH), BF16), jax.ShapeDtypeStruct((n, RET_WIDTH), BF16),
                   jax.ShapeDtypeStruct((n, GDN_WIDTH), BF16)],
        scratch_shapes=[pltpu.VMEM((HG_HEADS, HG_DK, HG_DK), F32),
                        pltpu.VMEM((T_BLK // HG_SUB, (2 + HG_LEVELS) * HG_SUB, HG_WIDTH), F32),
                        pltpu.VMEM((RET_WIDTH, RET_WIDTH), F32),
                        pltpu.VMEM((SUBLANES + T_BLK, 3 * GDN_WIDTH), F32),
                        pltpu.VMEM((GDN_WIDTH, GDN_WIDTH), F32),
                        pltpu.VMEM((T_BLK, GDN_WIDTH), F32)],
        compiler_params=pltpu.CompilerParams(dimension_semantics=("arbitrary", "arbitrary"),
                                             vmem_limit_bytes=VMEM_LIMIT),
        name="mixers",
    )(zh, *hg_tabs, zr, cos_t, sin_t, *ret_tabs[2:], zg, zab, *gdn_tabs, hm, bd)


def _outproj_kernel(ohg_ref, oret_ref, ogdn_ref, x_ref, mod_ref, g_ref, w_ref, rw_ref, rb_ref,
                    su_ref, x1_ref, hn_ref, ri_ref, gates_ref, cnt_ref, run_ref):
    @pl.when(pl.program_id(0) == 0)
    def _():
        run_ref[...] = jnp.zeros_like(run_ref)

    y = (_dot(ohg_ref[...], w_ref[0:HG_WIDTH, :])
         + _dot(oret_ref[...], w_ref[HG_WIDTH:HG_WIDTH + RET_WIDTH, :])
         + _dot(ogdn_ref[...], w_ref[HG_WIDTH + RET_WIDTH:, :]))
    x1 = x_ref[...] + mod_ref[0, 0:1, :] * y
    x1_ref[...] = x1
    n = x1 * lax.rsqrt(jnp.mean(x1 * x1, axis=-1, keepdims=True) + NORM_EPS)
    hn = (n * g_ref[0:1, :]) * (1.0 + mod_ref[0, 1:2, :]) + mod_ref[0, 2:3, :]
    hn_ref[...] = _pack_halves(hn)

    hn_hi, hn_lo = _split2(hn)
    rw_hi = rw_ref[0:LANES, :]
    logits = (_dot_nt(rw_hi, hn_hi) + _dot_nt(rw_ref[LANES:2 * LANES, :], hn_hi) + _dot_nt(rw_hi, hn_lo)
              + rb_ref[...])
    eid = lax.broadcasted_iota(jnp.int32, (LANES, TM), 0)
    work = jnp.where(eid < N_EXPERTS, logits, -jnp.inf)
    vals, idxs = [], []
    multihot = jnp.zeros((LANES, TM), F32)
    for _ in range(TOP_K):
        mx = jnp.max(work, axis=0, keepdims=True)
        ix = jnp.min(jnp.where(work == mx, eid, LANES), axis=0, keepdims=True)
        sel = eid == ix
        multihot = jnp.where(sel, 1.0, multihot)
        work = jnp.where(sel, -jnp.inf, work)
        vals.append(mx)
        idxs.append(ix)
    ex = [jnp.exp(vv - vals[0]) for vv in vals]
    den = ex[0] + ex[1] + ex[2] + ex[3]

    run = run_ref[...]
    before = _dot(_bf(multihot), su_ref[...]) + jnp.concatenate([run] * (TM // LANES), axis=1)
    ranks = [jnp.sum(jnp.where(eid == ix, before, 0.0), axis=0, keepdims=True) for ix in idxs]
    run = run + jnp.sum(multihot, axis=1, keepdims=True)
    run_ref[...] = run
    cnt_ref[...] = run

    ri_ref[...] = jnp.concatenate(idxs + [r.astype(jnp.int32) for r in ranks], axis=0)
    gates_ref[...] = jnp.concatenate([e / den for e in ex] + [jnp.zeros((TOP_K, TM), F32)], axis=0)


def _outproj(ohg, oret, ogdn, x2d, modv, gpar, w_bf, rw, rb, su, tiles_per_batch):
    n = x2d.shape[0]
    row = lambda i: (i, 0)
    col = lambda i: (0, i)
    const = lambda i: (0, 0)
    return pl.pallas_call(
        _outproj_kernel,
        grid=(n // TM,),
        in_specs=[pl.BlockSpec((TM, HG_WIDTH), row), pl.BlockSpec((TM, RET_WIDTH), row),
                  pl.BlockSpec((TM, GDN_WIDTH), row), pl.BlockSpec((TM, D_MODEL), row),
                  pl.BlockSpec((1, SUBLANES, D_MODEL), lambda i: (i // tiles_per_batch, 0, 0)),
                  pl.BlockSpec((SUBLANES, D_MODEL), const),
                  pl.BlockSpec((D_MODEL, D_MODEL), const),
                  pl.BlockSpec((2 * LANES, D_MODEL), const),
                  pl.BlockSpec((LANES, TM), const),
                  pl.BlockSpec((TM, TM), const)],
        out_specs=[pl.BlockSpec((TM, D_MODEL), row), pl.BlockSpec((TM, D_MODEL // 2), row),
                   pl.BlockSpec((2 * TOP_K, TM), col), pl.BlockSpec((2 * TOP_K, TM), col),
                   pl.BlockSpec((LANES, LANES), const)],
        out_shape=[jax.ShapeDtypeStruct((n, D_MODEL), F32),
                   jax.ShapeDtypeStruct((n, D_MODEL // 2), jnp.uint32),
                   jax.ShapeDtypeStruct((2 * TOP_K, n), jnp.int32),
                   jax.ShapeDtypeStruct((2 * TOP_K, n), F32),
                   jax.ShapeDtypeStruct((LANES, LANES), F32)],
        scratch_shapes=[pltpu.VMEM((LANES, LANES), F32)],
        compiler_params=pltpu.CompilerParams(dimension_semantics=("arbitrary",),
                                             vmem_limit_bytes=VMEM_LIMIT),
        name="outproj_router",
    )(ohg, oret, ogdn, x2d, modv, gpar, w_bf, rw, rb, su)


def _expert_kernel(be_ref, nu_ref, x_ref, w1_ref, b1_ref, w2_ref, b2_ref, y_ref, w1b_ref, w2b_ref):
    i = pl.program_id(0)
    prev = be_ref[jnp.maximum(i - 1, 0)]
    fresh = jnp.logical_or(i == 0, be_ref[i] != prev)

    @pl.when(fresh)
    def _():
        w1b_ref[...] = _bf(w1_ref[0])
        w2b_ref[...] = _bf(w2_ref[0])

    @pl.when(i < nu_ref[0])
    def _():
        half = D_MODEL // 2
        x_lo, x_hi = _unpack_halves(x_ref[...])
        hid = (_dot(_bf(x_lo), w1b_ref[0:half, :]) + _dot(_bf(x_hi), w1b_ref[half:, :])
               + b1_ref[0])
        x_glu = jnp.minimum(hid[:, 0:D_FF], SWIGLU_LIMIT)
        x_lin = jnp.clip(hid[:, D_FF:], -SWIGLU_LIMIT, SWIGLU_LIMIT)
        act = x_glu * _sigmoid(SWIGLU_ALPHA * x_glu) * (x_lin + 1.0)
        y_ref[...] = _pack_halves(_dot(_bf(act), w2b_ref[...]) + b2_ref[0])


def _experts(block_e, n_used, xs, w1, b1, w2, b2):
    p = xs.shape[0]
    n_blocks = p // R_BLK
    ne = w1.shape[0] * w1.shape[1]
    w1 = w1.reshape(ne, D_MODEL, 2 * D_FF)
    w2 = w2.reshape(ne, D_FF, D_MODEL)
    rowmap = lambda i, be, nu: (jnp.minimum(i, nu[0] - 1), 0)
    emap = lambda i, be, nu: (be[i], 0, 0)
    grid_spec = pltpu.PrefetchScalarGridSpec(
        num_scalar_prefetch=2,
        grid=(n_blocks,),
        in_specs=[pl.BlockSpec((R_BLK, D_MODEL // 2), rowmap),
                  pl.BlockSpec((1, D_MODEL, 2 * D_FF), emap),
                  pl.BlockSpec((1, 1, 2 * D_FF), emap),
                  pl.BlockSpec((1, D_FF, D_MODEL), emap),
                  pl.BlockSpec((1, 1, D_MODEL), emap)],
        out_specs=pl.BlockSpec((R_BLK, D_MODEL // 2), rowmap),
        scratch_shapes=[pltpu.VMEM((D_MODEL, 2 * D_FF), BF16), pltpu.VMEM((D_FF, D_MODEL), BF16)],
    )
    return pl.pallas_call(
        _expert_kernel,
        grid_spec=grid_spec,
        out_shape=jax.ShapeDtypeStruct((p, D_MODEL // 2), jnp.uint32),
        compiler_params=pltpu.CompilerParams(dimension_semantics=("arbitrary",),
                                             vmem_limit_bytes=VMEM_LIMIT),
        name="experts",
    )(block_e, n_used, xs, w1, b1.reshape(ne, 1, 2 * D_FF), w2, b2.reshape(ne, 1, D_MODEL))


def _sc_mesh():
    return plsc.VectorSubcoreMesh(core_axis_name="c", subcore_axis_name="s")


def _sc_scatter_rows(x, pos, p_rows):
    mesh = _sc_mesh()
    n, w = x.shape
    per_worker = n // (mesh.num_cores * mesh.num_subcores)
    assert per_worker % SC_ROWS == 0 and pos.shape == (TOP_K, n)

    @functools.partial(
        pl.kernel, out_type=jax.ShapeDtypeStruct((p_rows, w), x.dtype), mesh=mesh,
        scratch_types=[pltpu.VMEM((SC_ROWS,), jnp.int32)] * TOP_K
        + [pltpu.VMEM((SC_ROWS, w), x.dtype), pltpu.SemaphoreType.DMA],
        name="dispatch_rows")
    def scatter(x_hbm, p_hbm, o_hbm, i0, i1, i2, i3, rows_v, sem):
        idx = (i0, i1, i2, i3)
        worker = lax.axis_index("s") * mesh.num_cores + lax.axis_index("c")

        @pl.loop(0, per_worker // SC_ROWS)
        def _(g):
            base = pl.multiple_of(worker * per_worker + g * SC_ROWS, SC_ROWS)
            pltpu.sync_copy(x_hbm.at[pl.ds(base, SC_ROWS)], rows_v)
            for kk in range(TOP_K):
                pltpu.sync_copy(p_hbm.at[kk, pl.ds(base, SC_ROWS)], idx[kk])
            copies = [pltpu.async_copy(rows_v, o_hbm.at[idx[kk]], sem) for kk in range(TOP_K)]
            for cp in copies:
                cp.wait()

    return scatter(x, pos)


def _sc_gather_rows(table, idx):
    mesh = _sc_mesh()
    n_idx = idx.shape[0]
    w = table.shape[1]
    per_worker = n_idx // (mesh.num_cores * mesh.num_subcores)
    assert per_worker % SC_ROWS == 0

    @functools.partial(
        pl.kernel, out_type=jax.ShapeDtypeStruct((n_idx, w), table.dtype), mesh=mesh,
        scratch_types=[pltpu.VMEM((SC_ROWS,), jnp.int32), pltpu.VMEM((SC_ROWS, w), table.dtype),
                       pltpu.SemaphoreType.DMA],
        name="combine_rows")
    def gather(t_hbm, i_hbm, o_hbm, idx_v, rows_v, sem):
        worker = lax.axis_index("s") * mesh.num_cores + lax.axis_index("c")

        @pl.loop(0, per_worker // SC_ROWS)
        def _(g):
            base = pl.multiple_of(worker * per_worker + g * SC_ROWS, SC_ROWS)
            pltpu.sync_copy(i_hbm.at[pl.ds(base, SC_ROWS)], idx_v)
            pltpu.async_copy(t_hbm.at[idx_v], rows_v, sem).wait()
            pltpu.sync_copy(rows_v, o_hbm.at[pl.ds(base, SC_ROWS)])

    return gather(table, idx)


def _combine_kernel(x_ref, y0_ref, y1_ref, y2_ref, y3_ref, gates_ref, mod_ref, g_ref, o_ref, *, final):
    gt = gates_ref[...]
    half = D_MODEL // 2
    acc_lo, acc_hi = None, None
    for kk, y_ref in enumerate((y0_ref, y1_ref, y2_ref, y3_ref)):
        lo, hi = _unpack_halves(y_ref[0])
        g = gt[:, kk:kk + 1]
        acc_lo = g * lo if acc_lo is None else acc_lo + g * lo
        acc_hi = g * hi if acc_hi is None else acc_hi + g * hi
    x_lo = x_ref[:, 0:half] + mod_ref[0, 3:4, 0:half] * acc_lo
    x_hi = x_ref[:, half:] + mod_ref[0, 3:4, half:] * acc_hi
    if final:
        ssq = (jnp.sum(x_lo * x_lo, axis=-1, keepdims=True)
               + jnp.sum(x_hi * x_hi, axis=-1, keepdims=True))
        r = lax.rsqrt(ssq * (1.0 / D_MODEL) + NORM_EPS)
        x_lo = x_lo * r * g_ref[1:2, 0:half]
        x_hi = x_hi * r * g_ref[1:2, half:]
    o_ref[:, 0:half] = x_lo
    o_ref[:, half:] = x_hi


def _combine(x1, yg, gates, modv, gpar, tiles_per_batch, final):
    n = x1.shape[0]
    row = lambda i: (i, 0)
    ysp = [pl.BlockSpec((1, TM, D_MODEL // 2), (lambda i, kk=kk: (kk, i, 0))) for kk in range(TOP_K)]
    return pl.pallas_call(
        functools.partial(_combine_kernel, final=final),
        grid=(n // TM,),
        in_specs=[pl.BlockSpec((TM, D_MODEL), row)] + ysp + [
            pl.BlockSpec((TM, 2 * TOP_K), row),
            pl.BlockSpec((1, SUBLANES, D_MODEL), lambda i: (i // tiles_per_batch, 0, 0)),
            pl.BlockSpec((SUBLANES, D_MODEL), lambda i: (0, 0))],
        out_specs=pl.BlockSpec((TM, D_MODEL), row),
        out_shape=jax.ShapeDtypeStruct((n, D_MODEL), F32),
        compiler_params=pltpu.CompilerParams(dimension_semantics=("arbitrary",),
                                             vmem_limit_bytes=VMEM_LIMIT),
        name="combine",
    )(x1, yg, yg, yg, yg, gates, modv, gpar)


def _rot_cols(w):
    d = w.shape[0]
    w4 = w.reshape(d, RET_HEADS, 2, RET_DK // 2)
    return jnp.concatenate([-w4[:, :, 1:2], w4[:, :, 0:1]], axis=2).reshape(d, RET_WIDTH)


def _widen_w_in(w):
    o = 0
    hg = w[:, o:o + 4 * HG_WIDTH]; o += 4 * HG_WIDTH
    rq = w[:, o:o + RET_WIDTH]; o += RET_WIDTH
    rk = w[:, o:o + RET_WIDTH]; o += RET_WIDTH
    rv = w[:, o:o + RET_WIDTH]; o += RET_WIDTH
    rg = w[:, o:o + RET_WIDTH]; o += RET_WIDTH
    gd = w[:, o:o + 4 * GDN_WIDTH]; o += 4 * GDN_WIDTH
    gab = w[:, o:o + 2 * GDN_HEADS]
    pad = jnp.zeros((w.shape[0], ZAB_W - 2 * GDN_HEADS), w.dtype)
    return _bf(jnp.concatenate([hg, rq, _rot_cols(rq), rk, _rot_cols(rk), rv, rg, gd, gab, pad], axis=1))


def _pad_rows(a, rows=SUBLANES):
    return jnp.concatenate([a, jnp.zeros((rows - a.shape[0],) + a.shape[1:], a.dtype)], axis=0)


def kernel(x, c, ada_w, ada_b, norm1_g, norm2_g, w_in, w_out, hg_lb_logits, hg_norm_g, ret_norm_g,
           gdn_conv_w, gdn_A_log, gdn_dt_bias, gdn_norm_g, router_w, router_b, exp_w1, exp_b1,
           exp_w2, exp_b2, final_norm_g):
    batch, seq, d = x.shape
    depth = ada_w.shape[0]
    n = batch * seq
    steps = seq // T_BLK
    tiles_per_batch = seq // TM
    nk = n * TOP_K
    n_blocks = nk // R_BLK + N_EXPERTS
    p_rows = n_blocks * R_BLK

    lv_np = _level_matrix(T_BLK)
    lv = jnp.asarray(_level_matrix(HG_SUB))
    f_mat = jnp.asarray(_hgrn_exponent_matrix(HG_SUB), BF16)
    gm = jnp.asarray(_gdn_masks(T_BLK))
    bd = jnp.asarray(_block_diag_mask(T_BLK, GDN_CHUNK))
    tril = jnp.asarray(_block_diag_mask(T_BLK, GDN_CHUNK) * (lv_np >= 0), BF16)
    hm = jnp.asarray(_head_masks(RET_WIDTH, RET_HEADS))
    su = jnp.asarray(np.triu(np.ones((TM, TM), np.float32), 1), BF16)
    eab_np = np.zeros((LANES, 2 * GDN_WIDTH), np.float32)
    for h in range(GDN_HEADS):
        eab_np[h, h * GDN_DK:(h + 1) * GDN_DK] = 1.0
        eab_np[GDN_HEADS + h, GDN_WIDTH + h * GDN_DK:GDN_WIDTH + (h + 1) * GDN_DK] = 1.0
    eab = jnp.asarray(eab_np, BF16)

    half = RET_DK // 2
    inv = ROPE_BASE ** (-jnp.linspace(0.0, 1.0, half, dtype=F32))
    ang = jnp.arange(seq, dtype=F32)[:, None] * inv[None, :]
    cos_t = jnp.tile(jnp.cos(ang), (1, 2 * RET_HEADS))
    sin_t = jnp.tile(jnp.sin(ang), (1, 2 * RET_HEADS))
    log_g = jnp.log1p(-jnp.exp2(-5.0 - jnp.arange(RET_HEADS, dtype=F32)))
    jj = jnp.arange(T_BLK, dtype=F32)
    diff = jj[:, None] - jj[None, :]
    dmat = jnp.where(diff[None] >= 0, jnp.exp(diff[None] * log_g[:, None, None]), 0.0)
    lg_lane = jnp.repeat(log_g, RET_DK)[None, :]
    qw = jnp.exp(lg_lane * (jj[:, None] + 1.0))
    kw = jnp.exp(lg_lane * (T_BLK - 1.0 - jj[:, None]))
    cdec = jnp.exp(T_BLK * lg_lane)

    lb = jnp.cumsum(jax.nn.softmax(hg_lb_logits.astype(F32), axis=0), axis=0)
    lb = jnp.maximum(lb - lb[0], 0.0)
    c_pad = _pad_rows(c.astype(F32))
    mod = _adaln(c_pad, ada_w, ada_b)[:, :batch, :]

    x2d = x.reshape(n, d)
    out = None
    for l in range(depth):
        sh1, sc1, gt1, sh2, sc2, gt2 = [mod[l][:, i * d:(i + 1) * d] for i in range(6)]
        zeros = jnp.zeros_like(sh1)
        modv_a = jnp.stack([sc1, sh1, zeros, zeros, zeros, zeros, zeros, zeros], axis=1)
        modv_c = jnp.stack([gt1, sc2, sh2, gt2, zeros, zeros, zeros, zeros], axis=1)
        gpar_a = _pad_rows(norm1_g[l][None, :])
        gpar_c = _pad_rows(jnp.stack([norm2_g[l], final_norm_g], axis=0))

        zh, zr, zg, zab = _inproj(x2d, modv_a, gpar_a, _widen_w_in(w_in[l]), tiles_per_batch)

        hg_par = _pad_rows(jnp.stack([jnp.log(lb[l]), jnp.log1p(-lb[l]), 1.0 - lb[l],
                                      hg_norm_g[l].reshape(-1)], axis=0))
        ret_par = _pad_rows(jnp.concatenate([cdec, ret_norm_g[l].reshape(1, -1)], axis=0))
        gdn_par = _pad_rows(jnp.stack([jnp.repeat(-jnp.exp(gdn_A_log[l].astype(F32)), GDN_DK),
                                       jnp.repeat(gdn_dt_bias[l].astype(F32), GDN_DK),
                                       gdn_norm_g[l].reshape(-1)], axis=0))
        o_hg, o_ret, o_gdn = _mixers(
            zh, zr, zg, zab,
            (f_mat, lv, hg_par),
            (cos_t, sin_t, dmat, qw, kw, ret_par),
            (_pad_rows(gdn_conv_w[l].astype(F32)), gdn_par, eab, gm, tril),
            hm, bd, batch, steps)

        rw_f = jnp.concatenate([router_w[l].T, jnp.zeros((LANES - N_EXPERTS, d), F32)], axis=0)
        rw = jnp.concatenate(_split2(rw_f), axis=0)
        rb = jnp.broadcast_to(jnp.concatenate([router_b[l], jnp.zeros((LANES - N_EXPERTS,), F32)])[:, None],
                              (LANES, TM))
        x1, hn2, ri, gates_t, cnt = _outproj(o_hg, o_ret, o_gdn, x2d, modv_c, gpar_c, _bf(w_out[l]),
                                             rw, rb, su, tiles_per_batch)
        gates = gates_t.T

        counts = cnt[:N_EXPERTS, 0].astype(jnp.int32)
        padded = (counts + R_BLK - 1) // R_BLK * R_BLK
        pend = jnp.cumsum(padded)
        pstart = pend - padded
        n_used = (pend[-1] // R_BLK).astype(jnp.int32)
        blk_start = jnp.arange(n_blocks, dtype=jnp.int32) * R_BLK
        blk_start = jnp.minimum(blk_start, pend[-1] - R_BLK)
        block_e = jnp.sum(blk_start[:, None] >= pend[None, :], axis=1).astype(jnp.int32)
        eid = jnp.arange(N_EXPERTS, dtype=jnp.int32)
        pos_t = ri[TOP_K:] + jnp.sum(jnp.where(ri[:TOP_K, :, None] == eid, pstart, 0), axis=-1)
        xs = _sc_scatter_rows(hn2, pos_t, p_rows)
        y = _experts(block_e + l * N_EXPERTS, n_used.reshape(1), xs, exp_w1, exp_b1, exp_w2, exp_b2)
        yg = _sc_gather_rows(y, pos_t.reshape(-1)).reshape(TOP_K, n, d // 2)
        x2d = _combine(x1, yg, gates, modv_c, gpar_c, tiles_per_batch, final=(l == depth - 1))
    return x2d.reshape(batch, seq, d)
```

```python
import functools
import math

import numpy as np
import jax
import jax.numpy as jnp
from jax import lax
from jax.experimental import pallas as pl
from jax.experimental.pallas import tpu as pltpu
from jax.experimental.pallas import tpu_sc as plsc

F32 = jnp.float32
BF16 = jnp.bfloat16

D_MODEL = 1024
HG_HEADS, HG_DK = 4, 128
HG_WIDTH = HG_HEADS * HG_DK
RET_HEADS, RET_DK = 4, 64
RET_WIDTH = RET_HEADS * RET_DK
GDN_HEADS, GDN_DK = 4, 64
GDN_WIDTH = GDN_HEADS * GDN_DK
CONV_K = 4
ROPE_BASE = 10000.0
N_EXPERTS = 32
TOP_K = 4
D_FF = D_MODEL
SWIGLU_ALPHA = 1.702
SWIGLU_LIMIT = 7.0
NORM_EPS = 1e-6
L2_EPS = 1e-6
GDN_CHUNK = 64

LANES = 128
SUBLANES = 8
VMEM_LIMIT = 56 * 1024 * 1024

T_BLK = 256
TM = 256
TM_IN = 512
R_BLK = 512
SC_ROWS = 128
HG_SUB = 128
HG_LEVELS = 7

ZH_W = 4 * HG_WIDTH
ZR_W = 6 * RET_WIDTH
ZG_W = 4 * GDN_WIDTH
ZAB_W = LANES
Z_W = ZH_W + ZR_W + ZG_W + ZAB_W


def _dot(a, b):
    return jnp.dot(a, b, preferred_element_type=F32)


def _dot_nt(a, b):
    return lax.dot_general(a, b, (((1,), (1,)), ((), ())), preferred_element_type=F32)


def _dot_tn(a, b):
    return lax.dot_general(a, b, (((0,), (0,)), ((), ())), preferred_element_type=F32)


def _split2(x):
    hi = x.astype(BF16)
    return hi, (x - hi.astype(F32)).astype(BF16)


def _dot2_lhs01(c, x):
    hi, lo = _split2(x)
    return _dot(c, hi) + _dot(c, lo)


def _dot2_rhs01(x, c):
    hi, lo = _split2(x)
    return _dot(hi, c) + _dot(lo, c)


def _sigmoid(x):
    return 1.0 / (1.0 + jnp.exp(-x))


def _silu(x):
    return x * _sigmoid(x)


def _softplus(x):
    return jnp.maximum(x, 0.0) + jnp.log1p(jnp.exp(-jnp.abs(x)))


def _bf(x):
    return x.astype(BF16)


def _pack_halves(x):
    w = x.shape[1] // 2
    bits = lax.bitcast_convert_type(_bf(x).astype(F32), jnp.uint32)
    return (bits[:, :w] >> 16) | bits[:, w:]


def _unpack_halves(p):
    lo = lax.bitcast_convert_type(p << 16, F32)
    hi = lax.bitcast_convert_type(p & jnp.uint32(0xFFFF0000), F32)
    return lo, hi


def _level_matrix(t):
    i = np.arange(t)[:, None]
    j = np.arange(t)[None, :]
    x = i ^ j
    lv = np.floor(np.log2(np.maximum(x, 1))).astype(np.int32)
    lv = np.where(i == j, int(math.log2(t)), lv)
    lv = np.where(i < j, -1, lv)
    return lv.astype(np.int32)


def _hgrn_exponent_matrix(t):
    n_lev = int(math.log2(t))
    f = np.zeros((2 + n_lev, t, t), np.float32)
    u = np.arange(t)[None, :]
    r = np.arange(t)[:, None]
    f[0] = (u <= r)
    f[1] = (u > r)
    for l in range(n_lev):
        h = 1 << l
        base = (r // (2 * h)) * (2 * h)
        mid = base + h
        upper = (r - base) >= h
        f[2 + l] = np.where(upper, (u >= mid) & (u <= r), (u > r) & (u < mid))
    return f.reshape((2 + n_lev) * t, t)


def _gdn_masks(t):
    lv = _level_matrix(GDN_CHUNK)
    top = int(math.log2(GDN_CHUNK))
    incl = lv >= 0
    d8 = (lv >= 0) & (lv <= 2)
    merges = [(lv == l) for l in range(3, top)]
    eye = lv == top
    pats = np.stack([incl, d8] + merges + [eye]).astype(np.float32)
    return np.tile(pats, (1, t // GDN_CHUNK, GDN_HEADS))


def _block_diag_mask(t, blk):
    i = np.arange(t)
    return (i[:, None] // blk == i[None, :] // blk).astype(np.float32)


def _head_masks(width, heads):
    lane = np.arange(width)[None, :]
    m = np.zeros((SUBLANES, width), np.float32)
    for h in range(heads):
        m[h] = (lane // (width // heads) == h)[0]
    return m


def _adaln_kernel(c_ref, w_ref, b_ref, o_ref):
    cond = _silu(c_ref[...])
    o_ref[0] = jnp.dot(cond, w_ref[0], preferred_element_type=F32,
                       precision=lax.Precision.HIGHEST) + b_ref[0]


def _adaln(c_pad, ada_w, ada_b):
    depth, d, n6 = ada_w.shape
    tn = n6 // 4
    return pl.pallas_call(
        _adaln_kernel,
        grid=(depth, n6 // tn),
        in_specs=[pl.BlockSpec((SUBLANES, d), lambda l, j: (0, 0)),
                  pl.BlockSpec((1, d, tn), lambda l, j: (l, 0, j)),
                  pl.BlockSpec((1, 1, tn), lambda l, j: (l, 0, j))],
        out_specs=pl.BlockSpec((1, SUBLANES, tn), lambda l, j: (l, 0, j)),
        out_shape=jax.ShapeDtypeStruct((depth, SUBLANES, n6), F32),
        compiler_params=pltpu.CompilerParams(vmem_limit_bytes=VMEM_LIMIT),
        name="adaln",
    )(c_pad, ada_w, ada_b.reshape(depth, 1, n6))


def _inproj_kernel(x_ref, mod_ref, g_ref, w_ref, zh_ref, zr_ref, zg_ref, zab_ref):
    x = x_ref[...]
    y = x * lax.rsqrt(jnp.mean(x * x, axis=-1, keepdims=True) + NORM_EPS)
    hn = (y * g_ref[0:1, :]) * (1.0 + mod_ref[0, 0:1, :]) + mod_ref[0, 1:2, :]
    hb = _bf(hn)
    zh_ref[...] = _dot(hb, w_ref[:, 0:ZH_W])
    zr_ref[...] = _dot(hb, w_ref[:, ZH_W:ZH_W + ZR_W])
    zg_ref[...] = _dot(hb, w_ref[:, ZH_W + ZR_W:ZH_W + ZR_W + ZG_W])
    zab_ref[...] = _dot(hb, w_ref[:, ZH_W + ZR_W + ZG_W:Z_W])


def _inproj(x2d, modv, gpar, w_bf, seq):
    n = x2d.shape[0]
    row = lambda i: (i, 0)
    tiles_per_batch = seq // TM_IN
    return pl.pallas_call(
        _inproj_kernel,
        grid=(n // TM_IN,),
        in_specs=[pl.BlockSpec((TM_IN, D_MODEL), row),
                  pl.BlockSpec((1, SUBLANES, D_MODEL), lambda i: (i // tiles_per_batch, 0, 0)),
                  pl.BlockSpec((SUBLANES, D_MODEL), lambda i: (0, 0)),
                  pl.BlockSpec((D_MODEL, Z_W), lambda i: (0, 0))],
        out_specs=[pl.BlockSpec((TM_IN, ZH_W), row), pl.BlockSpec((TM_IN, ZR_W), row),
                   pl.BlockSpec((TM_IN, ZG_W), row), pl.BlockSpec((TM_IN, ZAB_W), row)],
        out_shape=[jax.ShapeDtypeStruct((n, ZH_W), F32), jax.ShapeDtypeStruct((n, ZR_W), F32),
                   jax.ShapeDtypeStruct((n, ZG_W), F32), jax.ShapeDtypeStruct((n, ZAB_W), F32)],
        compiler_params=pltpu.CompilerParams(dimension_semantics=("arbitrary",),
                                             vmem_limit_bytes=VMEM_LIMIT),
        name="inproj",
    )(x2d, modv, gpar, w_bf)


def _hgrn_stages(zh_ref, f_ref, lv_ref, par_ref, o_ref, state_ref, ex_ref):
    W = HG_WIDTH
    hq = zh_ref[:, 0:W]
    hf = zh_ref[:, W:2 * W]
    loglb = par_ref[0:1, :]
    log1mlb = par_ref[1:2, :]
    onemlb = par_ref[2:3, :]

    q = _silu(hq)
    e = jnp.exp(-jnp.abs(hf))
    inv = 1.0 / (1.0 + e)
    k = onemlb * (jnp.where(hf >= 0, e, 1.0) * inv)
    logsig = jnp.minimum(hf, 0.0) - jnp.log1p(e)
    c = log1mlb + logsig
    lf = jnp.maximum(loglb, c) + jnp.log1p(jnp.exp(-jnp.abs(loglb - c)))
    yield

    lv = lv_ref[...]
    t = HG_SUB
    for sb in range(T_BLK // HG_SUB):
        rows = slice(sb * t, (sb + 1) * t)
        ex_ref[sb] = jnp.exp(_dot2_lhs01(f_ref[...], lf[rows]))
        yield
        for h in range(HG_HEADS):
            cs = slice(h * HG_DK, (h + 1) * HG_DK)
            qh = q[rows, cs]
            kh = k[rows, cs]
            vh = _bf(zh_ref[rows, 2 * W + h * HG_DK:2 * W + (h + 1) * HG_DK])
            s = jnp.where(lv == HG_LEVELS, _dot_nt(_bf(qh), _bf(kh)), 0.0)
            for l in range(HG_LEVELS):
                el = ex_ref[sb, (2 + l) * t:(3 + l) * t, cs]
                s = jnp.where(lv == l, _dot_nt(_bf(qh * el), _bf(kh * el)), s)
                if l % 2 == 1:
                    yield
            eb = ex_ref[sb, 0:t, cs]
            ebl = ex_ref[sb, t:2 * t, cs]
            st = state_ref[h]
            o = _dot(_bf(s), vh) + _dot_nt(_bf(qh * eb), _bf(st))
            upd = _dot_tn(vh, _bf(kh * ebl))
            state_ref[h] = st * ex_ref[sb, t - 1:t, cs] + upd
            ms = jnp.mean(o * o, axis=-1, keepdims=True)
            gate = _silu(zh_ref[rows, 3 * W + h * HG_DK:3 * W + (h + 1) * HG_DK])
            o_ref[rows, cs] = _bf(o * lax.rsqrt(ms + NORM_EPS) * par_ref[3:4, cs] * gate)
            yield


def _ret_stages(zr_ref, cos_ref, sin_ref, dmat_ref, qw_ref, kw_ref, par_ref, hm_ref, bd_ref,
                o_ref, state_ref):
    W = RET_WIDTH
    cos = cos_ref[...]
    sin = sin_ref[...]
    q = zr_ref[:, 0:W] * cos + zr_ref[:, W:2 * W] * sin
    k = (zr_ref[:, 2 * W:3 * W] * cos + zr_ref[:, 3 * W:4 * W] * sin) * (RET_DK ** -0.5)
    v = zr_ref[:, 4 * W:5 * W]
    kb = _bf(k)
    bd = bd_ref[...]
    yield

    s_parts = []
    v_parts = []
    for h in range(RET_HEADS):
        hm = hm_ref[h:h + 1, :]
        s_parts.append(_bf(_dot_nt(_bf(q * hm), kb) * dmat_ref[h]))
        v_parts.append(_bf(v * hm))
        yield
    st = state_ref[...]
    o = (_dot(jnp.concatenate(s_parts, axis=1), jnp.concatenate(v_parts, axis=0))
         + _dot(_bf(q * qw_ref[...]), _bf(st)))
    yield
    kv = _dot_tn(_bf(k * kw_ref[...]), _bf(v))
    state_ref[...] = st * par_ref[0:1, :] + bd * kv
    yield

    ms = _dot2_rhs01(o * o, _bf(bd)) * (1.0 / RET_DK)
    gate = _silu(zr_ref[:, 5 * W:6 * W])
    o_ref[...] = _bf(o * lax.rsqrt(ms + NORM_EPS) * par_ref[1:2, :] * gate)


def _gdn_stages(zg_ref, zab_ref, convw_ref, par_ref, eab_ref, gm_ref, bd_ref, tril_ref, hm_ref,
                o_ref, ext_ref, state_ref, obuf_ref):
    W = GDN_WIDTH
    t = T_BLK
    u = zg_ref[:, 0:3 * W]
    ext_ref[SUBLANES:SUBLANES + t, :] = u
    conv = (convw_ref[3:4, :] * u
            + convw_ref[2:3, :] * ext_ref[SUBLANES - 1:SUBLANES - 1 + t, :]
            + convw_ref[1:2, :] * ext_ref[SUBLANES - 2:SUBLANES - 2 + t, :]
            + convw_ref[0:1, :] * ext_ref[SUBLANES - 3:SUBLANES - 3 + t, :])
    ext_ref[0:SUBLANES, :] = u[t - SUBLANES:t, :]
    qkv = _silu(conv)
    q = qkv[:, 0:W]
    k = qkv[:, W:2 * W]
    v = qkv[:, 2 * W:3 * W]
    yield

    bd = bd_ref[...]
    bdb = _bf(bd)
    ab = zab_ref[...]
    a_exp = _dot2_rhs01(ab, eab_ref[:, 0:W])
    b_exp = _dot2_rhs01(ab, eab_ref[:, W:2 * W])
    qn = q * lax.rsqrt(_dot2_rhs01(q * q, bdb) + L2_EPS) * (GDN_DK ** -0.5)
    kn = k * lax.rsqrt(_dot2_rhs01(k * k, bdb) + L2_EPS)
    yield
    beta = _sigmoid(b_exp)
    g = par_ref[0:1, :] * _softplus(a_exp + par_ref[1:2, :])
    gc = _dot2_lhs01(tril_ref[...], g)
    gl = _dot2_lhs01(bdb, g)
    yield
    eg = jnp.exp(gc)
    vb = v * beta
    kbeta = kn * beta * eg
    qdec = qn * eg
    kdec = kn * jnp.exp(gl - gc)

    n_chunks = t // GDN_CHUNK
    hms = [_bf(jnp.broadcast_to(hm_ref[h:h + 1, :], (GDN_CHUNK, W))) for h in range(GDN_HEADS)]
    hms2 = [jnp.concatenate([m, m], axis=1) for m in hms]

    def chunk(a, c):
        return a[c * GDN_CHUNK:(c + 1) * GDN_CHUNK]

    def expand(y, masks):
        yb = _bf(y)
        return jnp.concatenate([yb * m for m in masks], axis=0)

    def blockprod(x, y, masks):
        xb = _bf(x)
        return jnp.concatenate([_dot(chunk(xb, c), expand(chunk(y, c), masks))
                                for c in range(n_chunks)], axis=0)

    gc_row = _dot2_lhs01(bdb, gc * gm_ref[5])
    yield
    rel = jnp.exp(jnp.where(gm_ref[0] > 0, gc - gc_row, -jnp.inf))

    knbeta = kn * beta
    kq = [_dot_nt(_bf(jnp.concatenate([chunk(knbeta, c), chunk(qn, c)], axis=0)),
                  expand(chunk(kn, c), hms)) for c in range(n_chunks)]
    yield
    m = jnp.concatenate([r[0:GDN_CHUNK] for r in kq], axis=0) * rel
    qk = jnp.concatenate([r[GDN_CHUNK:] for r in kq], axis=0) * rel

    d = m * gm_ref[1]
    d2 = blockprod(d, d, hms)
    yield
    d4 = blockprod(d2, d2, hms)
    dd2 = blockprod(d, d2, hms)
    yield
    x = d2 - d - dd2
    xd4 = blockprod(x, d4, hms)
    yield
    x = x + d4 + xd4
    for lvl in range(2, 5):
        lo = m * gm_ref[lvl]
        xl = blockprod(x, lo, hms)
        yield
        y = lo + xl
        yx = blockprod(y, x, hms)
        yield
        x = x - (y + yx)

    vk = jnp.concatenate([vb, kbeta], axis=1)
    wk = vk + blockprod(x, vk, hms2)
    yield
    w = wk[:, 0:W]
    kcum = wk[:, W:2 * W]
    ag = blockprod(qk, wk, hms2)
    yield
    a1 = ag[:, 0:W]
    qeff = qdec - ag[:, W:2 * W]

    for c in range(t // GDN_CHUNK):
        rows = slice(c * GDN_CHUNK, (c + 1) * GDN_CHUNK)
        st = state_ref[...]
        stb = _bf(st)
        vnew = w[rows] - _dot(_bf(kcum[rows]), stb)
        obuf_ref[rows, :] = _dot(_bf(qeff[rows]), stb) + a1[rows]
        yield
        upd = _dot_tn(_bf(kdec[rows]), _bf(vnew))
        last = eg[(c + 1) * GDN_CHUNK - 1:(c + 1) * GDN_CHUNK, :]
        state_ref[...] = st * last + bd * upd
        yield

    o = obuf_ref[...]
    ms = _dot2_rhs01(o * o, bdb) * (1.0 / GDN_DK)
    gate = _silu(zg_ref[:, 3 * W:4 * W])
    o_ref[...] = _bf(o * lax.rsqrt(ms + NORM_EPS) * par_ref[2:3, :] * gate)


_DONE = object()

def _mixers_kernel(zh_ref, f_ref, lv_ref, hpar_ref,
                   zr_ref, cos_ref, sin_ref, dmat_ref, qw_ref, kw_ref, rpar_ref,
                   zg_ref, zab_ref, convw_ref, gpar_ref, eab_ref, gm_ref, tril_ref, hm_ref, bd_ref,
                   ohg_ref, oret_ref, ogdn_ref,
                   hstate_ref, hex_ref, rstate_ref, gext_ref, gstate_ref, gobuf_ref):
    @pl.when(pl.program_id(1) == 0)
    def _():
        hstate_ref[...] = jnp.zeros_like(hstate_ref)
        rstate_ref[...] = jnp.zeros_like(rstate_ref)
        gstate_ref[...] = jnp.zeros_like(gstate_ref)
        gext_ref[0:SUBLANES, :] = jnp.zeros((SUBLANES, 3 * GDN_WIDTH), F32)

    active = [
        (_gdn_stages(zg_ref, zab_ref, convw_ref, gpar_ref, eab_ref, gm_ref, bd_ref, tril_ref, hm_ref,
                     ogdn_ref, gext_ref, gstate_ref, gobuf_ref), 2),
        (_hgrn_stages(zh_ref, f_ref, lv_ref, hpar_ref, ohg_ref, hstate_ref, hex_ref), 3),
        (_ret_stages(zr_ref, cos_ref, sin_ref, dmat_ref, qw_ref, kw_ref, rpar_ref, hm_ref, bd_ref,
                     oret_ref, rstate_ref), 1),
    ]
    while active:
        for entry in list(active):
            gen, per_round = entry
            for _ in range(per_round):
                if next(gen, _DONE) is _DONE:
                    active.remove(entry)
                    break


def _mixers(zh, zr, zg, zab, hg_tabs, ret_tabs, gdn_tabs, hm, bd, batch, steps):
    n = zh.shape[0]
    seq_row = lambda b, j: (b * steps + j, 0)

    def const(a):
        return pl.BlockSpec(a.shape, lambda b, j, nd=a.ndim: (0,) * nd)

    def rows(width):
        return pl.BlockSpec((T_BLK, width), seq_row)

    pos_rows = pl.BlockSpec((T_BLK, RET_WIDTH), lambda b, j: (j, 0))
    cos_t, sin_t = ret_tabs[0], ret_tabs[1]
    in_specs = ([rows(ZH_W)] + [const(a) for a in hg_tabs]
                + [rows(ZR_W), pos_rows, pos_rows] + [const(a) for a in ret_tabs[2:]]
                + [rows(ZG_W), rows(ZAB_W)] + [const(a) for a in gdn_tabs] + [const(hm), const(bd)])
    return pl.pallas_call(
        _mixers_kernel,
        grid=(batch, steps),
        in_specs=in_specs,
        out_specs=[rows(HG_WIDTH), rows(RET_WIDTH), rows(GDN_WIDTH)],
        out_shape=[jax.ShapeDtypeStruct((n, HG_WIDTH), BF16), jax.ShapeDtypeStruct((n, RET_WIDTH), BF16),
                   jax.ShapeDtypeStruct((n, GDN_WIDTH), BF16)],
        scratch_shapes=[pltpu.VMEM((HG_HEADS, HG_DK, HG_DK), F32),
                        pltpu.VMEM((T_BLK // HG_SUB, (2 + HG_LEVELS) * HG_SUB, HG_WIDTH), F32),
                        pltpu.VMEM((RET_WIDTH, RET_WIDTH), F32),
                        pltpu.VMEM((SUBLANES + T_BLK, 3 * GDN_WIDTH), F32),
                        pltpu.VMEM((GDN_WIDTH, GDN_WIDTH), F32),
                        pltpu.VMEM((T_BLK, GDN_WIDTH), F32)],
        compiler_params=pltpu.CompilerParams(dimension_semantics=("arbitrary", "arbitrary"),
                                             vmem_limit_bytes=VMEM_LIMIT),
        name="mixers",
    )(zh, *hg_tabs, zr, cos_t, sin_t, *ret_tabs[2:], zg, zab, *gdn_tabs, hm, bd)


def _outproj_kernel(ohg_ref, oret_ref, ogdn_ref, x_ref, mod_ref, g_ref, w_ref, rw_ref, rb_ref,
                    su_ref, x1_ref, hn_ref, ri_ref, gates_ref, cnt_ref, run_ref):
    @pl.when(pl.program_id(0) == 0)
    def _():
        run_ref[...] = jnp.zeros_like(run_ref)

    y = (_dot(ohg_ref[...], w_ref[0:HG_WIDTH, :])
         + _dot(oret_ref[...], w_ref[HG_WIDTH:HG_WIDTH + RET_WIDTH, :])
         + _dot(ogdn_ref[...], w_ref[HG_WIDTH + RET_WIDTH:, :]))
    x1 = x_ref[...] + mod_ref[0, 0:1, :] * y
    x1_ref[...] = x1
    n = x1 * lax.rsqrt(jnp.mean(x1 * x1, axis=-1, keepdims=True) + NORM_EPS)
    hn = (n * g_ref[0:1, :]) * (1.0 + mod_ref[0, 1:2, :]) + mod_ref[0, 2:3, :]
    hn_ref[...] = _pack_halves(hn)

    hn_hi, hn_lo = _split2(hn)
    rw_hi = rw_ref[0:LANES, :]
    logits = (_dot_nt(rw_hi, hn_hi) + _dot_nt(rw_ref[LANES:2 * LANES, :], hn_hi) + _dot_nt(rw_hi, hn_lo)
              + rb_ref[...])
    eid = lax.broadcasted_iota(jnp.int32, (LANES, TM), 0)
    work = jnp.where(eid < N_EXPERTS, logits, -jnp.inf)
    vals, idxs = [], []
    multihot = jnp.zeros((LANES, TM), F32)
    for _ in range(TOP_K):
        mx = jnp.max(work, axis=0, keepdims=True)
        ix = jnp.min(jnp.where(work == mx, eid, LANES), axis=0, keepdims=True)
        sel = eid == ix
        multihot = jnp.where(sel, 1.0, multihot)
        work = jnp.where(sel, -jnp.inf, work)
        vals.append(mx)
        idxs.append(ix)
    ex = [jnp.exp(vv - vals[0]) for vv in vals]
    den = ex[0] + ex[1] + ex[2] + ex[3]

    run = run_ref[...]
    before = _dot(_bf(multihot), su_ref[...]) + jnp.concatenate([run] * (TM // LANES), axis=1)
    ranks = [jnp.sum(jnp.where(eid == ix, before, 0.0), axis=0, keepdims=True) for ix in idxs]
    run = run + jnp.sum(multihot, axis=1, keepdims=True)
    run_ref[...] = run
    cnt_ref[...] = run

    ri_ref[...] = jnp.concatenate(idxs + [r.astype(jnp.int32) for r in ranks], axis=0)
    gates_ref[...] = jnp.concatenate([e / den for e in ex] + [jnp.zeros((TOP_K, TM), F32)], axis=0)


def _outproj(ohg, oret, ogdn, x2d, modv, gpar, w_bf, rw, rb, su, tiles_per_batch):
    n = x2d.shape[0]
    row = lambda i: (i, 0)
    col = lambda i: (0, i)
    const = lambda i: (0, 0)
    return pl.pallas_call(
        _outproj_kernel,
        grid=(n // TM,),
        in_specs=[pl.BlockSpec((TM, HG_WIDTH), row), pl.BlockSpec((TM, RET_WIDTH), row),
                  pl.BlockSpec((TM, GDN_WIDTH), row), pl.BlockSpec((TM, D_MODEL), row),
                  pl.BlockSpec((1, SUBLANES, D_MODEL), lambda i: (i // tiles_per_batch, 0, 0)),
                  pl.BlockSpec((SUBLANES, D_MODEL), const),
                  pl.BlockSpec((D_MODEL, D_MODEL), const),
                  pl.BlockSpec((2 * LANES, D_MODEL), const),
                  pl.BlockSpec((LANES, TM), const),
                  pl.BlockSpec((TM, TM), const)],
        out_specs=[pl.BlockSpec((TM, D_MODEL), row), pl.BlockSpec((TM, D_MODEL // 2), row),
                   pl.BlockSpec((2 * TOP_K, TM), col), pl.BlockSpec((2 * TOP_K, TM), col),
                   pl.BlockSpec((LANES, LANES), const)],
        out_shape=[jax.ShapeDtypeStruct((n, D_MODEL), F32),
                   jax.ShapeDtypeStruct((n, D_MODEL // 2), jnp.uint32),
                   jax.ShapeDtypeStruct((2 * TOP_K, n), jnp.int32),
                   jax.ShapeDtypeStruct((2 * TOP_K, n), F32),
                   jax.ShapeDtypeStruct((LANES, LANES), F32)],
        scratch_shapes=[pltpu.VMEM((LANES, LANES), F32)],
        compiler_params=pltpu.CompilerParams(dimension_semantics=("arbitrary",),
                                             vmem_limit_bytes=VMEM_LIMIT),
        name="outproj_router",
    )(ohg, oret, ogdn, x2d, modv, gpar, w_bf, rw, rb, su)


def _expert_kernel(be_ref, nu_ref, x_ref, w1_ref, b1_ref, w2_ref, b2_ref, y_ref, w1b_ref, w2b_ref):
    i = pl.program_id(0)
    prev = be_ref[jnp.maximum(i - 1, 0)]
    fresh = jnp.logical_or(i == 0, be_ref[i] != prev)

    @pl.when(fresh)
    def _():
        w1b_ref[...] = _bf(w1_ref[0])
        w2b_ref[...] = _bf(w2_ref[0])

    @pl.when(i < nu_ref[0])
    def _():
        half = D_MODEL // 2
        x_lo, x_hi = _unpack_halves(x_ref[...])
        hid = (_dot(_bf(x_lo), w1b_ref[0:half, :]) + _dot(_bf(x_hi), w1b_ref[half:, :])
               + b1_ref[0])
        x_glu = jnp.minimum(hid[:, 0:D_FF], SWIGLU_LIMIT)
        x_lin = jnp.clip(hid[:, D_FF:], -SWIGLU_LIMIT, SWIGLU_LIMIT)
        act = x_glu * _sigmoid(SWIGLU_ALPHA * x_glu) * (x_lin + 1.0)
        y_ref[...] = _pack_halves(_dot(_bf(act), w2b_ref[...]) + b2_ref[0])


def _experts(block_e, n_used, xs, w1, b1, w2, b2):
    p = xs.shape[0]
    n_blocks = p // R_BLK
    ne = w1.shape[0] * w1.shape[1]
    w1 = w1.reshape(ne, D_MODEL, 2 * D_FF)
    w2 = w2.reshape(ne, D_FF, D_MODEL)
    rowmap = lambda i, be, nu: (jnp.minimum(i, nu[0] - 1), 0)
    emap = lambda i, be, nu: (be[i], 0, 0)
    grid_spec = pltpu.PrefetchScalarGridSpec(
        num_scalar_prefetch=2,
        grid=(n_blocks,),
        in_specs=[pl.BlockSpec((R_BLK, D_MODEL // 2), rowmap),
                  pl.BlockSpec((1, D_MODEL, 2 * D_FF), emap),
                  pl.BlockSpec((1, 1, 2 * D_FF), emap),
                  pl.BlockSpec((1, D_FF, D_MODEL), emap),
                  pl.BlockSpec((1, 1, D_MODEL), emap)],
        out_specs=pl.BlockSpec((R_BLK, D_MODEL // 2), rowmap),
        scratch_shapes=[pltpu.VMEM((D_MODEL, 2 * D_FF), BF16), pltpu.VMEM((D_FF, D_MODEL), BF16)],
    )
    return pl.pallas_call(
        _expert_kernel,
        grid_spec=grid_spec,
        out_shape=jax.ShapeDtypeStruct((p, D_MODEL // 2), jnp.uint32),
        compiler_params=pltpu.CompilerParams(dimension_semantics=("arbitrary",),
                                             vmem_limit_bytes=VMEM_LIMIT),
        name="experts",
    )(block_e, n_used, xs, w1, b1.reshape(ne, 1, 2 * D_FF), w2, b2.reshape(ne, 1, D_MODEL))


def _sc_mesh():
    return plsc.VectorSubcoreMesh(core_axis_name="c", subcore_axis_name="s")


def _sc_scatter_rows(x, pos, p_rows):
    mesh = _sc_mesh()
    n, w = x.shape
    per_worker = n // (mesh.num_cores * mesh.num_subcores)
    assert per_worker % SC_ROWS == 0 and pos.shape == (TOP_K, n)

    @functools.partial(
        pl.kernel, out_type=jax.ShapeDtypeStruct((p_rows, w), x.dtype), mesh=mesh,
        scratch_types=[pltpu.VMEM((SC_ROWS,), jnp.int32)] * TOP_K
        + [pltpu.VMEM((SC_ROWS, w), x.dtype), pltpu.SemaphoreType.DMA],
        name="dispatch_rows")
    def scatter(x_hbm, p_hbm, o_hbm, i0, i1, i2, i3, rows_v, sem):
        idx = (i0, i1, i2, i3)
        worker = lax.axis_index("s") * mesh.num_cores + lax.axis_index("c")

        @pl.loop(0, per_worker // SC_ROWS)
        def _(g):
            base = pl.multiple_of(worker * per_worker + g * SC_ROWS, SC_ROWS)
            pltpu.sync_copy(x_hbm.at[pl.ds(base, SC_ROWS)], rows_v)
            for kk in range(TOP_K):
                pltpu.sync_copy(p_hbm.at[kk, pl.ds(base, SC_ROWS)], idx[kk])
            copies = [pltpu.async_copy(rows_v, o_hbm.at[idx[kk]], sem) for kk in range(TOP_K)]
            for cp in copies:
                cp.wait()

    return scatter(x, pos)


def _sc_gather_rows(table, idx):
    mesh = _sc_mesh()
    n_idx = idx.shape[0]
    w = table.shape[1]
    per_worker = n_idx // (mesh.num_cores * mesh.num_subcores)
    assert per_worker % SC_ROWS == 0

    @functools.partial(
        pl.kernel, out_type=jax.ShapeDtypeStruct((n_idx, w), table.dtype), mesh=mesh,
        scratch_types=[pltpu.VMEM((SC_ROWS,), jnp.int32), pltpu.VMEM((SC_ROWS, w), table.dtype),
                       pltpu.SemaphoreType.DMA],
        name="combine_rows")
    def gather(t_hbm, i_hbm, o_hbm, idx_v, rows_v, sem):
        worker = lax.axis_index("s") * mesh.num_cores + lax.axis_index("c")

        @pl.loop(0, per_worker // SC_ROWS)
        def _(g):
            base = pl.multiple_of(worker * per_worker + g * SC_ROWS, SC_ROWS)
            pltpu.sync_copy(i_hbm.at[pl.ds(base, SC_ROWS)], idx_v)
            pltpu.async_copy(t_hbm.at[idx_v], rows_v, sem).wait()
            pltpu.sync_copy(rows_v, o_hbm.at[pl.ds(base, SC_ROWS)])

    return gather(table, idx)


def _combine_kernel(x_ref, y0_ref, y1_ref, y2_ref, y3_ref, gates_ref, mod_ref, g_ref, o_ref, *, final):
    gt = gates_ref[...]
    half = D_MODEL // 2
    acc_lo, acc_hi = None, None
    for kk, y_ref in enumerate((y0_ref, y1_ref, y2_ref, y3_ref)):
        lo, hi = _unpack_halves(y_ref[0])
        g = gt[:, kk:kk + 1]
        acc_lo = g * lo if acc_lo is None else acc_lo + g * lo
        acc_hi = g * hi if acc_hi is None else acc_hi + g * hi
    x_lo = x_ref[:, 0:half] + mod_ref[0, 3:4, 0:half] * acc_lo
    x_hi = x_ref[:, half:] + mod_ref[0, 3:4, half:] * acc_hi
    if final:
        ssq = (jnp.sum(x_lo * x_lo, axis=-1, keepdims=True)
               + jnp.sum(x_hi * x_hi, axis=-1, keepdims=True))
        r = lax.rsqrt(ssq * (1.0 / D_MODEL) + NORM_EPS)
        x_lo = x_lo * r * g_ref[1:2, 0:half]
        x_hi = x_hi * r * g_ref[1:2, half:]
    o_ref[:, 0:half] = x_lo
    o_ref[:, half:] = x_hi


def _combine(x1, yg, gates, modv, gpar, tiles_per_batch, final):
    n = x1.shape[0]
    row = lambda i: (i, 0)
    ysp = [pl.BlockSpec((1, TM, D_MODEL // 2), (lambda i, kk=kk: (kk, i, 0))) for kk in range(TOP_K)]
    return pl.pallas_call(
        functools.partial(_combine_kernel, final=final),
        grid=(n // TM,),
        in_specs=[pl.BlockSpec((TM, D_MODEL), row)] + ysp + [
            pl.BlockSpec((TM, 2 * TOP_K), row),
            pl.BlockSpec((1, SUBLANES, D_MODEL), lambda i: (i // tiles_per_batch, 0, 0)),
            pl.BlockSpec((SUBLANES, D_MODEL), lambda i: (0, 0))],
        out_specs=pl.BlockSpec((TM, D_MODEL), row),
        out_shape=jax.ShapeDtypeStruct((n, D_MODEL), F32),
        compiler_params=pltpu.CompilerParams(dimension_semantics=("arbitrary",),
                                             vmem_limit_bytes=VMEM_LIMIT),
        name="combine",
    )(x1, yg, yg, yg, yg, gates, modv, gpar)


def _rot_cols(w):
    d = w.shape[0]
    w4 = w.reshape(d, RET_HEADS, 2, RET_DK // 2)
    return jnp.concatenate([-w4[:, :, 1:2], w4[:, :, 0:1]], axis=2).reshape(d, RET_WIDTH)


def _widen_w_in(w):
    o = 0
    hg = w[:, o:o + 4 * HG_WIDTH]; o += 4 * HG_WIDTH
    rq = w[:, o:o + RET_WIDTH]; o += RET_WIDTH
    rk = w[:, o:o + RET_WIDTH]; o += RET_WIDTH
    rv = w[:, o:o + RET_WIDTH]; o += RET_WIDTH
    rg = w[:, o:o + RET_WIDTH]; o += RET_WIDTH
    gd = w[:, o:o + 4 * GDN_WIDTH]; o += 4 * GDN_WIDTH
    gab = w[:, o:o + 2 * GDN_HEADS]
    pad = jnp.zeros((w.shape[0], ZAB_W - 2 * GDN_HEADS), w.dtype)
    return _bf(jnp.concatenate([hg, rq, _rot_cols(rq), rk, _rot_cols(rk), rv, rg, gd, gab, pad], axis=1))


def _pad_rows(a, rows=SUBLANES):
    return jnp.concatenate([a, jnp.zeros((rows - a.shape[0],) + a.shape[1:], a.dtype)], axis=0)


def kernel(x, c, ada_w, ada_b, norm1_g, norm2_g, w_in, w_out, hg_lb_logits, hg_norm_g, ret_norm_g,
           gdn_conv_w, gdn_A_log, gdn_dt_bias, gdn_norm_g, router_w, router_b, exp_w1, exp_b1,
           exp_w2, exp_b2, final_norm_g):
    batch, seq, d = x.shape
    depth = ada_w.shape[0]
    n = batch * seq
    steps = seq // T_BLK
    tiles_per_batch = seq // TM
    nk = n * TOP_K
    n_blocks = nk // R_BLK + N_EXPERTS
    p_rows = n_blocks * R_BLK

    lv_np = _level_matrix(T_BLK)
    lv = jnp.asarray(_level_matrix(HG_SUB))
    f_mat = jnp.asarray(_hgrn_exponent_matrix(HG_SUB), BF16)
    gm = jnp.asarray(_gdn_masks(T_BLK))
    bd = jnp.asarray(_block_diag_mask(T_BLK, GDN_CHUNK))
    tril = jnp.asarray(_block_diag_mask(T_BLK, GDN_CHUNK) * (lv_np >= 0), BF16)
    hm = jnp.asarray(_head_masks(RET_WIDTH, RET_HEADS))
    su = jnp.asarray(np.triu(np.ones((TM, TM), np.float32), 1), BF16)
    eab_np = np.zeros((LANES, 2 * GDN_WIDTH), np.float32)
    for h in range(GDN_HEADS):
        eab_np[h, h * GDN_DK:(h + 1) * GDN_DK] = 1.0
        eab_np[GDN_HEADS + h, GDN_WIDTH + h * GDN_DK:GDN_WIDTH + (h + 1) * GDN_DK] = 1.0
    eab = jnp.asarray(eab_np, BF16)

    half = RET_DK // 2
    inv = ROPE_BASE ** (-jnp.linspace(0.0, 1.0, half, dtype=F32))
    ang = jnp.arange(seq, dtype=F32)[:, None] * inv[None, :]
    cos_t = jnp.tile(jnp.cos(ang), (1, 2 * RET_HEADS))
    sin_t = jnp.tile(jnp.sin(ang), (1, 2 * RET_HEADS))
    log_g = jnp.log1p(-jnp.exp2(-5.0 - jnp.arange(RET_HEADS, dtype=F32)))
    jj = jnp.arange(T_BLK, dtype=F32)
    diff = jj[:, None] - jj[None, :]
    dmat = jnp.where(diff[None] >= 0, jnp.exp(diff[None] * log_g[:, None, None]), 0.0)
    lg_lane = jnp.repeat(log_g, RET_DK)[None, :]
    qw = jnp.exp(lg_lane * (jj[:, None] + 1.0))
    kw = jnp.exp(lg_lane * (T_BLK - 1.0 - jj[:, None]))
    cdec = jnp.exp(T_BLK * lg_lane)

    lb = jnp.cumsum(jax.nn.softmax(hg_lb_logits.astype(F32), axis=0), axis=0)
    lb = jnp.maximum(lb - lb[0], 0.0)
    c_pad = _pad_rows(c.astype(F32))
    mod = _adaln(c_pad, ada_w, ada_b)[:, :batch, :]

    x2d = x.reshape(n, d)
    out = None
    for l in range(depth):
        sh1, sc1, gt1, sh2, sc2, gt2 = [mod[l][:, i * d:(i + 1) * d] for i in range(6)]
        zeros = jnp.zeros_like(sh1)
        modv_a = jnp.stack([sc1, sh1, zeros, zeros, zeros, zeros, zeros, zeros], axis=1)
        modv_c = jnp.stack([gt1, sc2, sh2, gt2, zeros, zeros, zeros, zeros], axis=1)
        gpar_a = _pad_rows(norm1_g[l][None, :])
        gpar_c = _pad_rows(jnp.stack([norm2_g[l], final_norm_g], axis=0))

        zh, zr, zg, zab = _inproj(x2d, modv_a, gpar_a, _widen_w_in(w_in[l]), seq)

        hg_par = _pad_rows(jnp.stack([jnp.log(lb[l]), jnp.log1p(-lb[l]), 1.0 - lb[l],
                                      hg_norm_g[l].reshape(-1)], axis=0))
        ret_par = _pad_rows(jnp.concatenate([cdec, ret_norm_g[l].reshape(1, -1)], axis=0))
        gdn_par = _pad_rows(jnp.stack([jnp.repeat(-jnp.exp(gdn_A_log[l].astype(F32)), GDN_DK),
                                       jnp.repeat(gdn_dt_bias[l].astype(F32), GDN_DK),
                                       gdn_norm_g[l].reshape(-1)], axis=0))
        o_hg, o_ret, o_gdn = _mixers(
            zh, zr, zg, zab,
            (f_mat, lv, hg_par),
            (cos_t, sin_t, dmat, qw, kw, ret_par),
            (_pad_rows(gdn_conv_w[l].astype(F32)), gdn_par, eab, gm, tril),
            hm, bd, batch, steps)

        rw_f = jnp.concatenate([router_w[l].T, jnp.zeros((LANES - N_EXPERTS, d), F32)], axis=0)
        rw = jnp.concatenate(_split2(rw_f), axis=0)
        rb = jnp.broadcast_to(jnp.concatenate([router_b[l], jnp.zeros((LANES - N_EXPERTS,), F32)])[:, None],
                              (LANES, TM))
        x1, hn2, ri, gates_t, cnt = _outproj(o_hg, o_ret, o_gdn, x2d, modv_c, gpar_c, _bf(w_out[l]),
                                             rw, rb, su, tiles_per_batch)
        gates = gates_t.T

        counts = cnt[:N_EXPERTS, 0].astype(jnp.int32)
        padded = (counts + R_BLK - 1) // R_BLK * R_BLK
        pend = jnp.cumsum(padded)
        pstart = pend - padded
        n_used = (pend[-1] // R_BLK).astype(jnp.int32)
        blk_start = jnp.arange(n_blocks, dtype=jnp.int32) * R_BLK
        blk_start = jnp.minimum(blk_start, pend[-1] - R_BLK)
        block_e = jnp.sum(blk_start[:, None] >= pend[None, :], axis=1).astype(jnp.int32)
        eid = jnp.arange(N_EXPERTS, dtype=jnp.int32)
        pos_t = ri[TOP_K:] + jnp.sum(jnp.where(ri[:TOP_K, :, None] == eid, pstart, 0), axis=-1)
        xs = _sc_scatter_rows(hn2, pos_t, p_rows)
        y = _experts(block_e + l * N_EXPERTS, n_used.reshape(1), xs, exp_w1, exp_b1, exp_w2, exp_b2)
        yg = _sc_gather_rows(y, pos_t.reshape(-1)).reshape(TOP_K, n, d // 2)
        x2d = _combine(x1, yg, gates, modv_c, gpar_c, tiles_per_batch, final=(l == depth - 1))
    return x2d.reshape(batch, seq, d)
```

```python
import functools
import math

import numpy as np
import jax
import jax.numpy as jnp
from jax import lax
from jax.experimental import pallas as pl
from jax.experimental.pallas import tpu as pltpu
from jax.experimental.pallas import tpu_sc as plsc

F32 = jnp.float32
BF16 = jnp.bfloat16

D_MODEL = 1024
HG_HEADS, HG_DK = 4, 128
HG_WIDTH = HG_HEADS * HG_DK
RET_HEADS, RET_DK = 4, 64
RET_WIDTH = RET_HEADS * RET_DK
GDN_HEADS, GDN_DK = 4, 64
GDN_WIDTH = GDN_HEADS * GDN_DK
CONV_K = 4
ROPE_BASE = 10000.0
N_EXPERTS = 32
TOP_K = 4
D_FF = D_MODEL
SWIGLU_ALPHA = 1.702
SWIGLU_LIMIT = 7.0
NORM_EPS = 1e-6
L2_EPS = 1e-6
GDN_CHUNK = 64

LANES = 128
SUBLANES = 8
VMEM_LIMIT = 56 * 1024 * 1024

T_BLK = 256
TM = 512
TM_IN = 512
ROUTE_SLAB = 128
COMBINE_PIECES = 4
R_BLK = 512
SC_ROWS = 128
HG_SUB = 128
HG_LEVELS = 7

ZH_W = 4 * HG_WIDTH
ZR_W = 6 * RET_WIDTH
ZG_W = 4 * GDN_WIDTH
ZAB_W = LANES
Z_W = ZH_W + ZR_W + ZG_W + ZAB_W


def _dot(a, b):
    return jnp.dot(a, b, preferred_element_type=F32)


def _dot_nt(a, b):
    return lax.dot_general(a, b, (((1,), (1,)), ((), ())), preferred_element_type=F32)


def _dot_tn(a, b):
    return lax.dot_general(a, b, (((0,), (0,)), ((), ())), preferred_element_type=F32)


def _split2(x):
    hi = x.astype(BF16)
    return hi, (x - hi.astype(F32)).astype(BF16)


def _dot2_lhs01(c, x):
    hi, lo = _split2(x)
    return _dot(c, hi) + _dot(c, lo)


def _dot2_rhs01(x, c):
    hi, lo = _split2(x)
    return _dot(hi, c) + _dot(lo, c)


def _sigmoid(x):
    return 1.0 / (1.0 + jnp.exp(-x))


def _silu(x):
    return x * _sigmoid(x)


def _softplus(x):
    return jnp.maximum(x, 0.0) + jnp.log1p(jnp.exp(-jnp.abs(x)))


def _bf(x):
    return x.astype(BF16)


def _pack_halves(x):
    w = x.shape[1] // 2
    bits = lax.bitcast_convert_type(_bf(x).astype(F32), jnp.uint32)
    return (bits[:, :w] >> 16) | bits[:, w:]


def _unpack_halves(p):
    lo = lax.bitcast_convert_type(p << 16, F32)
    hi = lax.bitcast_convert_type(p & jnp.uint32(0xFFFF0000), F32)
    return lo, hi


def _level_matrix(t):
    i = np.arange(t)[:, None]
    j = np.arange(t)[None, :]
    x = i ^ j
    lv = np.floor(np.log2(np.maximum(x, 1))).astype(np.int32)
    lv = np.where(i == j, int(math.log2(t)), lv)
    lv = np.where(i < j, -1, lv)
    return lv.astype(np.int32)


def _hgrn_exponent_matrix(t):
    n_lev = int(math.log2(t))
    f = np.zeros((2 + n_lev, t, t), np.float32)
    u = np.arange(t)[None, :]
    r = np.arange(t)[:, None]
    f[0] = (u <= r)
    f[1] = (u > r)
    for l in range(n_lev):
        h = 1 << l
        base = (r // (2 * h)) * (2 * h)
        mid = base + h
        upper = (r - base) >= h
        f[2 + l] = np.where(upper, (u >= mid) & (u <= r), (u > r) & (u < mid))
    return f.reshape((2 + n_lev) * t, t)


def _gdn_masks(t):
    lv = _level_matrix(GDN_CHUNK)
    top = int(math.log2(GDN_CHUNK))
    incl = lv >= 0
    d8 = (lv >= 0) & (lv <= 2)
    merges = [(lv == l) for l in range(3, top)]
    eye = lv == top
    pats = np.stack([incl, d8] + merges + [eye]).astype(np.float32)
    return np.tile(pats, (1, t // GDN_CHUNK, GDN_HEADS))


def _block_diag_mask(t, blk):
    i = np.arange(t)
    return (i[:, None] // blk == i[None, :] // blk).astype(np.float32)


def _head_masks(width, heads):
    lane = np.arange(width)[None, :]
    m = np.zeros((SUBLANES, width), np.float32)
    for h in range(heads):
        m[h] = (lane // (width // heads) == h)[0]
    return m


def _adaln_kernel(c_ref, w_ref, b_ref, o_ref):
    cond = _silu(c_ref[...])
    o_ref[0] = jnp.dot(cond, w_ref[0], preferred_element_type=F32,
                       precision=lax.Precision.HIGHEST) + b_ref[0]


def _adaln(c_pad, ada_w, ada_b):
    depth, d, n6 = ada_w.shape
    tn = n6 // 4
    return pl.pallas_call(
        _adaln_kernel,
        grid=(depth, n6 // tn),
        in_specs=[pl.BlockSpec((SUBLANES, d), lambda l, j: (0, 0)),
                  pl.BlockSpec((1, d, tn), lambda l, j: (l, 0, j)),
                  pl.BlockSpec((1, 1, tn), lambda l, j: (l, 0, j))],
        out_specs=pl.BlockSpec((1, SUBLANES, tn), lambda l, j: (l, 0, j)),
        out_shape=jax.ShapeDtypeStruct((depth, SUBLANES, n6), F32),
        compiler_params=pltpu.CompilerParams(vmem_limit_bytes=VMEM_LIMIT),
        name="adaln",
    )(c_pad, ada_w, ada_b.reshape(depth, 1, n6))


def _inproj_kernel(x_ref, mod_ref, g_ref, w_ref, zh_ref, zr_ref, zg_ref, zab_ref):
    x = x_ref[...]
    y = x * lax.rsqrt(jnp.mean(x * x, axis=-1, keepdims=True) + NORM_EPS)
    hn = (y * g_ref[0:1, :]) * (1.0 + mod_ref[0, 0:1, :]) + mod_ref[0, 1:2, :]
    hb = _bf(hn)
    zh_ref[...] = _dot(hb, w_ref[:, 0:ZH_W])
    zr_ref[...] = _dot(hb, w_ref[:, ZH_W:ZH_W + ZR_W])
    zg_ref[...] = _dot(hb, w_ref[:, ZH_W + ZR_W:ZH_W + ZR_W + ZG_W])
    zab_ref[...] = _dot(hb, w_ref[:, ZH_W + ZR_W + ZG_W:Z_W])


def _inproj(x2d, modv, gpar, w_bf, seq):
    n = x2d.shape[0]
    row = lambda i: (i, 0)
    tiles_per_batch = seq // TM_IN
    return pl.pallas_call(
        _inproj_kernel,
        grid=(n // TM_IN,),
        in_specs=[pl.BlockSpec((TM_IN, D_MODEL), row),
                  pl.BlockSpec((1, SUBLANES, D_MODEL), lambda i: (i // tiles_per_batch, 0, 0)),
                  pl.BlockSpec((SUBLANES, D_MODEL), lambda i: (0, 0)),
                  pl.BlockSpec((D_MODEL, Z_W), lambda i: (0, 0))],
        out_specs=[pl.BlockSpec((TM_IN, ZH_W), row), pl.BlockSpec((TM_IN, ZR_W), row),
                   pl.BlockSpec((TM_IN, ZG_W), row), pl.BlockSpec((TM_IN, ZAB_W), row)],
        out_shape=[jax.ShapeDtypeStruct((n, ZH_W), F32), jax.ShapeDtypeStruct((n, ZR_W), F32),
                   jax.ShapeDtypeStruct((n, ZG_W), F32), jax.ShapeDtypeStruct((n, ZAB_W), F32)],
        compiler_params=pltpu.CompilerParams(dimension_semantics=("arbitrary",),
                                             vmem_limit_bytes=VMEM_LIMIT),
        name="inproj",
    )(x2d, modv, gpar, w_bf)


def _hgrn_stages(zh_ref, f_ref, lv_ref, par_ref, o_ref, state_ref, ex_ref):
    W = HG_WIDTH
    hq = zh_ref[:, 0:W]
    hf = zh_ref[:, W:2 * W]
    loglb = par_ref[0:1, :]
    log1mlb = par_ref[1:2, :]
    onemlb = par_ref[2:3, :]

    q = _silu(hq)
    e = jnp.exp(-jnp.abs(hf))
    inv = 1.0 / (1.0 + e)
    k = onemlb * (jnp.where(hf >= 0, e, 1.0) * inv)
    logsig = jnp.minimum(hf, 0.0) - jnp.log1p(e)
    c = log1mlb + logsig
    lf = jnp.maximum(loglb, c) + jnp.log1p(jnp.exp(-jnp.abs(loglb - c)))
    yield

    lv = lv_ref[...]
    t = HG_SUB
    for sb in range(T_BLK // HG_SUB):
        rows = slice(sb * t, (sb + 1) * t)
        ex_ref[sb] = jnp.exp(_dot2_lhs01(f_ref[...], lf[rows]))
        yield
        for h in range(HG_HEADS):
            cs = slice(h * HG_DK, (h + 1) * HG_DK)
            qh = q[rows, cs]
            kh = k[rows, cs]
            vh = _bf(zh_ref[rows, 2 * W + h * HG_DK:2 * W + (h + 1) * HG_DK])
            s = jnp.where(lv == HG_LEVELS, _dot_nt(_bf(qh), _bf(kh)), 0.0)
            for l in range(HG_LEVELS):
                el = ex_ref[sb, (2 + l) * t:(3 + l) * t, cs]
                s = jnp.where(lv == l, _dot_nt(_bf(qh * el), _bf(kh * el)), s)
                if l % 2 == 1:
                    yield
            eb = ex_ref[sb, 0:t, cs]
            ebl = ex_ref[sb, t:2 * t, cs]
            st = state_ref[h]
            o = _dot(_bf(s), vh) + _dot_nt(_bf(qh * eb), _bf(st))
            upd = _dot_tn(vh, _bf(kh * ebl))
            state_ref[h] = st * ex_ref[sb, t - 1:t, cs] + upd
            ms = jnp.mean(o * o, axis=-1, keepdims=True)
            gate = _silu(zh_ref[rows, 3 * W + h * HG_DK:3 * W + (h + 1) * HG_DK])
            o_ref[rows, cs] = _bf(o * lax.rsqrt(ms + NORM_EPS) * par_ref[3:4, cs] * gate)
            yield


def _ret_stages(zr_ref, cos_ref, sin_ref, dmat_ref, qw_ref, kw_ref, par_ref, hm_ref, bd_ref,
                o_ref, state_ref):
    W = RET_WIDTH
    cos = cos_ref[...]
    sin = sin_ref[...]
    q = zr_ref[:, 0:W] * cos + zr_ref[:, W:2 * W] * sin
    k = (zr_ref[:, 2 * W:3 * W] * cos + zr_ref[:, 3 * W:4 * W] * sin) * (RET_DK ** -0.5)
    v = zr_ref[:, 4 * W:5 * W]
    kb = _bf(k)
    bd = bd_ref[...]
    yield

    s_parts = []
    v_parts = []
    for h in range(RET_HEADS):
        hm = hm_ref[h:h + 1, :]
        s_parts.append(_bf(_dot_nt(_bf(q * hm), kb) * dmat_ref[h]))
        v_parts.append(_bf(v * hm))
        yield
    st = state_ref[...]
    o = (_dot(jnp.concatenate(s_parts, axis=1), jnp.concatenate(v_parts, axis=0))
         + _dot(_bf(q * qw_ref[...]), _bf(st)))
    yield
    kv = _dot_tn(_bf(k * kw_ref[...]), _bf(v))
    state_ref[...] = st * par_ref[0:1, :] + bd * kv
    yield

    ms = _dot2_rhs01(o * o, _bf(bd)) * (1.0 / RET_DK)
    gate = _silu(zr_ref[:, 5 * W:6 * W])
    o_ref[...] = _bf(o * lax.rsqrt(ms + NORM_EPS) * par_ref[1:2, :] * gate)


def _gdn_stages(zg_ref, zab_ref, convw_ref, par_ref, eab_ref, gm_ref, bd_ref, tril_ref, hm_ref,
                o_ref, ext_ref, state_ref, obuf_ref):
    W = GDN_WIDTH
    t = T_BLK
    u = zg_ref[:, 0:3 * W]
    ext_ref[SUBLANES:SUBLANES + t, :] = u
    conv = (convw_ref[3:4, :] * u
            + convw_ref[2:3, :] * ext_ref[SUBLANES - 1:SUBLANES - 1 + t, :]
            + convw_ref[1:2, :] * ext_ref[SUBLANES - 2:SUBLANES - 2 + t, :]
            + convw_ref[0:1, :] * ext_ref[SUBLANES - 3:SUBLANES - 3 + t, :])
    ext_ref[0:SUBLANES, :] = u[t - SUBLANES:t, :]
    qkv = _silu(conv)
    q = qkv[:, 0:W]
    k = qkv[:, W:2 * W]
    v = qkv[:, 2 * W:3 * W]
    yield

    bd = bd_ref[...]
    bdb = _bf(bd)
    ab = zab_ref[...]
    a_exp = _dot2_rhs01(ab, eab_ref[:, 0:W])
    b_exp = _dot2_rhs01(ab, eab_ref[:, W:2 * W])
    qn = q * lax.rsqrt(_dot2_rhs01(q * q, bdb) + L2_EPS) * (GDN_DK ** -0.5)
    kn = k * lax.rsqrt(_dot2_rhs01(k * k, bdb) + L2_EPS)
    yield
    beta = _sigmoid(b_exp)
    g = par_ref[0:1, :] * _softplus(a_exp + par_ref[1:2, :])
    gc = _dot2_lhs01(tril_ref[...], g)
    gl = _dot2_lhs01(bdb, g)
    yield
    eg = jnp.exp(gc)
    vb = v * beta
    kbeta = kn * beta * eg
    qdec = qn * eg
    kdec = kn * jnp.exp(gl - gc)

    n_chunks = t // GDN_CHUNK
    hms = [_bf(jnp.broadcast_to(hm_ref[h:h + 1, :], (GDN_CHUNK, W))) for h in range(GDN_HEADS)]
    hms2 = [jnp.concatenate([m, m], axis=1) for m in hms]

    def chunk(a, c):
        return a[c * GDN_CHUNK:(c + 1) * GDN_CHUNK]

    def expand(y, masks):
        yb = _bf(y)
        return jnp.concatenate([yb * m for m in masks], axis=0)

    def blockprod(x, y, masks):
        xb = _bf(x)
        return jnp.concatenate([_dot(chunk(xb, c), expand(chunk(y, c), masks))
                                for c in range(n_chunks)], axis=0)

    gc_row = _dot2_lhs01(bdb, gc * gm_ref[5])
    yield
    rel = jnp.exp(jnp.where(gm_ref[0] > 0, gc - gc_row, -jnp.inf))

    knbeta = kn * beta
    kq = [_dot_nt(_bf(jnp.concatenate([chunk(knbeta, c), chunk(qn, c)], axis=0)),
                  expand(chunk(kn, c), hms)) for c in range(n_chunks)]
    yield
    m = jnp.concatenate([r[0:GDN_CHUNK] for r in kq], axis=0) * rel
    qk = jnp.concatenate([r[GDN_CHUNK:] for r in kq], axis=0) * rel

    d = m * gm_ref[1]
    d2 = blockprod(d, d, hms)
    yield
    d4 = blockprod(d2, d2, hms)
    dd2 = blockprod(d, d2, hms)
    yield
    x = d2 - d - dd2
    xd4 = blockprod(x, d4, hms)
    yield
    x = x + d4 + xd4
    for lvl in range(2, 5):
        lo = m * gm_ref[lvl]
        xl = blockprod(x, lo, hms)
        yield
        y = lo + xl
        yx = blockprod(y, x, hms)
        yield
        x = x - (y + yx)

    vk = jnp.concatenate([vb, kbeta], axis=1)
    wk = vk + blockprod(x, vk, hms2)
    yield
    w = wk[:, 0:W]
    kcum = wk[:, W:2 * W]
    ag = blockprod(qk, wk, hms2)
    yield
    a1 = ag[:, 0:W]
    qeff = qdec - ag[:, W:2 * W]

    for c in range(t // GDN_CHUNK):
        rows = slice(c * GDN_CHUNK, (c + 1) * GDN_CHUNK)
        st = state_ref[...]
        stb = _bf(st)
        vnew = w[rows] - _dot(_bf(kcum[rows]), stb)
        obuf_ref[rows, :] = _dot(_bf(qeff[rows]), stb) + a1[rows]
        yield
        upd = _dot_tn(_bf(kdec[rows]), _bf(vnew))
        last = eg[(c + 1) * GDN_CHUNK - 1:(c + 1) * GDN_CHUNK, :]
        state_ref[...] = st * last + bd * upd
        yield

    o = obuf_ref[...]
    ms = _dot2_rhs01(o * o, bdb) * (1.0 / GDN_DK)
    gate = _silu(zg_ref[:, 3 * W:4 * W])
    o_ref[...] = _bf(o * lax.rsqrt(ms + NORM_EPS) * par_ref[2:3, :] * gate)


_DONE = object()

def _mixers_kernel(zh_ref, f_ref, lv_ref, hpar_ref,
                   zr_ref, cos_ref, sin_ref, dmat_ref, qw_ref, kw_ref, rpar_ref,
                   zg_ref, zab_ref, convw_ref, gpar_ref, eab_ref, gm_ref, tril_ref, hm_ref, bd_ref,
                   ohg_ref, oret_ref, ogdn_ref,
                   hstate_ref, hex_ref, rstate_ref, gext_ref, gstate_ref, gobuf_ref):
    @pl.when(pl.program_id(1) == 0)
    def _():
        hstate_ref[...] = jnp.zeros_like(hstate_ref)
        rstate_ref[...] = jnp.zeros_like(rstate_ref)
        gstate_ref[...] = jnp.zeros_like(gstate_ref)
        gext_ref[0:SUBLANES, :] = jnp.zeros((SUBLANES, 3 * GDN_WIDTH), F32)

    active = [
        (_gdn_stages(zg_ref, zab_ref, convw_ref, gpar_ref, eab_ref, gm_ref, bd_ref, tril_ref, hm_ref,
                     ogdn_ref, gext_ref, gstate_ref, gobuf_ref), 2),
        (_hgrn_stages(zh_ref, f_ref, lv_ref, hpar_ref, ohg_ref, hstate_ref, hex_ref), 3),
        (_ret_stages(zr_ref, cos_ref, sin_ref, dmat_ref, qw_ref, kw_ref, rpar_ref, hm_ref, bd_ref,
                     oret_ref, rstate_ref), 1),
    ]
    while active:
        for entry in list(active):
            gen, per_round = entry
            for _ in range(per_round):
                if next(gen, _DONE) is _DONE:
                    active.remove(entry)
                    break


def _mixers(zh, zr, zg, zab, hg_tabs, ret_tabs, gdn_tabs, hm, bd, batch, steps):
    n = zh.shape[0]
    seq_row = lambda b, j: (b * steps + j, 0)

    def const(a):
        return pl.BlockSpec(a.shape, lambda b, j, nd=a.ndim: (0,) * nd)

    def rows(width):
        return pl.BlockSpec((T_BLK, width), seq_row)

    pos_rows = pl.BlockSpec((T_BLK, RET_WIDTH), lambda b, j: (j, 0))
    cos_t, sin_t = ret_tabs[0], ret_tabs[1]
    in_specs = ([rows(ZH_W)] + [const(a) for a in hg_tabs]
                + [rows(ZR_W), pos_rows, pos_rows] + [const(a) for a in ret_tabs[2:]]
                + [rows(ZG_W), rows(ZAB_W)] + [const(a) for a in gdn_tabs] + [const(hm), const(bd)])
    return pl.pallas_call(
        _mixers_kernel,
        grid=(batch, steps),
        in_specs=in_specs,
        out_specs=[rows(HG_WIDTH), rows(RET_WIDTH), rows(GDN_WIDTH)],
        out_shape=[jax.ShapeDtypeStruct((n, HG_WIDTH), BF16), jax.ShapeDtypeStruct((n, RET_WIDTH), BF16),
                   jax.ShapeDtypeStruct((n, GDN_WIDTH), BF16)],
        scratch_shapes=[pltpu.VMEM((HG_HEADS, HG_DK, HG_DK), F32),
                        pltpu.VMEM((T_BLK // HG_SUB, (2 + HG_LEVELS) * HG_SUB, HG_WIDTH), F32),
                        pltpu.VMEM((RET_WIDTH, RET_WIDTH), F32),
                        pltpu.VMEM((SUBLANES + T_BLK, 3 * GDN_WIDTH), F32),
                        pltpu.VMEM((GDN_WIDTH, GDN_WIDTH), F32),
                        pltpu.VMEM((T_BLK, GDN_WIDTH), F32)],
        compiler_params=pltpu.CompilerParams(dimension_semantics=("arbitrary", "arbitrary"),
                                             vmem_limit_bytes=VMEM_LIMIT),
        name="mixers",
    )(zh, *hg_tabs, zr, cos_t, sin_t, *ret_tabs[2:], zg, zab, *gdn_tabs, hm, bd)


def _outproj_kernel(ohg_ref, oret_ref, ogdn_ref, x_ref, mod_ref, g_ref, w_ref, rw_ref, rb_ref,
                    su_ref, x1_ref, hn_ref, ri_ref, gates_ref, cnt_ref, run_ref):
    @pl.when(pl.program_id(0) == 0)
    def _():
        run_ref[...] = jnp.zeros_like(run_ref)

    rw_hi = rw_ref[0:LANES, :]
    rw_lo = rw_ref[LANES:2 * LANES, :]
    eid = lax.broadcasted_iota(jnp.int32, (LANES, ROUTE_SLAB), 0)
    routed = {}

    def slab_stages(s):
        rows = slice(s * ROUTE_SLAB, (s + 1) * ROUTE_SLAB)
        y = (_dot(ohg_ref[rows, :], w_ref[0:HG_WIDTH, :])
             + _dot(oret_ref[rows, :], w_ref[HG_WIDTH:HG_WIDTH + RET_WIDTH, :])
             + _dot(ogdn_ref[rows, :], w_ref[HG_WIDTH + RET_WIDTH:, :]))
        yield
        x1 = x_ref[rows, :] + mod_ref[0, 0:1, :] * y
        x1_ref[rows, :] = x1
        n = x1 * lax.rsqrt(jnp.mean(x1 * x1, axis=-1, keepdims=True) + NORM_EPS)
        hn = (n * g_ref[0:1, :]) * (1.0 + mod_ref[0, 1:2, :]) + mod_ref[0, 2:3, :]
        hn_ref[rows, :] = _pack_halves(hn)
        yield
        hn_hi, hn_lo = _split2(hn)
        logits = (_dot_nt(rw_hi, hn_hi) + _dot_nt(rw_lo, hn_hi) + _dot_nt(rw_hi, hn_lo)
                  + rb_ref[:, rows])
        yield
        work = jnp.where(eid < N_EXPERTS, logits, -jnp.inf)
        vals, idxs = [], []
        multihot = jnp.zeros((LANES, ROUTE_SLAB), F32)
        for kk in range(TOP_K):
            mx = jnp.max(work, axis=0, keepdims=True)
            ix = jnp.min(jnp.where(work == mx, eid, LANES), axis=0, keepdims=True)
            sel = eid == ix
            multihot = jnp.where(sel, 1.0, multihot)
            work = jnp.where(sel, -jnp.inf, work)
            vals.append(mx)
            idxs.append(ix)
            if kk % 2 == 1:
                yield
        routed[s] = (vals, idxs, multihot)

    n_slabs = TM // ROUTE_SLAB
    active = [slab_stages(s) for s in range(n_slabs)]
    while active:
        for gen in list(active):
            if next(gen, _DONE) is _DONE:
                active.remove(gen)

    multihot = jnp.concatenate([routed[s][2] for s in range(n_slabs)], axis=1)
    run = run_ref[...]
    before = _dot(_bf(multihot), su_ref[...]) + jnp.concatenate([run] * (TM // LANES), axis=1)
    run = run + jnp.sum(multihot, axis=1, keepdims=True)
    run_ref[...] = run
    cnt_ref[...] = run

    for s in range(n_slabs):
        vals, idxs, _ = routed[s]
        cols = slice(s * ROUTE_SLAB, (s + 1) * ROUTE_SLAB)
        ex = [jnp.exp(vv - vals[0]) for vv in vals]
        den = ex[0] + ex[1] + ex[2] + ex[3]
        ranks = [jnp.sum(jnp.where(eid == ix, before[:, cols], 0.0), axis=0, keepdims=True) for ix in idxs]
        ri_ref[:, cols] = jnp.concatenate(idxs + [r.astype(jnp.int32) for r in ranks], axis=0)
        gates_ref[:, cols] = jnp.concatenate([e / den for e in ex]
                                             + [jnp.zeros((TOP_K, ROUTE_SLAB), F32)], axis=0)


def _outproj(ohg, oret, ogdn, x2d, modv, gpar, w_bf, rw, rb, su, tiles_per_batch):
    n = x2d.shape[0]
    row = lambda i: (i, 0)
    col = lambda i: (0, i)
    const = lambda i: (0, 0)
    return pl.pallas_call(
        _outproj_kernel,
        grid=(n // TM,),
        in_specs=[pl.BlockSpec((TM, HG_WIDTH), row), pl.BlockSpec((TM, RET_WIDTH), row),
                  pl.BlockSpec((TM, GDN_WIDTH), row), pl.BlockSpec((TM, D_MODEL), row),
                  pl.BlockSpec((1, SUBLANES, D_MODEL), lambda i: (i // tiles_per_batch, 0, 0)),
                  pl.BlockSpec((SUBLANES, D_MODEL), const),
                  pl.BlockSpec((D_MODEL, D_MODEL), const),
                  pl.BlockSpec((2 * LANES, D_MODEL), const),
                  pl.BlockSpec((LANES, TM), const),
                  pl.BlockSpec((TM, TM), const)],
        out_specs=[pl.BlockSpec((TM, D_MODEL), row), pl.BlockSpec((TM, D_MODEL // 2), row),
                   pl.BlockSpec((2 * TOP_K, TM), col), pl.BlockSpec((2 * TOP_K, TM), col),
                   pl.BlockSpec((LANES, LANES), const)],
        out_shape=[jax.ShapeDtypeStruct((n, D_MODEL), F32),
                   jax.ShapeDtypeStruct((n, D_MODEL // 2), jnp.uint32),
                   jax.ShapeDtypeStruct((2 * TOP_K, n), jnp.int32),
                   jax.ShapeDtypeStruct((2 * TOP_K, n), F32),
                   jax.ShapeDtypeStruct((LANES, LANES), F32)],
        scratch_shapes=[pltpu.VMEM((LANES, LANES), F32)],
        compiler_params=pltpu.CompilerParams(dimension_semantics=("arbitrary",),
                                             vmem_limit_bytes=VMEM_LIMIT),
        name="outproj_router",
    )(ohg, oret, ogdn, x2d, modv, gpar, w_bf, rw, rb, su)


def _expert_kernel(be_ref, nu_ref, x_ref, w1_ref, b1_ref, w2_ref, b2_ref, y_ref, w1b_ref, w2b_ref):
    i = pl.program_id(0)
    prev = be_ref[jnp.maximum(i - 1, 0)]
    fresh = jnp.logical_or(i == 0, be_ref[i] != prev)

    @pl.when(fresh)
    def _():
        w1b_ref[...] = _bf(w1_ref[0])
        w2b_ref[...] = _bf(w2_ref[0])

    @pl.when(i < nu_ref[0])
    def _():
        half = D_MODEL // 2
        x_lo, x_hi = _unpack_halves(x_ref[...])
        hid = (_dot(_bf(x_lo), w1b_ref[0:half, :]) + _dot(_bf(x_hi), w1b_ref[half:, :])
               + b1_ref[0])
        x_glu = jnp.minimum(hid[:, 0:D_FF], SWIGLU_LIMIT)
        x_lin = jnp.clip(hid[:, D_FF:], -SWIGLU_LIMIT, SWIGLU_LIMIT)
        act = x_glu * _sigmoid(SWIGLU_ALPHA * x_glu) * (x_lin + 1.0)
        y_ref[...] = _pack_halves(_dot(_bf(act), w2b_ref[...]) + b2_ref[0])


def _experts(block_e, n_used, xs, w1, b1, w2, b2):
    p = xs.shape[0]
    n_blocks = p // R_BLK
    ne = w1.shape[0] * w1.shape[1]
    w1 = w1.reshape(ne, D_MODEL, 2 * D_FF)
    w2 = w2.reshape(ne, D_FF, D_MODEL)
    rowmap = lambda i, be, nu: (jnp.minimum(i, nu[0] - 1), 0)
    emap = lambda i, be, nu: (be[i], 0, 0)
    grid_spec = pltpu.PrefetchScalarGridSpec(
        num_scalar_prefetch=2,
        grid=(n_blocks,),
        in_specs=[pl.BlockSpec((R_BLK, D_MODEL // 2), rowmap),
                  pl.BlockSpec((1, D_MODEL, 2 * D_FF), emap),
                  pl.BlockSpec((1, 1, 2 * D_FF), emap),
                  pl.BlockSpec((1, D_FF, D_MODEL), emap),
                  pl.BlockSpec((1, 1, D_MODEL), emap)],
        out_specs=pl.BlockSpec((R_BLK, D_MODEL // 2), rowmap),
        scratch_shapes=[pltpu.VMEM((D_MODEL, 2 * D_FF), BF16), pltpu.VMEM((D_FF, D_MODEL), BF16)],
    )
    return pl.pallas_call(
        _expert_kernel,
        grid_spec=grid_spec,
        out_shape=jax.ShapeDtypeStruct((p, D_MODEL // 2), jnp.uint32),
        compiler_params=pltpu.CompilerParams(dimension_semantics=("arbitrary",),
                                             vmem_limit_bytes=VMEM_LIMIT),
        name="experts",
    )(block_e, n_used, xs, w1, b1.reshape(ne, 1, 2 * D_FF), w2, b2.reshape(ne, 1, D_MODEL))


def _sc_mesh():
    return plsc.VectorSubcoreMesh(core_axis_name="c", subcore_axis_name="s")


def _sc_scatter_rows(x, pos, p_rows):
    mesh = _sc_mesh()
    n, w = x.shape
    per_worker = n // (mesh.num_cores * mesh.num_subcores)
    assert per_worker % SC_ROWS == 0 and pos.shape == (TOP_K, n)

    @functools.partial(
        pl.kernel, out_type=jax.ShapeDtypeStruct((p_rows, w), x.dtype), mesh=mesh,
        scratch_types=[pltpu.VMEM((SC_ROWS,), jnp.int32)] * TOP_K
        + [pltpu.VMEM((SC_ROWS, w), x.dtype), pltpu.SemaphoreType.DMA],
        name="dispatch_rows")
    def scatter(x_hbm, p_hbm, o_hbm, i0, i1, i2, i3, rows_v, sem):
        idx = (i0, i1, i2, i3)
        worker = lax.axis_index("s") * mesh.num_cores + lax.axis_index("c")

        @pl.loop(0, per_worker // SC_ROWS)
        def _(g):
            base = pl.multiple_of(worker * per_worker + g * SC_ROWS, SC_ROWS)
            pltpu.sync_copy(x_hbm.at[pl.ds(base, SC_ROWS)], rows_v)
            for kk in range(TOP_K):
                pltpu.sync_copy(p_hbm.at[kk, pl.ds(base, SC_ROWS)], idx[kk])
            copies = [pltpu.async_copy(rows_v, o_hbm.at[idx[kk]], sem) for kk in range(TOP_K)]
            for cp in copies:
                cp.wait()

    return scatter(x, pos)


def _sc_gather_rows(table, idx):
    mesh = _sc_mesh()
    n_idx = idx.shape[0]
    w = table.shape[1]
    per_worker = n_idx // (mesh.num_cores * mesh.num_subcores)
    assert per_worker % SC_ROWS == 0

    @functools.partial(
        pl.kernel, out_type=jax.ShapeDtypeStruct((n_idx, w), table.dtype), mesh=mesh,
        scratch_types=[pltpu.VMEM((SC_ROWS,), jnp.int32), pltpu.VMEM((SC_ROWS, w), table.dtype),
                       pltpu.SemaphoreType.DMA],
        name="combine_rows")
    def gather(t_hbm, i_hbm, o_hbm, idx_v, rows_v, sem):
        worker = lax.axis_index("s") * mesh.num_cores + lax.axis_index("c")

        @pl.loop(0, per_worker // SC_ROWS)
        def _(g):
            base = pl.multiple_of(worker * per_worker + g * SC_ROWS, SC_ROWS)
            pltpu.sync_copy(i_hbm.at[pl.ds(base, SC_ROWS)], idx_v)
            pltpu.async_copy(t_hbm.at[idx_v], rows_v, sem).wait()
            pltpu.sync_copy(rows_v, o_hbm.at[pl.ds(base, SC_ROWS)])

    return gather(table, idx)


def _combine_kernel(x_ref, y0_ref, y1_ref, y2_ref, y3_ref, gates_ref, mod_ref, g_ref, o_ref, *, final):
    gt = gates_ref[...]
    half = D_MODEL // 2
    acc_lo, acc_hi = None, None
    for kk, y_ref in enumerate((y0_ref, y1_ref, y2_ref, y3_ref)):
        lo, hi = _unpack_halves(y_ref[0])
        g = gt[:, kk:kk + 1]
        acc_lo = g * lo if acc_lo is None else acc_lo + g * lo
        acc_hi = g * hi if acc_hi is None else acc_hi + g * hi
    x_lo = x_ref[:, 0:half] + mod_ref[0, 3:4, 0:half] * acc_lo
    x_hi = x_ref[:, half:] + mod_ref[0, 3:4, half:] * acc_hi
    if final:
        ssq = (jnp.sum(x_lo * x_lo, axis=-1, keepdims=True)
               + jnp.sum(x_hi * x_hi, axis=-1, keepdims=True))
        r = lax.rsqrt(ssq * (1.0 / D_MODEL) + NORM_EPS)
        x_lo = x_lo * r * g_ref[1:2, 0:half]
        x_hi = x_hi * r * g_ref[1:2, half:]
    o_ref[:, 0:half] = x_lo
    o_ref[:, half:] = x_hi


def _combine(x, yg, gates, modv, gpar, tiles_per_batch, final, piece):
    n = x.shape[0]
    tiles = n // TM // COMBINE_PIECES
    first = piece * tiles
    row = lambda i: (first + i, 0)
    ysp = [pl.BlockSpec((1, TM, D_MODEL // 2), (lambda i, kk=kk: (kk, i, 0))) for kk in range(TOP_K)]
    return pl.pallas_call(
        functools.partial(_combine_kernel, final=final),
        grid=(tiles,),
        in_specs=[pl.BlockSpec((TM, D_MODEL), row)] + ysp + [
            pl.BlockSpec((TM, 2 * TOP_K), row),
            pl.BlockSpec((1, SUBLANES, D_MODEL), lambda i: ((first + i) // tiles_per_batch, 0, 0)),
            pl.BlockSpec((SUBLANES, D_MODEL), lambda i: (0, 0))],
        out_specs=pl.BlockSpec((TM, D_MODEL), row),
        out_shape=jax.ShapeDtypeStruct((n, D_MODEL), F32),
        input_output_aliases={0: 0},
        compiler_params=pltpu.CompilerParams(dimension_semantics=("arbitrary",),
                                             vmem_limit_bytes=VMEM_LIMIT),
        name="combine",
    )(x, yg, yg, yg, yg, gates, modv, gpar)


def _rot_cols(w):
    d = w.shape[0]
    w4 = w.reshape(d, RET_HEADS, 2, RET_DK // 2)
    return jnp.concatenate([-w4[:, :, 1:2], w4[:, :, 0:1]], axis=2).reshape(d, RET_WIDTH)


def _widen_w_in(w):
    o = 0
    hg = w[:, o:o + 4 * HG_WIDTH]; o += 4 * HG_WIDTH
    rq = w[:, o:o + RET_WIDTH]; o += RET_WIDTH
    rk = w[:, o:o + RET_WIDTH]; o += RET_WIDTH
    rv = w[:, o:o + RET_WIDTH]; o += RET_WIDTH
    rg = w[:, o:o + RET_WIDTH]; o += RET_WIDTH
    gd = w[:, o:o + 4 * GDN_WIDTH]; o += 4 * GDN_WIDTH
    gab = w[:, o:o + 2 * GDN_HEADS]
    pad = jnp.zeros((w.shape[0], ZAB_W - 2 * GDN_HEADS), w.dtype)
    return _bf(jnp.concatenate([hg, rq, _rot_cols(rq), rk, _rot_cols(rk), rv, rg, gd, gab, pad], axis=1))


def _pad_rows(a, rows=SUBLANES):
    return jnp.concatenate([a, jnp.zeros((rows - a.shape[0],) + a.shape[1:], a.dtype)], axis=0)


def kernel(x, c, ada_w, ada_b, norm1_g, norm2_g, w_in, w_out, hg_lb_logits, hg_norm_g, ret_norm_g,
           gdn_conv_w, gdn_A_log, gdn_dt_bias, gdn_norm_g, router_w, router_b, exp_w1, exp_b1,
           exp_w2, exp_b2, final_norm_g):
    batch, seq, d = x.shape
    depth = ada_w.shape[0]
    n = batch * seq
    steps = seq // T_BLK
    tiles_per_batch = seq // TM
    nk = n * TOP_K
    n_blocks = nk // R_BLK + N_EXPERTS
    p_rows = n_blocks * R_BLK

    lv_np = _level_matrix(T_BLK)
    lv = jnp.asarray(_level_matrix(HG_SUB))
    f_mat = jnp.asarray(_hgrn_exponent_matrix(HG_SUB), BF16)
    gm = jnp.asarray(_gdn_masks(T_BLK))
    bd = jnp.asarray(_block_diag_mask(T_BLK, GDN_CHUNK))
    tril = jnp.asarray(_block_diag_mask(T_BLK, GDN_CHUNK) * (lv_np >= 0), BF16)
    hm = jnp.asarray(_head_masks(RET_WIDTH, RET_HEADS))
    su = jnp.asarray(np.triu(np.ones((TM, TM), np.float32), 1), BF16)
    eab_np = np.zeros((LANES, 2 * GDN_WIDTH), np.float32)
    for h in range(GDN_HEADS):
        eab_np[h, h * GDN_DK:(h + 1) * GDN_DK] = 1.0
        eab_np[GDN_HEADS + h, GDN_WIDTH + h * GDN_DK:GDN_WIDTH + (h + 1) * GDN_DK] = 1.0
    eab = jnp.asarray(eab_np, BF16)

    half = RET_DK // 2
    inv = ROPE_BASE ** (-jnp.linspace(0.0, 1.0, half, dtype=F32))
    ang = jnp.arange(seq, dtype=F32)[:, None] * inv[None, :]
    cos_t = jnp.tile(jnp.cos(ang), (1, 2 * RET_HEADS))
    sin_t = jnp.tile(jnp.sin(ang), (1, 2 * RET_HEADS))
    log_g = jnp.log1p(-jnp.exp2(-5.0 - jnp.arange(RET_HEADS, dtype=F32)))
    jj = jnp.arange(T_BLK, dtype=F32)
    diff = jj[:, None] - jj[None, :]
    dmat = jnp.where(diff[None] >= 0, jnp.exp(diff[None] * log_g[:, None, None]), 0.0)
    lg_lane = jnp.repeat(log_g, RET_DK)[None, :]
    qw = jnp.exp(lg_lane * (jj[:, None] + 1.0))
    kw = jnp.exp(lg_lane * (T_BLK - 1.0 - jj[:, None]))
    cdec = jnp.exp(T_BLK * lg_lane)

    lb = jnp.cumsum(jax.nn.softmax(hg_lb_logits.astype(F32), axis=0), axis=0)
    lb = jnp.maximum(lb - lb[0], 0.0)
    c_pad = _pad_rows(c.astype(F32))
    mod = _adaln(c_pad, ada_w, ada_b)[:, :batch, :]

    x2d = x.reshape(n, d)
    out = None
    for l in range(depth):
        sh1, sc1, gt1, sh2, sc2, gt2 = [mod[l][:, i * d:(i + 1) * d] for i in range(6)]
        zeros = jnp.zeros_like(sh1)
        modv_a = jnp.stack([sc1, sh1, zeros, zeros, zeros, zeros, zeros, zeros], axis=1)
        modv_c = jnp.stack([gt1, sc2, sh2, gt2, zeros, zeros, zeros, zeros], axis=1)
        gpar_a = _pad_rows(norm1_g[l][None, :])
        gpar_c = _pad_rows(jnp.stack([norm2_g[l], final_norm_g], axis=0))

        zh, zr, zg, zab = _inproj(x2d, modv_a, gpar_a, _widen_w_in(w_in[l]), seq)

        hg_par = _pad_rows(jnp.stack([jnp.log(lb[l]), jnp.log1p(-lb[l]), 1.0 - lb[l],
                                      hg_norm_g[l].reshape(-1)], axis=0))
        ret_par = _pad_rows(jnp.concatenate([cdec, ret_norm_g[l].reshape(1, -1)], axis=0))
        gdn_par = _pad_rows(jnp.stack([jnp.repeat(-jnp.exp(gdn_A_log[l].astype(F32)), GDN_DK),
                                       jnp.repeat(gdn_dt_bias[l].astype(F32), GDN_DK),
                                       gdn_norm_g[l].reshape(-1)], axis=0))
        o_hg, o_ret, o_gdn = _mixers(
            zh, zr, zg, zab,
            (f_mat, lv, hg_par),
            (cos_t, sin_t, dmat, qw, kw, ret_par),
            (_pad_rows(gdn_conv_w[l].astype(F32)), gdn_par, eab, gm, tril),
            hm, bd, batch, steps)

        rw_f = jnp.concatenate([router_w[l].T, jnp.zeros((LANES - N_EXPERTS, d), F32)], axis=0)
        rw = jnp.concatenate(_split2(rw_f), axis=0)
        rb = jnp.broadcast_to(jnp.concatenate([router_b[l], jnp.zeros((LANES - N_EXPERTS,), F32)])[:, None],
                              (LANES, TM))
        x1, hn2, ri, gates_t, cnt = _outproj(o_hg, o_ret, o_gdn, x2d, modv_c, gpar_c, _bf(w_out[l]),
                                             rw, rb, su, tiles_per_batch)
        gates = gates_t.T

        counts = cnt[:N_EXPERTS, 0].astype(jnp.int32)
        padded = (counts + R_BLK - 1) // R_BLK * R_BLK
        pend = jnp.cumsum(padded)
        pstart = pend - padded
        n_used = (pend[-1] // R_BLK).astype(jnp.int32)
        blk_start = jnp.arange(n_blocks, dtype=jnp.int32) * R_BLK
        blk_start = jnp.minimum(blk_start, pend[-1] - R_BLK)
        block_e = jnp.sum(blk_start[:, None] >= pend[None, :], axis=1).astype(jnp.int32)
        eid = jnp.arange(N_EXPERTS, dtype=jnp.int32)
        pos_t = ri[TOP_K:] + jnp.sum(jnp.where(ri[:TOP_K, :, None] == eid, pstart, 0), axis=-1)
        xs = _sc_scatter_rows(hn2, pos_t, p_rows)
        y = _experts(block_e + l * N_EXPERTS, n_used.reshape(1), xs, exp_w1, exp_b1, exp_w2, exp_b2)
        n_piece = n // COMBINE_PIECES
        x2d = x1
        for p in range(COMBINE_PIECES):
            idx = pos_t[:, p * n_piece:(p + 1) * n_piece].reshape(-1)
            yg = _sc_gather_rows(y, idx).reshape(TOP_K, n_piece, d // 2)
            x2d = _combine(x2d, yg, gates, modv_c, gpar_c, tiles_per_batch, l == depth - 1, p)
    return x2d.reshape(batch, seq, d)
```

```python
import functools
import math

import numpy as np
import jax
import jax.numpy as jnp
from jax import lax
from jax.experimental import pallas as pl
from jax.experimental.pallas import tpu as pltpu
from jax.experimental.pallas import tpu_sc as plsc

F32 = jnp.float32
BF16 = jnp.bfloat16

D_MODEL = 1024
HG_HEADS, HG_DK = 4, 128
HG_WIDTH = HG_HEADS * HG_DK
RET_HEADS, RET_DK = 4, 64
RET_WIDTH = RET_HEADS * RET_DK
GDN_HEADS, GDN_DK = 4, 64
GDN_WIDTH = GDN_HEADS * GDN_DK
CONV_K = 4
ROPE_BASE = 10000.0
N_EXPERTS = 32
TOP_K = 4
D_FF = D_MODEL
SWIGLU_ALPHA = 1.702
SWIGLU_LIMIT = 7.0
NORM_EPS = 1e-6
L2_EPS = 1e-6
GDN_CHUNK = 64

LANES = 128
SUBLANES = 8
VMEM_LIMIT = 56 * 1024 * 1024

T_BLK = 256
TM = 512
TM_IN = 512
ROUTE_SLAB = 128
COMBINE_PIECES = 4
R_BLK = 512
EXPERT_SLAB = 256
SC_ROWS = 128
HG_SUB = 128
HG_LEVELS = 7

ZH_W = 4 * HG_WIDTH
ZR_W = 6 * RET_WIDTH
ZG_W = 4 * GDN_WIDTH
ZAB_W = LANES
Z_W = ZH_W + ZR_W + ZG_W + ZAB_W


def _dot(a, b):
    return jnp.dot(a, b, preferred_element_type=F32)


def _dot_nt(a, b):
    return lax.dot_general(a, b, (((1,), (1,)), ((), ())), preferred_element_type=F32)


def _dot_tn(a, b):
    return lax.dot_general(a, b, (((0,), (0,)), ((), ())), preferred_element_type=F32)


def _split2(x):
    hi = x.astype(BF16)
    return hi, (x - hi.astype(F32)).astype(BF16)


def _dot2_lhs01(c, x):
    hi, lo = _split2(x)
    return _dot(c, hi) + _dot(c, lo)


def _dot2_rhs01(x, c):
    hi, lo = _split2(x)
    return _dot(hi, c) + _dot(lo, c)


def _sigmoid(x):
    return 1.0 / (1.0 + jnp.exp(-x))


def _silu(x):
    return x * _sigmoid(x)


def _softplus(x):
    return jnp.maximum(x, 0.0) + jnp.log1p(jnp.exp(-jnp.abs(x)))


def _bf(x):
    return x.astype(BF16)


def _pack_halves(x):
    w = x.shape[1] // 2
    bits = lax.bitcast_convert_type(_bf(x).astype(F32), jnp.uint32)
    return (bits[:, :w] >> 16) | bits[:, w:]


def _unpack_halves(p):
    lo = lax.bitcast_convert_type(p << 16, F32)
    hi = lax.bitcast_convert_type(p & jnp.uint32(0xFFFF0000), F32)
    return lo, hi


def _level_matrix(t):
    i = np.arange(t)[:, None]
    j = np.arange(t)[None, :]
    x = i ^ j
    lv = np.floor(np.log2(np.maximum(x, 1))).astype(np.int32)
    lv = np.where(i == j, int(math.log2(t)), lv)
    lv = np.where(i < j, -1, lv)
    return lv.astype(np.int32)


def _hgrn_exponent_matrix(t):
    n_lev = int(math.log2(t))
    f = np.zeros((2 + n_lev, t, t), np.float32)
    u = np.arange(t)[None, :]
    r = np.arange(t)[:, None]
    f[0] = (u <= r)
    f[1] = (u > r)
    for l in range(n_lev):
        h = 1 << l
        base = (r // (2 * h)) * (2 * h)
        mid = base + h
        upper = (r - base) >= h
        f[2 + l] = np.where(upper, (u >= mid) & (u <= r), (u > r) & (u < mid))
    return f.reshape((2 + n_lev) * t, t)


def _gdn_masks(t):
    lv = _level_matrix(GDN_CHUNK)
    top = int(math.log2(GDN_CHUNK))
    incl = lv >= 0
    d8 = (lv >= 0) & (lv <= 2)
    merges = [(lv == l) for l in range(3, top)]
    eye = lv == top
    pats = np.stack([incl, d8] + merges + [eye]).astype(np.float32)
    return np.tile(pats, (1, t // GDN_CHUNK, GDN_HEADS))


def _block_diag_mask(t, blk):
    i = np.arange(t)
    return (i[:, None] // blk == i[None, :] // blk).astype(np.float32)


def _head_masks(width, heads):
    lane = np.arange(width)[None, :]
    m = np.zeros((SUBLANES, width), np.float32)
    for h in range(heads):
        m[h] = (lane // (width // heads) == h)[0]
    return m


def _adaln_kernel(c_ref, w_ref, b_ref, o_ref):
    cond = _silu(c_ref[...])
    o_ref[0] = jnp.dot(cond, w_ref[0], preferred_element_type=F32,
                       precision=lax.Precision.HIGHEST) + b_ref[0]


def _adaln(c_pad, ada_w, ada_b):
    depth, d, n6 = ada_w.shape
    tn = n6 // 4
    return pl.pallas_call(
        _adaln_kernel,
        grid=(depth, n6 // tn),
        in_specs=[pl.BlockSpec((SUBLANES, d), lambda l, j: (0, 0)),
                  pl.BlockSpec((1, d, tn), lambda l, j: (l, 0, j)),
                  pl.BlockSpec((1, 1, tn), lambda l, j: (l, 0, j))],
        out_specs=pl.BlockSpec((1, SUBLANES, tn), lambda l, j: (l, 0, j)),
        out_shape=jax.ShapeDtypeStruct((depth, SUBLANES, n6), F32),
        compiler_params=pltpu.CompilerParams(vmem_limit_bytes=VMEM_LIMIT),
        name="adaln",
    )(c_pad, ada_w, ada_b.reshape(depth, 1, n6))


def _inproj_kernel(x_ref, mod_ref, g_ref, w_ref, zh_ref, zr_ref, zg_ref, zab_ref):
    x = x_ref[...]
    y = x * lax.rsqrt(jnp.mean(x * x, axis=-1, keepdims=True) + NORM_EPS)
    hn = (y * g_ref[0:1, :]) * (1.0 + mod_ref[0, 0:1, :]) + mod_ref[0, 1:2, :]
    hb = _bf(hn)
    zh_ref[...] = _dot(hb, w_ref[:, 0:ZH_W])
    zr_ref[...] = _dot(hb, w_ref[:, ZH_W:ZH_W + ZR_W])
    zg_ref[...] = _dot(hb, w_ref[:, ZH_W + ZR_W:ZH_W + ZR_W + ZG_W])
    zab_ref[...] = _dot(hb, w_ref[:, ZH_W + ZR_W + ZG_W:Z_W])


def _inproj(x2d, modv, gpar, w_bf, seq):
    n = x2d.shape[0]
    row = lambda i: (i, 0)
    tiles_per_batch = seq // TM_IN
    return pl.pallas_call(
        _inproj_kernel,
        grid=(n // TM_IN,),
        in_specs=[pl.BlockSpec((TM_IN, D_MODEL), row),
                  pl.BlockSpec((1, SUBLANES, D_MODEL), lambda i: (i // tiles_per_batch, 0, 0)),
                  pl.BlockSpec((SUBLANES, D_MODEL), lambda i: (0, 0)),
                  pl.BlockSpec((D_MODEL, Z_W), lambda i: (0, 0))],
        out_specs=[pl.BlockSpec((TM_IN, ZH_W), row), pl.BlockSpec((TM_IN, ZR_W), row),
                   pl.BlockSpec((TM_IN, ZG_W), row), pl.BlockSpec((TM_IN, ZAB_W), row)],
        out_shape=[jax.ShapeDtypeStruct((n, ZH_W), F32), jax.ShapeDtypeStruct((n, ZR_W), F32),
                   jax.ShapeDtypeStruct((n, ZG_W), F32), jax.ShapeDtypeStruct((n, ZAB_W), F32)],
        compiler_params=pltpu.CompilerParams(dimension_semantics=("arbitrary",),
                                             vmem_limit_bytes=VMEM_LIMIT),
        name="inproj",
    )(x2d, modv, gpar, w_bf)


def _hgrn_stages(zh_ref, f_ref, lv_ref, par_ref, o_ref, state_ref, ex_ref):
    W = HG_WIDTH
    hq = zh_ref[:, 0:W]
    hf = zh_ref[:, W:2 * W]
    loglb = par_ref[0:1, :]
    log1mlb = par_ref[1:2, :]
    onemlb = par_ref[2:3, :]

    q = _silu(hq)
    e = jnp.exp(-jnp.abs(hf))
    inv = 1.0 / (1.0 + e)
    k = onemlb * (jnp.where(hf >= 0, e, 1.0) * inv)
    logsig = jnp.minimum(hf, 0.0) - jnp.log1p(e)
    c = log1mlb + logsig
    lf = jnp.maximum(loglb, c) + jnp.log1p(jnp.exp(-jnp.abs(loglb - c)))
    yield

    lv = lv_ref[...]
    t = HG_SUB
    for sb in range(T_BLK // HG_SUB):
        rows = slice(sb * t, (sb + 1) * t)
        ex_ref[sb] = jnp.exp(_dot(f_ref[...], jnp.concatenate(_split2(lf[rows]), axis=0)))
        yield
        for h in range(HG_HEADS):
            cs = slice(h * HG_DK, (h + 1) * HG_DK)
            qh = q[rows, cs]
            kh = k[rows, cs]
            vh = _bf(zh_ref[rows, 2 * W + h * HG_DK:2 * W + (h + 1) * HG_DK])
            s = jnp.where(lv == HG_LEVELS, _dot_nt(_bf(qh), _bf(kh)), 0.0)
            for l in range(HG_LEVELS):
                el = ex_ref[sb, (2 + l) * t:(3 + l) * t, cs]
                s = jnp.where(lv == l, _dot_nt(_bf(qh * el), _bf(kh * el)), s)
                if l % 2 == 1:
                    yield
            eb = ex_ref[sb, 0:t, cs]
            ebl = ex_ref[sb, t:2 * t, cs]
            st = state_ref[h]
            o = _dot(_bf(s), vh) + _dot_nt(_bf(qh * eb), _bf(st))
            upd = _dot_tn(vh, _bf(kh * ebl))
            state_ref[h] = st * ex_ref[sb, t - 1:t, cs] + upd
            ms = jnp.mean(o * o, axis=-1, keepdims=True)
            gate = _silu(zh_ref[rows, 3 * W + h * HG_DK:3 * W + (h + 1) * HG_DK])
            o_ref[rows, cs] = _bf(o * lax.rsqrt(ms + NORM_EPS) * par_ref[3:4, cs] * gate)
            yield


def _ret_stages(zr_ref, cos_ref, sin_ref, dmat_ref, qw_ref, kw_ref, par_ref, hm_ref, bd_ref,
                o_ref, state_ref):
    W = RET_WIDTH
    cos = cos_ref[...]
    sin = sin_ref[...]
    q = zr_ref[:, 0:W] * cos + zr_ref[:, W:2 * W] * sin
    k = (zr_ref[:, 2 * W:3 * W] * cos + zr_ref[:, 3 * W:4 * W] * sin) * (RET_DK ** -0.5)
    v = zr_ref[:, 4 * W:5 * W]
    kb = _bf(k)
    bd = bd_ref[...]
    yield

    s_parts = []
    v_parts = []
    for h in range(RET_HEADS):
        hm = hm_ref[h:h + 1, :]
        s_parts.append(_bf(_dot_nt(_bf(q * hm), kb) * dmat_ref[h]))
        v_parts.append(_bf(v * hm))
        yield
    st = state_ref[...]
    o = (_dot(jnp.concatenate(s_parts, axis=1), jnp.concatenate(v_parts, axis=0))
         + _dot(_bf(q * qw_ref[...]), _bf(st)))
    yield
    kv = _dot_tn(_bf(k * kw_ref[...]), _bf(v))
    state_ref[...] = st * par_ref[0:1, :] + bd * kv
    yield

    ms = _dot2_rhs01(o * o, _bf(bd)) * (1.0 / RET_DK)
    gate = _silu(zr_ref[:, 5 * W:6 * W])
    o_ref[...] = _bf(o * lax.rsqrt(ms + NORM_EPS) * par_ref[1:2, :] * gate)


def _gdn_stages(zg_ref, zab_ref, convw_ref, par_ref, eab_ref, gm_ref, bd_ref, tril_ref, hm_ref,
                o_ref, ext_ref, state_ref, obuf_ref):
    W = GDN_WIDTH
    t = T_BLK
    u = zg_ref[:, 0:3 * W]
    ext_ref[SUBLANES:SUBLANES + t, :] = u
    conv = (convw_ref[3:4, :] * u
            + convw_ref[2:3, :] * ext_ref[SUBLANES - 1:SUBLANES - 1 + t, :]
            + convw_ref[1:2, :] * ext_ref[SUBLANES - 2:SUBLANES - 2 + t, :]
            + convw_ref[0:1, :] * ext_ref[SUBLANES - 3:SUBLANES - 3 + t, :])
    ext_ref[0:SUBLANES, :] = u[t - SUBLANES:t, :]
    qkv = _silu(conv)
    q = qkv[:, 0:W]
    k = qkv[:, W:2 * W]
    v = qkv[:, 2 * W:3 * W]
    yield

    bd = bd_ref[...]
    bdb = _bf(bd)
    ab = zab_ref[...]
    a_exp = _dot2_rhs01(ab, eab_ref[:, 0:W])
    b_exp = _dot2_rhs01(ab, eab_ref[:, W:2 * W])
    qn = q * lax.rsqrt(_dot2_rhs01(q * q, bdb) + L2_EPS) * (GDN_DK ** -0.5)
    kn = k * lax.rsqrt(_dot2_rhs01(k * k, bdb) + L2_EPS)
    yield
    beta = _sigmoid(b_exp)
    g = par_ref[0:1, :] * _softplus(a_exp + par_ref[1:2, :])
    gc = _dot2_lhs01(tril_ref[...], g)
    gl = _dot2_lhs01(bdb, g)
    yield
    eg = jnp.exp(gc)
    vb = v * beta
    kbeta = kn * beta * eg
    qdec = qn * eg
    kdec = kn * jnp.exp(gl - gc)

    n_chunks = t // GDN_CHUNK
    hms = [_bf(jnp.broadcast_to(hm_ref[h:h + 1, :], (GDN_CHUNK, W))) for h in range(GDN_HEADS)]
    hms2 = [jnp.concatenate([m, m], axis=1) for m in hms]

    def chunk(a, c):
        return a[c * GDN_CHUNK:(c + 1) * GDN_CHUNK]

    def expand(y, masks):
        yb = _bf(y)
        return jnp.concatenate([yb * m for m in masks], axis=0)

    def blockprod(x, y, masks):
        xb = _bf(x)
        return jnp.concatenate([_dot(chunk(xb, c), expand(chunk(y, c), masks))
                                for c in range(n_chunks)], axis=0)

    gc_row = _dot2_lhs01(bdb, gc * gm_ref[5])
    yield
    rel = jnp.exp(jnp.where(gm_ref[0] > 0, gc - gc_row, -jnp.inf))

    knbeta = kn * beta
    kq = [_dot_nt(_bf(jnp.concatenate([chunk(knbeta, c), chunk(qn, c)], axis=0)),
                  expand(chunk(kn, c), hms)) for c in range(n_chunks)]
    yield
    m = jnp.concatenate([r[0:GDN_CHUNK] for r in kq], axis=0) * rel
    qk = jnp.concatenate([r[GDN_CHUNK:] for r in kq], axis=0) * rel

    d = m * gm_ref[1]
    d2 = blockprod(d, d, hms)
    yield
    d4 = blockprod(d2, d2, hms)
    dd2 = blockprod(d, d2, hms)
    yield
    x = d2 - d - dd2
    xd4 = blockprod(x, d4, hms)
    yield
    x = x + d4 + xd4
    for lvl in range(2, 5):
        lo = m * gm_ref[lvl]
        xl = blockprod(x, lo, hms)
        yield
        y = lo + xl
        yx = blockprod(y, x, hms)
        yield
        x = x - (y + yx)

    vk = jnp.concatenate([vb, kbeta], axis=1)
    wk = vk + blockprod(x, vk, hms2)
    yield
    w = wk[:, 0:W]
    kcum = wk[:, W:2 * W]
    ag = blockprod(qk, wk, hms2)
    yield
    a1 = ag[:, 0:W]
    qeff = qdec - ag[:, W:2 * W]

    for c in range(t // GDN_CHUNK):
        rows = slice(c * GDN_CHUNK, (c + 1) * GDN_CHUNK)
        st = state_ref[...]
        stb = _bf(st)
        vnew = w[rows] - _dot(_bf(kcum[rows]), stb)
        obuf_ref[rows, :] = _dot(_bf(qeff[rows]), stb) + a1[rows]
        yield
        upd = _dot_tn(_bf(kdec[rows]), _bf(vnew))
        last = eg[(c + 1) * GDN_CHUNK - 1:(c + 1) * GDN_CHUNK, :]
        state_ref[...] = st * last + bd * upd
        yield

    o = obuf_ref[...]
    ms = _dot2_rhs01(o * o, bdb) * (1.0 / GDN_DK)
    gate = _silu(zg_ref[:, 3 * W:4 * W])
    o_ref[...] = _bf(o * lax.rsqrt(ms + NORM_EPS) * par_ref[2:3, :] * gate)


_DONE = object()

def _mixers_kernel(zh_ref, f_ref, lv_ref, hpar_ref,
                   zr_ref, cos_ref, sin_ref, dmat_ref, qw_ref, kw_ref, rpar_ref,
                   zg_ref, zab_ref, convw_ref, gpar_ref, eab_ref, gm_ref, tril_ref, hm_ref, bd_ref,
                   ohg_ref, oret_ref, ogdn_ref,
                   hstate_ref, hex_ref, rstate_ref, gext_ref, gstate_ref, gobuf_ref):
    @pl.when(pl.program_id(1) == 0)
    def _():
        hstate_ref[...] = jnp.zeros_like(hstate_ref)
        rstate_ref[...] = jnp.zeros_like(rstate_ref)
        gstate_ref[...] = jnp.zeros_like(gstate_ref)
        gext_ref[0:SUBLANES, :] = jnp.zeros((SUBLANES, 3 * GDN_WIDTH), F32)

    active = [
        (_gdn_stages(zg_ref, zab_ref, convw_ref, gpar_ref, eab_ref, gm_ref, bd_ref, tril_ref, hm_ref,
                     ogdn_ref, gext_ref, gstate_ref, gobuf_ref), 2),
        (_hgrn_stages(zh_ref, f_ref, lv_ref, hpar_ref, ohg_ref, hstate_ref, hex_ref), 3),
        (_ret_stages(zr_ref, cos_ref, sin_ref, dmat_ref, qw_ref, kw_ref, rpar_ref, hm_ref, bd_ref,
                     oret_ref, rstate_ref), 1),
    ]
    while active:
        for entry in list(active):
            gen, per_round = entry
            for _ in range(per_round):
                if next(gen, _DONE) is _DONE:
                    active.remove(entry)
                    break


def _mixers(zh, zr, zg, zab, hg_tabs, ret_tabs, gdn_tabs, hm, bd, batch, steps):
    n = zh.shape[0]
    seq_row = lambda b, j: (b * steps + j, 0)

    def const(a):
        return pl.BlockSpec(a.shape, lambda b, j, nd=a.ndim: (0,) * nd)

    def rows(width):
        return pl.BlockSpec((T_BLK, width), seq_row)

    pos_rows = pl.BlockSpec((T_BLK, RET_WIDTH), lambda b, j: (j, 0))
    cos_t, sin_t = ret_tabs[0], ret_tabs[1]
    in_specs = ([rows(ZH_W)] + [const(a) for a in hg_tabs]
                + [rows(ZR_W), pos_rows, pos_rows] + [const(a) for a in ret_tabs[2:]]
                + [rows(ZG_W), rows(ZAB_W)] + [const(a) for a in gdn_tabs] + [const(hm), const(bd)])
    return pl.pallas_call(
        _mixers_kernel,
        grid=(batch, steps),
        in_specs=in_specs,
        out_specs=[rows(HG_WIDTH), rows(RET_WIDTH), rows(GDN_WIDTH)],
        out_shape=[jax.ShapeDtypeStruct((n, HG_WIDTH), BF16), jax.ShapeDtypeStruct((n, RET_WIDTH), BF16),
                   jax.ShapeDtypeStruct((n, GDN_WIDTH), BF16)],
        scratch_shapes=[pltpu.VMEM((HG_HEADS, HG_DK, HG_DK), F32),
                        pltpu.VMEM((T_BLK // HG_SUB, (2 + HG_LEVELS) * HG_SUB, HG_WIDTH), F32),
                        pltpu.VMEM((RET_WIDTH, RET_WIDTH), F32),
                        pltpu.VMEM((SUBLANES + T_BLK, 3 * GDN_WIDTH), F32),
                        pltpu.VMEM((GDN_WIDTH, GDN_WIDTH), F32),
                        pltpu.VMEM((T_BLK, GDN_WIDTH), F32)],
        compiler_params=pltpu.CompilerParams(dimension_semantics=("arbitrary", "arbitrary"),
                                             vmem_limit_bytes=VMEM_LIMIT),
        name="mixers",
    )(zh, *hg_tabs, zr, cos_t, sin_t, *ret_tabs[2:], zg, zab, *gdn_tabs, hm, bd)


def _outproj_kernel(ohg_ref, oret_ref, ogdn_ref, x_ref, mod_ref, g_ref, w_ref, rw_ref, rb_ref,
                    su_ref, x1_ref, hn_ref, ri_ref, gates_ref, cnt_ref, run_ref):
    @pl.when(pl.program_id(0) == 0)
    def _():
        run_ref[...] = jnp.zeros_like(run_ref)

    rw_hi = rw_ref[0:LANES, :]
    rw_lo = rw_ref[LANES:2 * LANES, :]
    eid = lax.broadcasted_iota(jnp.int32, (LANES, ROUTE_SLAB), 0)
    routed = {}

    def slab_stages(s):
        rows = slice(s * ROUTE_SLAB, (s + 1) * ROUTE_SLAB)
        y = (_dot(ohg_ref[rows, :], w_ref[0:HG_WIDTH, :])
             + _dot(oret_ref[rows, :], w_ref[HG_WIDTH:HG_WIDTH + RET_WIDTH, :])
             + _dot(ogdn_ref[rows, :], w_ref[HG_WIDTH + RET_WIDTH:, :]))
        yield
        x1 = x_ref[rows, :] + mod_ref[0, 0:1, :] * y
        x1_ref[rows, :] = x1
        n = x1 * lax.rsqrt(jnp.mean(x1 * x1, axis=-1, keepdims=True) + NORM_EPS)
        hn = (n * g_ref[0:1, :]) * (1.0 + mod_ref[0, 1:2, :]) + mod_ref[0, 2:3, :]
        hn_ref[rows, :] = _pack_halves(hn)
        yield
        hn_hi, hn_lo = _split2(hn)
        logits = (_dot_nt(rw_hi, hn_hi) + _dot_nt(rw_lo, hn_hi) + _dot_nt(rw_hi, hn_lo)
                  + rb_ref[:, rows])
        yield
        work = jnp.where(eid < N_EXPERTS, logits, -jnp.inf)
        vals, idxs = [], []
        multihot = jnp.zeros((LANES, ROUTE_SLAB), F32)
        for kk in range(TOP_K):
            mx = jnp.max(work, axis=0, keepdims=True)
            ix = jnp.min(jnp.where(work == mx, eid, LANES), axis=0, keepdims=True)
            sel = eid == ix
            multihot = jnp.where(sel, 1.0, multihot)
            work = jnp.where(sel, -jnp.inf, work)
            vals.append(mx)
            idxs.append(ix)
            if kk % 2 == 1:
                yield
        routed[s] = (vals, idxs, multihot)

    n_slabs = TM // ROUTE_SLAB
    active = [slab_stages(s) for s in range(n_slabs)]
    while active:
        for gen in list(active):
            if next(gen, _DONE) is _DONE:
                active.remove(gen)

    multihot = jnp.concatenate([routed[s][2] for s in range(n_slabs)], axis=1)
    run = run_ref[...]
    before = _dot(_bf(multihot), su_ref[...]) + jnp.concatenate([run] * (TM // LANES), axis=1)
    run = run + jnp.sum(multihot, axis=1, keepdims=True)
    run_ref[...] = run
    cnt_ref[...] = run

    for s in range(n_slabs):
        vals, idxs, _ = routed[s]
        cols = slice(s * ROUTE_SLAB, (s + 1) * ROUTE_SLAB)
        ex = [jnp.exp(vv - vals[0]) for vv in vals]
        den = ex[0] + ex[1] + ex[2] + ex[3]
        ranks = [jnp.sum(jnp.where(eid == ix, before[:, cols], 0.0), axis=0, keepdims=True) for ix in idxs]
        ri_ref[:, cols] = jnp.concatenate(idxs + [r.astype(jnp.int32) for r in ranks], axis=0)
        gates_ref[:, cols] = jnp.concatenate([e / den for e in ex]
                                             + [jnp.zeros((TOP_K, ROUTE_SLAB), F32)], axis=0)


def _outproj(ohg, oret, ogdn, x2d, modv, gpar, w_bf, rw, rb, su, tiles_per_batch):
    n = x2d.shape[0]
    row = lambda i: (i, 0)
    col = lambda i: (0, i)
    const = lambda i: (0, 0)
    return pl.pallas_call(
        _outproj_kernel,
        grid=(n // TM,),
        in_specs=[pl.BlockSpec((TM, HG_WIDTH), row), pl.BlockSpec((TM, RET_WIDTH), row),
                  pl.BlockSpec((TM, GDN_WIDTH), row), pl.BlockSpec((TM, D_MODEL), row),
                  pl.BlockSpec((1, SUBLANES, D_MODEL), lambda i: (i // tiles_per_batch, 0, 0)),
                  pl.BlockSpec((SUBLANES, D_MODEL), const),
                  pl.BlockSpec((D_MODEL, D_MODEL), const),
                  pl.BlockSpec((2 * LANES, D_MODEL), const),
                  pl.BlockSpec((LANES, TM), const),
                  pl.BlockSpec((TM, TM), const)],
        out_specs=[pl.BlockSpec((TM, D_MODEL), row), pl.BlockSpec((TM, D_MODEL // 2), row),
                   pl.BlockSpec((2 * TOP_K, TM), col), pl.BlockSpec((2 * TOP_K, TM), col),
                   pl.BlockSpec((LANES, LANES), const)],
        out_shape=[jax.ShapeDtypeStruct((n, D_MODEL), F32),
                   jax.ShapeDtypeStruct((n, D_MODEL // 2), jnp.uint32),
                   jax.ShapeDtypeStruct((2 * TOP_K, n), jnp.int32),
                   jax.ShapeDtypeStruct((2 * TOP_K, n), F32),
                   jax.ShapeDtypeStruct((LANES, LANES), F32)],
        scratch_shapes=[pltpu.VMEM((LANES, LANES), F32)],
        compiler_params=pltpu.CompilerParams(dimension_semantics=("arbitrary",),
                                             vmem_limit_bytes=VMEM_LIMIT),
        name="outproj_router",
    )(ohg, oret, ogdn, x2d, modv, gpar, w_bf, rw, rb, su)


def _expert_kernel(be_ref, nu_ref, x_ref, w1_ref, b1_ref, w2_ref, b2_ref, y_ref, w1b_ref, w2b_ref):
    i = pl.program_id(0)
    prev = be_ref[jnp.maximum(i - 1, 0)]
    fresh = jnp.logical_or(i == 0, be_ref[i] != prev)

    @pl.when(fresh)
    def _():
        w1b_ref[...] = _bf(w1_ref[0])
        w2b_ref[...] = _bf(w2_ref[0])

    @pl.when(i < nu_ref[0])
    def _():
        half = D_MODEL // 2

        def slab_stages(s):
            rows = slice(s * EXPERT_SLAB, (s + 1) * EXPERT_SLAB)
            x_lo, x_hi = _unpack_halves(x_ref[rows, :])
            hid = (_dot(_bf(x_lo), w1b_ref[0:half, :]) + _dot(_bf(x_hi), w1b_ref[half:, :])
                   + b1_ref[0])
            yield
            x_glu = jnp.minimum(hid[:, 0:D_FF], SWIGLU_LIMIT)
            x_lin = jnp.clip(hid[:, D_FF:], -SWIGLU_LIMIT, SWIGLU_LIMIT)
            act = x_glu * _sigmoid(SWIGLU_ALPHA * x_glu) * (x_lin + 1.0)
            yield
            y_ref[rows, :] = _pack_halves(_dot(_bf(act), w2b_ref[...]) + b2_ref[0])

        active = [slab_stages(s) for s in range(R_BLK // EXPERT_SLAB)]
        while active:
            for gen in list(active):
                if next(gen, _DONE) is _DONE:
                    active.remove(gen)


def _experts(block_e, n_used, xs, w1, b1, w2, b2):
    p = xs.shape[0]
    n_blocks = p // R_BLK
    ne = w1.shape[0] * w1.shape[1]
    w1 = w1.reshape(ne, D_MODEL, 2 * D_FF)
    w2 = w2.reshape(ne, D_FF, D_MODEL)
    rowmap = lambda i, be, nu: (jnp.minimum(i, nu[0] - 1), 0)
    emap = lambda i, be, nu: (be[i], 0, 0)
    grid_spec = pltpu.PrefetchScalarGridSpec(
        num_scalar_prefetch=2,
        grid=(n_blocks,),
        in_specs=[pl.BlockSpec((R_BLK, D_MODEL // 2), rowmap),
                  pl.BlockSpec((1, D_MODEL, 2 * D_FF), emap),
                  pl.BlockSpec((1, 1, 2 * D_FF), emap),
                  pl.BlockSpec((1, D_FF, D_MODEL), emap),
                  pl.BlockSpec((1, 1, D_MODEL), emap)],
        out_specs=pl.BlockSpec((R_BLK, D_MODEL // 2), rowmap),
        scratch_shapes=[pltpu.VMEM((D_MODEL, 2 * D_FF), BF16), pltpu.VMEM((D_FF, D_MODEL), BF16)],
    )
    return pl.pallas_call(
        _expert_kernel,
        grid_spec=grid_spec,
        out_shape=jax.ShapeDtypeStruct((p, D_MODEL // 2), jnp.uint32),
        compiler_params=pltpu.CompilerParams(dimension_semantics=("arbitrary",),
                                             vmem_limit_bytes=VMEM_LIMIT),
        name="experts",
    )(block_e, n_used, xs, w1, b1.reshape(ne, 1, 2 * D_FF), w2, b2.reshape(ne, 1, D_MODEL))


def _sc_mesh():
    return plsc.VectorSubcoreMesh(core_axis_name="c", subcore_axis_name="s")


def _sc_scatter_rows(x, pos, p_rows):
    mesh = _sc_mesh()
    n, w = x.shape
    per_worker = n // (mesh.num_cores * mesh.num_subcores)
    assert per_worker % SC_ROWS == 0 and pos.shape == (TOP_K, n)

    @functools.partial(
        pl.kernel, out_type=jax.ShapeDtypeStruct((p_rows, w), x.dtype), mesh=mesh,
        scratch_types=[pltpu.VMEM((SC_ROWS,), jnp.int32)] * TOP_K
        + [pltpu.VMEM((SC_ROWS, w), x.dtype), pltpu.SemaphoreType.DMA],
        name="dispatch_rows")
    def scatter(x_hbm, p_hbm, o_hbm, i0, i1, i2, i3, rows_v, sem):
        idx = (i0, i1, i2, i3)
        worker = lax.axis_index("s") * mesh.num_cores + lax.axis_index("c")

        @pl.loop(0, per_worker // SC_ROWS)
        def _(g):
            base = pl.multiple_of(worker * per_worker + g * SC_ROWS, SC_ROWS)
            pltpu.sync_copy(x_hbm.at[pl.ds(base, SC_ROWS)], rows_v)
            for kk in range(TOP_K):
                pltpu.sync_copy(p_hbm.at[kk, pl.ds(base, SC_ROWS)], idx[kk])
            copies = [pltpu.async_copy(rows_v, o_hbm.at[idx[kk]], sem) for kk in range(TOP_K)]
            for cp in copies:
                cp.wait()

    return scatter(x, pos)


def _sc_gather_rows(table, idx):
    mesh = _sc_mesh()
    n_idx = idx.shape[0]
    w = table.shape[1]
    per_worker = n_idx // (mesh.num_cores * mesh.num_subcores)
    assert per_worker % SC_ROWS == 0

    @functools.partial(
        pl.kernel, out_type=jax.ShapeDtypeStruct((n_idx, w), table.dtype), mesh=mesh,
        scratch_types=[pltpu.VMEM((SC_ROWS,), jnp.int32), pltpu.VMEM((SC_ROWS, w), table.dtype),
                       pltpu.SemaphoreType.DMA],
        name="combine_rows")
    def gather(t_hbm, i_hbm, o_hbm, idx_v, rows_v, sem):
        worker = lax.axis_index("s") * mesh.num_cores + lax.axis_index("c")

        @pl.loop(0, per_worker // SC_ROWS)
        def _(g):
            base = pl.multiple_of(worker * per_worker + g * SC_ROWS, SC_ROWS)
            pltpu.sync_copy(i_hbm.at[pl.ds(base, SC_ROWS)], idx_v)
            pltpu.async_copy(t_hbm.at[idx_v], rows_v, sem).wait()
            pltpu.sync_copy(rows_v, o_hbm.at[pl.ds(base, SC_ROWS)])

    return gather(table, idx)


def _combine_kernel(x_ref, y0_ref, y1_ref, y2_ref, y3_ref, gates_ref, mod_ref, g_ref, o_ref, *, final):
    gt = gates_ref[...]
    half = D_MODEL // 2
    acc_lo, acc_hi = None, None
    for kk, y_ref in enumerate((y0_ref, y1_ref, y2_ref, y3_ref)):
        lo, hi = _unpack_halves(y_ref[0])
        g = gt[:, kk:kk + 1]
        acc_lo = g * lo if acc_lo is None else acc_lo + g * lo
        acc_hi = g * hi if acc_hi is None else acc_hi + g * hi
    x_lo = x_ref[:, 0:half] + mod_ref[0, 3:4, 0:half] * acc_lo
    x_hi = x_ref[:, half:] + mod_ref[0, 3:4, half:] * acc_hi
    if final:
        ssq = (jnp.sum(x_lo * x_lo, axis=-1, keepdims=True)
               + jnp.sum(x_hi * x_hi, axis=-1, keepdims=True))
        r = lax.rsqrt(ssq * (1.0 / D_MODEL) + NORM_EPS)
        x_lo = x_lo * r * g_ref[1:2, 0:half]
        x_hi = x_hi * r * g_ref[1:2, half:]
    o_ref[:, 0:half] = x_lo
    o_ref[:, half:] = x_hi


def _combine(x, yg, gates, modv, gpar, tiles_per_batch, final, piece):
    n = x.shape[0]
    tiles = n // TM // COMBINE_PIECES
    first = piece * tiles
    row = lambda i: (first + i, 0)
    ysp = [pl.BlockSpec((1, TM, D_MODEL // 2), (lambda i, kk=kk: (kk, i, 0))) for kk in range(TOP_K)]
    return pl.pallas_call(
        functools.partial(_combine_kernel, final=final),
        grid=(tiles,),
        in_specs=[pl.BlockSpec((TM, D_MODEL), row)] + ysp + [
            pl.BlockSpec((TM, 2 * TOP_K), row),
            pl.BlockSpec((1, SUBLANES, D_MODEL), lambda i: ((first + i) // tiles_per_batch, 0, 0)),
            pl.BlockSpec((SUBLANES, D_MODEL), lambda i: (0, 0))],
        out_specs=pl.BlockSpec((TM, D_MODEL), row),
        out_shape=jax.ShapeDtypeStruct((n, D_MODEL), F32),
        input_output_aliases={0: 0},
        compiler_params=pltpu.CompilerParams(dimension_semantics=("arbitrary",),
                                             vmem_limit_bytes=VMEM_LIMIT),
        name="combine",
    )(x, yg, yg, yg, yg, gates, modv, gpar)


def _rot_cols(w):
    d = w.shape[0]
    w4 = w.reshape(d, RET_HEADS, 2, RET_DK // 2)
    return jnp.concatenate([-w4[:, :, 1:2], w4[:, :, 0:1]], axis=2).reshape(d, RET_WIDTH)


def _widen_w_in(w):
    o = 0
    hg = w[:, o:o + 4 * HG_WIDTH]; o += 4 * HG_WIDTH
    rq = w[:, o:o + RET_WIDTH]; o += RET_WIDTH
    rk = w[:, o:o + RET_WIDTH]; o += RET_WIDTH
    rv = w[:, o:o + RET_WIDTH]; o += RET_WIDTH
    rg = w[:, o:o + RET_WIDTH]; o += RET_WIDTH
    gd = w[:, o:o + 4 * GDN_WIDTH]; o += 4 * GDN_WIDTH
    gab = w[:, o:o + 2 * GDN_HEADS]
    pad = jnp.zeros((w.shape[0], ZAB_W - 2 * GDN_HEADS), w.dtype)
    return _bf(jnp.concatenate([hg, rq, _rot_cols(rq), rk, _rot_cols(rk), rv, rg, gd, gab, pad], axis=1))


def _pad_rows(a, rows=SUBLANES):
    return jnp.concatenate([a, jnp.zeros((rows - a.shape[0],) + a.shape[1:], a.dtype)], axis=0)


def kernel(x, c, ada_w, ada_b, norm1_g, norm2_g, w_in, w_out, hg_lb_logits, hg_norm_g, ret_norm_g,
           gdn_conv_w, gdn_A_log, gdn_dt_bias, gdn_norm_g, router_w, router_b, exp_w1, exp_b1,
           exp_w2, exp_b2, final_norm_g):
    batch, seq, d = x.shape
    depth = ada_w.shape[0]
    n = batch * seq
    steps = seq // T_BLK
    tiles_per_batch = seq // TM
    nk = n * TOP_K
    n_blocks = nk // R_BLK + N_EXPERTS
    p_rows = n_blocks * R_BLK

    lv_np = _level_matrix(T_BLK)
    lv = jnp.asarray(_level_matrix(HG_SUB))
    f_np = _hgrn_exponent_matrix(HG_SUB)
    f_mat = jnp.asarray(np.concatenate([f_np, f_np], axis=1), BF16)
    gm = jnp.asarray(_gdn_masks(T_BLK))
    bd = jnp.asarray(_block_diag_mask(T_BLK, GDN_CHUNK))
    tril = jnp.asarray(_block_diag_mask(T_BLK, GDN_CHUNK) * (lv_np >= 0), BF16)
    hm = jnp.asarray(_head_masks(RET_WIDTH, RET_HEADS))
    su = jnp.asarray(np.triu(np.ones((TM, TM), np.float32), 1), BF16)
    eab_np = np.zeros((LANES, 2 * GDN_WIDTH), np.float32)
    for h in range(GDN_HEADS):
        eab_np[h, h * GDN_DK:(h + 1) * GDN_DK] = 1.0
        eab_np[GDN_HEADS + h, GDN_WIDTH + h * GDN_DK:GDN_WIDTH + (h + 1) * GDN_DK] = 1.0
    eab = jnp.asarray(eab_np, BF16)

    half = RET_DK // 2
    inv = ROPE_BASE ** (-jnp.linspace(0.0, 1.0, half, dtype=F32))
    ang = jnp.arange(seq, dtype=F32)[:, None] * inv[None, :]
    cos_t = jnp.tile(jnp.cos(ang), (1, 2 * RET_HEADS))
    sin_t = jnp.tile(jnp.sin(ang), (1, 2 * RET_HEADS))
    log_g = jnp.log1p(-jnp.exp2(-5.0 - jnp.arange(RET_HEADS, dtype=F32)))
    jj = jnp.arange(T_BLK, dtype=F32)
    diff = jj[:, None] - jj[None, :]
    dmat = jnp.where(diff[None] >= 0, jnp.exp(diff[None] * log_g[:, None, None]), 0.0)
    lg_lane = jnp.repeat(log_g, RET_DK)[None, :]
    qw = jnp.exp(lg_lane * (jj[:, None] + 1.0))
    kw = jnp.exp(lg_lane * (T_BLK - 1.0 - jj[:, None]))
    cdec = jnp.exp(T_BLK * lg_lane)

    lb = jnp.cumsum(jax.nn.softmax(hg_lb_logits.astype(F32), axis=0), axis=0)
    lb = jnp.maximum(lb - lb[0], 0.0)
    c_pad = _pad_rows(c.astype(F32))
    mod = _adaln(c_pad, ada_w, ada_b)[:, :batch, :]

    x2d = x.reshape(n, d)
    out = None
    for l in range(depth):
        sh1, sc1, gt1, sh2, sc2, gt2 = [mod[l][:, i * d:(i + 1) * d] for i in range(6)]
        zeros = jnp.zeros_like(sh1)
        modv_a = jnp.stack([sc1, sh1, zeros, zeros, zeros, zeros, zeros, zeros], axis=1)
        modv_c = jnp.stack([gt1, sc2, sh2, gt2, zeros, zeros, zeros, zeros], axis=1)
        gpar_a = _pad_rows(norm1_g[l][None, :])
        gpar_c = _pad_rows(jnp.stack([norm2_g[l], final_norm_g], axis=0))

        zh, zr, zg, zab = _inproj(x2d, modv_a, gpar_a, _widen_w_in(w_in[l]), seq)

        hg_par = _pad_rows(jnp.stack([jnp.log(lb[l]), jnp.log1p(-lb[l]), 1.0 - lb[l],
                                      hg_norm_g[l].reshape(-1)], axis=0))
        ret_par = _pad_rows(jnp.concatenate([cdec, ret_norm_g[l].reshape(1, -1)], axis=0))
        gdn_par = _pad_rows(jnp.stack([jnp.repeat(-jnp.exp(gdn_A_log[l].astype(F32)), GDN_DK),
                                       jnp.repeat(gdn_dt_bias[l].astype(F32), GDN_DK),
                                       gdn_norm_g[l].reshape(-1)], axis=0))
        o_hg, o_ret, o_gdn = _mixers(
            zh, zr, zg, zab,
            (f_mat, lv, hg_par),
            (cos_t, sin_t, dmat, qw, kw, ret_par),
            (_pad_rows(gdn_conv_w[l].astype(F32)), gdn_par, eab, gm, tril),
            hm, bd, batch, steps)

        rw_f = jnp.concatenate([router_w[l].T, jnp.zeros((LANES - N_EXPERTS, d), F32)], axis=0)
        rw = jnp.concatenate(_split2(rw_f), axis=0)
        rb = jnp.broadcast_to(jnp.concatenate([router_b[l], jnp.zeros((LANES - N_EXPERTS,), F32)])[:, None],
                              (LANES, TM))
        x1, hn2, ri, gates_t, cnt = _outproj(o_hg, o_ret, o_gdn, x2d, modv_c, gpar_c, _bf(w_out[l]),
                                             rw, rb, su, tiles_per_batch)
        gates = gates_t.T

        counts = cnt[:N_EXPERTS, 0].astype(jnp.int32)
        padded = (counts + R_BLK - 1) // R_BLK * R_BLK
        pend = jnp.cumsum(padded)
        pstart = pend - padded
        n_used = (pend[-1] // R_BLK).astype(jnp.int32)
        blk_start = jnp.arange(n_blocks, dtype=jnp.int32) * R_BLK
        blk_start = jnp.minimum(blk_start, pend[-1] - R_BLK)
        block_e = jnp.sum(blk_start[:, None] >= pend[None, :], axis=1).astype(jnp.int32)
        eid = jnp.arange(N_EXPERTS, dtype=jnp.int32)
        pos_t = ri[TOP_K:] + jnp.sum(jnp.where(ri[:TOP_K, :, None] == eid, pstart, 0), axis=-1)
        xs = _sc_scatter_rows(hn2, pos_t, p_rows)
        y = _experts(block_e + l * N_EXPERTS, n_used.reshape(1), xs, exp_w1, exp_b1, exp_w2, exp_b2)
        n_piece = n // COMBINE_PIECES
        x2d = x1
        for p in range(COMBINE_PIECES):
            idx = pos_t[:, p * n_piece:(p + 1) * n_piece].reshape(-1)
            yg = _sc_gather_rows(y, idx).reshape(TOP_K, n_piece, d // 2)
            x2d = _combine(x2d, yg, gates, modv_c, gpar_c, tiles_per_batch, l == depth - 1, p)
    return x2d.reshape(batch, seq, d)
```

```python
import functools
import math

import numpy as np
import jax
import jax.numpy as jnp
from jax import lax
from jax.experimental import pallas as pl
from jax.experimental.pallas import tpu as pltpu
from jax.experimental.pallas import tpu_sc as plsc

F32 = jnp.float32
BF16 = jnp.bfloat16

D_MODEL = 1024
HG_HEADS, HG_DK = 4, 128
HG_WIDTH = HG_HEADS * HG_DK
RET_HEADS, RET_DK = 4, 64
RET_WIDTH = RET_HEADS * RET_DK
GDN_HEADS, GDN_DK = 4, 64
GDN_WIDTH = GDN_HEADS * GDN_DK
CONV_K = 4
ROPE_BASE = 10000.0
N_EXPERTS = 32
TOP_K = 4
D_FF = D_MODEL
SWIGLU_ALPHA = 1.702
SWIGLU_LIMIT = 7.0
NORM_EPS = 1e-6
L2_EPS = 1e-6
GDN_CHUNK = 64

LANES = 128
SUBLANES = 8
VMEM_LIMIT = 56 * 1024 * 1024

T_BLK = 256
TM = 512
TM_IN = 512
ROUTE_SLAB = 128
COMBINE_PIECES = 4
R_BLK = 512
EXPERT_SLAB = 256
SC_ROWS = 128
HG_SUB = 128
HG_LEVELS = 7

ZH_W = 4 * HG_WIDTH
ZR_W = 6 * RET_WIDTH
ZG_W = 4 * GDN_WIDTH
ZAB_W = LANES
Z_W = ZH_W + ZR_W + ZG_W + ZAB_W


def _dot(a, b):
    return jnp.dot(a, b, preferred_element_type=F32)


def _dot_nt(a, b):
    return lax.dot_general(a, b, (((1,), (1,)), ((), ())), preferred_element_type=F32)


def _dot_tn(a, b):
    return lax.dot_general(a, b, (((0,), (0,)), ((), ())), preferred_element_type=F32)


def _split2(x):
    hi = x.astype(BF16)
    return hi, (x - hi.astype(F32)).astype(BF16)


def _dot2_lhs01(c, x):
    hi, lo = _split2(x)
    return _dot(c, hi) + _dot(c, lo)


def _dot2_rhs01(x, c):
    hi, lo = _split2(x)
    return _dot(hi, c) + _dot(lo, c)


def _sigmoid(x):
    return 1.0 / (1.0 + jnp.exp(-x))


def _silu(x):
    return x * _sigmoid(x)


def _softplus(x):
    return jnp.maximum(x, 0.0) + jnp.log1p(jnp.exp(-jnp.abs(x)))


def _bf(x):
    return x.astype(BF16)


def _pack_halves(x):
    w = x.shape[1] // 2
    bits = lax.bitcast_convert_type(_bf(x).astype(F32), jnp.uint32)
    return (bits[:, :w] >> 16) | bits[:, w:]


def _unpack_halves(p):
    lo = lax.bitcast_convert_type(p << 16, F32)
    hi = lax.bitcast_convert_type(p & jnp.uint32(0xFFFF0000), F32)
    return lo, hi


def _level_matrix(t):
    i = np.arange(t)[:, None]
    j = np.arange(t)[None, :]
    x = i ^ j
    lv = np.floor(np.log2(np.maximum(x, 1))).astype(np.int32)
    lv = np.where(i == j, int(math.log2(t)), lv)
    lv = np.where(i < j, -1, lv)
    return lv.astype(np.int32)


def _hgrn_exponent_matrix(t):
    n_lev = int(math.log2(t))
    f = np.zeros((2 + n_lev, t, t), np.float32)
    u = np.arange(t)[None, :]
    r = np.arange(t)[:, None]
    f[0] = (u <= r)
    f[1] = (u > r)
    for l in range(n_lev):
        h = 1 << l
        base = (r // (2 * h)) * (2 * h)
        mid = base + h
        upper = (r - base) >= h
        f[2 + l] = np.where(upper, (u >= mid) & (u <= r), (u > r) & (u < mid))
    return f.reshape((2 + n_lev) * t, t)


def _gdn_masks(t):
    lv = _level_matrix(GDN_CHUNK)
    top = int(math.log2(GDN_CHUNK))
    incl = lv >= 0
    d8 = (lv >= 0) & (lv <= 2)
    merges = [(lv == l) for l in range(3, top)]
    eye = lv == top
    pats = np.stack([incl, d8] + merges + [eye]).astype(np.float32)
    return np.tile(pats, (1, t // GDN_CHUNK, GDN_HEADS))


def _block_diag_mask(t, blk):
    i = np.arange(t)
    return (i[:, None] // blk == i[None, :] // blk).astype(np.float32)


def _head_masks(width, heads):
    lane = np.arange(width)[None, :]
    m = np.zeros((SUBLANES, width), np.float32)
    for h in range(heads):
        m[h] = (lane // (width // heads) == h)[0]
    return m


def _adaln_kernel(c_ref, w_ref, b_ref, o_ref):
    cond = _silu(c_ref[...])
    o_ref[0] = jnp.dot(cond, w_ref[0], preferred_element_type=F32,
                       precision=lax.Precision.HIGHEST) + b_ref[0]


def _adaln(c_pad, ada_w, ada_b):
    depth, d, n6 = ada_w.shape
    tn = n6 // 4
    return pl.pallas_call(
        _adaln_kernel,
        grid=(depth, n6 // tn),
        in_specs=[pl.BlockSpec((SUBLANES, d), lambda l, j: (0, 0)),
                  pl.BlockSpec((1, d, tn), lambda l, j: (l, 0, j)),
                  pl.BlockSpec((1, 1, tn), lambda l, j: (l, 0, j))],
        out_specs=pl.BlockSpec((1, SUBLANES, tn), lambda l, j: (l, 0, j)),
        out_shape=jax.ShapeDtypeStruct((depth, SUBLANES, n6), F32),
        compiler_params=pltpu.CompilerParams(vmem_limit_bytes=VMEM_LIMIT),
        name="adaln",
    )(c_pad, ada_w, ada_b.reshape(depth, 1, n6))


def _inproj_kernel(x_ref, mod_ref, g_ref, w_ref, zh_ref, zr_ref, zg_ref, zab_ref):
    x = x_ref[...]
    y = x * lax.rsqrt(jnp.mean(x * x, axis=-1, keepdims=True) + NORM_EPS)
    hn = (y * g_ref[0:1, :]) * (1.0 + mod_ref[0, 0:1, :]) + mod_ref[0, 1:2, :]
    hb = _bf(hn)
    zh_ref[...] = _dot(hb, w_ref[:, 0:ZH_W])
    zr_ref[...] = _dot(hb, w_ref[:, ZH_W:ZH_W + ZR_W])
    zg_ref[...] = _dot(hb, w_ref[:, ZH_W + ZR_W:ZH_W + ZR_W + ZG_W])
    zab_ref[...] = _dot(hb, w_ref[:, ZH_W + ZR_W + ZG_W:Z_W])


def _inproj(x2d, modv, gpar, w_bf, seq):
    n = x2d.shape[0]
    row = lambda i: (i, 0)
    tiles_per_batch = seq // TM_IN
    return pl.pallas_call(
        _inproj_kernel,
        grid=(n // TM_IN,),
        in_specs=[pl.BlockSpec((TM_IN, D_MODEL), row),
                  pl.BlockSpec((1, SUBLANES, D_MODEL), lambda i: (i // tiles_per_batch, 0, 0)),
                  pl.BlockSpec((SUBLANES, D_MODEL), lambda i: (0, 0)),
                  pl.BlockSpec((D_MODEL, Z_W), lambda i: (0, 0))],
        out_specs=[pl.BlockSpec((TM_IN, ZH_W), row), pl.BlockSpec((TM_IN, ZR_W), row),
                   pl.BlockSpec((TM_IN, ZG_W), row), pl.BlockSpec((TM_IN, ZAB_W), row)],
        out_shape=[jax.ShapeDtypeStruct((n, ZH_W), F32), jax.ShapeDtypeStruct((n, ZR_W), F32),
                   jax.ShapeDtypeStruct((n, ZG_W), F32), jax.ShapeDtypeStruct((n, ZAB_W), F32)],
        compiler_params=pltpu.CompilerParams(dimension_semantics=("arbitrary",),
                                             vmem_limit_bytes=VMEM_LIMIT),
        name="inproj",
    )(x2d, modv, gpar, w_bf)


def _hgrn_stages(zh_ref, f_ref, lv_ref, par_ref, o_ref, state_ref, ex_ref):
    W = HG_WIDTH
    hq = zh_ref[:, 0:W]
    hf = zh_ref[:, W:2 * W]
    loglb = par_ref[0:1, :]
    log1mlb = par_ref[1:2, :]
    onemlb = par_ref[2:3, :]

    q = _silu(hq)
    e = jnp.exp(-jnp.abs(hf))
    inv = 1.0 / (1.0 + e)
    k = onemlb * (jnp.where(hf >= 0, e, 1.0) * inv)
    logsig = jnp.minimum(hf, 0.0) - jnp.log1p(e)
    c = log1mlb + logsig
    lf = jnp.maximum(loglb, c) + jnp.log1p(jnp.exp(-jnp.abs(loglb - c)))
    yield

    lv = lv_ref[...]
    t = HG_SUB
    for sb in range(T_BLK // HG_SUB):
        rows = slice(sb * t, (sb + 1) * t)
        ex_ref[sb] = jnp.exp(_dot(f_ref[...], jnp.concatenate(_split2(lf[rows]), axis=0)))
        yield
        for h in range(HG_HEADS):
            cs = slice(h * HG_DK, (h + 1) * HG_DK)
            qh = q[rows, cs]
            kh = k[rows, cs]
            vh = _bf(zh_ref[rows, 2 * W + h * HG_DK:2 * W + (h + 1) * HG_DK])
            s = jnp.where(lv == HG_LEVELS, _dot_nt(_bf(qh), _bf(kh)), 0.0)
            for l in range(HG_LEVELS):
                el = ex_ref[sb, (2 + l) * t:(3 + l) * t, cs]
                s = jnp.where(lv == l, _dot_nt(_bf(qh * el), _bf(kh * el)), s)
                if l % 2 == 1:
                    yield
            eb = ex_ref[sb, 0:t, cs]
            ebl = ex_ref[sb, t:2 * t, cs]
            st = state_ref[h]
            o = _dot(_bf(s), vh) + _dot_nt(_bf(qh * eb), _bf(st))
            upd = _dot_tn(vh, _bf(kh * ebl))
            state_ref[h] = st * ex_ref[sb, t - 1:t, cs] + upd
            ms = jnp.mean(o * o, axis=-1, keepdims=True)
            gate = _silu(zh_ref[rows, 3 * W + h * HG_DK:3 * W + (h + 1) * HG_DK])
            o_ref[rows, cs] = _bf(o * lax.rsqrt(ms + NORM_EPS) * par_ref[3:4, cs] * gate)
            yield


def _ret_stages(zr_ref, cos_ref, sin_ref, dmat_ref, qw_ref, kw_ref, par_ref, hm_ref, bd_ref,
                o_ref, state_ref):
    W = RET_WIDTH
    cos = cos_ref[...]
    sin = sin_ref[...]
    q = zr_ref[:, 0:W] * cos + zr_ref[:, W:2 * W] * sin
    k = (zr_ref[:, 2 * W:3 * W] * cos + zr_ref[:, 3 * W:4 * W] * sin) * (RET_DK ** -0.5)
    v = zr_ref[:, 4 * W:5 * W]
    kb = _bf(k)
    bd = bd_ref[...]
    yield

    s_parts = []
    v_parts = []
    for h in range(RET_HEADS):
        hm = hm_ref[h:h + 1, :]
        s_parts.append(_bf(_dot_nt(_bf(q * hm), kb) * dmat_ref[h]))
        v_parts.append(_bf(v * hm))
        yield
    st = state_ref[...]
    o = (_dot(jnp.concatenate(s_parts, axis=1), jnp.concatenate(v_parts, axis=0))
         + _dot(_bf(q * qw_ref[...]), _bf(st)))
    yield
    kv = _dot_tn(_bf(k * kw_ref[...]), _bf(v))
    state_ref[...] = st * par_ref[0:1, :] + bd * kv
    yield

    ms = _dot2_rhs01(o * o, _bf(bd)) * (1.0 / RET_DK)
    gate = _silu(zr_ref[:, 5 * W:6 * W])
    o_ref[...] = _bf(o * lax.rsqrt(ms + NORM_EPS) * par_ref[1:2, :] * gate)


def _gdn_stages(zg_ref, zab_ref, convw_ref, par_ref, eab_ref, gm_ref, bd_ref, tril_ref, hm_ref,
                o_ref, ext_ref, state_ref, obuf_ref):
    W = GDN_WIDTH
    t = T_BLK
    u = zg_ref[:, 0:3 * W]
    ext_ref[SUBLANES:SUBLANES + t, :] = u
    conv = (convw_ref[3:4, :] * u
            + convw_ref[2:3, :] * ext_ref[SUBLANES - 1:SUBLANES - 1 + t, :]
            + convw_ref[1:2, :] * ext_ref[SUBLANES - 2:SUBLANES - 2 + t, :]
            + convw_ref[0:1, :] * ext_ref[SUBLANES - 3:SUBLANES - 3 + t, :])
    ext_ref[0:SUBLANES, :] = u[t - SUBLANES:t, :]
    qkv = _silu(conv)
    q = qkv[:, 0:W]
    k = qkv[:, W:2 * W]
    v = qkv[:, 2 * W:3 * W]
    yield

    bd = bd_ref[...]
    bdb = _bf(bd)
    ab = zab_ref[...]
    a_exp = _dot2_rhs01(ab, eab_ref[:, 0:W])
    b_exp = _dot2_rhs01(ab, eab_ref[:, W:2 * W])
    qn = q * lax.rsqrt(_dot2_rhs01(q * q, bdb) + L2_EPS) * (GDN_DK ** -0.5)
    kn = k * lax.rsqrt(_dot2_rhs01(k * k, bdb) + L2_EPS)
    yield
    beta = _sigmoid(b_exp)
    g = par_ref[0:1, :] * _softplus(a_exp + par_ref[1:2, :])
    gc = _dot2_lhs01(tril_ref[...], g)
    gl = _dot2_lhs01(bdb, g)
    yield
    eg = jnp.exp(gc)
    vb = v * beta
    kbeta = kn * beta * eg
    qdec = qn * eg
    kdec = kn * jnp.exp(gl - gc)

    n_chunks = t // GDN_CHUNK
    hms = [_bf(jnp.broadcast_to(hm_ref[h:h + 1, :], (GDN_CHUNK, W))) for h in range(GDN_HEADS)]
    hms2 = [jnp.concatenate([m, m], axis=1) for m in hms]

    def chunk(a, c):
        return a[c * GDN_CHUNK:(c + 1) * GDN_CHUNK]

    def expand(y, masks):
        yb = _bf(y)
        return jnp.concatenate([yb * m for m in masks], axis=0)

    def blockprod(x, y, masks):
        xb = _bf(x)
        return jnp.concatenate([_dot(chunk(xb, c), expand(chunk(y, c), masks))
                                for c in range(n_chunks)], axis=0)

    gc_row = _dot2_lhs01(bdb, gc * gm_ref[5])
    yield
    rel = jnp.exp(jnp.where(gm_ref[0] > 0, gc - gc_row, -jnp.inf))

    knbeta = kn * beta
    kq = [_dot_nt(_bf(jnp.concatenate([chunk(knbeta, c), chunk(qn, c)], axis=0)),
                  expand(chunk(kn, c), hms)) for c in range(n_chunks)]
    yield
    m = jnp.concatenate([r[0:GDN_CHUNK] for r in kq], axis=0) * rel
    qk = jnp.concatenate([r[GDN_CHUNK:] for r in kq], axis=0) * rel

    d = m * gm_ref[1]
    d2 = blockprod(d, d, hms)
    yield
    d4 = blockprod(d2, d2, hms)
    dd2 = blockprod(d, d2, hms)
    yield
    x = d2 - d - dd2
    xd4 = blockprod(x, d4, hms)
    yield
    x = x + d4 + xd4
    for lvl in range(2, 5):
        lo = m * gm_ref[lvl]
        xl = blockprod(x, lo, hms)
        yield
        y = lo + xl
        yx = blockprod(y, x, hms)
        yield
        x = x - (y + yx)

    vk = jnp.concatenate([vb, kbeta], axis=1)
    wk = vk + blockprod(x, vk, hms2)
    yield
    w = wk[:, 0:W]
    kcum = wk[:, W:2 * W]
    ag = blockprod(qk, wk, hms2)
    yield
    a1 = ag[:, 0:W]
    qeff = qdec - ag[:, W:2 * W]

    for c in range(t // GDN_CHUNK):
        rows = slice(c * GDN_CHUNK, (c + 1) * GDN_CHUNK)
        st = state_ref[...]
        stb = _bf(st)
        vnew = w[rows] - _dot(_bf(kcum[rows]), stb)
        obuf_ref[rows, :] = _dot(_bf(qeff[rows]), stb) + a1[rows]
        yield
        upd = _dot_tn(_bf(kdec[rows]), _bf(vnew))
        last = eg[(c + 1) * GDN_CHUNK - 1:(c + 1) * GDN_CHUNK, :]
        state_ref[...] = st * last + bd * upd
        yield

    o = obuf_ref[...]
    ms = _dot2_rhs01(o * o, bdb) * (1.0 / GDN_DK)
    gate = _silu(zg_ref[:, 3 * W:4 * W])
    o_ref[...] = _bf(o * lax.rsqrt(ms + NORM_EPS) * par_ref[2:3, :] * gate)


_DONE = object()

def _mixers_kernel(zh_ref, f_ref, lv_ref, hpar_ref,
                   zr_ref, cos_ref, sin_ref, dmat_ref, qw_ref, kw_ref, rpar_ref,
                   zg_ref, zab_ref, convw_ref, gpar_ref, eab_ref, gm_ref, tril_ref, hm_ref, bd_ref,
                   ohg_ref, oret_ref, ogdn_ref,
                   hstate_ref, hex_ref, rstate_ref, gext_ref, gstate_ref, gobuf_ref):
    @pl.when(pl.program_id(1) == 0)
    def _():
        hstate_ref[...] = jnp.zeros_like(hstate_ref)
        rstate_ref[...] = jnp.zeros_like(rstate_ref)
        gstate_ref[...] = jnp.zeros_like(gstate_ref)
        gext_ref[0:SUBLANES, :] = jnp.zeros((SUBLANES, 3 * GDN_WIDTH), F32)

    active = [
        (_gdn_stages(zg_ref, zab_ref, convw_ref, gpar_ref, eab_ref, gm_ref, bd_ref, tril_ref, hm_ref,
                     ogdn_ref, gext_ref, gstate_ref, gobuf_ref), 2),
        (_hgrn_stages(zh_ref, f_ref, lv_ref, hpar_ref, ohg_ref, hstate_ref, hex_ref), 3),
        (_ret_stages(zr_ref, cos_ref, sin_ref, dmat_ref, qw_ref, kw_ref, rpar_ref, hm_ref, bd_ref,
                     oret_ref, rstate_ref), 1),
    ]
    while active:
        for entry in list(active):
            gen, per_round = entry
            for _ in range(per_round):
                if next(gen, _DONE) is _DONE:
                    active.remove(entry)
                    break


def _mixers(zh, zr, zg, zab, hg_tabs, ret_tabs, gdn_tabs, hm, bd, batch, steps):
    n = zh.shape[0]
    seq_row = lambda b, j: (b * steps + j, 0)

    def const(a):
        return pl.BlockSpec(a.shape, lambda b, j, nd=a.ndim: (0,) * nd)

    def rows(width):
        return pl.BlockSpec((T_BLK, width), seq_row)

    pos_rows = pl.BlockSpec((T_BLK, RET_WIDTH), lambda b, j: (j, 0))
    cos_t, sin_t = ret_tabs[0], ret_tabs[1]
    in_specs = ([rows(ZH_W)] + [const(a) for a in hg_tabs]
                + [rows(ZR_W), pos_rows, pos_rows] + [const(a) for a in ret_tabs[2:]]
                + [rows(ZG_W), rows(ZAB_W)] + [const(a) for a in gdn_tabs] + [const(hm), const(bd)])
    return pl.pallas_call(
        _mixers_kernel,
        grid=(batch, steps),
        in_specs=in_specs,
        out_specs=[rows(HG_WIDTH), rows(RET_WIDTH), rows(GDN_WIDTH)],
        out_shape=[jax.ShapeDtypeStruct((n, HG_WIDTH), BF16), jax.ShapeDtypeStruct((n, RET_WIDTH), BF16),
                   jax.ShapeDtypeStruct((n, GDN_WIDTH), BF16)],
        scratch_shapes=[pltpu.VMEM((HG_HEADS, HG_DK, HG_DK), F32),
                        pltpu.VMEM((T_BLK // HG_SUB, (2 + HG_LEVELS) * HG_SUB, HG_WIDTH), F32),
                        pltpu.VMEM((RET_WIDTH, RET_WIDTH), F32),
                        pltpu.VMEM((SUBLANES + T_BLK, 3 * GDN_WIDTH), F32),
                        pltpu.VMEM((GDN_WIDTH, GDN_WIDTH), F32),
                        pltpu.VMEM((T_BLK, GDN_WIDTH), F32)],
        compiler_params=pltpu.CompilerParams(dimension_semantics=("arbitrary", "arbitrary"),
                                             vmem_limit_bytes=VMEM_LIMIT),
        name="mixers",
    )(zh, *hg_tabs, zr, cos_t, sin_t, *ret_tabs[2:], zg, zab, *gdn_tabs, hm, bd)


def _outproj_kernel(ohg_ref, oret_ref, ogdn_ref, x_ref, mod_ref, g_ref, w_ref, rw_ref, rb_ref,
                    su_ref, x1_ref, hn_ref, ri_ref, gates_ref, cnt_ref, run_ref):
    @pl.when(pl.program_id(0) == 0)
    def _():
        run_ref[...] = jnp.zeros_like(run_ref)

    rw_hi = rw_ref[0:LANES, :]
    rw_lo = rw_ref[LANES:2 * LANES, :]
    eid = lax.broadcasted_iota(jnp.int32, (LANES, ROUTE_SLAB), 0)
    routed = {}

    def slab_stages(s):
        rows = slice(s * ROUTE_SLAB, (s + 1) * ROUTE_SLAB)
        y = (_dot(ohg_ref[rows, :], w_ref[0:HG_WIDTH, :])
             + _dot(oret_ref[rows, :], w_ref[HG_WIDTH:HG_WIDTH + RET_WIDTH, :])
             + _dot(ogdn_ref[rows, :], w_ref[HG_WIDTH + RET_WIDTH:, :]))
        yield
        x1 = x_ref[rows, :] + mod_ref[0, 0:1, :] * y
        x1_ref[rows, :] = x1
        n = x1 * lax.rsqrt(jnp.mean(x1 * x1, axis=-1, keepdims=True) + NORM_EPS)
        hn = (n * g_ref[0:1, :]) * (1.0 + mod_ref[0, 1:2, :]) + mod_ref[0, 2:3, :]
        hn_ref[rows, :] = _pack_halves(hn)
        yield
        hn_hi, hn_lo = _split2(hn)
        logits = (_dot_nt(rw_hi, hn_hi) + _dot_nt(rw_lo, hn_hi) + _dot_nt(rw_hi, hn_lo)
                  + rb_ref[:, rows])
        yield
        work = jnp.where(eid < N_EXPERTS, logits, -jnp.inf)
        vals, idxs = [], []
        multihot = jnp.zeros((LANES, ROUTE_SLAB), F32)
        for kk in range(TOP_K):
            mx = jnp.max(work, axis=0, keepdims=True)
            ix = jnp.min(jnp.where(work == mx, eid, LANES), axis=0, keepdims=True)
            sel = eid == ix
            multihot = jnp.where(sel, 1.0, multihot)
            work = jnp.where(sel, -jnp.inf, work)
            vals.append(mx)
            idxs.append(ix)
            if kk % 2 == 1:
                yield
        routed[s] = (vals, idxs, multihot)

    n_slabs = TM // ROUTE_SLAB
    active = [slab_stages(s) for s in range(n_slabs)]
    while active:
        for gen in list(active):
            if next(gen, _DONE) is _DONE:
                active.remove(gen)

    multihot = jnp.concatenate([routed[s][2] for s in range(n_slabs)], axis=1)
    run = run_ref[...]
    before = _dot(_bf(multihot), su_ref[...]) + jnp.concatenate([run] * (TM // LANES), axis=1)
    run = run + jnp.sum(multihot, axis=1, keepdims=True)
    run_ref[...] = run
    cnt_ref[...] = run

    for s in range(n_slabs):
        vals, idxs, _ = routed[s]
        cols = slice(s * ROUTE_SLAB, (s + 1) * ROUTE_SLAB)
        ex = [jnp.exp(vv - vals[0]) for vv in vals]
        den = ex[0] + ex[1] + ex[2] + ex[3]
        ranks = [jnp.sum(jnp.where(eid == ix, before[:, cols], 0.0), axis=0, keepdims=True) for ix in idxs]
        ri_ref[:, cols] = jnp.concatenate(idxs + [r.astype(jnp.int32) for r in ranks], axis=0)
        gates_ref[:, cols] = jnp.concatenate([e / den for e in ex]
                                             + [jnp.zeros((TOP_K, ROUTE_SLAB), F32)], axis=0)


def _outproj(ohg, oret, ogdn, x2d, modv, gpar, w_bf, rw, rb, su, tiles_per_batch):
    n = x2d.shape[0]
    row = lambda i: (i, 0)
    col = lambda i: (0, i)
    const = lambda i: (0, 0)
    return pl.pallas_call(
        _outproj_kernel,
        grid=(n // TM,),
        in_specs=[pl.BlockSpec((TM, HG_WIDTH), row), pl.BlockSpec((TM, RET_WIDTH), row),
                  pl.BlockSpec((TM, GDN_WIDTH), row), pl.BlockSpec((TM, D_MODEL), row),
                  pl.BlockSpec((1, SUBLANES, D_MODEL), lambda i: (i // tiles_per_batch, 0, 0)),
                  pl.BlockSpec((SUBLANES, D_MODEL), const),
                  pl.BlockSpec((D_MODEL, D_MODEL), const),
                  pl.BlockSpec((2 * LANES, D_MODEL), const),
                  pl.BlockSpec((LANES, TM), const),
                  pl.BlockSpec((TM, TM), const)],
        out_specs=[pl.BlockSpec((TM, D_MODEL), row), pl.BlockSpec((TM, D_MODEL // 2), row),
                   pl.BlockSpec((2 * TOP_K, TM), col), pl.BlockSpec((2 * TOP_K, TM), col),
                   pl.BlockSpec((LANES, LANES), const)],
        out_shape=[jax.ShapeDtypeStruct((n, D_MODEL), F32),
                   jax.ShapeDtypeStruct((n, D_MODEL // 2), jnp.uint32),
                   jax.ShapeDtypeStruct((2 * TOP_K, n), jnp.int32),
                   jax.ShapeDtypeStruct((2 * TOP_K, n), F32),
                   jax.ShapeDtypeStruct((LANES, LANES), F32)],
        scratch_shapes=[pltpu.VMEM((LANES, LANES), F32)],
        compiler_params=pltpu.CompilerParams(dimension_semantics=("arbitrary",),
                                             vmem_limit_bytes=VMEM_LIMIT),
        name="outproj_router",
    )(ohg, oret, ogdn, x2d, modv, gpar, w_bf, rw, rb, su)


def _expert_kernel(be_ref, nu_ref, x_ref, w1_ref, b1_ref, w2_ref, b2_ref, y_ref, w1b_ref, w2b_ref):
    i = pl.program_id(0)
    prev = be_ref[jnp.maximum(i - 1, 0)]
    fresh = jnp.logical_or(i == 0, be_ref[i] != prev)

    @pl.when(fresh)
    def _():
        w1b_ref[...] = _bf(w1_ref[0])
        w2b_ref[...] = _bf(w2_ref[0])

    @pl.when(i < nu_ref[0])
    def _():
        half = D_MODEL // 2

        def slab_stages(s):
            rows = slice(s * EXPERT_SLAB, (s + 1) * EXPERT_SLAB)
            x_lo, x_hi = _unpack_halves(x_ref[rows, :])
            hid = (_dot(_bf(x_lo), w1b_ref[0:half, :]) + _dot(_bf(x_hi), w1b_ref[half:, :])
                   + b1_ref[0])
            yield
            x_glu = jnp.minimum(hid[:, 0:D_FF], SWIGLU_LIMIT)
            x_lin = jnp.clip(hid[:, D_FF:], -SWIGLU_LIMIT, SWIGLU_LIMIT)
            act = x_glu * _sigmoid(SWIGLU_ALPHA * x_glu) * (x_lin + 1.0)
            yield
            y_ref[rows, :] = _pack_halves(_dot(_bf(act), w2b_ref[...]) + b2_ref[0])

        active = [slab_stages(s) for s in range(R_BLK // EXPERT_SLAB)]
        while active:
            for gen in list(active):
                if next(gen, _DONE) is _DONE:
                    active.remove(gen)


def _experts(block_e, n_used, xs, w1, b1, w2, b2):
    p = xs.shape[0]
    n_blocks = p // R_BLK
    ne = w1.shape[0] * w1.shape[1]
    w1 = w1.reshape(ne, D_MODEL, 2 * D_FF)
    w2 = w2.reshape(ne, D_FF, D_MODEL)
    rowmap = lambda i, be, nu: (jnp.minimum(i, nu[0] - 1), 0)
    emap = lambda i, be, nu: (be[i], 0, 0)
    grid_spec = pltpu.PrefetchScalarGridSpec(
        num_scalar_prefetch=2,
        grid=(n_blocks,),
        in_specs=[pl.BlockSpec((R_BLK, D_MODEL // 2), rowmap),
                  pl.BlockSpec((1, D_MODEL, 2 * D_FF), emap),
                  pl.BlockSpec((1, 1, 2 * D_FF), emap),
                  pl.BlockSpec((1, D_FF, D_MODEL), emap),
                  pl.BlockSpec((1, 1, D_MODEL), emap)],
        out_specs=pl.BlockSpec((R_BLK, D_MODEL // 2), rowmap),
        scratch_shapes=[pltpu.VMEM((D_MODEL, 2 * D_FF), BF16), pltpu.VMEM((D_FF, D_MODEL), BF16)],
    )
    return pl.pallas_call(
        _expert_kernel,
        grid_spec=grid_spec,
        out_shape=jax.ShapeDtypeStruct((p, D_MODEL // 2), jnp.uint32),
        compiler_params=pltpu.CompilerParams(dimension_semantics=("arbitrary",),
                                             vmem_limit_bytes=VMEM_LIMIT),
        name="experts",
    )(block_e, n_used, xs, w1, b1.reshape(ne, 1, 2 * D_FF), w2, b2.reshape(ne, 1, D_MODEL))


def _sc_mesh():
    return plsc.VectorSubcoreMesh(core_axis_name="c", subcore_axis_name="s")


def _sc_scatter_rows(x, pos, p_rows):
    mesh = _sc_mesh()
    n, w = x.shape
    per_worker = n // (mesh.num_cores * mesh.num_subcores)
    assert per_worker % SC_ROWS == 0 and pos.shape == (TOP_K, n)

    @functools.partial(
        pl.kernel, out_type=jax.ShapeDtypeStruct((p_rows, w), x.dtype), mesh=mesh,
        scratch_types=[pltpu.VMEM((SC_ROWS,), jnp.int32)] * TOP_K
        + [pltpu.VMEM((SC_ROWS, w), x.dtype), pltpu.SemaphoreType.DMA, pltpu.SemaphoreType.DMA],
        name="dispatch_rows")
    def scatter(x_hbm, p_hbm, o_hbm, i0, i1, i2, i3, rows_v, sem_in, sem_out):
        idx = (i0, i1, i2, i3)
        worker = lax.axis_index("s") * mesh.num_cores + lax.axis_index("c")

        @pl.loop(0, per_worker // SC_ROWS)
        def _(g):
            base = pl.multiple_of(worker * per_worker + g * SC_ROWS, SC_ROWS)
            loads = [pltpu.async_copy(x_hbm.at[pl.ds(base, SC_ROWS)], rows_v, sem_in)]
            loads += [pltpu.async_copy(p_hbm.at[kk, pl.ds(base, SC_ROWS)], idx[kk], sem_in)
                      for kk in range(TOP_K)]
            for cp in loads:
                cp.wait()
            stores = [pltpu.async_copy(rows_v, o_hbm.at[idx[kk]], sem_out) for kk in range(TOP_K)]
            for cp in stores:
                cp.wait()

    return scatter(x, pos)


def _sc_gather_rows(table, idx):
    mesh = _sc_mesh()
    n_idx = idx.shape[0]
    w = table.shape[1]
    per_worker = n_idx // (mesh.num_cores * mesh.num_subcores)
    rows = SC_ROWS // 2
    n_pairs = per_worker // (2 * rows)
    assert per_worker % (2 * rows) == 0

    @functools.partial(
        pl.kernel, out_type=jax.ShapeDtypeStruct((n_idx, w), table.dtype), mesh=mesh,
        scratch_types=[pltpu.VMEM((rows,), jnp.int32), pltpu.VMEM((rows,), jnp.int32),
                       pltpu.VMEM((rows, w), table.dtype), pltpu.VMEM((rows, w), table.dtype),
                       pltpu.SemaphoreType.DMA, pltpu.SemaphoreType.DMA],
        name="combine_rows")
    def gather(t_hbm, i_hbm, o_hbm, idx0, idx1, buf0, buf1, sem0, sem1):
        worker = lax.axis_index("s") * mesh.num_cores + lax.axis_index("c")
        first = worker * per_worker
        last = n_pairs * 2 - 1

        def start(chunk, idx_v, buf, sem):
            base = pl.multiple_of(first + chunk * rows, rows)
            pltpu.sync_copy(i_hbm.at[pl.ds(base, rows)], idx_v)
            pltpu.async_copy(t_hbm.at[idx_v], buf, sem)

        def finish(chunk, idx_v, buf, sem):
            base = pl.multiple_of(first + chunk * rows, rows)
            pltpu.make_async_copy(t_hbm.at[idx_v], buf, sem).wait()
            pltpu.sync_copy(buf, o_hbm.at[pl.ds(base, rows)])

        start(0, idx0, buf0, sem0)

        @pl.loop(0, n_pairs)
        def _(g):
            start(2 * g + 1, idx1, buf1, sem1)
            finish(2 * g, idx0, buf0, sem0)
            start(jnp.minimum(2 * g + 2, last), idx0, buf0, sem0)
            finish(2 * g + 1, idx1, buf1, sem1)

        pltpu.make_async_copy(t_hbm.at[idx0], buf0, sem0).wait()

    return gather(table, idx)


def _combine_kernel(x_ref, y0_ref, y1_ref, y2_ref, y3_ref, gates_ref, mod_ref, g_ref, o_ref, *, final):
    gt = gates_ref[...]
    half = D_MODEL // 2
    acc_lo, acc_hi = None, None
    for kk, y_ref in enumerate((y0_ref, y1_ref, y2_ref, y3_ref)):
        lo, hi = _unpack_halves(y_ref[0])
        g = gt[:, kk:kk + 1]
        acc_lo = g * lo if acc_lo is None else acc_lo + g * lo
        acc_hi = g * hi if acc_hi is None else acc_hi + g * hi
    x_lo = x_ref[:, 0:half] + mod_ref[0, 3:4, 0:half] * acc_lo
    x_hi = x_ref[:, half:] + mod_ref[0, 3:4, half:] * acc_hi
    if final:
        ssq = (jnp.sum(x_lo * x_lo, axis=-1, keepdims=True)
               + jnp.sum(x_hi * x_hi, axis=-1, keepdims=True))
        r = lax.rsqrt(ssq * (1.0 / D_MODEL) + NORM_EPS)
        x_lo = x_lo * r * g_ref[1:2, 0:half]
        x_hi = x_hi * r * g_ref[1:2, half:]
    o_ref[:, 0:half] = x_lo
    o_ref[:, half:] = x_hi


def _combine(x, yg, gates, modv, gpar, tiles_per_batch, final, piece):
    n = x.shape[0]
    tiles = n // TM // COMBINE_PIECES
    first = piece * tiles
    row = lambda i: (first + i, 0)
    ysp = [pl.BlockSpec((1, TM, D_MODEL // 2), (lambda i, kk=kk: (kk, i, 0))) for kk in range(TOP_K)]
    return pl.pallas_call(
        functools.partial(_combine_kernel, final=final),
        grid=(tiles,),
        in_specs=[pl.BlockSpec((TM, D_MODEL), row)] + ysp + [
            pl.BlockSpec((TM, 2 * TOP_K), row),
            pl.BlockSpec((1, SUBLANES, D_MODEL), lambda i: ((first + i) // tiles_per_batch, 0, 0)),
            pl.BlockSpec((SUBLANES, D_MODEL), lambda i: (0, 0))],
        out_specs=pl.BlockSpec((TM, D_MODEL), row),
        out_shape=jax.ShapeDtypeStruct((n, D_MODEL), F32),
        input_output_aliases={0: 0},
        compiler_params=pltpu.CompilerParams(dimension_semantics=("arbitrary",),
                                             vmem_limit_bytes=VMEM_LIMIT),
        name="combine",
    )(x, yg, yg, yg, yg, gates, modv, gpar)


def _rot_cols(w):
    d = w.shape[0]
    w4 = w.reshape(d, RET_HEADS, 2, RET_DK // 2)
    return jnp.concatenate([-w4[:, :, 1:2], w4[:, :, 0:1]], axis=2).reshape(d, RET_WIDTH)


def _widen_w_in(w):
    o = 0
    hg = w[:, o:o + 4 * HG_WIDTH]; o += 4 * HG_WIDTH
    rq = w[:, o:o + RET_WIDTH]; o += RET_WIDTH
    rk = w[:, o:o + RET_WIDTH]; o += RET_WIDTH
    rv = w[:, o:o + RET_WIDTH]; o += RET_WIDTH
    rg = w[:, o:o + RET_WIDTH]; o += RET_WIDTH
    gd = w[:, o:o + 4 * GDN_WIDTH]; o += 4 * GDN_WIDTH
    gab = w[:, o:o + 2 * GDN_HEADS]
    pad = jnp.zeros((w.shape[0], ZAB_W - 2 * GDN_HEADS), w.dtype)
    return _bf(jnp.concatenate([hg, rq, _rot_cols(rq), rk, _rot_cols(rk), rv, rg, gd, gab, pad], axis=1))


def _pad_rows(a, rows=SUBLANES):
    return jnp.concatenate([a, jnp.zeros((rows - a.shape[0],) + a.shape[1:], a.dtype)], axis=0)


def kernel(x, c, ada_w, ada_b, norm1_g, norm2_g, w_in, w_out, hg_lb_logits, hg_norm_g, ret_norm_g,
           gdn_conv_w, gdn_A_log, gdn_dt_bias, gdn_norm_g, router_w, router_b, exp_w1, exp_b1,
           exp_w2, exp_b2, final_norm_g):
    batch, seq, d = x.shape
    depth = ada_w.shape[0]
    n = batch * seq
    steps = seq // T_BLK
    tiles_per_batch = seq // TM
    nk = n * TOP_K
    n_blocks = nk // R_BLK + N_EXPERTS
    p_rows = n_blocks * R_BLK

    lv_np = _level_matrix(T_BLK)
    lv = jnp.asarray(_level_matrix(HG_SUB))
    f_np = _hgrn_exponent_matrix(HG_SUB)
    f_mat = jnp.asarray(np.concatenate([f_np, f_np], axis=1), BF16)
    gm = jnp.asarray(_gdn_masks(T_BLK))
    bd = jnp.asarray(_block_diag_mask(T_BLK, GDN_CHUNK))
    tril = jnp.asarray(_block_diag_mask(T_BLK, GDN_CHUNK) * (lv_np >= 0), BF16)
    hm = jnp.asarray(_head_masks(RET_WIDTH, RET_HEADS))
    su = jnp.asarray(np.triu(np.ones((TM, TM), np.float32), 1), BF16)
    eab_np = np.zeros((LANES, 2 * GDN_WIDTH), np.float32)
    for h in range(GDN_HEADS):
        eab_np[h, h * GDN_DK:(h + 1) * GDN_DK] = 1.0
        eab_np[GDN_HEADS + h, GDN_WIDTH + h * GDN_DK:GDN_WIDTH + (h + 1) * GDN_DK] = 1.0
    eab = jnp.asarray(eab_np, BF16)

    half = RET_DK // 2
    inv = ROPE_BASE ** (-jnp.linspace(0.0, 1.0, half, dtype=F32))
    ang = jnp.arange(seq, dtype=F32)[:, None] * inv[None, :]
    cos_t = jnp.tile(jnp.cos(ang), (1, 2 * RET_HEADS))
    sin_t = jnp.tile(jnp.sin(ang), (1, 2 * RET_HEADS))
    log_g = jnp.log1p(-jnp.exp2(-5.0 - jnp.arange(RET_HEADS, dtype=F32)))
    jj = jnp.arange(T_BLK, dtype=F32)
    diff = jj[:, None] - jj[None, :]
    dmat = jnp.where(diff[None] >= 0, jnp.exp(diff[None] * log_g[:, None, None]), 0.0)
    lg_lane = jnp.repeat(log_g, RET_DK)[None, :]
    qw = jnp.exp(lg_lane * (jj[:, None] + 1.0))
    kw = jnp.exp(lg_lane * (T_BLK - 1.0 - jj[:, None]))
    cdec = jnp.exp(T_BLK * lg_lane)

    lb = jnp.cumsum(jax.nn.softmax(hg_lb_logits.astype(F32), axis=0), axis=0)
    lb = jnp.maximum(lb - lb[0], 0.0)
    c_pad = _pad_rows(c.astype(F32))
    mod = _adaln(c_pad, ada_w, ada_b)[:, :batch, :]

    x2d = x.reshape(n, d)
    out = None
    for l in range(depth):
        sh1, sc1, gt1, sh2, sc2, gt2 = [mod[l][:, i * d:(i + 1) * d] for i in range(6)]
        zeros = jnp.zeros_like(sh1)
        modv_a = jnp.stack([sc1, sh1, zeros, zeros, zeros, zeros, zeros, zeros], axis=1)
        modv_c = jnp.stack([gt1, sc2, sh2, gt2, zeros, zeros, zeros, zeros], axis=1)
        gpar_a = _pad_rows(norm1_g[l][None, :])
        gpar_c = _pad_rows(jnp.stack([norm2_g[l], final_norm_g], axis=0))

        zh, zr, zg, zab = _inproj(x2d, modv_a, gpar_a, _widen_w_in(w_in[l]), seq)

        hg_par = _pad_rows(jnp.stack([jnp.log(lb[l]), jnp.log1p(-lb[l]), 1.0 - lb[l],
                                      hg_norm_g[l].reshape(-1)], axis=0))
        ret_par = _pad_rows(jnp.concatenate([cdec, ret_norm_g[l].reshape(1, -1)], axis=0))
        gdn_par = _pad_rows(jnp.stack([jnp.repeat(-jnp.exp(gdn_A_log[l].astype(F32)), GDN_DK),
                                       jnp.repeat(gdn_dt_bias[l].astype(F32), GDN_DK),
                                       gdn_norm_g[l].reshape(-1)], axis=0))
        o_hg, o_ret, o_gdn = _mixers(
            zh, zr, zg, zab,
            (f_mat, lv, hg_par),
            (cos_t, sin_t, dmat, qw, kw, ret_par),
            (_pad_rows(gdn_conv_w[l].astype(F32)), gdn_par, eab, gm, tril),
            hm, bd, batch, steps)

        rw_f = jnp.concatenate([router_w[l].T, jnp.zeros((LANES - N_EXPERTS, d), F32)], axis=0)
        rw = jnp.concatenate(_split2(rw_f), axis=0)
        rb = jnp.broadcast_to(jnp.concatenate([router_b[l], jnp.zeros((LANES - N_EXPERTS,), F32)])[:, None],
                              (LANES, TM))
        x1, hn2, ri, gates_t, cnt = _outproj(o_hg, o_ret, o_gdn, x2d, modv_c, gpar_c, _bf(w_out[l]),
                                             rw, rb, su, tiles_per_batch)
        gates = gates_t.T

        counts = cnt[:N_EXPERTS, 0].astype(jnp.int32)
        padded = (counts + R_BLK - 1) // R_BLK * R_BLK
        pend = jnp.cumsum(padded)
        pstart = pend - padded
        n_used = (pend[-1] // R_BLK).astype(jnp.int32)
        blk_start = jnp.arange(n_blocks, dtype=jnp.int32) * R_BLK
        blk_start = jnp.minimum(blk_start, pend[-1] - R_BLK)
        block_e = jnp.sum(blk_start[:, None] >= pend[None, :], axis=1).astype(jnp.int32)
        eid = jnp.arange(N_EXPERTS, dtype=jnp.int32)
        pos_t = ri[TOP_K:] + jnp.sum(jnp.where(ri[:TOP_K, :, None] == eid, pstart, 0), axis=-1)
        xs = _sc_scatter_rows(hn2, pos_t, p_rows)
        y = _experts(block_e + l * N_EXPERTS, n_used.reshape(1), xs, exp_w1, exp_b1, exp_w2, exp_b2)
        n_piece = n // COMBINE_PIECES
        x2d = x1
        for p in range(COMBINE_PIECES):
            idx = pos_t[:, p * n_piece:(p + 1) * n_piece].reshape(-1)
            yg = _sc_gather_rows(y, idx).reshape(TOP_K, n_piece, d // 2)
            x2d = _combine(x2d, yg, gates, modv_c, gpar_c, tiles_per_batch, l == depth - 1, p)
    return x2d.reshape(batch, seq, d)
```

```python
import functools
import math

import numpy as np
import jax
import jax.numpy as jnp
from jax import lax
from jax.experimental import pallas as pl
from jax.experimental.pallas import tpu as pltpu
from jax.experimental.pallas import tpu_sc as plsc

F32 = jnp.float32
BF16 = jnp.bfloat16

D_MODEL = 1024
HG_HEADS, HG_DK = 4, 128
HG_WIDTH = HG_HEADS * HG_DK
RET_HEADS, RET_DK = 4, 64
RET_WIDTH = RET_HEADS * RET_DK
GDN_HEADS, GDN_DK = 4, 64
GDN_WIDTH = GDN_HEADS * GDN_DK
CONV_K = 4
ROPE_BASE = 10000.0
N_EXPERTS = 32
TOP_K = 4
D_FF = D_MODEL
SWIGLU_ALPHA = 1.702
SWIGLU_LIMIT = 7.0
NORM_EPS = 1e-6
L2_EPS = 1e-6
GDN_CHUNK = 64

LANES = 128
SUBLANES = 8
VMEM_LIMIT = 56 * 1024 * 1024

T_BLK = 256
TM = 512
TM_IN = 512
ROUTE_SLAB = 128
COMBINE_PIECES = 4
R_BLK = 512
EXPERT_SLAB = 256
SC_ROWS = 128
HG_SUB = 128
HG_LEVELS = 7

ZH_W = 4 * HG_WIDTH
ZR_W = 4 * RET_WIDTH
ZG_W = 4 * GDN_WIDTH
ZAB_W = LANES
Z_W = ZH_W + ZR_W + ZG_W + ZAB_W


def _dot(a, b):
    return jnp.dot(a, b, preferred_element_type=F32)


def _dot_nt(a, b):
    return lax.dot_general(a, b, (((1,), (1,)), ((), ())), preferred_element_type=F32)


def _dot_tn(a, b):
    return lax.dot_general(a, b, (((0,), (0,)), ((), ())), preferred_element_type=F32)


def _split2(x):
    hi = x.astype(BF16)
    return hi, (x - hi.astype(F32)).astype(BF16)


def _dot2_lhs01(c, x):
    hi, lo = _split2(x)
    return _dot(c, hi) + _dot(c, lo)


def _dot2_rhs01(x, c):
    hi, lo = _split2(x)
    return _dot(hi, c) + _dot(lo, c)


def _sigmoid(x):
    return 1.0 / (1.0 + jnp.exp(-x))


def _silu(x):
    return x * _sigmoid(x)


def _softplus(x):
    return jnp.maximum(x, 0.0) + jnp.log1p(jnp.exp(-jnp.abs(x)))


def _bf(x):
    return x.astype(BF16)


def _pack_halves(x):
    w = x.shape[1] // 2
    bits = lax.bitcast_convert_type(_bf(x).astype(F32), jnp.uint32)
    return (bits[:, :w] >> 16) | bits[:, w:]


def _unpack_halves(p):
    lo = lax.bitcast_convert_type(p << 16, F32)
    hi = lax.bitcast_convert_type(p & jnp.uint32(0xFFFF0000), F32)
    return lo, hi


def _level_matrix(t):
    i = np.arange(t)[:, None]
    j = np.arange(t)[None, :]
    x = i ^ j
    lv = np.floor(np.log2(np.maximum(x, 1))).astype(np.int32)
    lv = np.where(i == j, int(math.log2(t)), lv)
    lv = np.where(i < j, -1, lv)
    return lv.astype(np.int32)


def _hgrn_exponent_matrix(t):
    n_lev = int(math.log2(t))
    f = np.zeros((2 + n_lev, t, t), np.float32)
    u = np.arange(t)[None, :]
    r = np.arange(t)[:, None]
    f[0] = (u <= r)
    f[1] = (u > r)
    for l in range(n_lev):
        h = 1 << l
        base = (r // (2 * h)) * (2 * h)
        mid = base + h
        upper = (r - base) >= h
        f[2 + l] = np.where(upper, (u >= mid) & (u <= r), (u > r) & (u < mid))
    return f.reshape((2 + n_lev) * t, t)


def _gdn_masks(t):
    lv = _level_matrix(GDN_CHUNK)
    top = int(math.log2(GDN_CHUNK))
    incl = lv >= 0
    d8 = (lv >= 0) & (lv <= 2)
    merges = [(lv == l) for l in range(3, top)]
    eye = lv == top
    pats = np.stack([incl, d8] + merges + [eye]).astype(np.float32)
    return np.tile(pats, (1, t // GDN_CHUNK, GDN_HEADS))


def _block_diag_mask(t, blk):
    i = np.arange(t)
    return (i[:, None] // blk == i[None, :] // blk).astype(np.float32)


def _head_masks(width, heads):
    lane = np.arange(width)[None, :]
    m = np.zeros((SUBLANES, width), np.float32)
    for h in range(heads):
        m[h] = (lane // (width // heads) == h)[0]
    return m


def _adaln_kernel(c_ref, w_ref, b_ref, o_ref):
    cond = _silu(c_ref[...])
    o_ref[0] = jnp.dot(cond, w_ref[0], preferred_element_type=F32,
                       precision=lax.Precision.HIGHEST) + b_ref[0]


def _adaln(c_pad, ada_w, ada_b):
    depth, d, n6 = ada_w.shape
    tn = n6 // 4
    return pl.pallas_call(
        _adaln_kernel,
        grid=(depth, n6 // tn),
        in_specs=[pl.BlockSpec((SUBLANES, d), lambda l, j: (0, 0)),
                  pl.BlockSpec((1, d, tn), lambda l, j: (l, 0, j)),
                  pl.BlockSpec((1, 1, tn), lambda l, j: (l, 0, j))],
        out_specs=pl.BlockSpec((1, SUBLANES, tn), lambda l, j: (l, 0, j)),
        out_shape=jax.ShapeDtypeStruct((depth, SUBLANES, n6), F32),
        compiler_params=pltpu.CompilerParams(vmem_limit_bytes=VMEM_LIMIT),
        name="adaln",
    )(c_pad, ada_w, ada_b.reshape(depth, 1, n6))


def _inproj_kernel(x_ref, mod_ref, g_ref, w_ref, zh_ref, zr_ref, zg_ref, zab_ref):
    x = x_ref[...]
    y = x * lax.rsqrt(jnp.mean(x * x, axis=-1, keepdims=True) + NORM_EPS)
    hn = (y * g_ref[0:1, :]) * (1.0 + mod_ref[0, 0:1, :]) + mod_ref[0, 1:2, :]
    hb = _bf(hn)
    zh_ref[...] = _dot(hb, w_ref[:, 0:ZH_W])
    zr_ref[...] = _dot(hb, w_ref[:, ZH_W:ZH_W + ZR_W])
    zg_ref[...] = _dot(hb, w_ref[:, ZH_W + ZR_W:ZH_W + ZR_W + ZG_W])
    zab_ref[...] = _dot(hb, w_ref[:, ZH_W + ZR_W + ZG_W:Z_W])


def _inproj(x2d, modv, gpar, w_bf, seq):
    n = x2d.shape[0]
    row = lambda i: (i, 0)
    tiles_per_batch = seq // TM_IN
    return pl.pallas_call(
        _inproj_kernel,
        grid=(n // TM_IN,),
        in_specs=[pl.BlockSpec((TM_IN, D_MODEL), row),
                  pl.BlockSpec((1, SUBLANES, D_MODEL), lambda i: (i // tiles_per_batch, 0, 0)),
                  pl.BlockSpec((SUBLANES, D_MODEL), lambda i: (0, 0)),
                  pl.BlockSpec((D_MODEL, Z_W), lambda i: (0, 0))],
        out_specs=[pl.BlockSpec((TM_IN, ZH_W), row), pl.BlockSpec((TM_IN, ZR_W), row),
                   pl.BlockSpec((TM_IN, ZG_W), row), pl.BlockSpec((TM_IN, ZAB_W), row)],
        out_shape=[jax.ShapeDtypeStruct((n, ZH_W), F32), jax.ShapeDtypeStruct((n, ZR_W), F32),
                   jax.ShapeDtypeStruct((n, ZG_W), F32), jax.ShapeDtypeStruct((n, ZAB_W), F32)],
        compiler_params=pltpu.CompilerParams(dimension_semantics=("arbitrary",),
                                             vmem_limit_bytes=VMEM_LIMIT),
        name="inproj",
    )(x2d, modv, gpar, w_bf)


def _hgrn_stages(zh_ref, f_ref, lv_ref, par_ref, o_ref, state_ref, ex_ref):
    W = HG_WIDTH
    hq = zh_ref[:, 0:W]
    hf = zh_ref[:, W:2 * W]
    loglb = par_ref[0:1, :]
    log1mlb = par_ref[1:2, :]
    onemlb = par_ref[2:3, :]

    q = _silu(hq)
    e = jnp.exp(-jnp.abs(hf))
    inv = 1.0 / (1.0 + e)
    k = onemlb * (jnp.where(hf >= 0, e, 1.0) * inv)
    logsig = jnp.minimum(hf, 0.0) - jnp.log1p(e)
    c = log1mlb + logsig
    lf = jnp.maximum(loglb, c) + jnp.log1p(jnp.exp(-jnp.abs(loglb - c)))
    yield

    lv = lv_ref[...]
    t = HG_SUB
    for sb in range(T_BLK // HG_SUB):
        rows = slice(sb * t, (sb + 1) * t)
        ex_ref[sb] = jnp.exp(_dot(f_ref[...], jnp.concatenate(_split2(lf[rows]), axis=0)))
        yield
        for h in range(HG_HEADS):
            cs = slice(h * HG_DK, (h + 1) * HG_DK)
            qh = q[rows, cs]
            kh = k[rows, cs]
            vh = _bf(zh_ref[rows, 2 * W + h * HG_DK:2 * W + (h + 1) * HG_DK])
            qb = _bf(qh)
            kb = _bf(kh)
            s = jnp.where(lv == HG_LEVELS, _dot_nt(qb, kb), 0.0)
            for l in range(HG_LEVELS):
                el = _bf(ex_ref[sb, (2 + l) * t:(3 + l) * t, cs])
                s = jnp.where(lv == l, _dot_nt(qb * el, kb * el), s)
                if l % 2 == 1:
                    yield
            eb = ex_ref[sb, 0:t, cs]
            ebl = ex_ref[sb, t:2 * t, cs]
            st = state_ref[h]
            o = _dot(_bf(s), vh) + _dot_nt(_bf(qh * eb), _bf(st))
            upd = _dot_tn(vh, _bf(kh * ebl))
            state_ref[h] = st * ex_ref[sb, t - 1:t, cs] + upd
            ms = jnp.mean(o * o, axis=-1, keepdims=True)
            gate = _silu(zh_ref[rows, 3 * W + h * HG_DK:3 * W + (h + 1) * HG_DK])
            o_ref[rows, cs] = _bf(o * lax.rsqrt(ms + NORM_EPS) * par_ref[3:4, cs] * gate)
            yield


def _ret_stages(zr_ref, cos_ref, sin_ref, dmat_ref, qw_ref, kw_ref, par_ref, hm_ref, bd_ref,
                o_ref, state_ref):
    W = RET_WIDTH
    cos = cos_ref[...]
    sin = sin_ref[...]
    first_half = par_ref[2:3, :] > 0

    def rotary(t):
        swapped = jnp.where(first_half, pltpu.roll(t, W - RET_DK // 2, 1), pltpu.roll(t, RET_DK // 2, 1))
        return t * cos + swapped * sin

    q = rotary(zr_ref[:, 0:W])
    k = rotary(zr_ref[:, W:2 * W]) * (RET_DK ** -0.5)
    v = zr_ref[:, 2 * W:3 * W]
    kb = _bf(k)
    bd = bd_ref[...]
    yield

    s_parts = []
    v_parts = []
    for h in range(RET_HEADS):
        hm = hm_ref[h:h + 1, :]
        s_parts.append(_bf(_dot_nt(_bf(q * hm), kb) * dmat_ref[h]))
        v_parts.append(_bf(v * hm))
        yield
    st = state_ref[...]
    o = (_dot(jnp.concatenate(s_parts, axis=1), jnp.concatenate(v_parts, axis=0))
         + _dot(_bf(q * qw_ref[...]), _bf(st)))
    yield
    kv = _dot_tn(_bf(k * kw_ref[...]), _bf(v))
    state_ref[...] = st * par_ref[0:1, :] + bd * kv
    yield

    ms = _dot2_rhs01(o * o, _bf(bd)) * (1.0 / RET_DK)
    gate = _silu(zr_ref[:, 3 * W:4 * W])
    o_ref[...] = _bf(o * lax.rsqrt(ms + NORM_EPS) * par_ref[1:2, :] * gate)


def _gdn_stages(zg_ref, zab_ref, convw_ref, par_ref, eab_ref, gm_ref, bd_ref, tril_ref, hm_ref,
                o_ref, ext_ref, state_ref, obuf_ref):
    W = GDN_WIDTH
    t = T_BLK
    u = zg_ref[:, 0:3 * W]
    ext_ref[SUBLANES:SUBLANES + t, :] = u
    conv = (convw_ref[3:4, :] * u
            + convw_ref[2:3, :] * ext_ref[SUBLANES - 1:SUBLANES - 1 + t, :]
            + convw_ref[1:2, :] * ext_ref[SUBLANES - 2:SUBLANES - 2 + t, :]
            + convw_ref[0:1, :] * ext_ref[SUBLANES - 3:SUBLANES - 3 + t, :])
    ext_ref[0:SUBLANES, :] = u[t - SUBLANES:t, :]
    qkv = _silu(conv)
    q = qkv[:, 0:W]
    k = qkv[:, W:2 * W]
    v = qkv[:, 2 * W:3 * W]
    yield

    bd = bd_ref[...]
    bdb = _bf(bd)
    ab = zab_ref[...]
    a_exp = _dot2_rhs01(ab, eab_ref[:, 0:W])
    b_exp = _dot2_rhs01(ab, eab_ref[:, W:2 * W])
    qn = q * lax.rsqrt(_dot2_rhs01(q * q, bdb) + L2_EPS) * (GDN_DK ** -0.5)
    kn = k * lax.rsqrt(_dot2_rhs01(k * k, bdb) + L2_EPS)
    yield
    beta = _sigmoid(b_exp)
    g = par_ref[0:1, :] * _softplus(a_exp + par_ref[1:2, :])
    gc = _dot2_lhs01(tril_ref[...], g)
    gl = _dot2_lhs01(bdb, g)
    yield
    eg = jnp.exp(gc)
    vb = v * beta
    kbeta = kn * beta * eg
    qdec = qn * eg
    kdec = kn * jnp.exp(gl - gc)

    n_chunks = t // GDN_CHUNK
    hms = [_bf(jnp.broadcast_to(hm_ref[h:h + 1, :], (GDN_CHUNK, W))) for h in range(GDN_HEADS)]
    hms2 = [jnp.concatenate([m, m], axis=1) for m in hms]

    def chunk(a, c):
        return a[c * GDN_CHUNK:(c + 1) * GDN_CHUNK]

    def expand(y, masks):
        yb = _bf(y)
        return jnp.concatenate([yb * m for m in masks], axis=0)

    def blockprod(x, y, masks):
        xb = _bf(x)
        return jnp.concatenate([_dot(chunk(xb, c), expand(chunk(y, c), masks))
                                for c in range(n_chunks)], axis=0)

    gc_row = _dot2_lhs01(bdb, gc * gm_ref[5])
    yield
    rel = jnp.exp(jnp.where(gm_ref[0] > 0, gc - gc_row, -jnp.inf))

    knbeta = kn * beta
    kq = [_dot_nt(_bf(jnp.concatenate([chunk(knbeta, c), chunk(qn, c)], axis=0)),
                  expand(chunk(kn, c), hms)) for c in range(n_chunks)]
    yield
    m = jnp.concatenate([r[0:GDN_CHUNK] for r in kq], axis=0) * rel
    qk = jnp.concatenate([r[GDN_CHUNK:] for r in kq], axis=0) * rel

    d = m * gm_ref[1]
    d2 = blockprod(d, d, hms)
    yield
    d4 = blockprod(d2, d2, hms)
    dd2 = blockprod(d, d2, hms)
    yield
    x = d2 - d - dd2
    xd4 = blockprod(x, d4, hms)
    yield
    x = x + d4 + xd4
    for lvl in range(2, 5):
        lo = m * gm_ref[lvl]
        xl = blockprod(x, lo, hms)
        yield
        y = lo + xl
        yx = blockprod(y, x, hms)
        yield
        x = x - (y + yx)

    vk = jnp.concatenate([vb, kbeta], axis=1)
    wk = vk + blockprod(x, vk, hms2)
    yield
    w = wk[:, 0:W]
    kcum = wk[:, W:2 * W]
    ag = blockprod(qk, wk, hms2)
    yield
    a1 = ag[:, 0:W]
    qeff = qdec - ag[:, W:2 * W]

    for c in range(t // GDN_CHUNK):
        rows = slice(c * GDN_CHUNK, (c + 1) * GDN_CHUNK)
        st = state_ref[...]
        stb = _bf(st)
        vnew = w[rows] - _dot(_bf(kcum[rows]), stb)
        obuf_ref[rows, :] = _dot(_bf(qeff[rows]), stb) + a1[rows]
        yield
        upd = _dot_tn(_bf(kdec[rows]), _bf(vnew))
        last = eg[(c + 1) * GDN_CHUNK - 1:(c + 1) * GDN_CHUNK, :]
        state_ref[...] = st * last + bd * upd
        yield

    o = obuf_ref[...]
    ms = _dot2_rhs01(o * o, bdb) * (1.0 / GDN_DK)
    gate = _silu(zg_ref[:, 3 * W:4 * W])
    o_ref[...] = _bf(o * lax.rsqrt(ms + NORM_EPS) * par_ref[2:3, :] * gate)


_DONE = object()

def _mixers_kernel(zh_ref, f_ref, lv_ref, hpar_ref,
                   zr_ref, cos_ref, sin_ref, dmat_ref, qw_ref, kw_ref, rpar_ref,
                   zg_ref, zab_ref, convw_ref, gpar_ref, eab_ref, gm_ref, tril_ref, hm_ref, bd_ref,
                   ohg_ref, oret_ref, ogdn_ref,
                   hstate_ref, hex_ref, rstate_ref, gext_ref, gstate_ref, gobuf_ref):
    @pl.when(pl.program_id(1) == 0)
    def _():
        hstate_ref[...] = jnp.zeros_like(hstate_ref)
        rstate_ref[...] = jnp.zeros_like(rstate_ref)
        gstate_ref[...] = jnp.zeros_like(gstate_ref)
        gext_ref[0:SUBLANES, :] = jnp.zeros((SUBLANES, 3 * GDN_WIDTH), F32)

    active = [
        (_gdn_stages(zg_ref, zab_ref, convw_ref, gpar_ref, eab_ref, gm_ref, bd_ref, tril_ref, hm_ref,
                     ogdn_ref, gext_ref, gstate_ref, gobuf_ref), 1),
        (_hgrn_stages(zh_ref, f_ref, lv_ref, hpar_ref, ohg_ref, hstate_ref, hex_ref), 2),
        (_ret_stages(zr_ref, cos_ref, sin_ref, dmat_ref, qw_ref, kw_ref, rpar_ref, hm_ref, bd_ref,
                     oret_ref, rstate_ref), 1),
    ]
    while active:
        for entry in list(active):
            gen, per_round = entry
            for _ in range(per_round):
                if next(gen, _DONE) is _DONE:
                    active.remove(entry)
                    break


def _mixers(zh, zr, zg, zab, hg_tabs, ret_tabs, gdn_tabs, hm, bd, batch, steps):
    n = zh.shape[0]
    seq_row = lambda b, j: (b * steps + j, 0)

    def const(a):
        return pl.BlockSpec(a.shape, lambda b, j, nd=a.ndim: (0,) * nd)

    def rows(width):
        return pl.BlockSpec((T_BLK, width), seq_row)

    pos_rows = pl.BlockSpec((T_BLK, RET_WIDTH), lambda b, j: (j, 0))
    cos_t, sin_t = ret_tabs[0], ret_tabs[1]
    in_specs = ([rows(ZH_W)] + [const(a) for a in hg_tabs]
                + [rows(ZR_W), pos_rows, pos_rows] + [const(a) for a in ret_tabs[2:]]
                + [rows(ZG_W), rows(ZAB_W)] + [const(a) for a in gdn_tabs] + [const(hm), const(bd)])
    return pl.pallas_call(
        _mixers_kernel,
        grid=(batch, steps),
        in_specs=in_specs,
        out_specs=[rows(HG_WIDTH), rows(RET_WIDTH), rows(GDN_WIDTH)],
        out_shape=[jax.ShapeDtypeStruct((n, HG_WIDTH), BF16), jax.ShapeDtypeStruct((n, RET_WIDTH), BF16),
                   jax.ShapeDtypeStruct((n, GDN_WIDTH), BF16)],
        scratch_shapes=[pltpu.VMEM((HG_HEADS, HG_DK, HG_DK), F32),
                        pltpu.VMEM((T_BLK // HG_SUB, (2 + HG_LEVELS) * HG_SUB, HG_WIDTH), F32),
                        pltpu.VMEM((RET_WIDTH, RET_WIDTH), F32),
                        pltpu.VMEM((SUBLANES + T_BLK, 3 * GDN_WIDTH), F32),
                        pltpu.VMEM((GDN_WIDTH, GDN_WIDTH), F32),
                        pltpu.VMEM((T_BLK, GDN_WIDTH), F32)],
        compiler_params=pltpu.CompilerParams(dimension_semantics=("arbitrary", "arbitrary"),
                                             vmem_limit_bytes=VMEM_LIMIT),
        name="mixers",
    )(zh, *hg_tabs, zr, cos_t, sin_t, *ret_tabs[2:], zg, zab, *gdn_tabs, hm, bd)


def _outproj_kernel(ohg_ref, oret_ref, ogdn_ref, x_ref, mod_ref, g_ref, w_ref, rw_ref, rb_ref,
                    su_ref, x1_ref, hn_ref, ri_ref, gates_ref, cnt_ref, run_ref):
    @pl.when(pl.program_id(0) == 0)
    def _():
        run_ref[...] = jnp.zeros_like(run_ref)

    rw_hi = rw_ref[0:LANES, :]
    rw_lo = rw_ref[LANES:2 * LANES, :]
    eid = lax.broadcasted_iota(jnp.int32, (LANES, ROUTE_SLAB), 0)
    routed = {}

    def slab_stages(s):
        rows = slice(s * ROUTE_SLAB, (s + 1) * ROUTE_SLAB)
        y = (_dot(ohg_ref[rows, :], w_ref[0:HG_WIDTH, :])
             + _dot(oret_ref[rows, :], w_ref[HG_WIDTH:HG_WIDTH + RET_WIDTH, :])
             + _dot(ogdn_ref[rows, :], w_ref[HG_WIDTH + RET_WIDTH:, :]))
        yield
        x1 = x_ref[rows, :] + mod_ref[0, 0:1, :] * y
        x1_ref[rows, :] = x1
        n = x1 * lax.rsqrt(jnp.mean(x1 * x1, axis=-1, keepdims=True) + NORM_EPS)
        hn = (n * g_ref[0:1, :]) * (1.0 + mod_ref[0, 1:2, :]) + mod_ref[0, 2:3, :]
        hn_ref[rows, :] = _pack_halves(hn)
        yield
        hn_hi, hn_lo = _split2(hn)
        logits = (_dot_nt(rw_hi, hn_hi) + _dot_nt(rw_lo, hn_hi) + _dot_nt(rw_hi, hn_lo)
                  + rb_ref[:, rows])
        yield
        work = jnp.where(eid < N_EXPERTS, logits, -jnp.inf)
        vals, idxs = [], []
        multihot = jnp.zeros((LANES, ROUTE_SLAB), F32)
        for kk in range(TOP_K):
            mx = jnp.max(work, axis=0, keepdims=True)
            ix = jnp.min(jnp.where(work == mx, eid, LANES), axis=0, keepdims=True)
            sel = eid == ix
            multihot = jnp.where(sel, 1.0, multihot)
            work = jnp.where(sel, -jnp.inf, work)
            vals.append(mx)
            idxs.append(ix)
            if kk % 2 == 1:
                yield
        routed[s] = (vals, idxs, multihot)

    n_slabs = TM // ROUTE_SLAB
    active = [slab_stages(s) for s in range(n_slabs)]
    while active:
        for gen in list(active):
            if next(gen, _DONE) is _DONE:
                active.remove(gen)

    multihot = jnp.concatenate([routed[s][2] for s in range(n_slabs)], axis=1)
    run = run_ref[...]
    before = _dot(_bf(multihot), su_ref[...]) + jnp.concatenate([run] * (TM // LANES), axis=1)
    run = run + jnp.sum(multihot, axis=1, keepdims=True)
    run_ref[...] = run
    cnt_ref[...] = run

    for s in range(n_slabs):
        vals, idxs, _ = routed[s]
        cols = slice(s * ROUTE_SLAB, (s + 1) * ROUTE_SLAB)
        ex = [jnp.exp(vv - vals[0]) for vv in vals]
        den = ex[0] + ex[1] + ex[2] + ex[3]
        ranks = [jnp.sum(jnp.where(eid == ix, before[:, cols], 0.0), axis=0, keepdims=True) for ix in idxs]
        ri_ref[:, cols] = jnp.concatenate(idxs + [r.astype(jnp.int32) for r in ranks], axis=0)
        gates_ref[:, cols] = jnp.concatenate([e / den for e in ex]
                                             + [jnp.zeros((TOP_K, ROUTE_SLAB), F32)], axis=0)


def _outproj(ohg, oret, ogdn, x2d, modv, gpar, w_bf, rw, rb, su, tiles_per_batch):
    n = x2d.shape[0]
    row = lambda i: (i, 0)
    col = lambda i: (0, i)
    const = lambda i: (0, 0)
    return pl.pallas_call(
        _outproj_kernel,
        grid=(n // TM,),
        in_specs=[pl.BlockSpec((TM, HG_WIDTH), row), pl.BlockSpec((TM, RET_WIDTH), row),
                  pl.BlockSpec((TM, GDN_WIDTH), row), pl.BlockSpec((TM, D_MODEL), row),
                  pl.BlockSpec((1, SUBLANES, D_MODEL), lambda i: (i // tiles_per_batch, 0, 0)),
                  pl.BlockSpec((SUBLANES, D_MODEL), const),
                  pl.BlockSpec((D_MODEL, D_MODEL), const),
                  pl.BlockSpec((2 * LANES, D_MODEL), const),
                  pl.BlockSpec((LANES, TM), const),
                  pl.BlockSpec((TM, TM), const)],
        out_specs=[pl.BlockSpec((TM, D_MODEL), row), pl.BlockSpec((TM, D_MODEL // 2), row),
                   pl.BlockSpec((2 * TOP_K, TM), col), pl.BlockSpec((2 * TOP_K, TM), col),
                   pl.BlockSpec((LANES, LANES), const)],
        out_shape=[jax.ShapeDtypeStruct((n, D_MODEL), F32),
                   jax.ShapeDtypeStruct((n, D_MODEL // 2), jnp.uint32),
                   jax.ShapeDtypeStruct((2 * TOP_K, n), jnp.int32),
                   jax.ShapeDtypeStruct((2 * TOP_K, n), F32),
                   jax.ShapeDtypeStruct((LANES, LANES), F32)],
        scratch_shapes=[pltpu.VMEM((LANES, LANES), F32)],
        compiler_params=pltpu.CompilerParams(dimension_semantics=("arbitrary",),
                                             vmem_limit_bytes=VMEM_LIMIT),
        name="outproj_router",
    )(ohg, oret, ogdn, x2d, modv, gpar, w_bf, rw, rb, su)


def _expert_kernel(be_ref, nu_ref, x_ref, w1_ref, b1_ref, w2_ref, b2_ref, y_ref, w1b_ref, w2b_ref):
    i = pl.program_id(0)
    prev = be_ref[jnp.maximum(i - 1, 0)]
    fresh = jnp.logical_or(i == 0, be_ref[i] != prev)

    @pl.when(fresh)
    def _():
        w1b_ref[...] = _bf(w1_ref[0])
        w2b_ref[...] = _bf(w2_ref[0])

    @pl.when(i < nu_ref[0])
    def _():
        half = D_MODEL // 2

        def slab_stages(s):
            rows = slice(s * EXPERT_SLAB, (s + 1) * EXPERT_SLAB)
            x_lo, x_hi = _unpack_halves(x_ref[rows, :])
            hid = (_dot(_bf(x_lo), w1b_ref[0:half, :]) + _dot(_bf(x_hi), w1b_ref[half:, :])
                   + b1_ref[0])
            yield
            x_glu = jnp.minimum(hid[:, 0:D_FF], SWIGLU_LIMIT)
            x_lin = jnp.clip(hid[:, D_FF:], -SWIGLU_LIMIT, SWIGLU_LIMIT)
            act = x_glu * _sigmoid(SWIGLU_ALPHA * x_glu) * (x_lin + 1.0)
            yield
            y_ref[rows, :] = _pack_halves(_dot(_bf(act), w2b_ref[...]) + b2_ref[0])

        active = [slab_stages(s) for s in range(R_BLK // EXPERT_SLAB)]
        while active:
            for gen in list(active):
                if next(gen, _DONE) is _DONE:
                    active.remove(gen)


def _experts(block_e, n_used, xs, w1, b1, w2, b2):
    p = xs.shape[0]
    n_blocks = p // R_BLK
    ne = w1.shape[0] * w1.shape[1]
    w1 = w1.reshape(ne, D_MODEL, 2 * D_FF)
    w2 = w2.reshape(ne, D_FF, D_MODEL)
    rowmap = lambda i, be, nu: (jnp.minimum(i, nu[0] - 1), 0)
    emap = lambda i, be, nu: (be[i], 0, 0)
    grid_spec = pltpu.PrefetchScalarGridSpec(
        num_scalar_prefetch=2,
        grid=(n_blocks,),
        in_specs=[pl.BlockSpec((R_BLK, D_MODEL // 2), rowmap),
                  pl.BlockSpec((1, D_MODEL, 2 * D_FF), emap),
                  pl.BlockSpec((1, 1, 2 * D_FF), emap),
                  pl.BlockSpec((1, D_FF, D_MODEL), emap),
                  pl.BlockSpec((1, 1, D_MODEL), emap)],
        out_specs=pl.BlockSpec((R_BLK, D_MODEL // 2), rowmap),
        scratch_shapes=[pltpu.VMEM((D_MODEL, 2 * D_FF), BF16), pltpu.VMEM((D_FF, D_MODEL), BF16)],
    )
    return pl.pallas_call(
        _expert_kernel,
        grid_spec=grid_spec,
        out_shape=jax.ShapeDtypeStruct((p, D_MODEL // 2), jnp.uint32),
        compiler_params=pltpu.CompilerParams(dimension_semantics=("arbitrary",),
                                             vmem_limit_bytes=VMEM_LIMIT),
        name="experts",
    )(block_e, n_used, xs, w1, b1.reshape(ne, 1, 2 * D_FF), w2, b2.reshape(ne, 1, D_MODEL))


def _sc_mesh():
    return plsc.VectorSubcoreMesh(core_axis_name="c", subcore_axis_name="s")


def _sc_scatter_rows(x, pos, p_rows):
    mesh = _sc_mesh()
    n, w = x.shape
    per_worker = n // (mesh.num_cores * mesh.num_subcores)
    assert per_worker % SC_ROWS == 0 and pos.shape == (TOP_K, n)

    @functools.partial(
        pl.kernel, out_type=jax.ShapeDtypeStruct((p_rows, w), x.dtype), mesh=mesh,
        scratch_types=[pltpu.VMEM((SC_ROWS,), jnp.int32)] * TOP_K
        + [pltpu.VMEM((SC_ROWS, w), x.dtype), pltpu.SemaphoreType.DMA],
        name="dispatch_rows")
    def scatter(x_hbm, p_hbm, o_hbm, i0, i1, i2, i3, rows_v, sem):
        idx = (i0, i1, i2, i3)
        worker = lax.axis_index("s") * mesh.num_cores + lax.axis_index("c")

        @pl.loop(0, per_worker // SC_ROWS)
        def _(g):
            base = pl.multiple_of(worker * per_worker + g * SC_ROWS, SC_ROWS)
            pltpu.sync_copy(x_hbm.at[pl.ds(base, SC_ROWS)], rows_v)
            for kk in range(TOP_K):
                pltpu.sync_copy(p_hbm.at[kk, pl.ds(base, SC_ROWS)], idx[kk])
            copies = [pltpu.async_copy(rows_v, o_hbm.at[idx[kk]], sem) for kk in range(TOP_K)]
            for cp in copies:
                cp.wait()

    return scatter(x, pos)


def _sc_gather_rows(table, idx):
    mesh = _sc_mesh()
    n_idx = idx.shape[0]
    w = table.shape[1]
    per_worker = n_idx // (mesh.num_cores * mesh.num_subcores)
    assert per_worker % SC_ROWS == 0

    @functools.partial(
        pl.kernel, out_type=jax.ShapeDtypeStruct((n_idx, w), table.dtype), mesh=mesh,
        scratch_types=[pltpu.VMEM((SC_ROWS,), jnp.int32), pltpu.VMEM((SC_ROWS, w), table.dtype),
                       pltpu.SemaphoreType.DMA],
        name="combine_rows")
    def gather(t_hbm, i_hbm, o_hbm, idx_v, rows_v, sem):
        worker = lax.axis_index("s") * mesh.num_cores + lax.axis_index("c")

        @pl.loop(0, per_worker // SC_ROWS)
        def _(g):
            base = pl.multiple_of(worker * per_worker + g * SC_ROWS, SC_ROWS)
            pltpu.sync_copy(i_hbm.at[pl.ds(base, SC_ROWS)], idx_v)
            pltpu.async_copy(t_hbm.at[idx_v], rows_v, sem).wait()
            pltpu.sync_copy(rows_v, o_hbm.at[pl.ds(base, SC_ROWS)])

    return gather(table, idx)


def _combine_kernel(x_ref, y0_ref, y1_ref, y2_ref, y3_ref, gates_ref, mod_ref, g_ref, o_ref, *, final):
    gt = gates_ref[...]
    half = D_MODEL // 2
    acc_lo, acc_hi = None, None
    for kk, y_ref in enumerate((y0_ref, y1_ref, y2_ref, y3_ref)):
        lo, hi = _unpack_halves(y_ref[0])
        g = gt[:, kk:kk + 1]
        acc_lo = g * lo if acc_lo is None else acc_lo + g * lo
        acc_hi = g * hi if acc_hi is None else acc_hi + g * hi
    x_lo = x_ref[:, 0:half] + mod_ref[0, 3:4, 0:half] * acc_lo
    x_hi = x_ref[:, half:] + mod_ref[0, 3:4, half:] * acc_hi
    if final:
        ssq = (jnp.sum(x_lo * x_lo, axis=-1, keepdims=True)
               + jnp.sum(x_hi * x_hi, axis=-1, keepdims=True))
        r = lax.rsqrt(ssq * (1.0 / D_MODEL) + NORM_EPS)
        x_lo = x_lo * r * g_ref[1:2, 0:half]
        x_hi = x_hi * r * g_ref[1:2, half:]
    o_ref[:, 0:half] = x_lo
    o_ref[:, half:] = x_hi


def _combine(x, yg, gates, modv, gpar, tiles_per_batch, final, piece):
    n = x.shape[0]
    tiles = n // TM // COMBINE_PIECES
    first = piece * tiles
    row = lambda i: (first + i, 0)
    ysp = [pl.BlockSpec((1, TM, D_MODEL // 2), (lambda i, kk=kk: (kk, i, 0))) for kk in range(TOP_K)]
    return pl.pallas_call(
        functools.partial(_combine_kernel, final=final),
        grid=(tiles,),
        in_specs=[pl.BlockSpec((TM, D_MODEL), row)] + ysp + [
            pl.BlockSpec((TM, 2 * TOP_K), row),
            pl.BlockSpec((1, SUBLANES, D_MODEL), lambda i: ((first + i) // tiles_per_batch, 0, 0)),
            pl.BlockSpec((SUBLANES, D_MODEL), lambda i: (0, 0))],
        out_specs=pl.BlockSpec((TM, D_MODEL), row),
        out_shape=jax.ShapeDtypeStruct((n, D_MODEL), F32),
        input_output_aliases={0: 0},
        compiler_params=pltpu.CompilerParams(dimension_semantics=("arbitrary",),
                                             vmem_limit_bytes=VMEM_LIMIT),
        name="combine",
    )(x, yg, yg, yg, yg, gates, modv, gpar)


def _widen_w_in(w):
    pad = jnp.zeros((w.shape[0], Z_W - w.shape[1]), w.dtype)
    return _bf(jnp.concatenate([w, pad], axis=1))


def _pad_rows(a, rows=SUBLANES):
    return jnp.concatenate([a, jnp.zeros((rows - a.shape[0],) + a.shape[1:], a.dtype)], axis=0)


def kernel(x, c, ada_w, ada_b, norm1_g, norm2_g, w_in, w_out, hg_lb_logits, hg_norm_g, ret_norm_g,
           gdn_conv_w, gdn_A_log, gdn_dt_bias, gdn_norm_g, router_w, router_b, exp_w1, exp_b1,
           exp_w2, exp_b2, final_norm_g):
    batch, seq, d = x.shape
    depth = ada_w.shape[0]
    n = batch * seq
    steps = seq // T_BLK
    tiles_per_batch = seq // TM
    nk = n * TOP_K
    n_blocks = nk // R_BLK + N_EXPERTS
    p_rows = n_blocks * R_BLK

    lv_np = _level_matrix(T_BLK)
    lv = jnp.asarray(_level_matrix(HG_SUB))
    f_np = _hgrn_exponent_matrix(HG_SUB)
    f_mat = jnp.asarray(np.concatenate([f_np, f_np], axis=1), BF16)
    gm = jnp.asarray(_gdn_masks(T_BLK))
    bd = jnp.asarray(_block_diag_mask(T_BLK, GDN_CHUNK))
    tril = jnp.asarray(_block_diag_mask(T_BLK, GDN_CHUNK) * (lv_np >= 0), BF16)
    hm = jnp.asarray(_head_masks(RET_WIDTH, RET_HEADS))
    su = jnp.asarray(np.triu(np.ones((TM, TM), np.float32), 1), BF16)
    eab_np = np.zeros((LANES, 2 * GDN_WIDTH), np.float32)
    for h in range(GDN_HEADS):
        eab_np[h, h * GDN_DK:(h + 1) * GDN_DK] = 1.0
        eab_np[GDN_HEADS + h, GDN_WIDTH + h * GDN_DK:GDN_WIDTH + (h + 1) * GDN_DK] = 1.0
    eab = jnp.asarray(eab_np, BF16)

    half = RET_DK // 2
    inv = ROPE_BASE ** (-jnp.linspace(0.0, 1.0, half, dtype=F32))
    ang = jnp.arange(seq, dtype=F32)[:, None] * inv[None, :]
    cos_t = jnp.tile(jnp.cos(ang), (1, 2 * RET_HEADS))
    first_half = jnp.asarray((np.arange(RET_WIDTH) % RET_DK < half).astype(np.float32))[None, :]
    sin_t = jnp.tile(jnp.sin(ang), (1, 2 * RET_HEADS)) * (1.0 - 2.0 * first_half)
    log_g = jnp.log1p(-jnp.exp2(-5.0 - jnp.arange(RET_HEADS, dtype=F32)))
    jj = jnp.arange(T_BLK, dtype=F32)
    diff = jj[:, None] - jj[None, :]
    dmat = jnp.where(diff[None] >= 0, jnp.exp(diff[None] * log_g[:, None, None]), 0.0)
    lg_lane = jnp.repeat(log_g, RET_DK)[None, :]
    qw = jnp.exp(lg_lane * (jj[:, None] + 1.0))
    kw = jnp.exp(lg_lane * (T_BLK - 1.0 - jj[:, None]))
    cdec = jnp.exp(T_BLK * lg_lane)

    lb = jnp.cumsum(jax.nn.softmax(hg_lb_logits.astype(F32), axis=0), axis=0)
    lb = jnp.maximum(lb - lb[0], 0.0)
    c_pad = _pad_rows(c.astype(F32))
    mod = _adaln(c_pad, ada_w, ada_b)[:, :batch, :]

    x2d = x.reshape(n, d)
    out = None
    for l in range(depth):
        sh1, sc1, gt1, sh2, sc2, gt2 = [mod[l][:, i * d:(i + 1) * d] for i in range(6)]
        zeros = jnp.zeros_like(sh1)
        modv_a = jnp.stack([sc1, sh1, zeros, zeros, zeros, zeros, zeros, zeros], axis=1)
        modv_c = jnp.stack([gt1, sc2, sh2, gt2, zeros, zeros, zeros, zeros], axis=1)
        gpar_a = _pad_rows(norm1_g[l][None, :])
        gpar_c = _pad_rows(jnp.stack([norm2_g[l], final_norm_g], axis=0))

        zh, zr, zg, zab = _inproj(x2d, modv_a, gpar_a, _widen_w_in(w_in[l]), seq)

        hg_par = _pad_rows(jnp.stack([jnp.log(lb[l]), jnp.log1p(-lb[l]), 1.0 - lb[l],
                                      hg_norm_g[l].reshape(-1)], axis=0))
        ret_par = _pad_rows(jnp.concatenate([cdec, ret_norm_g[l].reshape(1, -1), first_half], axis=0))
        gdn_par = _pad_rows(jnp.stack([jnp.repeat(-jnp.exp(gdn_A_log[l].astype(F32)), GDN_DK),
                                       jnp.repeat(gdn_dt_bias[l].astype(F32), GDN_DK),
                                       gdn_norm_g[l].reshape(-1)], axis=0))
        o_hg, o_ret, o_gdn = _mixers(
            zh, zr, zg, zab,
            (f_mat, lv, hg_par),
            (cos_t, sin_t, dmat, qw, kw, ret_par),
            (_pad_rows(gdn_conv_w[l].astype(F32)), gdn_par, eab, gm, tril),
            hm, bd, batch, steps)

        rw_f = jnp.concatenate([router_w[l].T, jnp.zeros((LANES - N_EXPERTS, d), F32)], axis=0)
        rw = jnp.concatenate(_split2(rw_f), axis=0)
        rb = jnp.broadcast_to(jnp.concatenate([router_b[l], jnp.zeros((LANES - N_EXPERTS,), F32)])[:, None],
                              (LANES, TM))
        x1, hn2, ri, gates_t, cnt = _outproj(o_hg, o_ret, o_gdn, x2d, modv_c, gpar_c, _bf(w_out[l]),
                                             rw, rb, su, tiles_per_batch)
        gates = gates_t.T

        counts = cnt[:N_EXPERTS, 0].astype(jnp.int32)
        padded = (counts + R_BLK - 1) // R_BLK * R_BLK
        pend = jnp.cumsum(padded)
        pstart = pend - padded
        n_used = (pend[-1] // R_BLK).astype(jnp.int32)
        blk_start = jnp.arange(n_blocks, dtype=jnp.int32) * R_BLK
        blk_start = jnp.minimum(blk_start, pend[-1] - R_BLK)
        block_e = jnp.sum(blk_start[:, None] >= pend[None, :], axis=1).astype(jnp.int32)
        eid = jnp.arange(N_EXPERTS, dtype=jnp.int32)
        pos_t = ri[TOP_K:] + jnp.sum(jnp.where(ri[:TOP_K, :, None] == eid, pstart, 0), axis=-1)
        xs = _sc_scatter_rows(hn2, pos_t, p_rows)
        y = _experts(block_e + l * N_EXPERTS, n_used.reshape(1), xs, exp_w1, exp_b1, exp_w2, exp_b2)
        n_piece = n // COMBINE_PIECES
        x2d = x1
        for p in range(COMBINE_PIECES):
            idx = pos_t[:, p * n_piece:(p + 1) * n_piece].reshape(-1)
            yg = _sc_gather_rows(y, idx).reshape(TOP_K, n_piece, d // 2)
            x2d = _combine(x2d, yg, gates, modv_c, gpar_c, tiles_per_batch, l == depth - 1, p)
    return x2d.reshape(batch, seq, d)
```

```python
import functools
import math

import numpy as np
import jax
import jax.numpy as jnp
from jax import lax
from jax.experimental import pallas as pl
from jax.experimental.pallas import tpu as pltpu
from jax.experimental.pallas import tpu_sc as plsc

F32 = jnp.float32
BF16 = jnp.bfloat16

D_MODEL = 1024
HG_HEADS, HG_DK = 4, 128
HG_WIDTH = HG_HEADS * HG_DK
RET_HEADS, RET_DK = 4, 64
RET_WIDTH = RET_HEADS * RET_DK
GDN_HEADS, GDN_DK = 4, 64
GDN_WIDTH = GDN_HEADS * GDN_DK
CONV_K = 4
ROPE_BASE = 10000.0
N_EXPERTS = 32
TOP_K = 4
D_FF = D_MODEL
SWIGLU_ALPHA = 1.702
SWIGLU_LIMIT = 7.0
NORM_EPS = 1e-6
L2_EPS = 1e-6
GDN_CHUNK = 64

LANES = 128
SUBLANES = 8
VMEM_LIMIT = 56 * 1024 * 1024

T_BLK = 256
SEQ_PER_STEP = 2
TM = 512
TM_IN = 512
ROUTE_SLAB = 128
COMBINE_PIECES = 4
R_BLK = 512
EXPERT_SLAB = 256
SC_ROWS = 128
HG_SUB = 128
HG_LEVELS = 7

ZH_W = 4 * HG_WIDTH
ZR_W = 4 * RET_WIDTH
ZG_W = 4 * GDN_WIDTH
ZAB_W = LANES
Z_W = ZH_W + ZR_W + ZG_W + ZAB_W


def _dot(a, b):
    return jnp.dot(a, b, preferred_element_type=F32)


def _dot_nt(a, b):
    return lax.dot_general(a, b, (((1,), (1,)), ((), ())), preferred_element_type=F32)


def _dot_tn(a, b):
    return lax.dot_general(a, b, (((0,), (0,)), ((), ())), preferred_element_type=F32)


def _split2(x):
    hi = x.astype(BF16)
    return hi, (x - hi.astype(F32)).astype(BF16)


def _dot2_lhs01(c, x):
    hi, lo = _split2(x)
    return _dot(c, hi) + _dot(c, lo)


def _dot2_rhs01(x, c):
    hi, lo = _split2(x)
    return _dot(hi, c) + _dot(lo, c)


def _sigmoid(x):
    return 1.0 / (1.0 + jnp.exp(-x))


def _silu(x):
    return x * _sigmoid(x)


def _softplus(x):
    return jnp.maximum(x, 0.0) + jnp.log1p(jnp.exp(-jnp.abs(x)))


def _bf(x):
    return x.astype(BF16)


def _pack_halves(x):
    w = x.shape[1] // 2
    bits = lax.bitcast_convert_type(_bf(x).astype(F32), jnp.uint32)
    return (bits[:, :w] >> 16) | bits[:, w:]


def _unpack_halves(p):
    lo = lax.bitcast_convert_type(p << 16, F32)
    hi = lax.bitcast_convert_type(p & jnp.uint32(0xFFFF0000), F32)
    return lo, hi


def _level_matrix(t):
    i = np.arange(t)[:, None]
    j = np.arange(t)[None, :]
    x = i ^ j
    lv = np.floor(np.log2(np.maximum(x, 1))).astype(np.int32)
    lv = np.where(i == j, int(math.log2(t)), lv)
    lv = np.where(i < j, -1, lv)
    return lv.astype(np.int32)


def _hgrn_exponent_matrix(t):
    n_lev = int(math.log2(t))
    f = np.zeros((2 + n_lev, t, t), np.float32)
    u = np.arange(t)[None, :]
    r = np.arange(t)[:, None]
    f[0] = (u <= r)
    f[1] = (u > r)
    for l in range(n_lev):
        h = 1 << l
        base = (r // (2 * h)) * (2 * h)
        mid = base + h
        upper = (r - base) >= h
        f[2 + l] = np.where(upper, (u >= mid) & (u <= r), (u > r) & (u < mid))
    return f.reshape((2 + n_lev) * t, t)


def _gdn_masks(t):
    lv = _level_matrix(GDN_CHUNK)
    top = int(math.log2(GDN_CHUNK))
    incl = lv >= 0
    d8 = (lv >= 0) & (lv <= 2)
    merges = [(lv == l) for l in range(3, top)]
    eye = lv == top
    pats = np.stack([incl, d8] + merges + [eye]).astype(np.float32)
    return np.tile(pats, (1, t // GDN_CHUNK, GDN_HEADS))


def _block_diag_mask(t, blk):
    i = np.arange(t)
    return (i[:, None] // blk == i[None, :] // blk).astype(np.float32)


def _head_masks(width, heads):
    lane = np.arange(width)[None, :]
    m = np.zeros((SUBLANES, width), np.float32)
    for h in range(heads):
        m[h] = (lane // (width // heads) == h)[0]
    return m


def _adaln_kernel(c_ref, w_ref, b_ref, o_ref):
    cond = _silu(c_ref[...])
    o_ref[0] = jnp.dot(cond, w_ref[0], preferred_element_type=F32,
                       precision=lax.Precision.HIGHEST) + b_ref[0]


def _adaln(c_pad, ada_w, ada_b):
    depth, d, n6 = ada_w.shape
    tn = n6 // 4
    return pl.pallas_call(
        _adaln_kernel,
        grid=(depth, n6 // tn),
        in_specs=[pl.BlockSpec((SUBLANES, d), lambda l, j: (0, 0)),
                  pl.BlockSpec((1, d, tn), lambda l, j: (l, 0, j)),
                  pl.BlockSpec((1, 1, tn), lambda l, j: (l, 0, j))],
        out_specs=pl.BlockSpec((1, SUBLANES, tn), lambda l, j: (l, 0, j)),
        out_shape=jax.ShapeDtypeStruct((depth, SUBLANES, n6), F32),
        compiler_params=pltpu.CompilerParams(vmem_limit_bytes=VMEM_LIMIT),
        name="adaln",
    )(c_pad, ada_w, ada_b.reshape(depth, 1, n6))


def _inproj_kernel(x_ref, mod_ref, g_ref, w_ref, zh_ref, zr_ref, zg_ref, zab_ref):
    x = x_ref[...]
    y = x * lax.rsqrt(jnp.mean(x * x, axis=-1, keepdims=True) + NORM_EPS)
    hn = (y * g_ref[0:1, :]) * (1.0 + mod_ref[0, 0:1, :]) + mod_ref[0, 1:2, :]
    hb = _bf(hn)
    zh_ref[...] = _dot(hb, w_ref[:, 0:ZH_W])
    zr_ref[...] = _dot(hb, w_ref[:, ZH_W:ZH_W + ZR_W])
    zg_ref[...] = _dot(hb, w_ref[:, ZH_W + ZR_W:ZH_W + ZR_W + ZG_W])
    zab_ref[...] = _dot(hb, w_ref[:, ZH_W + ZR_W + ZG_W:Z_W])


def _inproj(x2d, modv, gpar, w_bf, seq):
    n = x2d.shape[0]
    row = lambda i: (i, 0)
    tiles_per_batch = seq // TM_IN
    return pl.pallas_call(
        _inproj_kernel,
        grid=(n // TM_IN,),
        in_specs=[pl.BlockSpec((TM_IN, D_MODEL), row),
                  pl.BlockSpec((1, SUBLANES, D_MODEL), lambda i: (i // tiles_per_batch, 0, 0)),
                  pl.BlockSpec((SUBLANES, D_MODEL), lambda i: (0, 0)),
                  pl.BlockSpec((D_MODEL, Z_W), lambda i: (0, 0))],
        out_specs=[pl.BlockSpec((TM_IN, ZH_W), row), pl.BlockSpec((TM_IN, ZR_W), row),
                   pl.BlockSpec((TM_IN, ZG_W), row), pl.BlockSpec((TM_IN, ZAB_W), row)],
        out_shape=[jax.ShapeDtypeStruct((n, ZH_W), F32), jax.ShapeDtypeStruct((n, ZR_W), F32),
                   jax.ShapeDtypeStruct((n, ZG_W), F32), jax.ShapeDtypeStruct((n, ZAB_W), F32)],
        compiler_params=pltpu.CompilerParams(dimension_semantics=("arbitrary",),
                                             vmem_limit_bytes=VMEM_LIMIT),
        name="inproj",
    )(x2d, modv, gpar, w_bf)


def _hgrn_stages(zh_ref, f_ref, lv_ref, par_ref, o_ref, state_ref, ex_ref):
    W = HG_WIDTH
    hq = zh_ref[:, 0:W]
    hf = zh_ref[:, W:2 * W]
    loglb = par_ref[0:1, :]
    log1mlb = par_ref[1:2, :]
    onemlb = par_ref[2:3, :]

    q = _silu(hq)
    e = jnp.exp(-jnp.abs(hf))
    inv = 1.0 / (1.0 + e)
    k = onemlb * (jnp.where(hf >= 0, e, 1.0) * inv)
    logsig = jnp.minimum(hf, 0.0) - jnp.log1p(e)
    c = log1mlb + logsig
    lf = jnp.maximum(loglb, c) + jnp.log1p(jnp.exp(-jnp.abs(loglb - c)))
    yield

    lv = lv_ref[...]
    t = HG_SUB
    for sb in range(T_BLK // HG_SUB):
        rows = slice(sb * t, (sb + 1) * t)
        ex_ref[sb] = jnp.exp(_dot(f_ref[...], jnp.concatenate(_split2(lf[rows]), axis=0)))
        yield
        for h in range(HG_HEADS):
            cs = slice(h * HG_DK, (h + 1) * HG_DK)
            qh = q[rows, cs]
            kh = k[rows, cs]
            vh = _bf(zh_ref[rows, 2 * W + h * HG_DK:2 * W + (h + 1) * HG_DK])
            qb = _bf(qh)
            kb = _bf(kh)
            s = jnp.where(lv == HG_LEVELS, _dot_nt(qb, kb), 0.0)
            for l in range(HG_LEVELS):
                el = _bf(ex_ref[sb, (2 + l) * t:(3 + l) * t, cs])
                s = jnp.where(lv == l, _dot_nt(qb * el, kb * el), s)
                if l % 2 == 1:
                    yield
            eb = ex_ref[sb, 0:t, cs]
            ebl = ex_ref[sb, t:2 * t, cs]
            st = state_ref[h]
            o = _dot(_bf(s), vh) + _dot_nt(_bf(qh * eb), _bf(st))
            upd = _dot_tn(vh, _bf(kh * ebl))
            state_ref[h] = st * ex_ref[sb, t - 1:t, cs] + upd
            ms = jnp.mean(o * o, axis=-1, keepdims=True)
            gate = _silu(zh_ref[rows, 3 * W + h * HG_DK:3 * W + (h + 1) * HG_DK])
            o_ref[rows, cs] = _bf(o * lax.rsqrt(ms + NORM_EPS) * par_ref[3:4, cs] * gate)
            yield


def _ret_stages(zr_ref, cos_ref, sin_ref, dmat_ref, qw_ref, kw_ref, par_ref, hm_ref, bd_ref,
                o_ref, state_ref):
    W = RET_WIDTH
    cos = cos_ref[...]
    sin = sin_ref[...]
    first_half = par_ref[2:3, :] > 0

    def rotary(t):
        swapped = jnp.where(first_half, pltpu.roll(t, W - RET_DK // 2, 1), pltpu.roll(t, RET_DK // 2, 1))
        return t * cos + swapped * sin

    q = rotary(zr_ref[:, 0:W])
    k = rotary(zr_ref[:, W:2 * W]) * (RET_DK ** -0.5)
    v = zr_ref[:, 2 * W:3 * W]
    kb = _bf(k)
    bd = bd_ref[...]
    yield

    s_parts = []
    v_parts = []
    for h in range(RET_HEADS):
        hm = hm_ref[h:h + 1, :]
        s_parts.append(_bf(_dot_nt(_bf(q * hm), kb) * dmat_ref[h]))
        v_parts.append(_bf(v * hm))
        yield
    st = state_ref[...]
    o = (_dot(jnp.concatenate(s_parts, axis=1), jnp.concatenate(v_parts, axis=0))
         + _dot(_bf(q * qw_ref[...]), _bf(st)))
    yield
    kv = _dot_tn(_bf(k * kw_ref[...]), _bf(v))
    state_ref[...] = st * par_ref[0:1, :] + bd * kv
    yield

    ms = _dot2_rhs01(o * o, _bf(bd)) * (1.0 / RET_DK)
    gate = _silu(zr_ref[:, 3 * W:4 * W])
    o_ref[...] = _bf(o * lax.rsqrt(ms + NORM_EPS) * par_ref[1:2, :] * gate)


def _gdn_stages(zg_ref, zab_ref, convw_ref, par_ref, eab_ref, gm_ref, bd_ref, tril_ref, hm_ref,
                o_ref, ext_ref, state_ref, obuf_ref):
    W = GDN_WIDTH
    t = T_BLK
    u = zg_ref[:, 0:3 * W]
    ext_ref[SUBLANES:SUBLANES + t, :] = u
    conv = (convw_ref[3:4, :] * u
            + convw_ref[2:3, :] * ext_ref[SUBLANES - 1:SUBLANES - 1 + t, :]
            + convw_ref[1:2, :] * ext_ref[SUBLANES - 2:SUBLANES - 2 + t, :]
            + convw_ref[0:1, :] * ext_ref[SUBLANES - 3:SUBLANES - 3 + t, :])
    ext_ref[0:SUBLANES, :] = u[t - SUBLANES:t, :]
    qkv = _silu(conv)
    q = qkv[:, 0:W]
    k = qkv[:, W:2 * W]
    v = qkv[:, 2 * W:3 * W]
    yield

    bd = bd_ref[...]
    bdb = _bf(bd)
    ab = zab_ref[...]
    a_exp = _dot2_rhs01(ab, eab_ref[:, 0:W])
    b_exp = _dot2_rhs01(ab, eab_ref[:, W:2 * W])
    qn = q * lax.rsqrt(_dot2_rhs01(q * q, bdb) + L2_EPS) * (GDN_DK ** -0.5)
    kn = k * lax.rsqrt(_dot2_rhs01(k * k, bdb) + L2_EPS)
    yield
    beta = _sigmoid(b_exp)
    g = par_ref[0:1, :] * _softplus(a_exp + par_ref[1:2, :])
    gc = _dot2_lhs01(tril_ref[...], g)
    gl = _dot2_lhs01(bdb, g)
    yield
    eg = jnp.exp(gc)
    vb = v * beta
    kbeta = kn * beta * eg
    qdec = qn * eg
    kdec = kn * jnp.exp(gl - gc)

    n_chunks = t // GDN_CHUNK
    hms = [_bf(jnp.broadcast_to(hm_ref[h:h + 1, :], (GDN_CHUNK, W))) for h in range(GDN_HEADS)]
    hms2 = [jnp.concatenate([m, m], axis=1) for m in hms]

    def chunk(a, c):
        return a[c * GDN_CHUNK:(c + 1) * GDN_CHUNK]

    def expand(y, masks):
        yb = _bf(y)
        return jnp.concatenate([yb * m for m in masks], axis=0)

    def blockprod(x, y, masks):
        xb = _bf(x)
        return jnp.concatenate([_dot(chunk(xb, c), expand(chunk(y, c), masks))
                                for c in range(n_chunks)], axis=0)

    gc_row = _dot2_lhs01(bdb, gc * gm_ref[5])
    yield
    rel = jnp.exp(jnp.where(gm_ref[0] > 0, gc - gc_row, -jnp.inf))

    knbeta = kn * beta
    kq = [_dot_nt(_bf(jnp.concatenate([chunk(knbeta, c), chunk(qn, c)], axis=0)),
                  expand(chunk(kn, c), hms)) for c in range(n_chunks)]
    yield
    m = jnp.concatenate([r[0:GDN_CHUNK] for r in kq], axis=0) * rel
    qk = jnp.concatenate([r[GDN_CHUNK:] for r in kq], axis=0) * rel

    d = m * gm_ref[1]
    d2 = blockprod(d, d, hms)
    yield
    d4 = blockprod(d2, d2, hms)
    dd2 = blockprod(d, d2, hms)
    yield
    x = d2 - d - dd2
    xd4 = blockprod(x, d4, hms)
    yield
    x = x + d4 + xd4
    for lvl in range(2, 5):
        lo = m * gm_ref[lvl]
        xl = blockprod(x, lo, hms)
        yield
        y = lo + xl
        yx = blockprod(y, x, hms)
        yield
        x = x - (y + yx)

    vk = jnp.concatenate([vb, kbeta], axis=1)
    wk = vk + blockprod(x, vk, hms2)
    yield
    w = wk[:, 0:W]
    kcum = wk[:, W:2 * W]
    ag = blockprod(qk, wk, hms2)
    yield
    a1 = ag[:, 0:W]
    qeff = qdec - ag[:, W:2 * W]

    for c in range(t // GDN_CHUNK):
        rows = slice(c * GDN_CHUNK, (c + 1) * GDN_CHUNK)
        st = state_ref[...]
        stb = _bf(st)
        vnew = w[rows] - _dot(_bf(kcum[rows]), stb)
        obuf_ref[rows, :] = _dot(_bf(qeff[rows]), stb) + a1[rows]
        yield
        upd = _dot_tn(_bf(kdec[rows]), _bf(vnew))
        last = eg[(c + 1) * GDN_CHUNK - 1:(c + 1) * GDN_CHUNK, :]
        state_ref[...] = st * last + bd * upd
        yield

    o = obuf_ref[...]
    ms = _dot2_rhs01(o * o, bdb) * (1.0 / GDN_DK)
    gate = _silu(zg_ref[:, 3 * W:4 * W])
    o_ref[...] = _bf(o * lax.rsqrt(ms + NORM_EPS) * par_ref[2:3, :] * gate)


_DONE = object()

def _mixers_kernel(zh_ref, f_ref, lv_ref, hpar_ref,
                   zr_ref, cos_ref, sin_ref, dmat_ref, qw_ref, kw_ref, rpar_ref,
                   zg_ref, zab_ref, convw_ref, gpar_ref, eab_ref, gm_ref, tril_ref, hm_ref, bd_ref,
                   ohg_ref, oret_ref, ogdn_ref,
                   hstate_ref, hex_ref, rstate_ref, gext_ref, gstate_ref, gobuf_ref):
    @pl.when(pl.program_id(1) == 0)
    def _():
        hstate_ref[...] = jnp.zeros_like(hstate_ref)
        rstate_ref[...] = jnp.zeros_like(rstate_ref)
        gstate_ref[...] = jnp.zeros_like(gstate_ref)
        gext_ref[:, 0:SUBLANES, :] = jnp.zeros((SEQ_PER_STEP, SUBLANES, 3 * GDN_WIDTH), F32)

    active = []
    for i in range(SEQ_PER_STEP):
        active.append((_gdn_stages(zg_ref.at[i], zab_ref.at[i], convw_ref, gpar_ref, eab_ref, gm_ref, bd_ref,
                                   tril_ref, hm_ref, ogdn_ref.at[i], gext_ref.at[i], gstate_ref.at[i],
                                   gobuf_ref.at[i]), 1))
    for i in range(SEQ_PER_STEP):
        active.append((_hgrn_stages(zh_ref.at[i], f_ref, lv_ref, hpar_ref, ohg_ref.at[i], hstate_ref.at[i],
                                    hex_ref.at[i]), 2))
    for i in range(SEQ_PER_STEP):
        active.append((_ret_stages(zr_ref.at[i], cos_ref, sin_ref, dmat_ref, qw_ref, kw_ref, rpar_ref, hm_ref,
                                   bd_ref, oret_ref.at[i], rstate_ref.at[i]), 1))
    while active:
        for entry in list(active):
            gen, per_round = entry
            for _ in range(per_round):
                if next(gen, _DONE) is _DONE:
                    active.remove(entry)
                    break


def _mixers(zh, zr, zg, zab, hg_tabs, ret_tabs, gdn_tabs, hm, bd, batch, steps):
    n = zh.shape[0]
    seq = n // batch
    g = SEQ_PER_STEP
    assert batch % g == 0

    def const(a):
        return pl.BlockSpec(a.shape, lambda b, j, nd=a.ndim: (0,) * nd)

    def rows(width):
        return pl.BlockSpec((g, T_BLK, width), lambda b, j: (b, j, 0))

    def by_seq(a):
        return a.reshape(batch, seq, a.shape[-1])

    pos_rows = pl.BlockSpec((T_BLK, RET_WIDTH), lambda b, j: (j, 0))
    cos_t, sin_t = ret_tabs[0], ret_tabs[1]
    in_specs = ([rows(ZH_W)] + [const(a) for a in hg_tabs]
                + [rows(ZR_W), pos_rows, pos_rows] + [const(a) for a in ret_tabs[2:]]
                + [rows(ZG_W), rows(ZAB_W)] + [const(a) for a in gdn_tabs] + [const(hm), const(bd)])
    outs = pl.pallas_call(
        _mixers_kernel,
        grid=(batch // g, steps),
        in_specs=in_specs,
        out_specs=[rows(HG_WIDTH), rows(RET_WIDTH), rows(GDN_WIDTH)],
        out_shape=[jax.ShapeDtypeStruct((batch, seq, HG_WIDTH), BF16),
                   jax.ShapeDtypeStruct((batch, seq, RET_WIDTH), BF16),
                   jax.ShapeDtypeStruct((batch, seq, GDN_WIDTH), BF16)],
        scratch_shapes=[pltpu.VMEM((g, HG_HEADS, HG_DK, HG_DK), F32),
                        pltpu.VMEM((g, T_BLK // HG_SUB, (2 + HG_LEVELS) * HG_SUB, HG_WIDTH), F32),
                        pltpu.VMEM((g, RET_WIDTH, RET_WIDTH), F32),
                        pltpu.VMEM((g, SUBLANES + T_BLK, 3 * GDN_WIDTH), F32),
                        pltpu.VMEM((g, GDN_WIDTH, GDN_WIDTH), F32),
                        pltpu.VMEM((g, T_BLK, GDN_WIDTH), F32)],
        compiler_params=pltpu.CompilerParams(dimension_semantics=("arbitrary", "arbitrary"),
                                             vmem_limit_bytes=VMEM_LIMIT),
        name="mixers",
    )(by_seq(zh), *hg_tabs, by_seq(zr), cos_t, sin_t, *ret_tabs[2:], by_seq(zg), by_seq(zab),
      *gdn_tabs, hm, bd)
    return [o.reshape(n, o.shape[-1]) for o in outs]


def _outproj_kernel(ohg_ref, oret_ref, ogdn_ref, x_ref, mod_ref, g_ref, w_ref, rw_ref, rb_ref,
                    su_ref, x1_ref, hn_ref, ri_ref, gates_ref, cnt_ref, run_ref):
    @pl.when(pl.program_id(0) == 0)
    def _():
        run_ref[...] = jnp.zeros_like(run_ref)

    rw_hi = rw_ref[0:LANES, :]
    rw_lo = rw_ref[LANES:2 * LANES, :]
    eid = lax.broadcasted_iota(jnp.int32, (LANES, ROUTE_SLAB), 0)
    routed = {}

    def slab_stages(s):
        rows = slice(s * ROUTE_SLAB, (s + 1) * ROUTE_SLAB)
        y = (_dot(ohg_ref[rows, :], w_ref[0:HG_WIDTH, :])
             + _dot(oret_ref[rows, :], w_ref[HG_WIDTH:HG_WIDTH + RET_WIDTH, :])
             + _dot(ogdn_ref[rows, :], w_ref[HG_WIDTH + RET_WIDTH:, :]))
        yield
        x1 = x_ref[rows, :] + mod_ref[0, 0:1, :] * y
        x1_ref[rows, :] = x1
        n = x1 * lax.rsqrt(jnp.mean(x1 * x1, axis=-1, keepdims=True) + NORM_EPS)
        hn = (n * g_ref[0:1, :]) * (1.0 + mod_ref[0, 1:2, :]) + mod_ref[0, 2:3, :]
        hn_ref[rows, :] = _pack_halves(hn)
        yield
        hn_hi, hn_lo = _split2(hn)
        logits = (_dot_nt(rw_hi, hn_hi) + _dot_nt(rw_lo, hn_hi) + _dot_nt(rw_hi, hn_lo)
                  + rb_ref[:, rows])
        yield
        work = jnp.where(eid < N_EXPERTS, logits, -jnp.inf)
        vals, idxs = [], []
        multihot = jnp.zeros((LANES, ROUTE_SLAB), F32)
        for kk in range(TOP_K):
            mx = jnp.max(work, axis=0, keepdims=True)
            ix = jnp.min(jnp.where(work == mx, eid, LANES), axis=0, keepdims=True)
            sel = eid == ix
            multihot = jnp.where(sel, 1.0, multihot)
            work = jnp.where(sel, -jnp.inf, work)
            vals.append(mx)
            idxs.append(ix)
            if kk % 2 == 1:
                yield
        routed[s] = (vals, idxs, multihot)

    n_slabs = TM // ROUTE_SLAB
    active = [slab_stages(s) for s in range(n_slabs)]
    while active:
        for gen in list(active):
            if next(gen, _DONE) is _DONE:
                active.remove(gen)

    multihot = jnp.concatenate([routed[s][2] for s in range(n_slabs)], axis=1)
    run = run_ref[...]
    before = _dot(_bf(multihot), su_ref[...]) + jnp.concatenate([run] * (TM // LANES), axis=1)
    run = run + jnp.sum(multihot, axis=1, keepdims=True)
    run_ref[...] = run
    cnt_ref[...] = run

    for s in range(n_slabs):
        vals, idxs, _ = routed[s]
        cols = slice(s * ROUTE_SLAB, (s + 1) * ROUTE_SLAB)
        ex = [jnp.exp(vv - vals[0]) for vv in vals]
        den = ex[0] + ex[1] + ex[2] + ex[3]
        ranks = [jnp.sum(jnp.where(eid == ix, before[:, cols], 0.0), axis=0, keepdims=True) for ix in idxs]
        ri_ref[:, cols] = jnp.concatenate(idxs + [r.astype(jnp.int32) for r in ranks], axis=0)
        gates_ref[:, cols] = jnp.concatenate([e / den for e in ex]
                                             + [jnp.zeros((TOP_K, ROUTE_SLAB), F32)], axis=0)


def _outproj(ohg, oret, ogdn, x2d, modv, gpar, w_bf, rw, rb, su, tiles_per_batch):
    n = x2d.shape[0]
    row = lambda i: (i, 0)
    col = lambda i: (0, i)
    const = lambda i: (0, 0)
    return pl.pallas_call(
        _outproj_kernel,
        grid=(n // TM,),
        in_specs=[pl.BlockSpec((TM, HG_WIDTH), row), pl.BlockSpec((TM, RET_WIDTH), row),
                  pl.BlockSpec((TM, GDN_WIDTH), row), pl.BlockSpec((TM, D_MODEL), row),
                  pl.BlockSpec((1, SUBLANES, D_MODEL), lambda i: (i // tiles_per_batch, 0, 0)),
                  pl.BlockSpec((SUBLANES, D_MODEL), const),
                  pl.BlockSpec((D_MODEL, D_MODEL), const),
                  pl.BlockSpec((2 * LANES, D_MODEL), const),
                  pl.BlockSpec((LANES, TM), const),
                  pl.BlockSpec((TM, TM), const)],
        out_specs=[pl.BlockSpec((TM, D_MODEL), row), pl.BlockSpec((TM, D_MODEL // 2), row),
                   pl.BlockSpec((2 * TOP_K, TM), col), pl.BlockSpec((2 * TOP_K, TM), col),
                   pl.BlockSpec((LANES, LANES), const)],
        out_shape=[jax.ShapeDtypeStruct((n, D_MODEL), F32),
                   jax.ShapeDtypeStruct((n, D_MODEL // 2), jnp.uint32),
                   jax.ShapeDtypeStruct((2 * TOP_K, n), jnp.int32),
                   jax.ShapeDtypeStruct((2 * TOP_K, n), F32),
                   jax.ShapeDtypeStruct((LANES, LANES), F32)],
        scratch_shapes=[pltpu.VMEM((LANES, LANES), F32)],
        compiler_params=pltpu.CompilerParams(dimension_semantics=("arbitrary",),
                                             vmem_limit_bytes=VMEM_LIMIT),
        name="outproj_router",
    )(ohg, oret, ogdn, x2d, modv, gpar, w_bf, rw, rb, su)


def _expert_kernel(be_ref, nu_ref, x_ref, w1_ref, b1_ref, w2_ref, b2_ref, y_ref, w1b_ref, w2b_ref):
    i = pl.program_id(0)
    prev = be_ref[jnp.maximum(i - 1, 0)]
    fresh = jnp.logical_or(i == 0, be_ref[i] != prev)

    @pl.when(fresh)
    def _():
        w1b_ref[...] = _bf(w1_ref[0])
        w2b_ref[...] = _bf(w2_ref[0])

    @pl.when(i < nu_ref[0])
    def _():
        half = D_MODEL // 2

        def slab_stages(s):
            rows = slice(s * EXPERT_SLAB, (s + 1) * EXPERT_SLAB)
            x_lo, x_hi = _unpack_halves(x_ref[rows, :])
            hid = (_dot(_bf(x_lo), w1b_ref[0:half, :]) + _dot(_bf(x_hi), w1b_ref[half:, :])
                   + b1_ref[0])
            yield
            x_glu = jnp.minimum(hid[:, 0:D_FF], SWIGLU_LIMIT)
            x_lin = jnp.clip(hid[:, D_FF:], -SWIGLU_LIMIT, SWIGLU_LIMIT)
            act = x_glu * _sigmoid(SWIGLU_ALPHA * x_glu) * (x_lin + 1.0)
            yield
            y_ref[rows, :] = _pack_halves(_dot(_bf(act), w2b_ref[...]) + b2_ref[0])

        active = [slab_stages(s) for s in range(R_BLK // EXPERT_SLAB)]
        while active:
            for gen in list(active):
                if next(gen, _DONE) is _DONE:
                    active.remove(gen)


def _experts(block_e, n_used, xs, w1, b1, w2, b2):
    p = xs.shape[0]
    n_blocks = p // R_BLK
    ne = w1.shape[0] * w1.shape[1]
    w1 = w1.reshape(ne, D_MODEL, 2 * D_FF)
    w2 = w2.reshape(ne, D_FF, D_MODEL)
    rowmap = lambda i, be, nu: (jnp.minimum(i, nu[0] - 1), 0)
    emap = lambda i, be, nu: (be[i], 0, 0)
    grid_spec = pltpu.PrefetchScalarGridSpec(
        num_scalar_prefetch=2,
        grid=(n_blocks,),
        in_specs=[pl.BlockSpec((R_BLK, D_MODEL // 2), rowmap),
                  pl.BlockSpec((1, D_MODEL, 2 * D_FF), emap),
                  pl.BlockSpec((1, 1, 2 * D_FF), emap),
                  pl.BlockSpec((1, D_FF, D_MODEL), emap),
                  pl.BlockSpec((1, 1, D_MODEL), emap)],
        out_specs=pl.BlockSpec((R_BLK, D_MODEL // 2), rowmap),
        scratch_shapes=[pltpu.VMEM((D_MODEL, 2 * D_FF), BF16), pltpu.VMEM((D_FF, D_MODEL), BF16)],
    )
    return pl.pallas_call(
        _expert_kernel,
        grid_spec=grid_spec,
        out_shape=jax.ShapeDtypeStruct((p, D_MODEL // 2), jnp.uint32),
        compiler_params=pltpu.CompilerParams(dimension_semantics=("arbitrary",),
                                             vmem_limit_bytes=VMEM_LIMIT),
        name="experts",
    )(block_e, n_used, xs, w1, b1.reshape(ne, 1, 2 * D_FF), w2, b2.reshape(ne, 1, D_MODEL))


def _sc_mesh():
    return plsc.VectorSubcoreMesh(core_axis_name="c", subcore_axis_name="s")


def _sc_scatter_rows(x, pos, p_rows):
    mesh = _sc_mesh()
    n, w = x.shape
    per_worker = n // (mesh.num_cores * mesh.num_subcores)
    assert per_worker % SC_ROWS == 0 and pos.shape == (TOP_K, n)

    @functools.partial(
        pl.kernel, out_type=jax.ShapeDtypeStruct((p_rows, w), x.dtype), mesh=mesh,
        scratch_types=[pltpu.VMEM((SC_ROWS,), jnp.int32)] * TOP_K
        + [pltpu.VMEM((SC_ROWS, w), x.dtype), pltpu.SemaphoreType.DMA],
        name="dispatch_rows")
    def scatter(x_hbm, p_hbm, o_hbm, i0, i1, i2, i3, rows_v, sem):
        idx = (i0, i1, i2, i3)
        worker = lax.axis_index("s") * mesh.num_cores + lax.axis_index("c")

        @pl.loop(0, per_worker // SC_ROWS)
        def _(g):
            base = pl.multiple_of(worker * per_worker + g * SC_ROWS, SC_ROWS)
            pltpu.sync_copy(x_hbm.at[pl.ds(base, SC_ROWS)], rows_v)
            for kk in range(TOP_K):
                pltpu.sync_copy(p_hbm.at[kk, pl.ds(base, SC_ROWS)], idx[kk])
            copies = [pltpu.async_copy(rows_v, o_hbm.at[idx[kk]], sem) for kk in range(TOP_K)]
            for cp in copies:
                cp.wait()

    return scatter(x, pos)


def _sc_gather_rows(table, idx):
    mesh = _sc_mesh()
    n_idx = idx.shape[0]
    w = table.shape[1]
    per_worker = n_idx // (mesh.num_cores * mesh.num_subcores)
    assert per_worker % SC_ROWS == 0

    @functools.partial(
        pl.kernel, out_type=jax.ShapeDtypeStruct((n_idx, w), table.dtype), mesh=mesh,
        scratch_types=[pltpu.VMEM((SC_ROWS,), jnp.int32), pltpu.VMEM((SC_ROWS, w), table.dtype),
                       pltpu.SemaphoreType.DMA],
        name="combine_rows")
    def gather(t_hbm, i_hbm, o_hbm, idx_v, rows_v, sem):
        worker = lax.axis_index("s") * mesh.num_cores + lax.axis_index("c")

        @pl.loop(0, per_worker // SC_ROWS)
        def _(g):
            base = pl.multiple_of(worker * per_worker + g * SC_ROWS, SC_ROWS)
            pltpu.sync_copy(i_hbm.at[pl.ds(base, SC_ROWS)], idx_v)
            pltpu.async_copy(t_hbm.at[idx_v], rows_v, sem).wait()
            pltpu.sync_copy(rows_v, o_hbm.at[pl.ds(base, SC_ROWS)])

    return gather(table, idx)


def _combine_kernel(x_ref, y0_ref, y1_ref, y2_ref, y3_ref, gates_ref, mod_ref, g_ref, o_ref, *, final):
    gt = gates_ref[...]
    half = D_MODEL // 2
    acc_lo, acc_hi = None, None
    for kk, y_ref in enumerate((y0_ref, y1_ref, y2_ref, y3_ref)):
        lo, hi = _unpack_halves(y_ref[0])
        g = gt[:, kk:kk + 1]
        acc_lo = g * lo if acc_lo is None else acc_lo + g * lo
        acc_hi = g * hi if acc_hi is None else acc_hi + g * hi
    x_lo = x_ref[:, 0:half] + mod_ref[0, 3:4, 0:half] * acc_lo
    x_hi = x_ref[:, half:] + mod_ref[0, 3:4, half:] * acc_hi
    if final:
        ssq = (jnp.sum(x_lo * x_lo, axis=-1, keepdims=True)
               + jnp.sum(x_hi * x_hi, axis=-1, keepdims=True))
        r = lax.rsqrt(ssq * (1.0 / D_MODEL) + NORM_EPS)
        x_lo = x_lo * r * g_ref[1:2, 0:half]
        x_hi = x_hi * r * g_ref[1:2, half:]
    o_ref[:, 0:half] = x_lo
    o_ref[:, half:] = x_hi


def _combine(x, yg, gates, modv, gpar, tiles_per_batch, final, piece):
    n = x.shape[0]
    tiles = n // TM // COMBINE_PIECES
    first = piece * tiles
    row = lambda i: (first + i, 0)
    ysp = [pl.BlockSpec((1, TM, D_MODEL // 2), (lambda i, kk=kk: (kk, i, 0))) for kk in range(TOP_K)]
    return pl.pallas_call(
        functools.partial(_combine_kernel, final=final),
        grid=(tiles,),
        in_specs=[pl.BlockSpec((TM, D_MODEL), row)] + ysp + [
            pl.BlockSpec((TM, 2 * TOP_K), row),
            pl.BlockSpec((1, SUBLANES, D_MODEL), lambda i: ((first + i) // tiles_per_batch, 0, 0)),
            pl.BlockSpec((SUBLANES, D_MODEL), lambda i: (0, 0))],
        out_specs=pl.BlockSpec((TM, D_MODEL), row),
        out_shape=jax.ShapeDtypeStruct((n, D_MODEL), F32),
        input_output_aliases={0: 0},
        compiler_params=pltpu.CompilerParams(dimension_semantics=("arbitrary",),
                                             vmem_limit_bytes=VMEM_LIMIT),
        name="combine",
    )(x, yg, yg, yg, yg, gates, modv, gpar)


def _widen_w_in(w):
    pad = jnp.zeros((w.shape[0], Z_W - w.shape[1]), w.dtype)
    return _bf(jnp.concatenate([w, pad], axis=1))


def _pad_rows(a, rows=SUBLANES):
    return jnp.concatenate([a, jnp.zeros((rows - a.shape[0],) + a.shape[1:], a.dtype)], axis=0)


def kernel(x, c, ada_w, ada_b, norm1_g, norm2_g, w_in, w_out, hg_lb_logits, hg_norm_g, ret_norm_g,
           gdn_conv_w, gdn_A_log, gdn_dt_bias, gdn_norm_g, router_w, router_b, exp_w1, exp_b1,
           exp_w2, exp_b2, final_norm_g):
    batch, seq, d = x.shape
    depth = ada_w.shape[0]
    n = batch * seq
    steps = seq // T_BLK
    tiles_per_batch = seq // TM
    nk = n * TOP_K
    n_blocks = nk // R_BLK + N_EXPERTS
    p_rows = n_blocks * R_BLK

    lv_np = _level_matrix(T_BLK)
    lv = jnp.asarray(_level_matrix(HG_SUB))
    f_np = _hgrn_exponent_matrix(HG_SUB)
    f_mat = jnp.asarray(np.concatenate([f_np, f_np], axis=1), BF16)
    gm = jnp.asarray(_gdn_masks(T_BLK))
    bd = jnp.asarray(_block_diag_mask(T_BLK, GDN_CHUNK))
    tril = jnp.asarray(_block_diag_mask(T_BLK, GDN_CHUNK) * (lv_np >= 0), BF16)
    hm = jnp.asarray(_head_masks(RET_WIDTH, RET_HEADS))
    su = jnp.asarray(np.triu(np.ones((TM, TM), np.float32), 1), BF16)
    eab_np = np.zeros((LANES, 2 * GDN_WIDTH), np.float32)
    for h in range(GDN_HEADS):
        eab_np[h, h * GDN_DK:(h + 1) * GDN_DK] = 1.0
        eab_np[GDN_HEADS + h, GDN_WIDTH + h * GDN_DK:GDN_WIDTH + (h + 1) * GDN_DK] = 1.0
    eab = jnp.asarray(eab_np, BF16)

    half = RET_DK // 2
    inv = ROPE_BASE ** (-jnp.linspace(0.0, 1.0, half, dtype=F32))
    ang = jnp.arange(seq, dtype=F32)[:, None] * inv[None, :]
    cos_t = jnp.tile(jnp.cos(ang), (1, 2 * RET_HEADS))
    first_half = jnp.asarray((np.arange(RET_WIDTH) % RET_DK < half).astype(np.float32))[None, :]
    sin_t = jnp.tile(jnp.sin(ang), (1, 2 * RET_HEADS)) * (1.0 - 2.0 * first_half)
    log_g = jnp.log1p(-jnp.exp2(-5.0 - jnp.arange(RET_HEADS, dtype=F32)))
    jj = jnp.arange(T_BLK, dtype=F32)
    diff = jj[:, None] - jj[None, :]
    dmat = jnp.where(diff[None] >= 0, jnp.exp(diff[None] * log_g[:, None, None]), 0.0)
    lg_lane = jnp.repeat(log_g, RET_DK)[None, :]
    qw = jnp.exp(lg_lane * (jj[:, None] + 1.0))
    kw = jnp.exp(lg_lane * (T_BLK - 1.0 - jj[:, None]))
    cdec = jnp.exp(T_BLK * lg_lane)

    lb = jnp.cumsum(jax.nn.softmax(hg_lb_logits.astype(F32), axis=0), axis=0)
    lb = jnp.maximum(lb - lb[0], 0.0)
    c_pad = _pad_rows(c.astype(F32))
    mod = _adaln(c_pad, ada_w, ada_b)[:, :batch, :]

    x2d = x.reshape(n, d)
    out = None
    for l in range(depth):
        sh1, sc1, gt1, sh2, sc2, gt2 = [mod[l][:, i * d:(i + 1) * d] for i in range(6)]
        zeros = jnp.zeros_like(sh1)
        modv_a = jnp.stack([sc1, sh1, zeros, zeros, zeros, zeros, zeros, zeros], axis=1)
        modv_c = jnp.stack([gt1, sc2, sh2, gt2, zeros, zeros, zeros, zeros], axis=1)
        gpar_a = _pad_rows(norm1_g[l][None, :])
        gpar_c = _pad_rows(jnp.stack([norm2_g[l], final_norm_g], axis=0))

        zh, zr, zg, zab = _inproj(x2d, modv_a, gpar_a, _widen_w_in(w_in[l]), seq)

        hg_par = _pad_rows(jnp.stack([jnp.log(lb[l]), jnp.log1p(-lb[l]), 1.0 - lb[l],
                                      hg_norm_g[l].reshape(-1)], axis=0))
        ret_par = _pad_rows(jnp.concatenate([cdec, ret_norm_g[l].reshape(1, -1), first_half], axis=0))
        gdn_par = _pad_rows(jnp.stack([jnp.repeat(-jnp.exp(gdn_A_log[l].astype(F32)), GDN_DK),
                                       jnp.repeat(gdn_dt_bias[l].astype(F32), GDN_DK),
                                       gdn_norm_g[l].reshape(-1)], axis=0))
        o_hg, o_ret, o_gdn = _mixers(
            zh, zr, zg, zab,
            (f_mat, lv, hg_par),
            (cos_t, sin_t, dmat, qw, kw, ret_par),
            (_pad_rows(gdn_conv_w[l].astype(F32)), gdn_par, eab, gm, tril),
            hm, bd, batch, steps)

        rw_f = jnp.concatenate([router_w[l].T, jnp.zeros((LANES - N_EXPERTS, d), F32)], axis=0)
        rw = jnp.concatenate(_split2(rw_f), axis=0)
        rb = jnp.broadcast_to(jnp.concatenate([router_b[l], jnp.zeros((LANES - N_EXPERTS,), F32)])[:, None],
                              (LANES, TM))
        x1, hn2, ri, gates_t, cnt = _outproj(o_hg, o_ret, o_gdn, x2d, modv_c, gpar_c, _bf(w_out[l]),
                                             rw, rb, su, tiles_per_batch)
        gates = gates_t.T

        counts = cnt[:N_EXPERTS, 0].astype(jnp.int32)
        padded = (counts + R_BLK - 1) // R_BLK * R_BLK
        pend = jnp.cumsum(padded)
        pstart = pend - padded
        n_used = (pend[-1] // R_BLK).astype(jnp.int32)
        blk_start = jnp.arange(n_blocks, dtype=jnp.int32) * R_BLK
        blk_start = jnp.minimum(blk_start, pend[-1] - R_BLK)
        block_e = jnp.sum(blk_start[:, None] >= pend[None, :], axis=1).astype(jnp.int32)
        eid = jnp.arange(N_EXPERTS, dtype=jnp.int32)
        pos_t = ri[TOP_K:] + jnp.sum(jnp.where(ri[:TOP_K, :, None] == eid, pstart, 0), axis=-1)
        xs = _sc_scatter_rows(hn2, pos_t, p_rows)
        y = _experts(block_e + l * N_EXPERTS, n_used.reshape(1), xs, exp_w1, exp_b1, exp_w2, exp_b2)
        n_piece = n // COMBINE_PIECES
        x2d = x1
        for p in range(COMBINE_PIECES):
            idx = pos_t[:, p * n_piece:(p + 1) * n_piece].reshape(-1)
            yg = _sc_gather_rows(y, idx).reshape(TOP_K, n_piece, d // 2)
            x2d = _combine(x2d, yg, gates, modv_c, gpar_c, tiles_per_batch, l == depth - 1, p)
    return x2d.reshape(batch, seq, d)
```

```python
import functools
import math

import numpy as np
import jax
import jax.numpy as jnp
from jax import lax
from jax.experimental import pallas as pl
from jax.experimental.pallas import tpu as pltpu
from jax.experimental.pallas import tpu_sc as plsc

F32 = jnp.float32
BF16 = jnp.bfloat16

D_MODEL = 1024
HG_HEADS, HG_DK = 4, 128
HG_WIDTH = HG_HEADS * HG_DK
RET_HEADS, RET_DK = 4, 64
RET_WIDTH = RET_HEADS * RET_DK
GDN_HEADS, GDN_DK = 4, 64
GDN_WIDTH = GDN_HEADS * GDN_DK
CONV_K = 4
ROPE_BASE = 10000.0
N_EXPERTS = 32
TOP_K = 4
D_FF = D_MODEL
SWIGLU_ALPHA = 1.702
SWIGLU_LIMIT = 7.0
NORM_EPS = 1e-6
L2_EPS = 1e-6
GDN_CHUNK = 64

LANES = 128
SUBLANES = 8
VMEM_LIMIT = 56 * 1024 * 1024

T_BLK = 256
SEQ_PER_STEP = 2
TM = 512
TM_IN = 512
ROUTE_SLAB = 128
COMBINE_PIECES = 4
R_BLK = 528
EXPERT_SLAB = 176
SC_ROWS = 128
HG_SUB = 128
HG_LEVELS = 7

ZH_W = 4 * HG_WIDTH
ZR_W = 4 * RET_WIDTH
ZG_W = 4 * GDN_WIDTH
ZAB_W = LANES
Z_W = ZH_W + ZR_W + ZG_W + ZAB_W


def _dot(a, b):
    return jnp.dot(a, b, preferred_element_type=F32)


def _dot_nt(a, b):
    return lax.dot_general(a, b, (((1,), (1,)), ((), ())), preferred_element_type=F32)


def _dot_tn(a, b):
    return lax.dot_general(a, b, (((0,), (0,)), ((), ())), preferred_element_type=F32)


def _split2(x):
    hi = x.astype(BF16)
    return hi, (x - hi.astype(F32)).astype(BF16)


def _dot2_lhs01(c, x):
    hi, lo = _split2(x)
    return _dot(c, hi) + _dot(c, lo)


def _dot2_rhs01(x, c):
    hi, lo = _split2(x)
    return _dot(hi, c) + _dot(lo, c)


def _sigmoid(x):
    return 1.0 / (1.0 + jnp.exp(-x))


def _silu(x):
    return x * _sigmoid(x)


def _softplus(x):
    return jnp.maximum(x, 0.0) + jnp.log1p(jnp.exp(-jnp.abs(x)))


def _bf(x):
    return x.astype(BF16)


def _pack_halves(x):
    w = x.shape[1] // 2
    bits = lax.bitcast_convert_type(_bf(x).astype(F32), jnp.uint32)
    return (bits[:, :w] >> 16) | bits[:, w:]


def _unpack_halves(p):
    lo = lax.bitcast_convert_type(p << 16, F32)
    hi = lax.bitcast_convert_type(p & jnp.uint32(0xFFFF0000), F32)
    return lo, hi


def _level_matrix(t):
    i = np.arange(t)[:, None]
    j = np.arange(t)[None, :]
    x = i ^ j
    lv = np.floor(np.log2(np.maximum(x, 1))).astype(np.int32)
    lv = np.where(i == j, int(math.log2(t)), lv)
    lv = np.where(i < j, -1, lv)
    return lv.astype(np.int32)


def _hgrn_exponent_matrix(t):
    n_lev = int(math.log2(t))
    f = np.zeros((2 + n_lev, t, t), np.float32)
    u = np.arange(t)[None, :]
    r = np.arange(t)[:, None]
    f[0] = (u <= r)
    f[1] = (u > r)
    for l in range(n_lev):
        h = 1 << l
        base = (r // (2 * h)) * (2 * h)
        mid = base + h
        upper = (r - base) >= h
        f[2 + l] = np.where(upper, (u >= mid) & (u <= r), (u > r) & (u < mid))
    return f.reshape((2 + n_lev) * t, t)


def _gdn_masks(t):
    lv = _level_matrix(GDN_CHUNK)
    top = int(math.log2(GDN_CHUNK))
    incl = lv >= 0
    d8 = (lv >= 0) & (lv <= 2)
    merges = [(lv == l) for l in range(3, top)]
    eye = lv == top
    pats = np.stack([incl, d8] + merges + [eye]).astype(np.float32)
    return np.tile(pats, (1, t // GDN_CHUNK, GDN_HEADS))


def _block_diag_mask(t, blk):
    i = np.arange(t)
    return (i[:, None] // blk == i[None, :] // blk).astype(np.float32)


def _head_masks(width, heads):
    lane = np.arange(width)[None, :]
    m = np.zeros((SUBLANES, width), np.float32)
    for h in range(heads):
        m[h] = (lane // (width // heads) == h)[0]
    return m


def _adaln_kernel(c_ref, w_ref, b_ref, o_ref):
    cond = _silu(c_ref[...])
    o_ref[0] = jnp.dot(cond, w_ref[0], preferred_element_type=F32,
                       precision=lax.Precision.HIGHEST) + b_ref[0]


def _adaln(c_pad, ada_w, ada_b):
    depth, d, n6 = ada_w.shape
    tn = n6 // 4
    return pl.pallas_call(
        _adaln_kernel,
        grid=(depth, n6 // tn),
        in_specs=[pl.BlockSpec((SUBLANES, d), lambda l, j: (0, 0)),
                  pl.BlockSpec((1, d, tn), lambda l, j: (l, 0, j)),
                  pl.BlockSpec((1, 1, tn), lambda l, j: (l, 0, j))],
        out_specs=pl.BlockSpec((1, SUBLANES, tn), lambda l, j: (l, 0, j)),
        out_shape=jax.ShapeDtypeStruct((depth, SUBLANES, n6), F32),
        compiler_params=pltpu.CompilerParams(vmem_limit_bytes=VMEM_LIMIT),
        name="adaln",
    )(c_pad, ada_w, ada_b.reshape(depth, 1, n6))


def _inproj_kernel(x_ref, mod_ref, g_ref, w_ref, zh_ref, zr_ref, zg_ref, zab_ref):
    x = x_ref[...]
    y = x * lax.rsqrt(jnp.mean(x * x, axis=-1, keepdims=True) + NORM_EPS)
    hn = (y * g_ref[0:1, :]) * (1.0 + mod_ref[0, 0:1, :]) + mod_ref[0, 1:2, :]
    hb = _bf(hn)
    zh_ref[...] = _dot(hb, w_ref[:, 0:ZH_W])
    zr_ref[...] = _dot(hb, w_ref[:, ZH_W:ZH_W + ZR_W])
    zg_ref[...] = _dot(hb, w_ref[:, ZH_W + ZR_W:ZH_W + ZR_W + ZG_W])
    zab_ref[...] = _dot(hb, w_ref[:, ZH_W + ZR_W + ZG_W:Z_W])


def _inproj(x2d, modv, gpar, w_bf, seq):
    n = x2d.shape[0]
    row = lambda i: (i, 0)
    tiles_per_batch = seq // TM_IN
    return pl.pallas_call(
        _inproj_kernel,
        grid=(n // TM_IN,),
        in_specs=[pl.BlockSpec((TM_IN, D_MODEL), row),
                  pl.BlockSpec((1, SUBLANES, D_MODEL), lambda i: (i // tiles_per_batch, 0, 0)),
                  pl.BlockSpec((SUBLANES, D_MODEL), lambda i: (0, 0)),
                  pl.BlockSpec((D_MODEL, Z_W), lambda i: (0, 0))],
        out_specs=[pl.BlockSpec((TM_IN, ZH_W), row), pl.BlockSpec((TM_IN, ZR_W), row),
                   pl.BlockSpec((TM_IN, ZG_W), row), pl.BlockSpec((TM_IN, ZAB_W), row)],
        out_shape=[jax.ShapeDtypeStruct((n, ZH_W), F32), jax.ShapeDtypeStruct((n, ZR_W), F32),
                   jax.ShapeDtypeStruct((n, ZG_W), F32), jax.ShapeDtypeStruct((n, ZAB_W), F32)],
        compiler_params=pltpu.CompilerParams(dimension_semantics=("arbitrary",),
                                             vmem_limit_bytes=VMEM_LIMIT),
        name="inproj",
    )(x2d, modv, gpar, w_bf)


def _hgrn_stages(zh_ref, f_ref, lv_ref, par_ref, o_ref, state_ref, ex_ref):
    W = HG_WIDTH
    hq = zh_ref[:, 0:W]
    hf = zh_ref[:, W:2 * W]
    loglb = par_ref[0:1, :]
    log1mlb = par_ref[1:2, :]
    onemlb = par_ref[2:3, :]

    q = _silu(hq)
    e = jnp.exp(-jnp.abs(hf))
    inv = 1.0 / (1.0 + e)
    k = onemlb * (jnp.where(hf >= 0, e, 1.0) * inv)
    logsig = jnp.minimum(hf, 0.0) - jnp.log1p(e)
    c = log1mlb + logsig
    lf = jnp.maximum(loglb, c) + jnp.log1p(jnp.exp(-jnp.abs(loglb - c)))
    yield

    lv = lv_ref[...]
    t = HG_SUB
    for sb in range(T_BLK // HG_SUB):
        rows = slice(sb * t, (sb + 1) * t)
        ex_ref[sb] = jnp.exp(_dot(f_ref[...], jnp.concatenate(_split2(lf[rows]), axis=0)))
        yield
        for h in range(HG_HEADS):
            cs = slice(h * HG_DK, (h + 1) * HG_DK)
            qh = q[rows, cs]
            kh = k[rows, cs]
            vh = _bf(zh_ref[rows, 2 * W + h * HG_DK:2 * W + (h + 1) * HG_DK])
            qb = _bf(qh)
            kb = _bf(kh)
            s = jnp.where(lv == HG_LEVELS, _dot_nt(qb, kb), 0.0)
            for l in range(HG_LEVELS):
                el = _bf(ex_ref[sb, (2 + l) * t:(3 + l) * t, cs])
                s = jnp.where(lv == l, _dot_nt(qb * el, kb * el), s)
                if l % 2 == 1:
                    yield
            eb = ex_ref[sb, 0:t, cs]
            ebl = ex_ref[sb, t:2 * t, cs]
            st = state_ref[h]
            o = _dot(_bf(s), vh) + _dot_nt(_bf(qh * eb), _bf(st))
            upd = _dot_tn(vh, _bf(kh * ebl))
            state_ref[h] = st * ex_ref[sb, t - 1:t, cs] + upd
            ms = jnp.mean(o * o, axis=-1, keepdims=True)
            gate = _silu(zh_ref[rows, 3 * W + h * HG_DK:3 * W + (h + 1) * HG_DK])
            o_ref[rows, cs] = _bf(o * lax.rsqrt(ms + NORM_EPS) * par_ref[3:4, cs] * gate)
            yield


def _ret_stages(zr_ref, cos_ref, sin_ref, dmat_ref, qw_ref, kw_ref, par_ref, hm_ref, bd_ref,
                o_ref, state_ref):
    W = RET_WIDTH
    cos = cos_ref[...]
    sin = sin_ref[...]
    first_half = par_ref[2:3, :] > 0

    def rotary(t):
        swapped = jnp.where(first_half, pltpu.roll(t, W - RET_DK // 2, 1), pltpu.roll(t, RET_DK // 2, 1))
        return t * cos + swapped * sin

    q = rotary(zr_ref[:, 0:W])
    k = rotary(zr_ref[:, W:2 * W]) * (RET_DK ** -0.5)
    v = zr_ref[:, 2 * W:3 * W]
    kb = _bf(k)
    bd = bd_ref[...]
    yield

    s_parts = []
    v_parts = []
    for h in range(RET_HEADS):
        hm = hm_ref[h:h + 1, :]
        s_parts.append(_bf(_dot_nt(_bf(q * hm), kb) * dmat_ref[h]))
        v_parts.append(_bf(v * hm))
        yield
    st = state_ref[...]
    o = (_dot(jnp.concatenate(s_parts, axis=1), jnp.concatenate(v_parts, axis=0))
         + _dot(_bf(q * qw_ref[...]), _bf(st)))
    yield
    kv = _dot_tn(_bf(k * kw_ref[...]), _bf(v))
    state_ref[...] = st * par_ref[0:1, :] + bd * kv
    yield

    ms = _dot2_rhs01(o * o, _bf(bd)) * (1.0 / RET_DK)
    gate = _silu(zr_ref[:, 3 * W:4 * W])
    o_ref[...] = _bf(o * lax.rsqrt(ms + NORM_EPS) * par_ref[1:2, :] * gate)


def _gdn_stages(zg_ref, zab_ref, convw_ref, par_ref, eab_ref, gm_ref, bd_ref, tril_ref, hm_ref,
                o_ref, ext_ref, state_ref, obuf_ref):
    W = GDN_WIDTH
    t = T_BLK
    u = zg_ref[:, 0:3 * W]
    ext_ref[SUBLANES:SUBLANES + t, :] = u
    conv = (convw_ref[3:4, :] * u
            + convw_ref[2:3, :] * ext_ref[SUBLANES - 1:SUBLANES - 1 + t, :]
            + convw_ref[1:2, :] * ext_ref[SUBLANES - 2:SUBLANES - 2 + t, :]
            + convw_ref[0:1, :] * ext_ref[SUBLANES - 3:SUBLANES - 3 + t, :])
    ext_ref[0:SUBLANES, :] = u[t - SUBLANES:t, :]
    qkv = _silu(conv)
    q = qkv[:, 0:W]
    k = qkv[:, W:2 * W]
    v = qkv[:, 2 * W:3 * W]
    yield

    bd = bd_ref[...]
    bdb = _bf(bd)
    ab = zab_ref[...]
    a_exp = _dot2_rhs01(ab, eab_ref[:, 0:W])
    b_exp = _dot2_rhs01(ab, eab_ref[:, W:2 * W])
    qn = q * lax.rsqrt(_dot2_rhs01(q * q, bdb) + L2_EPS) * (GDN_DK ** -0.5)
    kn = k * lax.rsqrt(_dot2_rhs01(k * k, bdb) + L2_EPS)
    yield
    beta = _sigmoid(b_exp)
    g = par_ref[0:1, :] * _softplus(a_exp + par_ref[1:2, :])
    gc = _dot2_lhs01(tril_ref[...], g)
    gl = _dot2_lhs01(bdb, g)
    yield
    eg = jnp.exp(gc)
    vb = v * beta
    kbeta = kn * beta * eg
    qdec = qn * eg
    kdec = kn * jnp.exp(gl - gc)

    n_chunks = t // GDN_CHUNK
    hms = [_bf(jnp.broadcast_to(hm_ref[h:h + 1, :], (GDN_CHUNK, W))) for h in range(GDN_HEADS)]
    hms2 = [jnp.concatenate([m, m], axis=1) for m in hms]

    def chunk(a, c):
        return a[c * GDN_CHUNK:(c + 1) * GDN_CHUNK]

    def expand(y, masks):
        yb = _bf(y)
        return jnp.concatenate([yb * m for m in masks], axis=0)

    def blockprod(x, y, masks):
        xb = _bf(x)
        return jnp.concatenate([_dot(chunk(xb, c), expand(chunk(y, c), masks))
                                for c in range(n_chunks)], axis=0)

    gc_row = _dot2_lhs01(bdb, gc * gm_ref[5])
    yield
    rel = jnp.exp(jnp.where(gm_ref[0] > 0, gc - gc_row, -jnp.inf))

    knbeta = kn * beta
    kq = [_dot_nt(_bf(jnp.concatenate([chunk(knbeta, c), chunk(qn, c)], axis=0)),
                  expand(chunk(kn, c), hms)) for c in range(n_chunks)]
    yield
    m = jnp.concatenate([r[0:GDN_CHUNK] for r in kq], axis=0) * rel
    qk = jnp.concatenate([r[GDN_CHUNK:] for r in kq], axis=0) * rel

    d = m * gm_ref[1]
    d2 = blockprod(d, d, hms)
    yield
    d4 = blockprod(d2, d2, hms)
    dd2 = blockprod(d, d2, hms)
    yield
    x = d2 - d - dd2
    xd4 = blockprod(x, d4, hms)
    yield
    x = x + d4 + xd4
    for lvl in range(2, 5):
        lo = m * gm_ref[lvl]
        xl = blockprod(x, lo, hms)
        yield
        y = lo + xl
        yx = blockprod(y, x, hms)
        yield
        x = x - (y + yx)

    vk = jnp.concatenate([vb, kbeta], axis=1)
    wk = vk + blockprod(x, vk, hms2)
    yield
    w = wk[:, 0:W]
    kcum = wk[:, W:2 * W]
    ag = blockprod(qk, wk, hms2)
    yield
    a1 = ag[:, 0:W]
    qeff = qdec - ag[:, W:2 * W]

    for c in range(t // GDN_CHUNK):
        rows = slice(c * GDN_CHUNK, (c + 1) * GDN_CHUNK)
        st = state_ref[...]
        stb = _bf(st)
        vnew = w[rows] - _dot(_bf(kcum[rows]), stb)
        obuf_ref[rows, :] = _dot(_bf(qeff[rows]), stb) + a1[rows]
        yield
        upd = _dot_tn(_bf(kdec[rows]), _bf(vnew))
        last = eg[(c + 1) * GDN_CHUNK - 1:(c + 1) * GDN_CHUNK, :]
        state_ref[...] = st * last + bd * upd
        yield

    o = obuf_ref[...]
    ms = _dot2_rhs01(o * o, bdb) * (1.0 / GDN_DK)
    gate = _silu(zg_ref[:, 3 * W:4 * W])
    o_ref[...] = _bf(o * lax.rsqrt(ms + NORM_EPS) * par_ref[2:3, :] * gate)


_DONE = object()

def _mixers_kernel(zh_ref, f_ref, lv_ref, hpar_ref,
                   zr_ref, cos_ref, sin_ref, dmat_ref, qw_ref, kw_ref, rpar_ref,
                   zg_ref, zab_ref, convw_ref, gpar_ref, eab_ref, gm_ref, tril_ref, hm_ref, bd_ref,
                   ohg_ref, oret_ref, ogdn_ref,
                   hstate_ref, hex_ref, rstate_ref, gext_ref, gstate_ref, gobuf_ref):
    @pl.when(pl.program_id(1) == 0)
    def _():
        hstate_ref[...] = jnp.zeros_like(hstate_ref)
        rstate_ref[...] = jnp.zeros_like(rstate_ref)
        gstate_ref[...] = jnp.zeros_like(gstate_ref)
        gext_ref[:, 0:SUBLANES, :] = jnp.zeros((SEQ_PER_STEP, SUBLANES, 3 * GDN_WIDTH), F32)

    active = []
    for i in range(SEQ_PER_STEP):
        active.append((_gdn_stages(zg_ref.at[i], zab_ref.at[i], convw_ref, gpar_ref, eab_ref, gm_ref, bd_ref,
                                   tril_ref, hm_ref, ogdn_ref.at[i], gext_ref.at[i], gstate_ref.at[i],
                                   gobuf_ref.at[i]), 1))
    for i in range(SEQ_PER_STEP):
        active.append((_hgrn_stages(zh_ref.at[i], f_ref, lv_ref, hpar_ref, ohg_ref.at[i], hstate_ref.at[i],
                                    hex_ref.at[i]), 2))
    for i in range(SEQ_PER_STEP):
        active.append((_ret_stages(zr_ref.at[i], cos_ref, sin_ref, dmat_ref, qw_ref, kw_ref, rpar_ref, hm_ref,
                                   bd_ref, oret_ref.at[i], rstate_ref.at[i]), 1))
    while active:
        for entry in list(active):
            gen, per_round = entry
            for _ in range(per_round):
                if next(gen, _DONE) is _DONE:
                    active.remove(entry)
                    break


def _mixers(zh, zr, zg, zab, hg_tabs, ret_tabs, gdn_tabs, hm, bd, batch, steps):
    n = zh.shape[0]
    seq = n // batch
    g = SEQ_PER_STEP
    assert batch % g == 0

    def const(a):
        return pl.BlockSpec(a.shape, lambda b, j, nd=a.ndim: (0,) * nd)

    def rows(width):
        return pl.BlockSpec((g, T_BLK, width), lambda b, j: (b, j, 0))

    def by_seq(a):
        return a.reshape(batch, seq, a.shape[-1])

    pos_rows = pl.BlockSpec((T_BLK, RET_WIDTH), lambda b, j: (j, 0))
    cos_t, sin_t = ret_tabs[0], ret_tabs[1]
    in_specs = ([rows(ZH_W)] + [const(a) for a in hg_tabs]
                + [rows(ZR_W), pos_rows, pos_rows] + [const(a) for a in ret_tabs[2:]]
                + [rows(ZG_W), rows(ZAB_W)] + [const(a) for a in gdn_tabs] + [const(hm), const(bd)])
    outs = pl.pallas_call(
        _mixers_kernel,
        grid=(batch // g, steps),
        in_specs=in_specs,
        out_specs=[rows(HG_WIDTH), rows(RET_WIDTH), rows(GDN_WIDTH)],
        out_shape=[jax.ShapeDtypeStruct((batch, seq, HG_WIDTH), BF16),
                   jax.ShapeDtypeStruct((batch, seq, RET_WIDTH), BF16),
                   jax.ShapeDtypeStruct((batch, seq, GDN_WIDTH), BF16)],
        scratch_shapes=[pltpu.VMEM((g, HG_HEADS, HG_DK, HG_DK), F32),
                        pltpu.VMEM((g, T_BLK // HG_SUB, (2 + HG_LEVELS) * HG_SUB, HG_WIDTH), F32),
                        pltpu.VMEM((g, RET_WIDTH, RET_WIDTH), F32),
                        pltpu.VMEM((g, SUBLANES + T_BLK, 3 * GDN_WIDTH), F32),
                        pltpu.VMEM((g, GDN_WIDTH, GDN_WIDTH), F32),
                        pltpu.VMEM((g, T_BLK, GDN_WIDTH), F32)],
        compiler_params=pltpu.CompilerParams(dimension_semantics=("arbitrary", "arbitrary"),
                                             vmem_limit_bytes=VMEM_LIMIT),
        name="mixers",
    )(by_seq(zh), *hg_tabs, by_seq(zr), cos_t, sin_t, *ret_tabs[2:], by_seq(zg), by_seq(zab),
      *gdn_tabs, hm, bd)
    return [o.reshape(n, o.shape[-1]) for o in outs]


def _outproj_kernel(ohg_ref, oret_ref, ogdn_ref, x_ref, mod_ref, g_ref, w_ref, rw_ref, rb_ref,
                    su_ref, x1_ref, hn_ref, ri_ref, gates_ref, cnt_ref, run_ref):
    @pl.when(pl.program_id(0) == 0)
    def _():
        run_ref[...] = jnp.zeros_like(run_ref)

    rw_hi = rw_ref[0:LANES, :]
    rw_lo = rw_ref[LANES:2 * LANES, :]
    eid = lax.broadcasted_iota(jnp.int32, (LANES, ROUTE_SLAB), 0)
    routed = {}

    def slab_stages(s):
        rows = slice(s * ROUTE_SLAB, (s + 1) * ROUTE_SLAB)
        y = (_dot(ohg_ref[rows, :], w_ref[0:HG_WIDTH, :])
             + _dot(oret_ref[rows, :], w_ref[HG_WIDTH:HG_WIDTH + RET_WIDTH, :])
             + _dot(ogdn_ref[rows, :], w_ref[HG_WIDTH + RET_WIDTH:, :]))
        yield
        x1 = x_ref[rows, :] + mod_ref[0, 0:1, :] * y
        x1_ref[rows, :] = x1
        n = x1 * lax.rsqrt(jnp.mean(x1 * x1, axis=-1, keepdims=True) + NORM_EPS)
        hn = (n * g_ref[0:1, :]) * (1.0 + mod_ref[0, 1:2, :]) + mod_ref[0, 2:3, :]
        hn_ref[rows, :] = _pack_halves(hn)
        yield
        hn_hi, hn_lo = _split2(hn)
        logits = (_dot_nt(rw_hi, hn_hi) + _dot_nt(rw_lo, hn_hi) + _dot_nt(rw_hi, hn_lo)
                  + rb_ref[:, rows])
        yield
        work = jnp.where(eid < N_EXPERTS, logits, -jnp.inf)
        vals, idxs = [], []
        multihot = jnp.zeros((LANES, ROUTE_SLAB), F32)
        for kk in range(TOP_K):
            mx = jnp.max(work, axis=0, keepdims=True)
            ix = jnp.min(jnp.where(work == mx, eid, LANES), axis=0, keepdims=True)
            sel = eid == ix
            multihot = jnp.where(sel, 1.0, multihot)
            work = jnp.where(sel, -jnp.inf, work)
            vals.append(mx)
            idxs.append(ix)
            if kk % 2 == 1:
                yield
        routed[s] = (vals, idxs, multihot)

    n_slabs = TM // ROUTE_SLAB
    active = [slab_stages(s) for s in range(n_slabs)]
    while active:
        for gen in list(active):
            if next(gen, _DONE) is _DONE:
                active.remove(gen)

    multihot = jnp.concatenate([routed[s][2] for s in range(n_slabs)], axis=1)
    run = run_ref[...]
    before = _dot(_bf(multihot), su_ref[...]) + jnp.concatenate([run] * (TM // LANES), axis=1)
    run = run + jnp.sum(multihot, axis=1, keepdims=True)
    run_ref[...] = run
    cnt_ref[...] = run

    for s in range(n_slabs):
        vals, idxs, _ = routed[s]
        cols = slice(s * ROUTE_SLAB, (s + 1) * ROUTE_SLAB)
        ex = [jnp.exp(vv - vals[0]) for vv in vals]
        den = ex[0] + ex[1] + ex[2] + ex[3]
        ranks = [jnp.sum(jnp.where(eid == ix, before[:, cols], 0.0), axis=0, keepdims=True) for ix in idxs]
        ri_ref[:, cols] = jnp.concatenate(idxs + [r.astype(jnp.int32) for r in ranks], axis=0)
        gates_ref[:, cols] = jnp.concatenate([e / den for e in ex]
                                             + [jnp.zeros((TOP_K, ROUTE_SLAB), F32)], axis=0)


def _outproj(ohg, oret, ogdn, x2d, modv, gpar, w_bf, rw, rb, su, tiles_per_batch):
    n = x2d.shape[0]
    row = lambda i: (i, 0)
    col = lambda i: (0, i)
    const = lambda i: (0, 0)
    return pl.pallas_call(
        _outproj_kernel,
        grid=(n // TM,),
        in_specs=[pl.BlockSpec((TM, HG_WIDTH), row), pl.BlockSpec((TM, RET_WIDTH), row),
                  pl.BlockSpec((TM, GDN_WIDTH), row), pl.BlockSpec((TM, D_MODEL), row),
                  pl.BlockSpec((1, SUBLANES, D_MODEL), lambda i: (i // tiles_per_batch, 0, 0)),
                  pl.BlockSpec((SUBLANES, D_MODEL), const),
                  pl.BlockSpec((D_MODEL, D_MODEL), const),
                  pl.BlockSpec((2 * LANES, D_MODEL), const),
                  pl.BlockSpec((LANES, TM), const),
                  pl.BlockSpec((TM, TM), const)],
        out_specs=[pl.BlockSpec((TM, D_MODEL), row), pl.BlockSpec((TM, D_MODEL // 2), row),
                   pl.BlockSpec((2 * TOP_K, TM), col), pl.BlockSpec((2 * TOP_K, TM), col),
                   pl.BlockSpec((LANES, LANES), const)],
        out_shape=[jax.ShapeDtypeStruct((n, D_MODEL), F32),
                   jax.ShapeDtypeStruct((n, D_MODEL // 2), jnp.uint32),
                   jax.ShapeDtypeStruct((2 * TOP_K, n), jnp.int32),
                   jax.ShapeDtypeStruct((2 * TOP_K, n), F32),
                   jax.ShapeDtypeStruct((LANES, LANES), F32)],
        scratch_shapes=[pltpu.VMEM((LANES, LANES), F32)],
        compiler_params=pltpu.CompilerParams(dimension_semantics=("arbitrary",),
                                             vmem_limit_bytes=VMEM_LIMIT),
        name="outproj_router",
    )(ohg, oret, ogdn, x2d, modv, gpar, w_bf, rw, rb, su)


def _expert_kernel(be_ref, nu_ref, x_ref, w1_ref, b1_ref, w2_ref, b2_ref, y_ref, w1b_ref, w2b_ref):
    i = pl.program_id(0)
    prev = be_ref[jnp.maximum(i - 1, 0)]
    fresh = jnp.logical_or(i == 0, be_ref[i] != prev)

    @pl.when(fresh)
    def _():
        w1b_ref[...] = _bf(w1_ref[0])
        w2b_ref[...] = _bf(w2_ref[0])

    @pl.when(i < nu_ref[0])
    def _():
        half = D_MODEL // 2

        def slab_stages(s):
            rows = slice(s * EXPERT_SLAB, (s + 1) * EXPERT_SLAB)
            x_lo, x_hi = _unpack_halves(x_ref[rows, :])
            hid = (_dot(_bf(x_lo), w1b_ref[0:half, :]) + _dot(_bf(x_hi), w1b_ref[half:, :])
                   + b1_ref[0])
            yield
            x_glu = jnp.minimum(hid[:, 0:D_FF], SWIGLU_LIMIT)
            x_lin = jnp.clip(hid[:, D_FF:], -SWIGLU_LIMIT, SWIGLU_LIMIT)
            act = x_glu * _sigmoid(SWIGLU_ALPHA * x_glu) * (x_lin + 1.0)
            yield
            y_ref[rows, :] = _pack_halves(_dot(_bf(act), w2b_ref[...]) + b2_ref[0])

        active = [slab_stages(s) for s in range(R_BLK // EXPERT_SLAB)]
        while active:
            for gen in list(active):
                if next(gen, _DONE) is _DONE:
                    active.remove(gen)


def _experts(block_e, n_used, xs, w1, b1, w2, b2):
    p = xs.shape[0]
    n_blocks = p // R_BLK
    ne = w1.shape[0] * w1.shape[1]
    w1 = w1.reshape(ne, D_MODEL, 2 * D_FF)
    w2 = w2.reshape(ne, D_FF, D_MODEL)
    rowmap = lambda i, be, nu: (jnp.minimum(i, nu[0] - 1), 0)
    emap = lambda i, be, nu: (be[i], 0, 0)
    grid_spec = pltpu.PrefetchScalarGridSpec(
        num_scalar_prefetch=2,
        grid=(n_blocks,),
        in_specs=[pl.BlockSpec((R_BLK, D_MODEL // 2), rowmap),
                  pl.BlockSpec((1, D_MODEL, 2 * D_FF), emap),
                  pl.BlockSpec((1, 1, 2 * D_FF), emap),
                  pl.BlockSpec((1, D_FF, D_MODEL), emap),
                  pl.BlockSpec((1, 1, D_MODEL), emap)],
        out_specs=pl.BlockSpec((R_BLK, D_MODEL // 2), rowmap),
        scratch_shapes=[pltpu.VMEM((D_MODEL, 2 * D_FF), BF16), pltpu.VMEM((D_FF, D_MODEL), BF16)],
    )
    return pl.pallas_call(
        _expert_kernel,
        grid_spec=grid_spec,
        out_shape=jax.ShapeDtypeStruct((p, D_MODEL // 2), jnp.uint32),
        compiler_params=pltpu.CompilerParams(dimension_semantics=("arbitrary",),
                                             vmem_limit_bytes=VMEM_LIMIT),
        name="experts",
    )(block_e, n_used, xs, w1, b1.reshape(ne, 1, 2 * D_FF), w2, b2.reshape(ne, 1, D_MODEL))


def _sc_mesh():
    return plsc.VectorSubcoreMesh(core_axis_name="c", subcore_axis_name="s")


def _sc_scatter_rows(x, pos, p_rows):
    mesh = _sc_mesh()
    n, w = x.shape
    per_worker = n // (mesh.num_cores * mesh.num_subcores)
    assert per_worker % SC_ROWS == 0 and pos.shape == (TOP_K, n)

    @functools.partial(
        pl.kernel, out_type=jax.ShapeDtypeStruct((p_rows, w), x.dtype), mesh=mesh,
        scratch_types=[pltpu.VMEM((SC_ROWS,), jnp.int32)] * TOP_K
        + [pltpu.VMEM((SC_ROWS, w), x.dtype), pltpu.SemaphoreType.DMA],
        name="dispatch_rows")
    def scatter(x_hbm, p_hbm, o_hbm, i0, i1, i2, i3, rows_v, sem):
        idx = (i0, i1, i2, i3)
        worker = lax.axis_index("s") * mesh.num_cores + lax.axis_index("c")

        @pl.loop(0, per_worker // SC_ROWS)
        def _(g):
            base = pl.multiple_of(worker * per_worker + g * SC_ROWS, SC_ROWS)
            pltpu.sync_copy(x_hbm.at[pl.ds(base, SC_ROWS)], rows_v)
            for kk in range(TOP_K):
                pltpu.sync_copy(p_hbm.at[kk, pl.ds(base, SC_ROWS)], idx[kk])
            copies = [pltpu.async_copy(rows_v, o_hbm.at[idx[kk]], sem) for kk in range(TOP_K)]
            for cp in copies:
                cp.wait()

    return scatter(x, pos)


def _sc_gather_rows(table, idx):
    mesh = _sc_mesh()
    n_idx = idx.shape[0]
    w = table.shape[1]
    per_worker = n_idx // (mesh.num_cores * mesh.num_subcores)
    assert per_worker % SC_ROWS == 0

    @functools.partial(
        pl.kernel, out_type=jax.ShapeDtypeStruct((n_idx, w), table.dtype), mesh=mesh,
        scratch_types=[pltpu.VMEM((SC_ROWS,), jnp.int32), pltpu.VMEM((SC_ROWS, w), table.dtype),
                       pltpu.SemaphoreType.DMA],
        name="combine_rows")
    def gather(t_hbm, i_hbm, o_hbm, idx_v, rows_v, sem):
        worker = lax.axis_index("s") * mesh.num_cores + lax.axis_index("c")

        @pl.loop(0, per_worker // SC_ROWS)
        def _(g):
            base = pl.multiple_of(worker * per_worker + g * SC_ROWS, SC_ROWS)
            pltpu.sync_copy(i_hbm.at[pl.ds(base, SC_ROWS)], idx_v)
            pltpu.async_copy(t_hbm.at[idx_v], rows_v, sem).wait()
            pltpu.sync_copy(rows_v, o_hbm.at[pl.ds(base, SC_ROWS)])

    return gather(table, idx)


def _combine_kernel(x_ref, y0_ref, y1_ref, y2_ref, y3_ref, gates_ref, mod_ref, g_ref, o_ref, *, final):
    gt = gates_ref[...]
    half = D_MODEL // 2
    acc_lo, acc_hi = None, None
    for kk, y_ref in enumerate((y0_ref, y1_ref, y2_ref, y3_ref)):
        lo, hi = _unpack_halves(y_ref[0])
        g = gt[:, kk:kk + 1]
        acc_lo = g * lo if acc_lo is None else acc_lo + g * lo
        acc_hi = g * hi if acc_hi is None else acc_hi + g * hi
    x_lo = x_ref[:, 0:half] + mod_ref[0, 3:4, 0:half] * acc_lo
    x_hi = x_ref[:, half:] + mod_ref[0, 3:4, half:] * acc_hi
    if final:
        ssq = (jnp.sum(x_lo * x_lo, axis=-1, keepdims=True)
               + jnp.sum(x_hi * x_hi, axis=-1, keepdims=True))
        r = lax.rsqrt(ssq * (1.0 / D_MODEL) + NORM_EPS)
        x_lo = x_lo * r * g_ref[1:2, 0:half]
        x_hi = x_hi * r * g_ref[1:2, half:]
    o_ref[:, 0:half] = x_lo
    o_ref[:, half:] = x_hi


def _combine(x, yg, gates, modv, gpar, tiles_per_batch, final, piece):
    n = x.shape[0]
    tiles = n // TM // COMBINE_PIECES
    first = piece * tiles
    row = lambda i: (first + i, 0)
    ysp = [pl.BlockSpec((1, TM, D_MODEL // 2), (lambda i, kk=kk: (kk, i, 0))) for kk in range(TOP_K)]
    return pl.pallas_call(
        functools.partial(_combine_kernel, final=final),
        grid=(tiles,),
        in_specs=[pl.BlockSpec((TM, D_MODEL), row)] + ysp + [
            pl.BlockSpec((TM, 2 * TOP_K), row),
            pl.BlockSpec((1, SUBLANES, D_MODEL), lambda i: ((first + i) // tiles_per_batch, 0, 0)),
            pl.BlockSpec((SUBLANES, D_MODEL), lambda i: (0, 0))],
        out_specs=pl.BlockSpec((TM, D_MODEL), row),
        out_shape=jax.ShapeDtypeStruct((n, D_MODEL), F32),
        input_output_aliases={0: 0},
        compiler_params=pltpu.CompilerParams(dimension_semantics=("arbitrary",),
                                             vmem_limit_bytes=VMEM_LIMIT),
        name="combine",
    )(x, yg, yg, yg, yg, gates, modv, gpar)


def _widen_w_in(w):
    pad = jnp.zeros((w.shape[0], Z_W - w.shape[1]), w.dtype)
    return _bf(jnp.concatenate([w, pad], axis=1))


def _pad_rows(a, rows=SUBLANES):
    return jnp.concatenate([a, jnp.zeros((rows - a.shape[0],) + a.shape[1:], a.dtype)], axis=0)


def kernel(x, c, ada_w, ada_b, norm1_g, norm2_g, w_in, w_out, hg_lb_logits, hg_norm_g, ret_norm_g,
           gdn_conv_w, gdn_A_log, gdn_dt_bias, gdn_norm_g, router_w, router_b, exp_w1, exp_b1,
           exp_w2, exp_b2, final_norm_g):
    batch, seq, d = x.shape
    depth = ada_w.shape[0]
    n = batch * seq
    steps = seq // T_BLK
    tiles_per_batch = seq // TM
    nk = n * TOP_K
    n_blocks = nk // R_BLK + N_EXPERTS
    p_rows = n_blocks * R_BLK

    lv_np = _level_matrix(T_BLK)
    lv = jnp.asarray(_level_matrix(HG_SUB))
    f_np = _hgrn_exponent_matrix(HG_SUB)
    f_mat = jnp.asarray(np.concatenate([f_np, f_np], axis=1), BF16)
    gm = jnp.asarray(_gdn_masks(T_BLK))
    bd = jnp.asarray(_block_diag_mask(T_BLK, GDN_CHUNK))
    tril = jnp.asarray(_block_diag_mask(T_BLK, GDN_CHUNK) * (lv_np >= 0), BF16)
    hm = jnp.asarray(_head_masks(RET_WIDTH, RET_HEADS))
    su = jnp.asarray(np.triu(np.ones((TM, TM), np.float32), 1), BF16)
    eab_np = np.zeros((LANES, 2 * GDN_WIDTH), np.float32)
    for h in range(GDN_HEADS):
        eab_np[h, h * GDN_DK:(h + 1) * GDN_DK] = 1.0
        eab_np[GDN_HEADS + h, GDN_WIDTH + h * GDN_DK:GDN_WIDTH + (h + 1) * GDN_DK] = 1.0
    eab = jnp.asarray(eab_np, BF16)

    half = RET_DK // 2
    inv = ROPE_BASE ** (-jnp.linspace(0.0, 1.0, half, dtype=F32))
    ang = jnp.arange(seq, dtype=F32)[:, None] * inv[None, :]
    cos_t = jnp.tile(jnp.cos(ang), (1, 2 * RET_HEADS))
    first_half = jnp.asarray((np.arange(RET_WIDTH) % RET_DK < half).astype(np.float32))[None, :]
    sin_t = jnp.tile(jnp.sin(ang), (1, 2 * RET_HEADS)) * (1.0 - 2.0 * first_half)
    log_g = jnp.log1p(-jnp.exp2(-5.0 - jnp.arange(RET_HEADS, dtype=F32)))
    jj = jnp.arange(T_BLK, dtype=F32)
    diff = jj[:, None] - jj[None, :]
    dmat = jnp.where(diff[None] >= 0, jnp.exp(diff[None] * log_g[:, None, None]), 0.0)
    lg_lane = jnp.repeat(log_g, RET_DK)[None, :]
    qw = jnp.exp(lg_lane * (jj[:, None] + 1.0))
    kw = jnp.exp(lg_lane * (T_BLK - 1.0 - jj[:, None]))
    cdec = jnp.exp(T_BLK * lg_lane)

    lb = jnp.cumsum(jax.nn.softmax(hg_lb_logits.astype(F32), axis=0), axis=0)
    lb = jnp.maximum(lb - lb[0], 0.0)
    c_pad = _pad_rows(c.astype(F32))
    mod = _adaln(c_pad, ada_w, ada_b)[:, :batch, :]

    x2d = x.reshape(n, d)
    out = None
    for l in range(depth):
        sh1, sc1, gt1, sh2, sc2, gt2 = [mod[l][:, i * d:(i + 1) * d] for i in range(6)]
        zeros = jnp.zeros_like(sh1)
        modv_a = jnp.stack([sc1, sh1, zeros, zeros, zeros, zeros, zeros, zeros], axis=1)
        modv_c = jnp.stack([gt1, sc2, sh2, gt2, zeros, zeros, zeros, zeros], axis=1)
        gpar_a = _pad_rows(norm1_g[l][None, :])
        gpar_c = _pad_rows(jnp.stack([norm2_g[l], final_norm_g], axis=0))

        zh, zr, zg, zab = _inproj(x2d, modv_a, gpar_a, _widen_w_in(w_in[l]), seq)

        hg_par = _pad_rows(jnp.stack([jnp.log(lb[l]), jnp.log1p(-lb[l]), 1.0 - lb[l],
                                      hg_norm_g[l].reshape(-1)], axis=0))
        ret_par = _pad_rows(jnp.concatenate([cdec, ret_norm_g[l].reshape(1, -1), first_half], axis=0))
        gdn_par = _pad_rows(jnp.stack([jnp.repeat(-jnp.exp(gdn_A_log[l].astype(F32)), GDN_DK),
                                       jnp.repeat(gdn_dt_bias[l].astype(F32), GDN_DK),
                                       gdn_norm_g[l].reshape(-1)], axis=0))
        o_hg, o_ret, o_gdn = _mixers(
            zh, zr, zg, zab,
            (f_mat, lv, hg_par),
            (cos_t, sin_t, dmat, qw, kw, ret_par),
            (_pad_rows(gdn_conv_w[l].astype(F32)), gdn_par, eab, gm, tril),
            hm, bd, batch, steps)

        rw_f = jnp.concatenate([router_w[l].T, jnp.zeros((LANES - N_EXPERTS, d), F32)], axis=0)
        rw = jnp.concatenate(_split2(rw_f), axis=0)
        rb = jnp.broadcast_to(jnp.concatenate([router_b[l], jnp.zeros((LANES - N_EXPERTS,), F32)])[:, None],
                              (LANES, TM))
        x1, hn2, ri, gates_t, cnt = _outproj(o_hg, o_ret, o_gdn, x2d, modv_c, gpar_c, _bf(w_out[l]),
                                             rw, rb, su, tiles_per_batch)
        gates = gates_t.T

        counts = cnt[:N_EXPERTS, 0].astype(jnp.int32)
        padded = (counts + R_BLK - 1) // R_BLK * R_BLK
        pend = jnp.cumsum(padded)
        pstart = pend - padded
        n_used = (pend[-1] // R_BLK).astype(jnp.int32)
        blk_start = jnp.arange(n_blocks, dtype=jnp.int32) * R_BLK
        blk_start = jnp.minimum(blk_start, pend[-1] - R_BLK)
        block_e = jnp.sum(blk_start[:, None] >= pend[None, :], axis=1).astype(jnp.int32)
        eid = jnp.arange(N_EXPERTS, dtype=jnp.int32)
        pos_t = ri[TOP_K:] + jnp.sum(jnp.where(ri[:TOP_K, :, None] == eid, pstart, 0), axis=-1)
        xs = _sc_scatter_rows(hn2, pos_t, p_rows)
        y = _experts(block_e + l * N_EXPERTS, n_used.reshape(1), xs, exp_w1, exp_b1, exp_w2, exp_b2)
        n_piece = n // COMBINE_PIECES
        x2d = x1
        for p in range(COMBINE_PIECES):
            idx = pos_t[:, p * n_piece:(p + 1) * n_piece].reshape(-1)
            yg = _sc_gather_rows(y, idx).reshape(TOP_K, n_piece, d // 2)
            x2d = _combine(x2d, yg, gates, modv_c, gpar_c, tiles_per_batch, l == depth - 1, p)
    return x2d.reshape(batch, seq, d)
```

```python
import functools
import math

import numpy as np
import jax
import jax.numpy as jnp
from jax import lax
from jax.experimental import pallas as pl
from jax.experimental.pallas import tpu as pltpu
from jax.experimental.pallas import tpu_sc as plsc

F32 = jnp.float32
BF16 = jnp.bfloat16

D_MODEL = 1024
HG_HEADS, HG_DK = 4, 128
HG_WIDTH = HG_HEADS * HG_DK
RET_HEADS, RET_DK = 4, 64
RET_WIDTH = RET_HEADS * RET_DK
GDN_HEADS, GDN_DK = 4, 64
GDN_WIDTH = GDN_HEADS * GDN_DK
CONV_K = 4
ROPE_BASE = 10000.0
N_EXPERTS = 32
TOP_K = 4
D_FF = D_MODEL
SWIGLU_ALPHA = 1.702
SWIGLU_LIMIT = 7.0
NORM_EPS = 1e-6
L2_EPS = 1e-6
GDN_CHUNK = 64

LANES = 128
SUBLANES = 8
VMEM_LIMIT = 56 * 1024 * 1024

T_BLK = 256
SEQ_PER_STEP = 2
TM = 512
TM_IN = 512
ROUTE_SLAB = 128
COMBINE_PIECES = 8
R_BLK = 528
EXPERT_SLAB = 176
SC_ROWS = 128
HG_SUB = 128
HG_LEVELS = 7

ZH_W = 4 * HG_WIDTH
ZR_W = 4 * RET_WIDTH
ZG_W = 4 * GDN_WIDTH
ZAB_W = LANES
Z_W = ZH_W + ZR_W + ZG_W + ZAB_W


def _dot(a, b):
    return jnp.dot(a, b, preferred_element_type=F32)


def _dot_nt(a, b):
    return lax.dot_general(a, b, (((1,), (1,)), ((), ())), preferred_element_type=F32)


def _dot_tn(a, b):
    return lax.dot_general(a, b, (((0,), (0,)), ((), ())), preferred_element_type=F32)


def _split2(x):
    hi = x.astype(BF16)
    return hi, (x - hi.astype(F32)).astype(BF16)


def _dot2_lhs01(c, x):
    hi, lo = _split2(x)
    return _dot(c, hi) + _dot(c, lo)


def _dot2_rhs01(x, c):
    hi, lo = _split2(x)
    return _dot(hi, c) + _dot(lo, c)


def _sigmoid(x):
    return 1.0 / (1.0 + jnp.exp(-x))


def _silu(x):
    return x * _sigmoid(x)


def _softplus(x):
    return jnp.maximum(x, 0.0) + jnp.log1p(jnp.exp(-jnp.abs(x)))


def _bf(x):
    return x.astype(BF16)


def _pack_halves(x):
    w = x.shape[1] // 2
    bits = lax.bitcast_convert_type(_bf(x).astype(F32), jnp.uint32)
    return (bits[:, :w] >> 16) | bits[:, w:]


def _unpack_halves(p):
    lo = lax.bitcast_convert_type(p << 16, F32)
    hi = lax.bitcast_convert_type(p & jnp.uint32(0xFFFF0000), F32)
    return lo, hi


def _level_matrix(t):
    i = np.arange(t)[:, None]
    j = np.arange(t)[None, :]
    x = i ^ j
    lv = np.floor(np.log2(np.maximum(x, 1))).astype(np.int32)
    lv = np.where(i == j, int(math.log2(t)), lv)
    lv = np.where(i < j, -1, lv)
    return lv.astype(np.int32)


def _hgrn_exponent_matrix(t):
    n_lev = int(math.log2(t))
    f = np.zeros((2 + n_lev, t, t), np.float32)
    u = np.arange(t)[None, :]
    r = np.arange(t)[:, None]
    f[0] = (u <= r)
    f[1] = (u > r)
    for l in range(n_lev):
        h = 1 << l
        base = (r // (2 * h)) * (2 * h)
        mid = base + h
        upper = (r - base) >= h
        f[2 + l] = np.where(upper, (u >= mid) & (u <= r), (u > r) & (u < mid))
    return f.reshape((2 + n_lev) * t, t)


def _gdn_masks(t):
    lv = _level_matrix(GDN_CHUNK)
    top = int(math.log2(GDN_CHUNK))
    incl = lv >= 0
    d8 = (lv >= 0) & (lv <= 2)
    merges = [(lv == l) for l in range(3, top)]
    eye = lv == top
    pats = np.stack([incl, d8] + merges + [eye]).astype(np.float32)
    return np.tile(pats, (1, t // GDN_CHUNK, GDN_HEADS))


def _block_diag_mask(t, blk):
    i = np.arange(t)
    return (i[:, None] // blk == i[None, :] // blk).astype(np.float32)


def _head_masks(width, heads):
    lane = np.arange(width)[None, :]
    m = np.zeros((SUBLANES, width), np.float32)
    for h in range(heads):
        m[h] = (lane // (width // heads) == h)[0]
    return m


def _adaln_kernel(c_ref, w_ref, b_ref, o_ref):
    cond = _silu(c_ref[...])
    o_ref[0] = jnp.dot(cond, w_ref[0], preferred_element_type=F32,
                       precision=lax.Precision.HIGHEST) + b_ref[0]


def _adaln(c_pad, ada_w, ada_b):
    depth, d, n6 = ada_w.shape
    tn = n6 // 4
    return pl.pallas_call(
        _adaln_kernel,
        grid=(depth, n6 // tn),
        in_specs=[pl.BlockSpec((SUBLANES, d), lambda l, j: (0, 0)),
                  pl.BlockSpec((1, d, tn), lambda l, j: (l, 0, j)),
                  pl.BlockSpec((1, 1, tn), lambda l, j: (l, 0, j))],
        out_specs=pl.BlockSpec((1, SUBLANES, tn), lambda l, j: (l, 0, j)),
        out_shape=jax.ShapeDtypeStruct((depth, SUBLANES, n6), F32),
        compiler_params=pltpu.CompilerParams(vmem_limit_bytes=VMEM_LIMIT),
        name="adaln",
    )(c_pad, ada_w, ada_b.reshape(depth, 1, n6))


def _inproj_kernel(x_ref, mod_ref, g_ref, w_ref, zh_ref, zr_ref, zg_ref, zab_ref):
    x = x_ref[...]
    y = x * lax.rsqrt(jnp.mean(x * x, axis=-1, keepdims=True) + NORM_EPS)
    hn = (y * g_ref[0:1, :]) * (1.0 + mod_ref[0, 0:1, :]) + mod_ref[0, 1:2, :]
    hb = _bf(hn)
    zh_ref[...] = _dot(hb, w_ref[:, 0:ZH_W])
    zr_ref[...] = _dot(hb, w_ref[:, ZH_W:ZH_W + ZR_W])
    zg_ref[...] = _dot(hb, w_ref[:, ZH_W + ZR_W:ZH_W + ZR_W + ZG_W])
    zab_ref[...] = _dot(hb, w_ref[:, ZH_W + ZR_W + ZG_W:Z_W])


def _inproj(x2d, modv, gpar, w_bf, seq):
    n = x2d.shape[0]
    row = lambda i: (i, 0)
    tiles_per_batch = seq // TM_IN
    return pl.pallas_call(
        _inproj_kernel,
        grid=(n // TM_IN,),
        in_specs=[pl.BlockSpec((TM_IN, D_MODEL), row),
                  pl.BlockSpec((1, SUBLANES, D_MODEL), lambda i: (i // tiles_per_batch, 0, 0)),
                  pl.BlockSpec((SUBLANES, D_MODEL), lambda i: (0, 0)),
                  pl.BlockSpec((D_MODEL, Z_W), lambda i: (0, 0))],
        out_specs=[pl.BlockSpec((TM_IN, ZH_W), row), pl.BlockSpec((TM_IN, ZR_W), row),
                   pl.BlockSpec((TM_IN, ZG_W), row), pl.BlockSpec((TM_IN, ZAB_W), row)],
        out_shape=[jax.ShapeDtypeStruct((n, ZH_W), F32), jax.ShapeDtypeStruct((n, ZR_W), F32),
                   jax.ShapeDtypeStruct((n, ZG_W), F32), jax.ShapeDtypeStruct((n, ZAB_W), F32)],
        compiler_params=pltpu.CompilerParams(dimension_semantics=("arbitrary",),
                                             vmem_limit_bytes=VMEM_LIMIT),
        name="inproj",
    )(x2d, modv, gpar, w_bf)


def _hgrn_stages(zh_ref, f_ref, lv_ref, par_ref, o_ref, state_ref, ex_ref):
    W = HG_WIDTH
    hq = zh_ref[:, 0:W]
    hf = zh_ref[:, W:2 * W]
    loglb = par_ref[0:1, :]
    log1mlb = par_ref[1:2, :]
    onemlb = par_ref[2:3, :]

    q = _silu(hq)
    e = jnp.exp(-jnp.abs(hf))
    inv = 1.0 / (1.0 + e)
    k = onemlb * (jnp.where(hf >= 0, e, 1.0) * inv)
    logsig = jnp.minimum(hf, 0.0) - jnp.log1p(e)
    c = log1mlb + logsig
    lf = jnp.maximum(loglb, c) + jnp.log1p(jnp.exp(-jnp.abs(loglb - c)))
    yield

    lv = lv_ref[...]
    t = HG_SUB
    for sb in range(T_BLK // HG_SUB):
        rows = slice(sb * t, (sb + 1) * t)
        ex_ref[sb] = jnp.exp(_dot(f_ref[...], jnp.concatenate(_split2(lf[rows]), axis=0)))
        yield
        for h in range(HG_HEADS):
            cs = slice(h * HG_DK, (h + 1) * HG_DK)
            qh = q[rows, cs]
            kh = k[rows, cs]
            vh = _bf(zh_ref[rows, 2 * W + h * HG_DK:2 * W + (h + 1) * HG_DK])
            qb = _bf(qh)
            kb = _bf(kh)
            s = jnp.where(lv == HG_LEVELS, _dot_nt(qb, kb), 0.0)
            for l in range(HG_LEVELS):
                el = _bf(ex_ref[sb, (2 + l) * t:(3 + l) * t, cs])
                s = jnp.where(lv == l, _dot_nt(qb * el, kb * el), s)
                if l % 2 == 1:
                    yield
            eb = ex_ref[sb, 0:t, cs]
            ebl = ex_ref[sb, t:2 * t, cs]
            st = state_ref[h]
            o = _dot(_bf(s), vh) + _dot_nt(_bf(qh * eb), _bf(st))
            upd = _dot_tn(vh, _bf(kh * ebl))
            state_ref[h] = st * ex_ref[sb, t - 1:t, cs] + upd
            ms = jnp.mean(o * o, axis=-1, keepdims=True)
            gate = _silu(zh_ref[rows, 3 * W + h * HG_DK:3 * W + (h + 1) * HG_DK])
            o_ref[rows, cs] = _bf(o * lax.rsqrt(ms + NORM_EPS) * par_ref[3:4, cs] * gate)
            yield


def _ret_stages(zr_ref, cos_ref, sin_ref, dmat_ref, qw_ref, kw_ref, par_ref, hm_ref, bd_ref,
                o_ref, state_ref):
    W = RET_WIDTH
    cos = cos_ref[...]
    sin = sin_ref[...]
    first_half = par_ref[2:3, :] > 0

    def rotary(t):
        swapped = jnp.where(first_half, pltpu.roll(t, W - RET_DK // 2, 1), pltpu.roll(t, RET_DK // 2, 1))
        return t * cos + swapped * sin

    q = rotary(zr_ref[:, 0:W])
    k = rotary(zr_ref[:, W:2 * W]) * (RET_DK ** -0.5)
    v = zr_ref[:, 2 * W:3 * W]
    kb = _bf(k)
    bd = bd_ref[...]
    yield

    s_parts = []
    v_parts = []
    for h in range(RET_HEADS):
        hm = hm_ref[h:h + 1, :]
        s_parts.append(_bf(_dot_nt(_bf(q * hm), kb) * dmat_ref[h]))
        v_parts.append(_bf(v * hm))
        yield
    st = state_ref[...]
    o = (_dot(jnp.concatenate(s_parts, axis=1), jnp.concatenate(v_parts, axis=0))
         + _dot(_bf(q * qw_ref[...]), _bf(st)))
    yield
    kv = _dot_tn(_bf(k * kw_ref[...]), _bf(v))
    state_ref[...] = st * par_ref[0:1, :] + bd * kv
    yield

    ms = _dot2_rhs01(o * o, _bf(bd)) * (1.0 / RET_DK)
    gate = _silu(zr_ref[:, 3 * W:4 * W])
    o_ref[...] = _bf(o * lax.rsqrt(ms + NORM_EPS) * par_ref[1:2, :] * gate)


def _gdn_stages(zg_ref, zab_ref, convw_ref, par_ref, eab_ref, gm_ref, bd_ref, tril_ref, hm_ref,
                o_ref, ext_ref, state_ref, obuf_ref):
    W = GDN_WIDTH
    t = T_BLK
    u = zg_ref[:, 0:3 * W]
    ext_ref[SUBLANES:SUBLANES + t, :] = u
    conv = (convw_ref[3:4, :] * u
            + convw_ref[2:3, :] * ext_ref[SUBLANES - 1:SUBLANES - 1 + t, :]
            + convw_ref[1:2, :] * ext_ref[SUBLANES - 2:SUBLANES - 2 + t, :]
            + convw_ref[0:1, :] * ext_ref[SUBLANES - 3:SUBLANES - 3 + t, :])
    ext_ref[0:SUBLANES, :] = u[t - SUBLANES:t, :]
    qkv = _silu(conv)
    q = qkv[:, 0:W]
    k = qkv[:, W:2 * W]
    v = qkv[:, 2 * W:3 * W]
    yield

    bd = bd_ref[...]
    bdb = _bf(bd)
    ab = zab_ref[...]
    a_exp = _dot2_rhs01(ab, eab_ref[:, 0:W])
    b_exp = _dot2_rhs01(ab, eab_ref[:, W:2 * W])
    qn = q * lax.rsqrt(_dot2_rhs01(q * q, bdb) + L2_EPS) * (GDN_DK ** -0.5)
    kn = k * lax.rsqrt(_dot2_rhs01(k * k, bdb) + L2_EPS)
    yield
    beta = _sigmoid(b_exp)
    g = par_ref[0:1, :] * _softplus(a_exp + par_ref[1:2, :])
    gc = _dot2_lhs01(tril_ref[...], g)
    gl = _dot2_lhs01(bdb, g)
    yield
    eg = jnp.exp(gc)
    vb = v * beta
    kbeta = kn * beta * eg
    qdec = qn * eg
    kdec = kn * jnp.exp(gl - gc)

    n_chunks = t // GDN_CHUNK
    hms = [_bf(jnp.broadcast_to(hm_ref[h:h + 1, :], (GDN_CHUNK, W))) for h in range(GDN_HEADS)]
    hms2 = [jnp.concatenate([m, m], axis=1) for m in hms]

    def chunk(a, c):
        return a[c * GDN_CHUNK:(c + 1) * GDN_CHUNK]

    def expand(y, masks):
        yb = _bf(y)
        return jnp.concatenate([yb * m for m in masks], axis=0)

    def blockprod(x, y, masks):
        xb = _bf(x)
        return jnp.concatenate([_dot(chunk(xb, c), expand(chunk(y, c), masks))
                                for c in range(n_chunks)], axis=0)

    gc_row = _dot2_lhs01(bdb, gc * gm_ref[5])
    yield
    rel = jnp.exp(jnp.where(gm_ref[0] > 0, gc - gc_row, -jnp.inf))

    knbeta = kn * beta
    kq = [_dot_nt(_bf(jnp.concatenate([chunk(knbeta, c), chunk(qn, c)], axis=0)),
                  expand(chunk(kn, c), hms)) for c in range(n_chunks)]
    yield
    m = jnp.concatenate([r[0:GDN_CHUNK] for r in kq], axis=0) * rel
    qk = jnp.concatenate([r[GDN_CHUNK:] for r in kq], axis=0) * rel

    d = m * gm_ref[1]
    d2 = blockprod(d, d, hms)
    yield
    d4 = blockprod(d2, d2, hms)
    dd2 = blockprod(d, d2, hms)
    yield
    x = d2 - d - dd2
    xd4 = blockprod(x, d4, hms)
    yield
    x = x + d4 + xd4
    for lvl in range(2, 5):
        lo = m * gm_ref[lvl]
        xl = blockprod(x, lo, hms)
        yield
        y = lo + xl
        yx = blockprod(y, x, hms)
        yield
        x = x - (y + yx)

    vk = jnp.concatenate([vb, kbeta], axis=1)
    wk = vk + blockprod(x, vk, hms2)
    yield
    w = wk[:, 0:W]
    kcum = wk[:, W:2 * W]
    ag = blockprod(qk, wk, hms2)
    yield
    a1 = ag[:, 0:W]
    qeff = qdec - ag[:, W:2 * W]

    for c in range(t // GDN_CHUNK):
        rows = slice(c * GDN_CHUNK, (c + 1) * GDN_CHUNK)
        st = state_ref[...]
        stb = _bf(st)
        vnew = w[rows] - _dot(_bf(kcum[rows]), stb)
        obuf_ref[rows, :] = _dot(_bf(qeff[rows]), stb) + a1[rows]
        yield
        upd = _dot_tn(_bf(kdec[rows]), _bf(vnew))
        last = eg[(c + 1) * GDN_CHUNK - 1:(c + 1) * GDN_CHUNK, :]
        state_ref[...] = st * last + bd * upd
        yield

    o = obuf_ref[...]
    ms = _dot2_rhs01(o * o, bdb) * (1.0 / GDN_DK)
    gate = _silu(zg_ref[:, 3 * W:4 * W])
    o_ref[...] = _bf(o * lax.rsqrt(ms + NORM_EPS) * par_ref[2:3, :] * gate)


_DONE = object()

def _mixers_kernel(zh_ref, f_ref, lv_ref, hpar_ref,
                   zr_ref, cos_ref, sin_ref, dmat_ref, qw_ref, kw_ref, rpar_ref,
                   zg_ref, zab_ref, convw_ref, gpar_ref, eab_ref, gm_ref, tril_ref, hm_ref, bd_ref,
                   ohg_ref, oret_ref, ogdn_ref,
                   hstate_ref, hex_ref, rstate_ref, gext_ref, gstate_ref, gobuf_ref):
    @pl.when(pl.program_id(1) == 0)
    def _():
        hstate_ref[...] = jnp.zeros_like(hstate_ref)
        rstate_ref[...] = jnp.zeros_like(rstate_ref)
        gstate_ref[...] = jnp.zeros_like(gstate_ref)
        gext_ref[:, 0:SUBLANES, :] = jnp.zeros((SEQ_PER_STEP, SUBLANES, 3 * GDN_WIDTH), F32)

    active = []
    for i in range(SEQ_PER_STEP):
        active.append((_gdn_stages(zg_ref.at[i], zab_ref.at[i], convw_ref, gpar_ref, eab_ref, gm_ref, bd_ref,
                                   tril_ref, hm_ref, ogdn_ref.at[i], gext_ref.at[i], gstate_ref.at[i],
                                   gobuf_ref.at[i]), 1))
    for i in range(SEQ_PER_STEP):
        active.append((_hgrn_stages(zh_ref.at[i], f_ref, lv_ref, hpar_ref, ohg_ref.at[i], hstate_ref.at[i],
                                    hex_ref.at[i]), 2))
    for i in range(SEQ_PER_STEP):
        active.append((_ret_stages(zr_ref.at[i], cos_ref, sin_ref, dmat_ref, qw_ref, kw_ref, rpar_ref, hm_ref,
                                   bd_ref, oret_ref.at[i], rstate_ref.at[i]), 1))
    while active:
        for entry in list(active):
            gen, per_round = entry
            for _ in range(per_round):
                if next(gen, _DONE) is _DONE:
                    active.remove(entry)
                    break


def _mixers(zh, zr, zg, zab, hg_tabs, ret_tabs, gdn_tabs, hm, bd, batch, steps):
    n = zh.shape[0]
    seq = n // batch
    g = SEQ_PER_STEP
    assert batch % g == 0

    def const(a):
        return pl.BlockSpec(a.shape, lambda b, j, nd=a.ndim: (0,) * nd)

    def rows(width):
        return pl.BlockSpec((g, T_BLK, width), lambda b, j: (b, j, 0))

    def by_seq(a):
        return a.reshape(batch, seq, a.shape[-1])

    pos_rows = pl.BlockSpec((T_BLK, RET_WIDTH), lambda b, j: (j, 0))
    cos_t, sin_t = ret_tabs[0], ret_tabs[1]
    in_specs = ([rows(ZH_W)] + [const(a) for a in hg_tabs]
                + [rows(ZR_W), pos_rows, pos_rows] + [const(a) for a in ret_tabs[2:]]
                + [rows(ZG_W), rows(ZAB_W)] + [const(a) for a in gdn_tabs] + [const(hm), const(bd)])
    outs = pl.pallas_call(
        _mixers_kernel,
        grid=(batch // g, steps),
        in_specs=in_specs,
        out_specs=[rows(HG_WIDTH), rows(RET_WIDTH), rows(GDN_WIDTH)],
        out_shape=[jax.ShapeDtypeStruct((batch, seq, HG_WIDTH), BF16),
                   jax.ShapeDtypeStruct((batch, seq, RET_WIDTH), BF16),
                   jax.ShapeDtypeStruct((batch, seq, GDN_WIDTH), BF16)],
        scratch_shapes=[pltpu.VMEM((g, HG_HEADS, HG_DK, HG_DK), F32),
                        pltpu.VMEM((g, T_BLK // HG_SUB, (2 + HG_LEVELS) * HG_SUB, HG_WIDTH), F32),
                        pltpu.VMEM((g, RET_WIDTH, RET_WIDTH), F32),
                        pltpu.VMEM((g, SUBLANES + T_BLK, 3 * GDN_WIDTH), F32),
                        pltpu.VMEM((g, GDN_WIDTH, GDN_WIDTH), F32),
                        pltpu.VMEM((g, T_BLK, GDN_WIDTH), F32)],
        compiler_params=pltpu.CompilerParams(dimension_semantics=("arbitrary", "arbitrary"),
                                             vmem_limit_bytes=VMEM_LIMIT),
        name="mixers",
    )(by_seq(zh), *hg_tabs, by_seq(zr), cos_t, sin_t, *ret_tabs[2:], by_seq(zg), by_seq(zab),
      *gdn_tabs, hm, bd)
    return [o.reshape(n, o.shape[-1]) for o in outs]


def _outproj_kernel(ohg_ref, oret_ref, ogdn_ref, x_ref, mod_ref, g_ref, w_ref, rw_ref, rb_ref,
                    su_ref, x1_ref, hn_ref, ri_ref, gates_ref, cnt_ref, run_ref):
    @pl.when(pl.program_id(0) == 0)
    def _():
        run_ref[...] = jnp.zeros_like(run_ref)

    rw_hi = rw_ref[0:LANES, :]
    rw_lo = rw_ref[LANES:2 * LANES, :]
    eid = lax.broadcasted_iota(jnp.int32, (LANES, ROUTE_SLAB), 0)
    routed = {}

    def slab_stages(s):
        rows = slice(s * ROUTE_SLAB, (s + 1) * ROUTE_SLAB)
        y = (_dot(ohg_ref[rows, :], w_ref[0:HG_WIDTH, :])
             + _dot(oret_ref[rows, :], w_ref[HG_WIDTH:HG_WIDTH + RET_WIDTH, :])
             + _dot(ogdn_ref[rows, :], w_ref[HG_WIDTH + RET_WIDTH:, :]))
        yield
        x1 = x_ref[rows, :] + mod_ref[0, 0:1, :] * y
        x1_ref[rows, :] = x1
        n = x1 * lax.rsqrt(jnp.mean(x1 * x1, axis=-1, keepdims=True) + NORM_EPS)
        hn = (n * g_ref[0:1, :]) * (1.0 + mod_ref[0, 1:2, :]) + mod_ref[0, 2:3, :]
        hn_ref[rows, :] = _pack_halves(hn)
        yield
        hn_hi, hn_lo = _split2(hn)
        logits = (_dot_nt(rw_hi, hn_hi) + _dot_nt(rw_lo, hn_hi) + _dot_nt(rw_hi, hn_lo)
                  + rb_ref[:, rows])
        yield
        work = jnp.where(eid < N_EXPERTS, logits, -jnp.inf)
        vals, idxs = [], []
        multihot = jnp.zeros((LANES, ROUTE_SLAB), F32)
        for kk in range(TOP_K):
            mx = jnp.max(work, axis=0, keepdims=True)
            ix = jnp.min(jnp.where(work == mx, eid, LANES), axis=0, keepdims=True)
            sel = eid == ix
            multihot = jnp.where(sel, 1.0, multihot)
            work = jnp.where(sel, -jnp.inf, work)
            vals.append(mx)
            idxs.append(ix)
            if kk % 2 == 1:
                yield
        routed[s] = (vals, idxs, multihot)

    n_slabs = TM // ROUTE_SLAB
    active = [slab_stages(s) for s in range(n_slabs)]
    while active:
        for gen in list(active):
            if next(gen, _DONE) is _DONE:
                active.remove(gen)

    multihot = jnp.concatenate([routed[s][2] for s in range(n_slabs)], axis=1)
    run = run_ref[...]
    before = _dot(_bf(multihot), su_ref[...]) + jnp.concatenate([run] * (TM // LANES), axis=1)
    run = run + jnp.sum(multihot, axis=1, keepdims=True)
    run_ref[...] = run
    cnt_ref[...] = run

    for s in range(n_slabs):
        vals, idxs, _ = routed[s]
        cols = slice(s * ROUTE_SLAB, (s + 1) * ROUTE_SLAB)
        ex = [jnp.exp(vv - vals[0]) for vv in vals]
        den = ex[0] + ex[1] + ex[2] + ex[3]
        ranks = [jnp.sum(jnp.where(eid == ix, before[:, cols], 0.0), axis=0, keepdims=True) for ix in idxs]
        ri_ref[:, cols] = jnp.concatenate(idxs + [r.astype(jnp.int32) for r in ranks], axis=0)
        gates_ref[:, cols] = jnp.concatenate([e / den for e in ex]
                                             + [jnp.zeros((TOP_K, ROUTE_SLAB), F32)], axis=0)


def _outproj(ohg, oret, ogdn, x2d, modv, gpar, w_bf, rw, rb, su, tiles_per_batch):
    n = x2d.shape[0]
    row = lambda i: (i, 0)
    col = lambda i: (0, i)
    const = lambda i: (0, 0)
    return pl.pallas_call(
        _outproj_kernel,
        grid=(n // TM,),
        in_specs=[pl.BlockSpec((TM, HG_WIDTH), row), pl.BlockSpec((TM, RET_WIDTH), row),
                  pl.BlockSpec((TM, GDN_WIDTH), row), pl.BlockSpec((TM, D_MODEL), row),
                  pl.BlockSpec((1, SUBLANES, D_MODEL), lambda i: (i // tiles_per_batch, 0, 0)),
                  pl.BlockSpec((SUBLANES, D_MODEL), const),
                  pl.BlockSpec((D_MODEL, D_MODEL), const),
                  pl.BlockSpec((2 * LANES, D_MODEL), const),
                  pl.BlockSpec((LANES, TM), const),
                  pl.BlockSpec((TM, TM), const)],
        out_specs=[pl.BlockSpec((TM, D_MODEL), row), pl.BlockSpec((TM, D_MODEL // 2), row),
                   pl.BlockSpec((2 * TOP_K, TM), col), pl.BlockSpec((2 * TOP_K, TM), col),
                   pl.BlockSpec((LANES, LANES), const)],
        out_shape=[jax.ShapeDtypeStruct((n, D_MODEL), F32),
                   jax.ShapeDtypeStruct((n, D_MODEL // 2), jnp.uint32),
                   jax.ShapeDtypeStruct((2 * TOP_K, n), jnp.int32),
                   jax.ShapeDtypeStruct((2 * TOP_K, n), F32),
                   jax.ShapeDtypeStruct((LANES, LANES), F32)],
        scratch_shapes=[pltpu.VMEM((LANES, LANES), F32)],
        compiler_params=pltpu.CompilerParams(dimension_semantics=("arbitrary",),
                                             vmem_limit_bytes=VMEM_LIMIT),
        name="outproj_router",
    )(ohg, oret, ogdn, x2d, modv, gpar, w_bf, rw, rb, su)


def _expert_kernel(be_ref, nu_ref, x_ref, w1_ref, b1_ref, w2_ref, b2_ref, y_ref, w1b_ref, w2b_ref):
    i = pl.program_id(0)
    prev = be_ref[jnp.maximum(i - 1, 0)]
    fresh = jnp.logical_or(i == 0, be_ref[i] != prev)

    @pl.when(fresh)
    def _():
        w1b_ref[...] = _bf(w1_ref[0])
        w2b_ref[...] = _bf(w2_ref[0])

    @pl.when(i < nu_ref[0])
    def _():
        half = D_MODEL // 2

        def slab_stages(s):
            rows = slice(s * EXPERT_SLAB, (s + 1) * EXPERT_SLAB)
            x_lo, x_hi = _unpack_halves(x_ref[rows, :])
            hid = (_dot(_bf(x_lo), w1b_ref[0:half, :]) + _dot(_bf(x_hi), w1b_ref[half:, :])
                   + b1_ref[0])
            yield
            x_glu = jnp.minimum(hid[:, 0:D_FF], SWIGLU_LIMIT)
            x_lin = jnp.clip(hid[:, D_FF:], -SWIGLU_LIMIT, SWIGLU_LIMIT)
            act = x_glu * _sigmoid(SWIGLU_ALPHA * x_glu) * (x_lin + 1.0)
            yield
            y_ref[rows, :] = _pack_halves(_dot(_bf(act), w2b_ref[...]) + b2_ref[0])

        active = [slab_stages(s) for s in range(R_BLK // EXPERT_SLAB)]
        while active:
            for gen in list(active):
                if next(gen, _DONE) is _DONE:
                    active.remove(gen)


def _experts(block_e, n_used, xs, w1, b1, w2, b2):
    p = xs.shape[0]
    n_blocks = p // R_BLK
    ne = w1.shape[0] * w1.shape[1]
    w1 = w1.reshape(ne, D_MODEL, 2 * D_FF)
    w2 = w2.reshape(ne, D_FF, D_MODEL)
    rowmap = lambda i, be, nu: (jnp.minimum(i, nu[0] - 1), 0)
    emap = lambda i, be, nu: (be[i], 0, 0)
    grid_spec = pltpu.PrefetchScalarGridSpec(
        num_scalar_prefetch=2,
        grid=(n_blocks,),
        in_specs=[pl.BlockSpec((R_BLK, D_MODEL // 2), rowmap),
                  pl.BlockSpec((1, D_MODEL, 2 * D_FF), emap),
                  pl.BlockSpec((1, 1, 2 * D_FF), emap),
                  pl.BlockSpec((1, D_FF, D_MODEL), emap),
                  pl.BlockSpec((1, 1, D_MODEL), emap)],
        out_specs=pl.BlockSpec((R_BLK, D_MODEL // 2), rowmap),
        scratch_shapes=[pltpu.VMEM((D_MODEL, 2 * D_FF), BF16), pltpu.VMEM((D_FF, D_MODEL), BF16)],
    )
    return pl.pallas_call(
        _expert_kernel,
        grid_spec=grid_spec,
        out_shape=jax.ShapeDtypeStruct((p, D_MODEL // 2), jnp.uint32),
        compiler_params=pltpu.CompilerParams(dimension_semantics=("arbitrary",),
                                             vmem_limit_bytes=VMEM_LIMIT),
        name="experts",
    )(block_e, n_used, xs, w1, b1.reshape(ne, 1, 2 * D_FF), w2, b2.reshape(ne, 1, D_MODEL))


def _sc_mesh():
    return plsc.VectorSubcoreMesh(core_axis_name="c", subcore_axis_name="s")


def _sc_scatter_rows(x, pos, p_rows):
    mesh = _sc_mesh()
    n, w = x.shape
    per_worker = n // (mesh.num_cores * mesh.num_subcores)
    assert per_worker % SC_ROWS == 0 and pos.shape == (TOP_K, n)

    @functools.partial(
        pl.kernel, out_type=jax.ShapeDtypeStruct((p_rows, w), x.dtype), mesh=mesh,
        scratch_types=[pltpu.VMEM((SC_ROWS,), jnp.int32)] * TOP_K
        + [pltpu.VMEM((SC_ROWS, w), x.dtype), pltpu.SemaphoreType.DMA, pltpu.SemaphoreType.DMA],
        name="dispatch_rows")
    def scatter(x_hbm, p_hbm, o_hbm, i0, i1, i2, i3, rows_v, sem_in, sem_out):
        idx = (i0, i1, i2, i3)
        worker = lax.axis_index("s") * mesh.num_cores + lax.axis_index("c")

        @pl.loop(0, per_worker // SC_ROWS)
        def _(g):
            base = pl.multiple_of(worker * per_worker + g * SC_ROWS, SC_ROWS)
            loads = [pltpu.async_copy(x_hbm.at[pl.ds(base, SC_ROWS)], rows_v, sem_in)]
            loads += [pltpu.async_copy(p_hbm.at[kk, pl.ds(base, SC_ROWS)], idx[kk], sem_in)
                      for kk in range(TOP_K)]
            for cp in loads:
                cp.wait()
            stores = [pltpu.async_copy(rows_v, o_hbm.at[idx[kk]], sem_out) for kk in range(TOP_K)]
            for cp in stores:
                cp.wait()

    return scatter(x, pos)


def _sc_gather_rows(table, idx):
    mesh = _sc_mesh()
    n_idx = idx.shape[0]
    w = table.shape[1]
    per_worker = n_idx // (mesh.num_cores * mesh.num_subcores)
    assert per_worker % SC_ROWS == 0

    @functools.partial(
        pl.kernel, out_type=jax.ShapeDtypeStruct((n_idx, w), table.dtype), mesh=mesh,
        scratch_types=[pltpu.VMEM((SC_ROWS,), jnp.int32), pltpu.VMEM((SC_ROWS, w), table.dtype),
                       pltpu.SemaphoreType.DMA],
        name="combine_rows")
    def gather(t_hbm, i_hbm, o_hbm, idx_v, rows_v, sem):
        worker = lax.axis_index("s") * mesh.num_cores + lax.axis_index("c")

        @pl.loop(0, per_worker // SC_ROWS)
        def _(g):
            base = pl.multiple_of(worker * per_worker + g * SC_ROWS, SC_ROWS)
            pltpu.sync_copy(i_hbm.at[pl.ds(base, SC_ROWS)], idx_v)
            pltpu.async_copy(t_hbm.at[idx_v], rows_v, sem).wait()
            pltpu.sync_copy(rows_v, o_hbm.at[pl.ds(base, SC_ROWS)])

    return gather(table, idx)


def _combine_kernel(x_ref, y0_ref, y1_ref, y2_ref, y3_ref, gates_ref, mod_ref, g_ref, o_ref, *, final):
    gt = gates_ref[...]
    half = D_MODEL // 2
    acc_lo, acc_hi = None, None
    for kk, y_ref in enumerate((y0_ref, y1_ref, y2_ref, y3_ref)):
        lo, hi = _unpack_halves(y_ref[0])
        g = gt[:, kk:kk + 1]
        acc_lo = g * lo if acc_lo is None else acc_lo + g * lo
        acc_hi = g * hi if acc_hi is None else acc_hi + g * hi
    x_lo = x_ref[:, 0:half] + mod_ref[0, 3:4, 0:half] * acc_lo
    x_hi = x_ref[:, half:] + mod_ref[0, 3:4, half:] * acc_hi
    if final:
        ssq = (jnp.sum(x_lo * x_lo, axis=-1, keepdims=True)
               + jnp.sum(x_hi * x_hi, axis=-1, keepdims=True))
        r = lax.rsqrt(ssq * (1.0 / D_MODEL) + NORM_EPS)
        x_lo = x_lo * r * g_ref[1:2, 0:half]
        x_hi = x_hi * r * g_ref[1:2, half:]
    o_ref[:, 0:half] = x_lo
    o_ref[:, half:] = x_hi


def _combine(x, yg, gates, modv, gpar, tiles_per_batch, final, piece):
    n = x.shape[0]
    tiles = n // TM // COMBINE_PIECES
    first = piece * tiles
    row = lambda i: (first + i, 0)
    ysp = [pl.BlockSpec((1, TM, D_MODEL // 2), (lambda i, kk=kk: (kk, i, 0))) for kk in range(TOP_K)]
    return pl.pallas_call(
        functools.partial(_combine_kernel, final=final),
        grid=(tiles,),
        in_specs=[pl.BlockSpec((TM, D_MODEL), row)] + ysp + [
            pl.BlockSpec((TM, 2 * TOP_K), row),
            pl.BlockSpec((1, SUBLANES, D_MODEL), lambda i: ((first + i) // tiles_per_batch, 0, 0)),
            pl.BlockSpec((SUBLANES, D_MODEL), lambda i: (0, 0))],
        out_specs=pl.BlockSpec((TM, D_MODEL), row),
        out_shape=jax.ShapeDtypeStruct((n, D_MODEL), F32),
        input_output_aliases={0: 0},
        compiler_params=pltpu.CompilerParams(dimension_semantics=("arbitrary",),
                                             vmem_limit_bytes=VMEM_LIMIT),
        name="combine",
    )(x, yg, yg, yg, yg, gates, modv, gpar)


def _widen_w_in(w):
    pad = jnp.zeros((w.shape[0], Z_W - w.shape[1]), w.dtype)
    return _bf(jnp.concatenate([w, pad], axis=1))


def _pad_rows(a, rows=SUBLANES):
    return jnp.concatenate([a, jnp.zeros((rows - a.shape[0],) + a.shape[1:], a.dtype)], axis=0)


def kernel(x, c, ada_w, ada_b, norm1_g, norm2_g, w_in, w_out, hg_lb_logits, hg_norm_g, ret_norm_g,
           gdn_conv_w, gdn_A_log, gdn_dt_bias, gdn_norm_g, router_w, router_b, exp_w1, exp_b1,
           exp_w2, exp_b2, final_norm_g):
    batch, seq, d = x.shape
    depth = ada_w.shape[0]
    n = batch * seq
    steps = seq // T_BLK
    tiles_per_batch = seq // TM
    nk = n * TOP_K
    n_blocks = nk // R_BLK + N_EXPERTS
    p_rows = n_blocks * R_BLK

    lv_np = _level_matrix(T_BLK)
    lv = jnp.asarray(_level_matrix(HG_SUB))
    f_np = _hgrn_exponent_matrix(HG_SUB)
    f_mat = jnp.asarray(np.concatenate([f_np, f_np], axis=1), BF16)
    gm = jnp.asarray(_gdn_masks(T_BLK))
    bd = jnp.asarray(_block_diag_mask(T_BLK, GDN_CHUNK))
    tril = jnp.asarray(_block_diag_mask(T_BLK, GDN_CHUNK) * (lv_np >= 0), BF16)
    hm = jnp.asarray(_head_masks(RET_WIDTH, RET_HEADS))
    su = jnp.asarray(np.triu(np.ones((TM, TM), np.float32), 1), BF16)
    eab_np = np.zeros((LANES, 2 * GDN_WIDTH), np.float32)
    for h in range(GDN_HEADS):
        eab_np[h, h * GDN_DK:(h + 1) * GDN_DK] = 1.0
        eab_np[GDN_HEADS + h, GDN_WIDTH + h * GDN_DK:GDN_WIDTH + (h + 1) * GDN_DK] = 1.0
    eab = jnp.asarray(eab_np, BF16)

    half = RET_DK // 2
    inv = ROPE_BASE ** (-jnp.linspace(0.0, 1.0, half, dtype=F32))
    ang = jnp.arange(seq, dtype=F32)[:, None] * inv[None, :]
    cos_t = jnp.tile(jnp.cos(ang), (1, 2 * RET_HEADS))
    first_half = jnp.asarray((np.arange(RET_WIDTH) % RET_DK < half).astype(np.float32))[None, :]
    sin_t = jnp.tile(jnp.sin(ang), (1, 2 * RET_HEADS)) * (1.0 - 2.0 * first_half)
    log_g = jnp.log1p(-jnp.exp2(-5.0 - jnp.arange(RET_HEADS, dtype=F32)))
    jj = jnp.arange(T_BLK, dtype=F32)
    diff = jj[:, None] - jj[None, :]
    dmat = jnp.where(diff[None] >= 0, jnp.exp(diff[None] * log_g[:, None, None]), 0.0)
    lg_lane = jnp.repeat(log_g, RET_DK)[None, :]
    qw = jnp.exp(lg_lane * (jj[:, None] + 1.0))
    kw = jnp.exp(lg_lane * (T_BLK - 1.0 - jj[:, None]))
    cdec = jnp.exp(T_BLK * lg_lane)

    lb = jnp.cumsum(jax.nn.softmax(hg_lb_logits.astype(F32), axis=0), axis=0)
    lb = jnp.maximum(lb - lb[0], 0.0)
    c_pad = _pad_rows(c.astype(F32))
    mod = _adaln(c_pad, ada_w, ada_b)[:, :batch, :]

    x2d = x.reshape(n, d)
    out = None
    for l in range(depth):
        sh1, sc1, gt1, sh2, sc2, gt2 = [mod[l][:, i * d:(i + 1) * d] for i in range(6)]
        zeros = jnp.zeros_like(sh1)
        modv_a = jnp.stack([sc1, sh1, zeros, zeros, zeros, zeros, zeros, zeros], axis=1)
        modv_c = jnp.stack([gt1, sc2, sh2, gt2, zeros, zeros, zeros, zeros], axis=1)
        gpar_a = _pad_rows(norm1_g[l][None, :])
        gpar_c = _pad_rows(jnp.stack([norm2_g[l], final_norm_g], axis=0))

        zh, zr, zg, zab = _inproj(x2d, modv_a, gpar_a, _widen_w_in(w_in[l]), seq)

        hg_par = _pad_rows(jnp.stack([jnp.log(lb[l]), jnp.log1p(-lb[l]), 1.0 - lb[l],
                                      hg_norm_g[l].reshape(-1)], axis=0))
        ret_par = _pad_rows(jnp.concatenate([cdec, ret_norm_g[l].reshape(1, -1), first_half], axis=0))
        gdn_par = _pad_rows(jnp.stack([jnp.repeat(-jnp.exp(gdn_A_log[l].astype(F32)), GDN_DK),
                                       jnp.repeat(gdn_dt_bias[l].astype(F32), GDN_DK),
                                       gdn_norm_g[l].reshape(-1)], axis=0))
        o_hg, o_ret, o_gdn = _mixers(
            zh, zr, zg, zab,
            (f_mat, lv, hg_par),
            (cos_t, sin_t, dmat, qw, kw, ret_par),
            (_pad_rows(gdn_conv_w[l].astype(F32)), gdn_par, eab, gm, tril),
            hm, bd, batch, steps)

        rw_f = jnp.concatenate([router_w[l].T, jnp.zeros((LANES - N_EXPERTS, d), F32)], axis=0)
        rw = jnp.concatenate(_split2(rw_f), axis=0)
        rb = jnp.broadcast_to(jnp.concatenate([router_b[l], jnp.zeros((LANES - N_EXPERTS,), F32)])[:, None],
                              (LANES, TM))
        x1, hn2, ri, gates_t, cnt = _outproj(o_hg, o_ret, o_gdn, x2d, modv_c, gpar_c, _bf(w_out[l]),
                                             rw, rb, su, tiles_per_batch)
        gates = gates_t.T

        counts = cnt[:N_EXPERTS, 0].astype(jnp.int32)
        padded = (counts + R_BLK - 1) // R_BLK * R_BLK
        pend = jnp.cumsum(padded)
        pstart = pend - padded
        n_used = (pend[-1] // R_BLK).astype(jnp.int32)
        blk_start = jnp.arange(n_blocks, dtype=jnp.int32) * R_BLK
        blk_start = jnp.minimum(blk_start, pend[-1] - R_BLK)
        block_e = jnp.sum(blk_start[:, None] >= pend[None, :], axis=1).astype(jnp.int32)
        eid = jnp.arange(N_EXPERTS, dtype=jnp.int32)
        pos_t = ri[TOP_K:] + jnp.sum(jnp.where(ri[:TOP_K, :, None] == eid, pstart, 0), axis=-1)
        xs = _sc_scatter_rows(hn2, pos_t, p_rows)
        y = _experts(block_e + l * N_EXPERTS, n_used.reshape(1), xs, exp_w1, exp_b1, exp_w2, exp_b2)
        n_piece = n // COMBINE_PIECES
        x2d = x1
        for p in range(COMBINE_PIECES):
            idx = pos_t[:, p * n_piece:(p + 1) * n_piece].reshape(-1)
            yg = _sc_gather_rows(y, idx).reshape(TOP_K, n_piece, d // 2)
            x2d = _combine(x2d, yg, gates, modv_c, gpar_c, tiles_per_batch, l == depth - 1, p)
    return x2d.reshape(batch, seq, d)
```

```python
import functools
import math

import numpy as np
import jax
import jax.numpy as jnp
from jax import lax
from jax.experimental import pallas as pl
from jax.experimental.pallas import tpu as pltpu
from jax.experimental.pallas import tpu_sc as plsc

F32 = jnp.float32
BF16 = jnp.bfloat16

D_MODEL = 1024
HG_HEADS, HG_DK = 4, 128
HG_WIDTH = HG_HEADS * HG_DK
RET_HEADS, RET_DK = 4, 64
RET_WIDTH = RET_HEADS * RET_DK
GDN_HEADS, GDN_DK = 4, 64
GDN_WIDTH = GDN_HEADS * GDN_DK
CONV_K = 4
ROPE_BASE = 10000.0
N_EXPERTS = 32
TOP_K = 4
D_FF = D_MODEL
SWIGLU_ALPHA = 1.702
SWIGLU_LIMIT = 7.0
NORM_EPS = 1e-6
L2_EPS = 1e-6
GDN_CHUNK = 64

LANES = 128
SUBLANES = 8
VMEM_LIMIT = 56 * 1024 * 1024

T_BLK = 256
SEQ_PER_STEP = 2
TM = 512
TM_IN = 1024
ROUTE_SLAB = 128
COMBINE_PIECES = 8
R_BLK = 528
EXPERT_SLAB = 176
SC_ROWS = 128
HG_SUB = 128
HG_LEVELS = 7

ZH_W = 4 * HG_WIDTH
ZR_W = 4 * RET_WIDTH
ZG_W = 4 * GDN_WIDTH
ZAB_W = LANES
Z_W = ZH_W + ZR_W + ZG_W + ZAB_W


def _dot(a, b):
    return jnp.dot(a, b, preferred_element_type=F32)


def _dot_nt(a, b):
    return lax.dot_general(a, b, (((1,), (1,)), ((), ())), preferred_element_type=F32)


def _dot_tn(a, b):
    return lax.dot_general(a, b, (((0,), (0,)), ((), ())), preferred_element_type=F32)


def _split2(x):
    hi = x.astype(BF16)
    return hi, (x - hi.astype(F32)).astype(BF16)


def _dot2_lhs01(c, x):
    hi, lo = _split2(x)
    return _dot(c, hi) + _dot(c, lo)


def _dot2_rhs01(x, c):
    hi, lo = _split2(x)
    return _dot(hi, c) + _dot(lo, c)


def _sigmoid(x):
    return 1.0 / (1.0 + jnp.exp(-x))


def _silu(x):
    return x * _sigmoid(x)


def _softplus(x):
    return jnp.maximum(x, 0.0) + jnp.log1p(jnp.exp(-jnp.abs(x)))


def _bf(x):
    return x.astype(BF16)


def _pack_halves(x):
    w = x.shape[1] // 2
    bits = lax.bitcast_convert_type(_bf(x).astype(F32), jnp.uint32)
    return (bits[:, :w] >> 16) | bits[:, w:]


def _unpack_halves(p):
    lo = lax.bitcast_convert_type(p << 16, F32)
    hi = lax.bitcast_convert_type(p & jnp.uint32(0xFFFF0000), F32)
    return lo, hi


def _level_matrix(t):
    i = np.arange(t)[:, None]
    j = np.arange(t)[None, :]
    x = i ^ j
    lv = np.floor(np.log2(np.maximum(x, 1))).astype(np.int32)
    lv = np.where(i == j, int(math.log2(t)), lv)
    lv = np.where(i < j, -1, lv)
    return lv.astype(np.int32)


def _hgrn_exponent_matrix(t):
    n_lev = int(math.log2(t))
    f = np.zeros((2 + n_lev, t, t), np.float32)
    u = np.arange(t)[None, :]
    r = np.arange(t)[:, None]
    f[0] = (u <= r)
    f[1] = (u > r)
    for l in range(n_lev):
        h = 1 << l
        base = (r // (2 * h)) * (2 * h)
        mid = base + h
        upper = (r - base) >= h
        f[2 + l] = np.where(upper, (u >= mid) & (u <= r), (u > r) & (u < mid))
    return f.reshape((2 + n_lev) * t, t)


def _gdn_masks(t):
    lv = _level_matrix(GDN_CHUNK)
    top = int(math.log2(GDN_CHUNK))
    incl = lv >= 0
    d8 = (lv >= 0) & (lv <= 2)
    merges = [(lv == l) for l in range(3, top)]
    eye = lv == top
    pats = np.stack([incl, d8] + merges + [eye]).astype(np.float32)
    return np.tile(pats, (1, t // GDN_CHUNK, GDN_HEADS))


def _block_diag_mask(t, blk):
    i = np.arange(t)
    return (i[:, None] // blk == i[None, :] // blk).astype(np.float32)


def _head_masks(width, heads):
    lane = np.arange(width)[None, :]
    m = np.zeros((SUBLANES, width), np.float32)
    for h in range(heads):
        m[h] = (lane // (width // heads) == h)[0]
    return m


def _adaln_kernel(c_ref, w_ref, b_ref, o_ref):
    cond = _silu(c_ref[...])
    o_ref[0] = jnp.dot(cond, w_ref[0], preferred_element_type=F32,
                       precision=lax.Precision.HIGHEST) + b_ref[0]


def _adaln(c_pad, ada_w, ada_b):
    depth, d, n6 = ada_w.shape
    tn = n6 // 4
    return pl.pallas_call(
        _adaln_kernel,
        grid=(depth, n6 // tn),
        in_specs=[pl.BlockSpec((SUBLANES, d), lambda l, j: (0, 0)),
                  pl.BlockSpec((1, d, tn), lambda l, j: (l, 0, j)),
                  pl.BlockSpec((1, 1, tn), lambda l, j: (l, 0, j))],
        out_specs=pl.BlockSpec((1, SUBLANES, tn), lambda l, j: (l, 0, j)),
        out_shape=jax.ShapeDtypeStruct((depth, SUBLANES, n6), F32),
        compiler_params=pltpu.CompilerParams(vmem_limit_bytes=VMEM_LIMIT),
        name="adaln",
    )(c_pad, ada_w, ada_b.reshape(depth, 1, n6))


def _inproj_kernel(x_ref, mod_ref, g_ref, w_ref, zh_ref, zr_ref, zg_ref, zab_ref):
    x = x_ref[...]
    y = x * lax.rsqrt(jnp.mean(x * x, axis=-1, keepdims=True) + NORM_EPS)
    hn = (y * g_ref[0:1, :]) * (1.0 + mod_ref[0, 0:1, :]) + mod_ref[0, 1:2, :]
    hb = _bf(hn)
    zh_ref[...] = _dot(hb, w_ref[:, 0:ZH_W])
    zr_ref[...] = _dot(hb, w_ref[:, ZH_W:ZH_W + ZR_W])
    zg_ref[...] = _dot(hb, w_ref[:, ZH_W + ZR_W:ZH_W + ZR_W + ZG_W])
    zab_ref[...] = _dot(hb, w_ref[:, ZH_W + ZR_W + ZG_W:Z_W])


def _inproj(x2d, modv, gpar, w_bf, seq):
    n = x2d.shape[0]
    row = lambda i: (i, 0)
    tiles_per_batch = seq // TM_IN
    return pl.pallas_call(
        _inproj_kernel,
        grid=(n // TM_IN,),
        in_specs=[pl.BlockSpec((TM_IN, D_MODEL), row),
                  pl.BlockSpec((1, SUBLANES, D_MODEL), lambda i: (i // tiles_per_batch, 0, 0)),
                  pl.BlockSpec((SUBLANES, D_MODEL), lambda i: (0, 0)),
                  pl.BlockSpec((D_MODEL, Z_W), lambda i: (0, 0), pipeline_mode=pl.Buffered(1))],
        out_specs=[pl.BlockSpec((TM_IN, ZH_W), row), pl.BlockSpec((TM_IN, ZR_W), row),
                   pl.BlockSpec((TM_IN, ZG_W), row), pl.BlockSpec((TM_IN, ZAB_W), row)],
        out_shape=[jax.ShapeDtypeStruct((n, ZH_W), F32), jax.ShapeDtypeStruct((n, ZR_W), F32),
                   jax.ShapeDtypeStruct((n, ZG_W), F32), jax.ShapeDtypeStruct((n, ZAB_W), F32)],
        compiler_params=pltpu.CompilerParams(dimension_semantics=("arbitrary",),
                                             vmem_limit_bytes=VMEM_LIMIT),
        name="inproj",
    )(x2d, modv, gpar, w_bf)


def _hgrn_stages(zh_ref, f_ref, lv_ref, par_ref, o_ref, state_ref, ex_ref):
    W = HG_WIDTH
    hq = zh_ref[:, 0:W]
    hf = zh_ref[:, W:2 * W]
    loglb = par_ref[0:1, :]
    log1mlb = par_ref[1:2, :]
    onemlb = par_ref[2:3, :]

    q = _silu(hq)
    e = jnp.exp(-jnp.abs(hf))
    inv = 1.0 / (1.0 + e)
    k = onemlb * (jnp.where(hf >= 0, e, 1.0) * inv)
    logsig = jnp.minimum(hf, 0.0) - jnp.log1p(e)
    c = log1mlb + logsig
    lf = jnp.maximum(loglb, c) + jnp.log1p(jnp.exp(-jnp.abs(loglb - c)))
    yield

    lv = lv_ref[...]
    t = HG_SUB
    for sb in range(T_BLK // HG_SUB):
        rows = slice(sb * t, (sb + 1) * t)
        ex_ref[sb] = jnp.exp(_dot(f_ref[...], jnp.concatenate(_split2(lf[rows]), axis=0)))
        yield
        for h in range(HG_HEADS):
            cs = slice(h * HG_DK, (h + 1) * HG_DK)
            qh = q[rows, cs]
            kh = k[rows, cs]
            vh = _bf(zh_ref[rows, 2 * W + h * HG_DK:2 * W + (h + 1) * HG_DK])
            qb = _bf(qh)
            kb = _bf(kh)
            s = jnp.where(lv == HG_LEVELS, _dot_nt(qb, kb), 0.0)
            for l in range(HG_LEVELS):
                el = _bf(ex_ref[sb, (2 + l) * t:(3 + l) * t, cs])
                s = jnp.where(lv == l, _dot_nt(qb * el, kb * el), s)
                if l % 2 == 1:
                    yield
            eb = ex_ref[sb, 0:t, cs]
            ebl = ex_ref[sb, t:2 * t, cs]
            st = state_ref[h]
            o = _dot(_bf(s), vh) + _dot_nt(_bf(qh * eb), _bf(st))
            upd = _dot_tn(vh, _bf(kh * ebl))
            state_ref[h] = st * ex_ref[sb, t - 1:t, cs] + upd
            ms = jnp.mean(o * o, axis=-1, keepdims=True)
            gate = _silu(zh_ref[rows, 3 * W + h * HG_DK:3 * W + (h + 1) * HG_DK])
            o_ref[rows, cs] = _bf(o * lax.rsqrt(ms + NORM_EPS) * par_ref[3:4, cs] * gate)
            yield


def _ret_stages(zr_ref, cos_ref, sin_ref, dmat_ref, qw_ref, kw_ref, par_ref, hm_ref, bd_ref,
                o_ref, state_ref):
    W = RET_WIDTH
    cos = cos_ref[...]
    sin = sin_ref[...]
    first_half = par_ref[2:3, :] > 0

    def rotary(t):
        swapped = jnp.where(first_half, pltpu.roll(t, W - RET_DK // 2, 1), pltpu.roll(t, RET_DK // 2, 1))
        return t * cos + swapped * sin

    q = rotary(zr_ref[:, 0:W])
    k = rotary(zr_ref[:, W:2 * W]) * (RET_DK ** -0.5)
    v = zr_ref[:, 2 * W:3 * W]
    kb = _bf(k)
    bd = bd_ref[...]
    yield

    s_parts = []
    v_parts = []
    for h in range(RET_HEADS):
        hm = hm_ref[h:h + 1, :]
        s_parts.append(_bf(_dot_nt(_bf(q * hm), kb) * dmat_ref[h]))
        v_parts.append(_bf(v * hm))
        yield
    st = state_ref[...]
    o = (_dot(jnp.concatenate(s_parts, axis=1), jnp.concatenate(v_parts, axis=0))
         + _dot(_bf(q * qw_ref[...]), _bf(st)))
    yield
    kv = _dot_tn(_bf(k * kw_ref[...]), _bf(v))
    state_ref[...] = st * par_ref[0:1, :] + bd * kv
    yield

    ms = _dot2_rhs01(o * o, _bf(bd)) * (1.0 / RET_DK)
    gate = _silu(zr_ref[:, 3 * W:4 * W])
    o_ref[...] = _bf(o * lax.rsqrt(ms + NORM_EPS) * par_ref[1:2, :] * gate)


def _gdn_stages(zg_ref, zab_ref, convw_ref, par_ref, eab_ref, gm_ref, bd_ref, tril_ref, hm_ref,
                o_ref, ext_ref, state_ref, obuf_ref):
    W = GDN_WIDTH
    t = T_BLK
    u = zg_ref[:, 0:3 * W]
    ext_ref[SUBLANES:SUBLANES + t, :] = u
    conv = (convw_ref[3:4, :] * u
            + convw_ref[2:3, :] * ext_ref[SUBLANES - 1:SUBLANES - 1 + t, :]
            + convw_ref[1:2, :] * ext_ref[SUBLANES - 2:SUBLANES - 2 + t, :]
            + convw_ref[0:1, :] * ext_ref[SUBLANES - 3:SUBLANES - 3 + t, :])
    ext_ref[0:SUBLANES, :] = u[t - SUBLANES:t, :]
    qkv = _silu(conv)
    q = qkv[:, 0:W]
    k = qkv[:, W:2 * W]
    v = qkv[:, 2 * W:3 * W]
    yield

    bd = bd_ref[...]
    bdb = _bf(bd)
    ab = zab_ref[...]
    a_exp = _dot2_rhs01(ab, eab_ref[:, 0:W])
    b_exp = _dot2_rhs01(ab, eab_ref[:, W:2 * W])
    qn = q * lax.rsqrt(_dot2_rhs01(q * q, bdb) + L2_EPS) * (GDN_DK ** -0.5)
    kn = k * lax.rsqrt(_dot2_rhs01(k * k, bdb) + L2_EPS)
    yield
    beta = _sigmoid(b_exp)
    g = par_ref[0:1, :] * _softplus(a_exp + par_ref[1:2, :])
    gc = _dot2_lhs01(tril_ref[...], g)
    gl = _dot2_lhs01(bdb, g)
    yield
    eg = jnp.exp(gc)
    vb = v * beta
    kbeta = kn * beta * eg
    qdec = qn * eg
    kdec = kn * jnp.exp(gl - gc)

    n_chunks = t // GDN_CHUNK
    hms = [_bf(jnp.broadcast_to(hm_ref[h:h + 1, :], (GDN_CHUNK, W))) for h in range(GDN_HEADS)]
    hms2 = [jnp.concatenate([m, m], axis=1) for m in hms]

    def chunk(a, c):
        return a[c * GDN_CHUNK:(c + 1) * GDN_CHUNK]

    def expand(y, masks):
        yb = _bf(y)
        return jnp.concatenate([yb * m for m in masks], axis=0)

    def blockprod(x, y, masks):
        xb = _bf(x)
        return jnp.concatenate([_dot(chunk(xb, c), expand(chunk(y, c), masks))
                                for c in range(n_chunks)], axis=0)

    gc_row = _dot2_lhs01(bdb, gc * gm_ref[5])
    yield
    rel = jnp.exp(jnp.where(gm_ref[0] > 0, gc - gc_row, -jnp.inf))

    knbeta = kn * beta
    kq = [_dot_nt(_bf(jnp.concatenate([chunk(knbeta, c), chunk(qn, c)], axis=0)),
                  expand(chunk(kn, c), hms)) for c in range(n_chunks)]
    yield
    m = jnp.concatenate([r[0:GDN_CHUNK] for r in kq], axis=0) * rel
    qk = jnp.concatenate([r[GDN_CHUNK:] for r in kq], axis=0) * rel

    d = m * gm_ref[1]
    d2 = blockprod(d, d, hms)
    yield
    d4 = blockprod(d2, d2, hms)
    dd2 = blockprod(d, d2, hms)
    yield
    x = d2 - d - dd2
    xd4 = blockprod(x, d4, hms)
    yield
    x = x + d4 + xd4
    for lvl in range(2, 5):
        lo = m * gm_ref[lvl]
        xl = blockprod(x, lo, hms)
        yield
        y = lo + xl
        yx = blockprod(y, x, hms)
        yield
        x = x - (y + yx)

    vk = jnp.concatenate([vb, kbeta], axis=1)
    wk = vk + blockprod(x, vk, hms2)
    yield
    w = wk[:, 0:W]
    kcum = wk[:, W:2 * W]
    ag = blockprod(qk, wk, hms2)
    yield
    a1 = ag[:, 0:W]
    qeff = qdec - ag[:, W:2 * W]

    for c in range(t // GDN_CHUNK):
        rows = slice(c * GDN_CHUNK, (c + 1) * GDN_CHUNK)
        st = state_ref[...]
        stb = _bf(st)
        vnew = w[rows] - _dot(_bf(kcum[rows]), stb)
        obuf_ref[rows, :] = _dot(_bf(qeff[rows]), stb) + a1[rows]
        yield
        upd = _dot_tn(_bf(kdec[rows]), _bf(vnew))
        last = eg[(c + 1) * GDN_CHUNK - 1:(c + 1) * GDN_CHUNK, :]
        state_ref[...] = st * last + bd * upd
        yield

    o = obuf_ref[...]
    ms = _dot2_rhs01(o * o, bdb) * (1.0 / GDN_DK)
    gate = _silu(zg_ref[:, 3 * W:4 * W])
    o_ref[...] = _bf(o * lax.rsqrt(ms + NORM_EPS) * par_ref[2:3, :] * gate)


_DONE = object()

def _mixers_kernel(zh_ref, f_ref, lv_ref, hpar_ref,
                   zr_ref, cos_ref, sin_ref, dmat_ref, qw_ref, kw_ref, rpar_ref,
                   zg_ref, zab_ref, convw_ref, gpar_ref, eab_ref, gm_ref, tril_ref, hm_ref, bd_ref,
                   ohg_ref, oret_ref, ogdn_ref,
                   hstate_ref, hex_ref, rstate_ref, gext_ref, gstate_ref, gobuf_ref):
    @pl.when(pl.program_id(1) == 0)
    def _():
        hstate_ref[...] = jnp.zeros_like(hstate_ref)
        rstate_ref[...] = jnp.zeros_like(rstate_ref)
        gstate_ref[...] = jnp.zeros_like(gstate_ref)
        gext_ref[:, 0:SUBLANES, :] = jnp.zeros((SEQ_PER_STEP, SUBLANES, 3 * GDN_WIDTH), F32)

    active = []
    for i in range(SEQ_PER_STEP):
        active.append((_gdn_stages(zg_ref.at[i], zab_ref.at[i], convw_ref, gpar_ref, eab_ref, gm_ref, bd_ref,
                                   tril_ref, hm_ref, ogdn_ref.at[i], gext_ref.at[i], gstate_ref.at[i],
                                   gobuf_ref.at[i]), 1))
    for i in range(SEQ_PER_STEP):
        active.append((_hgrn_stages(zh_ref.at[i], f_ref, lv_ref, hpar_ref, ohg_ref.at[i], hstate_ref.at[i],
                                    hex_ref.at[i]), 2))
    for i in range(SEQ_PER_STEP):
        active.append((_ret_stages(zr_ref.at[i], cos_ref, sin_ref, dmat_ref, qw_ref, kw_ref, rpar_ref, hm_ref,
                                   bd_ref, oret_ref.at[i], rstate_ref.at[i]), 1))
    while active:
        for entry in list(active):
            gen, per_round = entry
            for _ in range(per_round):
                if next(gen, _DONE) is _DONE:
                    active.remove(entry)
                    break


def _mixers(zh, zr, zg, zab, hg_tabs, ret_tabs, gdn_tabs, hm, bd, batch, steps):
    n = zh.shape[0]
    seq = n // batch
    g = SEQ_PER_STEP
    assert batch % g == 0

    def const(a):
        return pl.BlockSpec(a.shape, lambda b, j, nd=a.ndim: (0,) * nd)

    def rows(width):
        return pl.BlockSpec((g, T_BLK, width), lambda b, j: (b, j, 0))

    def by_seq(a):
        return a.reshape(batch, seq, a.shape[-1])

    pos_rows = pl.BlockSpec((T_BLK, RET_WIDTH), lambda b, j: (j, 0))
    cos_t, sin_t = ret_tabs[0], ret_tabs[1]
    in_specs = ([rows(ZH_W)] + [const(a) for a in hg_tabs]
                + [rows(ZR_W), pos_rows, pos_rows] + [const(a) for a in ret_tabs[2:]]
                + [rows(ZG_W), rows(ZAB_W)] + [const(a) for a in gdn_tabs] + [const(hm), const(bd)])
    outs = pl.pallas_call(
        _mixers_kernel,
        grid=(batch // g, steps),
        in_specs=in_specs,
        out_specs=[rows(HG_WIDTH), rows(RET_WIDTH), rows(GDN_WIDTH)],
        out_shape=[jax.ShapeDtypeStruct((batch, seq, HG_WIDTH), BF16),
                   jax.ShapeDtypeStruct((batch, seq, RET_WIDTH), BF16),
                   jax.ShapeDtypeStruct((batch, seq, GDN_WIDTH), BF16)],
        scratch_shapes=[pltpu.VMEM((g, HG_HEADS, HG_DK, HG_DK), F32),
                        pltpu.VMEM((g, T_BLK // HG_SUB, (2 + HG_LEVELS) * HG_SUB, HG_WIDTH), F32),
                        pltpu.VMEM((g, RET_WIDTH, RET_WIDTH), F32),
                        pltpu.VMEM((g, SUBLANES + T_BLK, 3 * GDN_WIDTH), F32),
                        pltpu.VMEM((g, GDN_WIDTH, GDN_WIDTH), F32),
                        pltpu.VMEM((g, T_BLK, GDN_WIDTH), F32)],
        compiler_params=pltpu.CompilerParams(dimension_semantics=("arbitrary", "arbitrary"),
                                             vmem_limit_bytes=VMEM_LIMIT),
        name="mixers",
    )(by_seq(zh), *hg_tabs, by_seq(zr), cos_t, sin_t, *ret_tabs[2:], by_seq(zg), by_seq(zab),
      *gdn_tabs, hm, bd)
    return [o.reshape(n, o.shape[-1]) for o in outs]


def _outproj_kernel(ohg_ref, oret_ref, ogdn_ref, x_ref, mod_ref, g_ref, w_ref, rw_ref, rb_ref,
                    su_ref, x1_ref, hn_ref, ri_ref, gates_ref, cnt_ref, run_ref):
    @pl.when(pl.program_id(0) == 0)
    def _():
        run_ref[...] = jnp.zeros_like(run_ref)

    rw_hi = rw_ref[0:LANES, :]
    rw_lo = rw_ref[LANES:2 * LANES, :]
    eid = lax.broadcasted_iota(jnp.int32, (LANES, ROUTE_SLAB), 0)
    routed = {}

    def slab_stages(s):
        rows = slice(s * ROUTE_SLAB, (s + 1) * ROUTE_SLAB)
        y = (_dot(ohg_ref[rows, :], w_ref[0:HG_WIDTH, :])
             + _dot(oret_ref[rows, :], w_ref[HG_WIDTH:HG_WIDTH + RET_WIDTH, :])
             + _dot(ogdn_ref[rows, :], w_ref[HG_WIDTH + RET_WIDTH:, :]))
        yield
        x1 = x_ref[rows, :] + mod_ref[0, 0:1, :] * y
        x1_ref[rows, :] = x1
        n = x1 * lax.rsqrt(jnp.mean(x1 * x1, axis=-1, keepdims=True) + NORM_EPS)
        hn = (n * g_ref[0:1, :]) * (1.0 + mod_ref[0, 1:2, :]) + mod_ref[0, 2:3, :]
        hn_ref[rows, :] = _pack_halves(hn)
        yield
        hn_hi, hn_lo = _split2(hn)
        logits = (_dot_nt(rw_hi, hn_hi) + _dot_nt(rw_lo, hn_hi) + _dot_nt(rw_hi, hn_lo)
                  + rb_ref[:, rows])
        yield
        work = jnp.where(eid < N_EXPERTS, logits, -jnp.inf)
        vals, idxs = [], []
        multihot = jnp.zeros((LANES, ROUTE_SLAB), F32)
        for kk in range(TOP_K):
            mx = jnp.max(work, axis=0, keepdims=True)
            ix = jnp.min(jnp.where(work == mx, eid, LANES), axis=0, keepdims=True)
            sel = eid == ix
            multihot = jnp.where(sel, 1.0, multihot)
            work = jnp.where(sel, -jnp.inf, work)
            vals.append(mx)
            idxs.append(ix)
            if kk % 2 == 1:
                yield
        routed[s] = (vals, idxs, multihot)

    n_slabs = TM // ROUTE_SLAB
    active = [slab_stages(s) for s in range(n_slabs)]
    while active:
        for gen in list(active):
            if next(gen, _DONE) is _DONE:
                active.remove(gen)

    multihot = jnp.concatenate([routed[s][2] for s in range(n_slabs)], axis=1)
    run = run_ref[...]
    before = _dot(_bf(multihot), su_ref[...]) + jnp.concatenate([run] * (TM // LANES), axis=1)
    run = run + jnp.sum(multihot, axis=1, keepdims=True)
    run_ref[...] = run
    cnt_ref[...] = run

    for s in range(n_slabs):
        vals, idxs, _ = routed[s]
        cols = slice(s * ROUTE_SLAB, (s + 1) * ROUTE_SLAB)
        ex = [jnp.exp(vv - vals[0]) for vv in vals]
        den = ex[0] + ex[1] + ex[2] + ex[3]
        ranks = [jnp.sum(jnp.where(eid == ix, before[:, cols], 0.0), axis=0, keepdims=True) for ix in idxs]
        ri_ref[:, cols] = jnp.concatenate(idxs + [r.astype(jnp.int32) for r in ranks], axis=0)
        gates_ref[:, cols] = jnp.concatenate([e / den for e in ex]
                                             + [jnp.zeros((TOP_K, ROUTE_SLAB), F32)], axis=0)


def _outproj(ohg, oret, ogdn, x2d, modv, gpar, w_bf, rw, rb, su, tiles_per_batch):
    n = x2d.shape[0]
    row = lambda i: (i, 0)
    col = lambda i: (0, i)
    const = lambda i: (0, 0)
    return pl.pallas_call(
        _outproj_kernel,
        grid=(n // TM,),
        in_specs=[pl.BlockSpec((TM, HG_WIDTH), row), pl.BlockSpec((TM, RET_WIDTH), row),
                  pl.BlockSpec((TM, GDN_WIDTH), row), pl.BlockSpec((TM, D_MODEL), row),
                  pl.BlockSpec((1, SUBLANES, D_MODEL), lambda i: (i // tiles_per_batch, 0, 0)),
                  pl.BlockSpec((SUBLANES, D_MODEL), const),
                  pl.BlockSpec((D_MODEL, D_MODEL), const),
                  pl.BlockSpec((2 * LANES, D_MODEL), const),
                  pl.BlockSpec((LANES, TM), const),
                  pl.BlockSpec((TM, TM), const)],
        out_specs=[pl.BlockSpec((TM, D_MODEL), row), pl.BlockSpec((TM, D_MODEL // 2), row),
                   pl.BlockSpec((2 * TOP_K, TM), col), pl.BlockSpec((2 * TOP_K, TM), col),
                   pl.BlockSpec((LANES, LANES), const)],
        out_shape=[jax.ShapeDtypeStruct((n, D_MODEL), F32),
                   jax.ShapeDtypeStruct((n, D_MODEL // 2), jnp.uint32),
                   jax.ShapeDtypeStruct((2 * TOP_K, n), jnp.int32),
                   jax.ShapeDtypeStruct((2 * TOP_K, n), F32),
                   jax.ShapeDtypeStruct((LANES, LANES), F32)],
        scratch_shapes=[pltpu.VMEM((LANES, LANES), F32)],
        compiler_params=pltpu.CompilerParams(dimension_semantics=("arbitrary",),
                                             vmem_limit_bytes=VMEM_LIMIT),
        name="outproj_router",
    )(ohg, oret, ogdn, x2d, modv, gpar, w_bf, rw, rb, su)


def _expert_kernel(be_ref, nu_ref, x_ref, w1_ref, b1_ref, w2_ref, b2_ref, y_ref, w1b_ref, w2b_ref):
    i = pl.program_id(0)
    prev = be_ref[jnp.maximum(i - 1, 0)]
    fresh = jnp.logical_or(i == 0, be_ref[i] != prev)

    @pl.when(fresh)
    def _():
        w1b_ref[...] = _bf(w1_ref[0])
        w2b_ref[...] = _bf(w2_ref[0])

    @pl.when(i < nu_ref[0])
    def _():
        half = D_MODEL // 2

        def slab_stages(s):
            rows = slice(s * EXPERT_SLAB, (s + 1) * EXPERT_SLAB)
            x_lo, x_hi = _unpack_halves(x_ref[rows, :])
            hid = (_dot(_bf(x_lo), w1b_ref[0:half, :]) + _dot(_bf(x_hi), w1b_ref[half:, :])
                   + b1_ref[0])
            yield
            x_glu = jnp.minimum(hid[:, 0:D_FF], SWIGLU_LIMIT)
            x_lin = jnp.clip(hid[:, D_FF:], -SWIGLU_LIMIT, SWIGLU_LIMIT)
            act = x_glu * _sigmoid(SWIGLU_ALPHA * x_glu) * (x_lin + 1.0)
            yield
            y_ref[rows, :] = _pack_halves(_dot(_bf(act), w2b_ref[...]) + b2_ref[0])

        active = [slab_stages(s) for s in range(R_BLK // EXPERT_SLAB)]
        while active:
            for gen in list(active):
                if next(gen, _DONE) is _DONE:
                    active.remove(gen)


def _experts(block_e, n_used, xs, w1, b1, w2, b2):
    p = xs.shape[0]
    n_blocks = p // R_BLK
    ne = w1.shape[0] * w1.shape[1]
    w1 = w1.reshape(ne, D_MODEL, 2 * D_FF)
    w2 = w2.reshape(ne, D_FF, D_MODEL)
    rowmap = lambda i, be, nu: (jnp.minimum(i, nu[0] - 1), 0)
    emap = lambda i, be, nu: (be[i], 0, 0)
    grid_spec = pltpu.PrefetchScalarGridSpec(
        num_scalar_prefetch=2,
        grid=(n_blocks,),
        in_specs=[pl.BlockSpec((R_BLK, D_MODEL // 2), rowmap),
                  pl.BlockSpec((1, D_MODEL, 2 * D_FF), emap),
                  pl.BlockSpec((1, 1, 2 * D_FF), emap),
                  pl.BlockSpec((1, D_FF, D_MODEL), emap),
                  pl.BlockSpec((1, 1, D_MODEL), emap)],
        out_specs=pl.BlockSpec((R_BLK, D_MODEL // 2), rowmap),
        scratch_shapes=[pltpu.VMEM((D_MODEL, 2 * D_FF), BF16), pltpu.VMEM((D_FF, D_MODEL), BF16)],
    )
    return pl.pallas_call(
        _expert_kernel,
        grid_spec=grid_spec,
        out_shape=jax.ShapeDtypeStruct((p, D_MODEL // 2), jnp.uint32),
        compiler_params=pltpu.CompilerParams(dimension_semantics=("arbitrary",),
                                             vmem_limit_bytes=VMEM_LIMIT),
        name="experts",
    )(block_e, n_used, xs, w1, b1.reshape(ne, 1, 2 * D_FF), w2, b2.reshape(ne, 1, D_MODEL))


def _sc_mesh():
    return plsc.VectorSubcoreMesh(core_axis_name="c", subcore_axis_name="s")


def _sc_scatter_rows(x, pos, p_rows):
    mesh = _sc_mesh()
    n, w = x.shape
    per_worker = n // (mesh.num_cores * mesh.num_subcores)
    assert per_worker % SC_ROWS == 0 and pos.shape == (TOP_K, n)

    @functools.partial(
        pl.kernel, out_type=jax.ShapeDtypeStruct((p_rows, w), x.dtype), mesh=mesh,
        scratch_types=[pltpu.VMEM((SC_ROWS,), jnp.int32)] * TOP_K
        + [pltpu.VMEM((SC_ROWS, w), x.dtype), pltpu.SemaphoreType.DMA, pltpu.SemaphoreType.DMA],
        name="dispatch_rows")
    def scatter(x_hbm, p_hbm, o_hbm, i0, i1, i2, i3, rows_v, sem_in, sem_out):
        idx = (i0, i1, i2, i3)
        worker = lax.axis_index("s") * mesh.num_cores + lax.axis_index("c")

        @pl.loop(0, per_worker // SC_ROWS)
        def _(g):
            base = pl.multiple_of(worker * per_worker + g * SC_ROWS, SC_ROWS)
            loads = [pltpu.async_copy(x_hbm.at[pl.ds(base, SC_ROWS)], rows_v, sem_in)]
            loads += [pltpu.async_copy(p_hbm.at[kk, pl.ds(base, SC_ROWS)], idx[kk], sem_in)
                      for kk in range(TOP_K)]
            for cp in loads:
                cp.wait()
            stores = [pltpu.async_copy(rows_v, o_hbm.at[idx[kk]], sem_out) for kk in range(TOP_K)]
            for cp in stores:
                cp.wait()

    return scatter(x, pos)


def _sc_gather_rows(table, idx):
    mesh = _sc_mesh()
    n_idx = idx.shape[0]
    w = table.shape[1]
    per_worker = n_idx // (mesh.num_cores * mesh.num_subcores)
    assert per_worker % SC_ROWS == 0

    @functools.partial(
        pl.kernel, out_type=jax.ShapeDtypeStruct((n_idx, w), table.dtype), mesh=mesh,
        scratch_types=[pltpu.VMEM((SC_ROWS,), jnp.int32), pltpu.VMEM((SC_ROWS, w), table.dtype),
                       pltpu.SemaphoreType.DMA],
        name="combine_rows")
    def gather(t_hbm, i_hbm, o_hbm, idx_v, rows_v, sem):
        worker = lax.axis_index("s") * mesh.num_cores + lax.axis_index("c")

        @pl.loop(0, per_worker // SC_ROWS)
        def _(g):
            base = pl.multiple_of(worker * per_worker + g * SC_ROWS, SC_ROWS)
            pltpu.sync_copy(i_hbm.at[pl.ds(base, SC_ROWS)], idx_v)
            pltpu.async_copy(t_hbm.at[idx_v], rows_v, sem).wait()
            pltpu.sync_copy(rows_v, o_hbm.at[pl.ds(base, SC_ROWS)])

    return gather(table, idx)


def _combine_kernel(x_ref, y0_ref, y1_ref, y2_ref, y3_ref, gates_ref, mod_ref, g_ref, o_ref, *, final):
    gt = gates_ref[...]
    half = D_MODEL // 2
    acc_lo, acc_hi = None, None
    for kk, y_ref in enumerate((y0_ref, y1_ref, y2_ref, y3_ref)):
        lo, hi = _unpack_halves(y_ref[0])
        g = gt[:, kk:kk + 1]
        acc_lo = g * lo if acc_lo is None else acc_lo + g * lo
        acc_hi = g * hi if acc_hi is None else acc_hi + g * hi
    x_lo = x_ref[:, 0:half] + mod_ref[0, 3:4, 0:half] * acc_lo
    x_hi = x_ref[:, half:] + mod_ref[0, 3:4, half:] * acc_hi
    if final:
        ssq = (jnp.sum(x_lo * x_lo, axis=-1, keepdims=True)
               + jnp.sum(x_hi * x_hi, axis=-1, keepdims=True))
        r = lax.rsqrt(ssq * (1.0 / D_MODEL) + NORM_EPS)
        x_lo = x_lo * r * g_ref[1:2, 0:half]
        x_hi = x_hi * r * g_ref[1:2, half:]
    o_ref[:, 0:half] = x_lo
    o_ref[:, half:] = x_hi


def _combine(x, yg, gates, modv, gpar, tiles_per_batch, final, piece):
    n = x.shape[0]
    tiles = n // TM // COMBINE_PIECES
    first = piece * tiles
    row = lambda i: (first + i, 0)
    ysp = [pl.BlockSpec((1, TM, D_MODEL // 2), (lambda i, kk=kk: (kk, i, 0))) for kk in range(TOP_K)]
    return pl.pallas_call(
        functools.partial(_combine_kernel, final=final),
        grid=(tiles,),
        in_specs=[pl.BlockSpec((TM, D_MODEL), row)] + ysp + [
            pl.BlockSpec((TM, 2 * TOP_K), row),
            pl.BlockSpec((1, SUBLANES, D_MODEL), lambda i: ((first + i) // tiles_per_batch, 0, 0)),
            pl.BlockSpec((SUBLANES, D_MODEL), lambda i: (0, 0))],
        out_specs=pl.BlockSpec((TM, D_MODEL), row),
        out_shape=jax.ShapeDtypeStruct((n, D_MODEL), F32),
        input_output_aliases={0: 0},
        compiler_params=pltpu.CompilerParams(dimension_semantics=("arbitrary",),
                                             vmem_limit_bytes=VMEM_LIMIT),
        name="combine",
    )(x, yg, yg, yg, yg, gates, modv, gpar)


def _widen_w_in(w):
    pad = jnp.zeros((w.shape[0], Z_W - w.shape[1]), w.dtype)
    return _bf(jnp.concatenate([w, pad], axis=1))


def _pad_rows(a, rows=SUBLANES):
    return jnp.concatenate([a, jnp.zeros((rows - a.shape[0],) + a.shape[1:], a.dtype)], axis=0)


def kernel(x, c, ada_w, ada_b, norm1_g, norm2_g, w_in, w_out, hg_lb_logits, hg_norm_g, ret_norm_g,
           gdn_conv_w, gdn_A_log, gdn_dt_bias, gdn_norm_g, router_w, router_b, exp_w1, exp_b1,
           exp_w2, exp_b2, final_norm_g):
    batch, seq, d = x.shape
    depth = ada_w.shape[0]
    n = batch * seq
    steps = seq // T_BLK
    tiles_per_batch = seq // TM
    nk = n * TOP_K
    n_blocks = nk // R_BLK + N_EXPERTS
    p_rows = n_blocks * R_BLK

    lv_np = _level_matrix(T_BLK)
    lv = jnp.asarray(_level_matrix(HG_SUB))
    f_np = _hgrn_exponent_matrix(HG_SUB)
    f_mat = jnp.asarray(np.concatenate([f_np, f_np], axis=1), BF16)
    gm = jnp.asarray(_gdn_masks(T_BLK))
    bd = jnp.asarray(_block_diag_mask(T_BLK, GDN_CHUNK))
    tril = jnp.asarray(_block_diag_mask(T_BLK, GDN_CHUNK) * (lv_np >= 0), BF16)
    hm = jnp.asarray(_head_masks(RET_WIDTH, RET_HEADS))
    su = jnp.asarray(np.triu(np.ones((TM, TM), np.float32), 1), BF16)
    eab_np = np.zeros((LANES, 2 * GDN_WIDTH), np.float32)
    for h in range(GDN_HEADS):
        eab_np[h, h * GDN_DK:(h + 1) * GDN_DK] = 1.0
        eab_np[GDN_HEADS + h, GDN_WIDTH + h * GDN_DK:GDN_WIDTH + (h + 1) * GDN_DK] = 1.0
    eab = jnp.asarray(eab_np, BF16)

    half = RET_DK // 2
    inv = ROPE_BASE ** (-jnp.linspace(0.0, 1.0, half, dtype=F32))
    ang = jnp.arange(seq, dtype=F32)[:, None] * inv[None, :]
    cos_t = jnp.tile(jnp.cos(ang), (1, 2 * RET_HEADS))
    first_half = jnp.asarray((np.arange(RET_WIDTH) % RET_DK < half).astype(np.float32))[None, :]
    sin_t = jnp.tile(jnp.sin(ang), (1, 2 * RET_HEADS)) * (1.0 - 2.0 * first_half)
    log_g = jnp.log1p(-jnp.exp2(-5.0 - jnp.arange(RET_HEADS, dtype=F32)))
    jj = jnp.arange(T_BLK, dtype=F32)
    diff = jj[:, None] - jj[None, :]
    dmat = jnp.where(diff[None] >= 0, jnp.exp(diff[None] * log_g[:, None, None]), 0.0)
    lg_lane = jnp.repeat(log_g, RET_DK)[None, :]
    qw = jnp.exp(lg_lane * (jj[:, None] + 1.0))
    kw = jnp.exp(lg_lane * (T_BLK - 1.0 - jj[:, None]))
    cdec = jnp.exp(T_BLK * lg_lane)

    lb = jnp.cumsum(jax.nn.softmax(hg_lb_logits.astype(F32), axis=0), axis=0)
    lb = jnp.maximum(lb - lb[0], 0.0)
    c_pad = _pad_rows(c.astype(F32))
    mod = _adaln(c_pad, ada_w, ada_b)[:, :batch, :]

    x2d = x.reshape(n, d)
    out = None
    for l in range(depth):
        sh1, sc1, gt1, sh2, sc2, gt2 = [mod[l][:, i * d:(i + 1) * d] for i in range(6)]
        zeros = jnp.zeros_like(sh1)
        modv_a = jnp.stack([sc1, sh1, zeros, zeros, zeros, zeros, zeros, zeros], axis=1)
        modv_c = jnp.stack([gt1, sc2, sh2, gt2, zeros, zeros, zeros, zeros], axis=1)
        gpar_a = _pad_rows(norm1_g[l][None, :])
        gpar_c = _pad_rows(jnp.stack([norm2_g[l], final_norm_g], axis=0))

        zh, zr, zg, zab = _inproj(x2d, modv_a, gpar_a, _widen_w_in(w_in[l]), seq)

        hg_par = _pad_rows(jnp.stack([jnp.log(lb[l]), jnp.log1p(-lb[l]), 1.0 - lb[l],
                                      hg_norm_g[l].reshape(-1)], axis=0))
        ret_par = _pad_rows(jnp.concatenate([cdec, ret_norm_g[l].reshape(1, -1), first_half], axis=0))
        gdn_par = _pad_rows(jnp.stack([jnp.repeat(-jnp.exp(gdn_A_log[l].astype(F32)), GDN_DK),
                                       jnp.repeat(gdn_dt_bias[l].astype(F32), GDN_DK),
                                       gdn_norm_g[l].reshape(-1)], axis=0))
        o_hg, o_ret, o_gdn = _mixers(
            zh, zr, zg, zab,
            (f_mat, lv, hg_par),
            (cos_t, sin_t, dmat, qw, kw, ret_par),
            (_pad_rows(gdn_conv_w[l].astype(F32)), gdn_par, eab, gm, tril),
            hm, bd, batch, steps)

        rw_f = jnp.concatenate([router_w[l].T, jnp.zeros((LANES - N_EXPERTS, d), F32)], axis=0)
        rw = jnp.concatenate(_split2(rw_f), axis=0)
        rb = jnp.broadcast_to(jnp.concatenate([router_b[l], jnp.zeros((LANES - N_EXPERTS,), F32)])[:, None],
                              (LANES, TM))
        x1, hn2, ri, gates_t, cnt = _outproj(o_hg, o_ret, o_gdn, x2d, modv_c, gpar_c, _bf(w_out[l]),
                                             rw, rb, su, tiles_per_batch)
        gates = gates_t.T

        counts = cnt[:N_EXPERTS, 0].astype(jnp.int32)
        padded = (counts + R_BLK - 1) // R_BLK * R_BLK
        pend = jnp.cumsum(padded)
        pstart = pend - padded
        n_used = (pend[-1] // R_BLK).astype(jnp.int32)
        blk_start = jnp.arange(n_blocks, dtype=jnp.int32) * R_BLK
        blk_start = jnp.minimum(blk_start, pend[-1] - R_BLK)
        block_e = jnp.sum(blk_start[:, None] >= pend[None, :], axis=1).astype(jnp.int32)
        eid = jnp.arange(N_EXPERTS, dtype=jnp.int32)
        pos_t = ri[TOP_K:] + jnp.sum(jnp.where(ri[:TOP_K, :, None] == eid, pstart, 0), axis=-1)
        xs = _sc_scatter_rows(hn2, pos_t, p_rows)
        y = _experts(block_e + l * N_EXPERTS, n_used.reshape(1), xs, exp_w1, exp_b1, exp_w2, exp_b2)
        n_piece = n // COMBINE_PIECES
        x2d = x1
        for p in range(COMBINE_PIECES):
            idx = pos_t[:, p * n_piece:(p + 1) * n_piece].reshape(-1)
            yg = _sc_gather_rows(y, idx).reshape(TOP_K, n_piece, d // 2)
            x2d = _combine(x2d, yg, gates, modv_c, gpar_c, tiles_per_batch, l == depth - 1, p)
    return x2d.reshape(batch, seq, d)
```
